```python
import math
import jax, jax.numpy as jnp
from jax import lax
import numpy as np

D_MODEL = 1024
BATCH = 8
SEQ = 4096
DEPTH = 1
DEC_BATCH = 2
DEC_SEQ = 8192
PAST_LEN = 128

N_HEADS = 8
QK_DIM = 64
V_DIM = 2 * QK_DIM
ATTN_WIDTH = N_HEADS * V_DIM
Q_COLS = N_HEADS * 2 * QK_DIM
ROT_DIM = QK_DIM // 4
ROPE_THETA = 500000.0
Q_BLOCK = 128
SUBLN_EPS = 1e-5
CONV_WIDTH = D_MODEL
CONV_K = 3
IN_SPLITS = [Q_COLS, 2 * Q_COLS, 2 * Q_COLS + ATTN_WIDTH,
             2 * Q_COLS + ATTN_WIDTH + CONV_WIDTH,
             2 * Q_COLS + ATTN_WIDTH + 2 * CONV_WIDTH,
             2 * Q_COLS + ATTN_WIDTH + 3 * CONV_WIDTH]
IN_COLS = 2 * Q_COLS + ATTN_WIDTH + 3 * CONV_WIDTH + 2 * D_MODEL
N_EXPERTS = 32
TOP_K = 4
D_FF = D_MODEL
SWIGLU_LIMIT = 7.0
SWIGLU_ALPHA = 1.702
MOE_BLOCK = 128
LN_EPS = 1e-5
DEEPNORM_ALPHA = (2 * DEPTH) ** 0.25
DEEPNORM_BETA = (8 * DEPTH) ** -0.25

kernel_name = "hybrid_diffattn_shortconv_moe_encoder"


def layer_norm(x, g, b):
    xf = x.astype(jnp.float32)
    mu = jnp.mean(xf, axis=-1, keepdims=True)
    var = jnp.mean(jnp.square(xf - mu), axis=-1, keepdims=True)
    return ((xf - mu) * lax.rsqrt(var + LN_EPS) * g + b).astype(x.dtype)


def rotary(t, pos):
    inv = ROPE_THETA ** (-jnp.arange(0, ROT_DIM, 2, dtype=jnp.float32) / ROT_DIM)
    ang = pos[:, None] * inv[None, :]
    cos = jnp.cos(ang)[None, :, None, None, :].astype(t.dtype)
    sin = jnp.sin(ang)[None, :, None, None, :].astype(t.dtype)
    half = ROT_DIM // 2
    t1 = t[..., :half]
    t2 = t[..., half:ROT_DIM]
    return jnp.concatenate([t1 * cos - t2 * sin, t2 * cos + t1 * sin, t[..., ROT_DIM:]], axis=-1)


def diff_attention(q, k, v, lam, lam_init, subln_g):
    B, S = q.shape[0], q.shape[1]
    nb = S // Q_BLOCK
    qb = jnp.moveaxis(q.reshape(B, nb, Q_BLOCK, N_HEADS, 2, QK_DIM), 1, 0)
    scale = QK_DIM ** -0.5

    def block(qi):
        s = jnp.einsum('bqhmd,bkhmd->bhmqk', qi, k, preferred_element_type=jnp.float32) * scale
        p = jax.nn.softmax(s, axis=-1)
        w = p[:, :, 0] - lam * p[:, :, 1]
        return jnp.einsum('bhqk,bkhe->bqhe', w.astype(v.dtype), v)

    o = lax.map(block, qb)
    o = jnp.moveaxis(o, 0, 1).reshape(B, S, N_HEADS, V_DIM)
    of = o.astype(jnp.float32)
    of = of * lax.rsqrt(jnp.mean(jnp.square(of), axis=-1, keepdims=True) + SUBLN_EPS)
    of = of * subln_g * (1.0 - lam_init)
    return of.astype(q.dtype).reshape(B, S, ATTN_WIDTH)


def short_conv(u, w):
    up = jnp.pad(u, ((0, 0), (1, 1), (0, 0)))
    return w[0] * up[:, :-2] + w[1] * up[:, 1:-1] + w[2] * up[:, 2:]


def mixer(x, lam_init, w_in, b_branch_gate, lambda_q1, lambda_k1, lambda_q2, lambda_k2,
          subln_g, conv_w, w_attn_br, w_conv_br, w_o):
    B, S, _ = x.shape
    h = x @ w_in
    q, k, v, cb, cc, cx, gates = jnp.split(h, IN_SPLITS, axis=-1)
    q = q.reshape(B, S, N_HEADS, 2, QK_DIM)
    k = k.reshape(B, S, N_HEADS, 2, QK_DIM)
    v = v.reshape(B, S, N_HEADS, V_DIM)
    pos = jnp.arange(S, dtype=jnp.float32)
    q = rotary(q, pos)
    k = rotary(k, pos)
    lam = (jnp.exp(jnp.sum(lambda_q1.astype(jnp.float32) * lambda_k1.astype(jnp.float32)))
           - jnp.exp(jnp.sum(lambda_q2.astype(jnp.float32) * lambda_k2.astype(jnp.float32)))
           + lam_init)
    a = diff_attention(q, k, v, lam, lam_init, subln_g)
    c = cb * short_conv(cc * cx, conv_w)
    g = jax.nn.sigmoid(gates + b_branch_gate)
    g_a, g_c = jnp.split(g, 2, axis=-1)
    merged = g_a * (a @ w_attn_br) + g_c * (c @ w_conv_br)
    return merged @ w_o


def moe(x, w_router, b_router, w_exp_gate, b_exp_gate, w_exp_up, b_exp_up, w_exp_down, b_exp_down):
    B, S, D = x.shape
    N = B * S
    x2 = x.reshape(N, D)
    logits = jnp.einsum('nd,de->ne', x2, w_router, preferred_element_type=jnp.float32) + b_router.astype(jnp.float32)
    top_vals, top_idx = lax.top_k(logits, TOP_K)
    gates = jax.nn.softmax(top_vals, axis=-1).astype(x.dtype)
    NK = N * TOP_K
    P = NK + N_EXPERTS * MOE_BLOCK
    nb = P // MOE_BLOCK
    flat_e = top_idx.reshape(-1).astype(jnp.int32)
    order = jnp.argsort(flat_e)
    sorted_e = flat_e[order]
    counts = jnp.bincount(flat_e, length=N_EXPERTS)
    starts = jnp.cumsum(counts) - counts
    padded = ((counts + MOE_BLOCK - 1) // MOE_BLOCK) * MOE_BLOCK
    pad_ends = jnp.cumsum(padded)
    pad_starts = pad_ends - padded
    dest = pad_starts[sorted_e] + jnp.arange(NK, dtype=jnp.int32) - starts[sorted_e]
    slot_tok = jnp.zeros((P,), jnp.int32).at[dest].set((order // TOP_K).astype(jnp.int32))
    slot_w = jnp.zeros((P,), x.dtype).at[dest].set(gates.reshape(-1)[order])
    block_e = jnp.minimum(jnp.searchsorted(pad_ends, jnp.arange(nb) * MOE_BLOCK, side='right'),
                          N_EXPERTS - 1).astype(jnp.int32)

    def expert_block(args):
        tok, e, wts = args
        xb = x2[tok]
        hg = jnp.minimum(xb @ w_exp_gate[e] + b_exp_gate[e], SWIGLU_LIMIT)
        hu = jnp.clip(xb @ w_exp_up[e] + b_exp_up[e], -SWIGLU_LIMIT, SWIGLU_LIMIT)
        act = hg * jax.nn.sigmoid(SWIGLU_ALPHA * hg) * (hu + 1.0)
        y = act @ w_exp_down[e] + b_exp_down[e]
        return y * wts[:, None]

    ys = lax.map(expert_block, (slot_tok.reshape(nb, MOE_BLOCK), block_e, slot_w.reshape(nb, MOE_BLOCK)))
    out = jnp.zeros((N, D), x.dtype).at[slot_tok].add(ys.reshape(P, D))
    return out.reshape(B, S, D)


def trunk(x, w_in, b_branch_gate, lambda_q1, lambda_k1, lambda_q2, lambda_k2, subln_g, conv_w,
          w_attn_br, w_conv_br, w_o, ln1_g, ln1_b, w_router, b_router, w_exp_gate, b_exp_gate,
          w_exp_up, b_exp_up, w_exp_down, b_exp_down, ln2_g, ln2_b):
    for l in range(DEPTH):
        lam_init = 0.8 - 0.6 * math.exp(-0.3 * l)
        m = mixer(x, lam_init, w_in[l], b_branch_gate[l], lambda_q1[l], lambda_k1[l], lambda_q2[l],
                  lambda_k2[l], subln_g[l], conv_w[l], w_attn_br[l], w_conv_br[l], w_o[l])
        x = layer_norm(DEEPNORM_ALPHA * x + m, ln1_g[l], ln1_b[l])
        f = moe(x, w_router[l], b_router[l], w_exp_gate[l], b_exp_gate[l], w_exp_up[l], b_exp_up[l],
                w_exp_down[l], b_exp_down[l])
        x = layer_norm(DEEPNORM_ALPHA * x + f, ln2_g[l], ln2_b[l])
    return x


def setup_inputs(seed: int = 0) -> dict:
    key = jax.random.key(seed)
    ks = jax.random.split(key, 32)
    f32 = jnp.float32
    nrm = lambda k, shape, s: jax.random.normal(k, shape, f32) * s
    L, D, E, F = DEPTH, D_MODEL, N_EXPERTS, D_FF
    return {
        "x_prompt": nrm(ks[0], (BATCH, SEQ, D), 1.0),
        "x_sample": nrm(ks[1], (DEC_BATCH, DEC_SEQ, D), 1.0),
        "w_in": nrm(ks[2], (L, D, IN_COLS), D ** -0.5),
        "b_branch_gate": nrm(ks[3], (L, 2 * D), 0.01),
        "lambda_q1": nrm(ks[4], (L, QK_DIM), 0.1),
        "lambda_k1": nrm(ks[5], (L, QK_DIM), 0.1),
        "lambda_q2": nrm(ks[6], (L, QK_DIM), 0.1),
        "lambda_k2": nrm(ks[7], (L, QK_DIM), 0.1),
        "subln_g": 1.0 + nrm(ks[8], (L, V_DIM), 0.01),
        "conv_w": nrm(ks[9], (L, CONV_K, CONV_WIDTH), CONV_K ** -0.5),
        "w_attn_br": nrm(ks[10], (L, ATTN_WIDTH, D), ATTN_WIDTH ** -0.5),
        "w_conv_br": nrm(ks[11], (L, CONV_WIDTH, D), CONV_WIDTH ** -0.5),
        "w_o": nrm(ks[12], (L, D, D), D ** -0.5 * DEEPNORM_BETA),
        "ln1_g": 1.0 + nrm(ks[13], (L, D), 0.01),
        "ln1_b": nrm(ks[14], (L, D), 0.01),
        "w_router": nrm(ks[15], (L, D, E), D ** -0.5),
        "b_router": nrm(ks[16], (L, E), 0.01),
        "w_exp_gate": nrm(ks[17], (L, E, D, F), D ** -0.5),
        "b_exp_gate": nrm(ks[18], (L, E, F), 0.01),
        "w_exp_up": nrm(ks[19], (L, E, D, F), D ** -0.5),
        "b_exp_up": nrm(ks[20], (L, E, F), 0.01),
        "w_exp_down": nrm(ks[21], (L, E, F, D), F ** -0.5 * DEEPNORM_BETA),
        "b_exp_down": nrm(ks[22], (L, E, D), 0.01),
        "ln2_g": 1.0 + nrm(ks[23], (L, D), 0.01),
        "ln2_b": nrm(ks[24], (L, D), 0.01),
    }


def reference(x_prompt, x_sample, w_in, b_branch_gate, lambda_q1, lambda_k1, lambda_q2, lambda_k2,
              subln_g, conv_w, w_attn_br, w_conv_br, w_o, ln1_g, ln1_b, w_router, b_router,
              w_exp_gate, b_exp_gate, w_exp_up, b_exp_up, w_exp_down, b_exp_down, ln2_g, ln2_b):
    params = (w_in, b_branch_gate, lambda_q1, lambda_k1, lambda_q2, lambda_k2, subln_g, conv_w,
              w_attn_br, w_conv_br, w_o, ln1_g, ln1_b, w_router, b_router, w_exp_gate, b_exp_gate,
              w_exp_up, b_exp_up, w_exp_down, b_exp_down, ln2_g, ln2_b)
    y_prompt = trunk(x_prompt, *params)
    y_sample = trunk(x_sample, *params)
    return (y_prompt, y_sample)
```

```python
import functools
import math

import jax
import jax.numpy as jnp
from jax import lax
from jax.experimental import pallas as pl
from jax.experimental.pallas import tpu as pltpu

F32 = jnp.float32
BF16 = jnp.bfloat16

D_MODEL = 1024
N_HEADS = 8
QK_DIM = 64
V_DIM = 128
HEAD_COLS = 2 * QK_DIM
ROT_DIM = QK_DIM // 4
ROT_HALF = ROT_DIM // 2
ROPE_THETA = 500000.0
SUBLN_EPS = 1e-5
IN_COLS = 8192
COL_BLOCK = 1024
CB_BLK, CC_BLK, CX_BLK, GA_BLK, GC_BLK = 3, 4, 5, 6, 7
K_HEAD0, V_HEAD0 = 8, 16
N_EXPERTS = 32
TOP_K = 4
SWIGLU_LIMIT = 7.0
SWIGLU_ALPHA = 1.702
LN_EPS = 1e-5
DEPTH = 1
DEEPNORM_ALPHA = (2 * DEPTH) ** 0.25
LANES = 128
SUBLANES = 8
ROUTER_COLS = LANES

VMEM_LIMIT = 56 * 1024 * 1024

PROJ_TM = 1024
ATTN_TQ = 512
ATTN_TK = 512
MIX_TM = 256
MOE_TM = 512
LN2_TM = 512


def _inproj_kernel(x_ref, w_ref, cos_ref, sa_ref, sb_ref, o_ref):
    j = pl.program_id(1)
    acc = jnp.dot(x_ref[...].astype(BF16), w_ref[...], preferred_element_type=F32)

    @pl.when(j >= 2)
    def _():
        o_ref[...] = acc.astype(o_ref.dtype)

    @pl.when(j < 2)
    def _():
        scale = jnp.where(j == 0, QK_DIM ** -0.5, 1.0).astype(F32)
        cos = cos_ref[...] * scale
        sa = sa_ref[...] * scale
        sb = sb_ref[...] * scale
        for c in range(COL_BLOCK // LANES):
            a = acc[:, c * LANES:(c + 1) * LANES]
            r = (a * cos + pltpu.roll(a, LANES - ROT_HALF, 1) * sa + pltpu.roll(a, ROT_HALF, 1) * sb)
            o_ref[:, c * LANES:(c + 1) * LANES] = r.astype(o_ref.dtype)


def _inproj(x2d, w_in_b, cos_t, sa_t, sb_t, seq):
    n = x2d.shape[0]
    tm = PROJ_TM
    pos_blocks = seq // tm
    tab_spec = pl.BlockSpec((tm, LANES), lambda i, j: (i % pos_blocks, 0))
    return pl.pallas_call(
        _inproj_kernel,
        out_shape=jax.ShapeDtypeStruct((n, IN_COLS), BF16),
        grid=(n // tm, IN_COLS // COL_BLOCK),
        in_specs=[
            pl.BlockSpec((tm, D_MODEL), lambda i, j: (i, 0)),
            pl.BlockSpec((D_MODEL, COL_BLOCK), lambda i, j: (0, j)),
            tab_spec, tab_spec, tab_spec,
        ],
        out_specs=pl.BlockSpec((tm, COL_BLOCK), lambda i, j: (i, j)),
        compiler_params=pltpu.CompilerParams(
            dimension_semantics=("parallel", "arbitrary"), vmem_limit_bytes=VMEM_LIMIT),
        name="inproj",
    )(x2d, w_in_b, cos_t, sa_t, sb_t)


def _attn_kernel(lam_ref, q_ref, k_ref, v_ref, g_ref, o_ref,
                 q1_s, q2_s, m1_s, l1_s, a1_s, m2_s, l2_s, a2_s, *, out_scale):
    ki = pl.program_id(3)

    @pl.when(ki == 0)
    def _():
        q = q_ref[...]
        lane = lax.broadcasted_iota(jnp.int32, q.shape, 1)
        zero = jnp.zeros_like(q)
        q1_s[...] = jnp.where(lane < QK_DIM, q, zero)
        q2_s[...] = jnp.where(lane >= QK_DIM, q, zero)
        for m_s, l_s, a_s in ((m1_s, l1_s, a1_s), (m2_s, l2_s, a2_s)):
            m_s[...] = jnp.full(m_s.shape, -jnp.inf, F32)
            l_s[...] = jnp.zeros(l_s.shape, F32)
            a_s[...] = jnp.zeros(a_s.shape, F32)

    k = k_ref[...]
    v = v_ref[...]
    nt = (((1,), (1,)), ((), ()))
    for qm_s, m_s, l_s, a_s in ((q1_s, m1_s, l1_s, a1_s), (q2_s, m2_s, l2_s, a2_s)):
        s = lax.dot_general(qm_s[...], k, nt, preferred_element_type=F32)
        m_prev = m_s[...]
        m_new = jnp.maximum(m_prev, jnp.max(s, axis=-1, keepdims=True))
        alpha = jnp.exp(m_prev - m_new)
        p = jnp.exp(s - m_new)
        l_s[...] = alpha * l_s[...] + jnp.sum(p, axis=-1, keepdims=True)
        a_s[...] = alpha * a_s[...] + jnp.dot(p.astype(BF16), v, preferred_element_type=F32)
        m_s[...] = m_new

    @pl.when(ki == pl.num_programs(3) - 1)
    def _():
        lam = lam_ref[0]
        o = a1_s[...] / l1_s[...] - lam * (a2_s[...] / l2_s[...])
        o = o * lax.rsqrt(jnp.mean(o * o, axis=-1, keepdims=True) + SUBLN_EPS)
        o_ref[...] = (o * g_ref[...] * out_scale).astype(o_ref.dtype)


def _attention(h3, lam, subln_g, lam_init):
    b, s, _ = h3.shape
    tq, tk = ATTN_TQ, ATTN_TK
    kern = functools.partial(_attn_kernel, out_scale=1.0 - lam_init)
    return pl.pallas_call(
        kern,
        out_shape=jax.ShapeDtypeStruct((b, s, N_HEADS * V_DIM), BF16),
        grid=(b, N_HEADS, s // tq, s // tk),
        in_specs=[
            pl.BlockSpec(memory_space=pltpu.SMEM),
            pl.BlockSpec((None, tq, HEAD_COLS), lambda bi, hi, qi, ki: (bi, qi, hi)),
            pl.BlockSpec((None, tk, HEAD_COLS), lambda bi, hi, qi, ki: (bi, ki, K_HEAD0 + hi)),
            pl.BlockSpec((None, tk, V_DIM), lambda bi, hi, qi, ki: (bi, ki, V_HEAD0 + hi)),
            pl.BlockSpec((1, V_DIM), lambda bi, hi, qi, ki: (0, 0)),
        ],
        out_specs=pl.BlockSpec((None, tq, V_DIM), lambda bi, hi, qi, ki: (bi, qi, hi)),
        scratch_shapes=[
            pltpu.VMEM((tq, HEAD_COLS), BF16), pltpu.VMEM((tq, HEAD_COLS), BF16),
            pltpu.VMEM((tq, 1), F32), pltpu.VMEM((tq, 1), F32), pltpu.VMEM((tq, V_DIM), F32),
            pltpu.VMEM((tq, 1), F32), pltpu.VMEM((tq, 1), F32), pltpu.VMEM((tq, V_DIM), F32),
        ],
        compiler_params=pltpu.CompilerParams(
            dimension_semantics=("parallel", "parallel", "parallel", "arbitrary"),
            vmem_limit_bytes=VMEM_LIMIT),
        name="diff_attention",
    )(lam, h3, h3, h3, subln_g)


def _layer_norm(y, g, b):
    mu = jnp.mean(y, axis=-1, keepdims=True)
    d = y - mu
    var = jnp.mean(d * d, axis=-1, keepdims=True)
    return d * lax.rsqrt(var + LN_EPS) * g + b


def _mix_kernel(a_ref, cb_ref, cc_ref, cx_ref, ga_ref, gc_ref, ccp_ref, cxp_ref, ccn_ref, cxn_ref,
                x_ref, bg_ref, cw_ref, wa_ref, wc_ref, wo_ref, g1_ref, b1_ref, wr_ref, br_ref,
                x1_ref, x1b_ref, lg_ref, *, tiles_per_seq):
    i = pl.program_id(0)
    tm = x_ref.shape[0]
    u = cc_ref[...].astype(F32) * cx_ref[...].astype(F32)
    first = (i % tiles_per_seq) == 0
    last = (i % tiles_per_seq) == tiles_per_seq - 1
    up = ccp_ref[SUBLANES - 1:SUBLANES, :].astype(F32) * cxp_ref[SUBLANES - 1:SUBLANES, :].astype(F32)
    un = ccn_ref[0:1, :].astype(F32) * cxn_ref[0:1, :].astype(F32)
    up = jnp.where(first, 0.0, up)
    un = jnp.where(last, 0.0, un)
    row = lax.broadcasted_iota(jnp.int32, u.shape, 0)
    u_prev = jnp.where(row == 0, up, pltpu.roll(u, 1, 0))
    u_next = jnp.where(row == tm - 1, un, pltpu.roll(u, tm - 1, 0))
    conv = cw_ref[0:1, :] * u_prev + cw_ref[1:2, :] * u + cw_ref[2:3, :] * u_next
    c = (cb_ref[...].astype(F32) * conv).astype(BF16)
    g_a = jax.nn.sigmoid(ga_ref[...].astype(F32) + bg_ref[:, :D_MODEL])
    g_c = jax.nn.sigmoid(gc_ref[...].astype(F32) + bg_ref[:, D_MODEL:])
    merged = (g_a * jnp.dot(a_ref[...], wa_ref[...], preferred_element_type=F32)
              + g_c * jnp.dot(c, wc_ref[...], preferred_element_type=F32))
    m = jnp.dot(merged.astype(BF16), wo_ref[...], preferred_element_type=F32)
    x1 = _layer_norm(DEEPNORM_ALPHA * x_ref[...] + m, g1_ref[...], b1_ref[...])
    x1_ref[...] = x1
    x1b = x1.astype(BF16)
    x1b_ref[...] = x1b
    lg_ref[...] = jnp.dot(x1b, wr_ref[...], preferred_element_type=F32) + br_ref[...]


def _mix(a2d, h2d, x2d, seq, bg, cw, wa, wc, wo, g1, b1, wr, br):
    n = x2d.shape[0]
    tm = MIX_TM
    tiles_per_seq = seq // tm
    halo_blocks = tm // SUBLANES
    n_halo = n // SUBLANES

    def col(blk):
        return pl.BlockSpec((tm, COL_BLOCK), lambda i: (i, blk))

    def prev(blk):
        return pl.BlockSpec((SUBLANES, COL_BLOCK), lambda i: (jnp.maximum(i * halo_blocks - 1, 0), blk))

    def nxt(blk):
        return pl.BlockSpec((SUBLANES, COL_BLOCK),
                            lambda i: (jnp.minimum((i + 1) * halo_blocks, n_halo - 1), blk))

    def full(shape):
        return pl.BlockSpec(shape, lambda i: (0,) * len(shape))

    kern = functools.partial(_mix_kernel, tiles_per_seq=tiles_per_seq)
    return pl.pallas_call(
        kern,
        out_shape=(jax.ShapeDtypeStruct((n, D_MODEL), F32),
                   jax.ShapeDtypeStruct((n, D_MODEL), BF16),
                   jax.ShapeDtypeStruct((n, ROUTER_COLS), F32)),
        grid=(n // tm,),
        in_specs=[
            pl.BlockSpec((tm, D_MODEL), lambda i: (i, 0)),
            col(CB_BLK), col(CC_BLK), col(CX_BLK), col(GA_BLK), col(GC_BLK),
            prev(CC_BLK), prev(CX_BLK), nxt(CC_BLK), nxt(CX_BLK),
            pl.BlockSpec((tm, D_MODEL), lambda i: (i, 0)),
            full((1, 2 * D_MODEL)), full((SUBLANES, D_MODEL)),
            full((D_MODEL, D_MODEL)), full((D_MODEL, D_MODEL)), full((D_MODEL, D_MODEL)),
            full((1, D_MODEL)), full((1, D_MODEL)),
            full((D_MODEL, ROUTER_COLS)), full((1, ROUTER_COLS)),
        ],
        out_specs=(pl.BlockSpec((tm, D_MODEL), lambda i: (i, 0)),
                   pl.BlockSpec((tm, D_MODEL), lambda i: (i, 0)),
                   pl.BlockSpec((tm, ROUTER_COLS), lambda i: (i, 0))),
        compiler_params=pltpu.CompilerParams(
            dimension_semantics=("parallel",), vmem_limit_bytes=VMEM_LIMIT),
        name="mix_ln1_router",
    )(a2d, h2d, h2d, h2d, h2d, h2d, h2d, h2d, h2d, h2d, x2d, bg, cw, wa, wc, wo, g1, b1, wr, br)


def _moe_kernel(be_ref, nu_ref, xs_ref, sw_ref, wg_ref, bg_ref, wu_ref, bu_ref, wd_ref, bd_ref, o_ref):
    i = pl.program_id(0)

    @pl.when(i < nu_ref[0])
    def _():
        x = xs_ref[...]
        hg = jnp.minimum(jnp.dot(x, wg_ref[...], preferred_element_type=F32) + bg_ref[...], SWIGLU_LIMIT)
        hu = jnp.clip(jnp.dot(x, wu_ref[...], preferred_element_type=F32) + bu_ref[...],
                      -SWIGLU_LIMIT, SWIGLU_LIMIT)
        act = hg * jax.nn.sigmoid(SWIGLU_ALPHA * hg) * (hu + 1.0)
        y = jnp.dot(act.astype(BF16), wd_ref[...], preferred_element_type=F32) + bd_ref[...]
        o_ref[...] = (y * sw_ref[...]).astype(o_ref.dtype)

    @pl.when(i >= nu_ref[0])
    def _():
        o_ref[...] = jnp.zeros(o_ref.shape, o_ref.dtype)


def _moe_experts(block_e, n_used, xs, slot_w, wg, bg, wu, bu, wd, bd):
    p = xs.shape[0]
    tm = MOE_TM
    nb = p // tm

    def row_map(i, be, nu):
        return (jnp.minimum(i, nu[0] - 1), 0)

    def w_map(i, be, nu):
        return (be[jnp.minimum(i, nu[0] - 1)], 0, 0)

    w_spec = pl.BlockSpec((None, D_MODEL, D_MODEL), w_map)
    b_spec = pl.BlockSpec((None, 1, D_MODEL), w_map)
    return pl.pallas_call(
        _moe_kernel,
        out_shape=jax.ShapeDtypeStruct((p, D_MODEL), BF16),
        grid_spec=pltpu.PrefetchScalarGridSpec(
            num_scalar_prefetch=2,
            grid=(nb,),
            in_specs=[
                pl.BlockSpec((tm, D_MODEL), row_map),
                pl.BlockSpec((tm, 1), row_map),
                w_spec, b_spec, w_spec, b_spec, w_spec, b_spec,
            ],
            out_specs=pl.BlockSpec((tm, D_MODEL), lambda i, be, nu: (i, 0)),
        ),
        compiler_params=pltpu.CompilerParams(
            dimension_semantics=("arbitrary",), vmem_limit_bytes=VMEM_LIMIT),
        name="moe_experts",
    )(block_e, n_used, xs, slot_w, wg, bg, wu, bu, wd, bd)


def _ln2_kernel(x1_ref, yg_ref, g_ref, b_ref, o_ref):
    f = (yg_ref[0].astype(F32) + yg_ref[1].astype(F32)) + (yg_ref[2].astype(F32) + yg_ref[3].astype(F32))
    o_ref[...] = _layer_norm(DEEPNORM_ALPHA * x1_ref[...] + f, g_ref[...], b_ref[...])


def _combine_ln2(x1, yg, g2, b2):
    n = x1.shape[0]
    tm = LN2_TM
    return pl.pallas_call(
        _ln2_kernel,
        out_shape=jax.ShapeDtypeStruct((n, D_MODEL), F32),
        grid=(n // tm,),
        in_specs=[
            pl.BlockSpec((tm, D_MODEL), lambda i: (i, 0)),
            pl.BlockSpec((TOP_K, tm, D_MODEL), lambda i: (0, i, 0)),
            pl.BlockSpec((1, D_MODEL), lambda i: (0, 0)),
            pl.BlockSpec((1, D_MODEL), lambda i: (0, 0)),
        ],
        out_specs=pl.BlockSpec((tm, D_MODEL), lambda i: (i, 0)),
        compiler_params=pltpu.CompilerParams(
            dimension_semantics=("parallel",), vmem_limit_bytes=VMEM_LIMIT),
        name="combine_ln2",
    )(x1, yg, g2, b2)


def _rotary_tables(seq):
    inv = ROPE_THETA ** (-jnp.arange(0, ROT_DIM, 2, dtype=F32) / ROT_DIM)
    ang = jnp.arange(seq, dtype=F32)[:, None] * inv[None, :]
    cos, sin = jnp.cos(ang), jnp.sin(ang)
    ones = jnp.ones((seq, QK_DIM - ROT_DIM), F32)
    zeros = jnp.zeros((seq, QK_DIM - ROT_DIM), F32)
    zh = jnp.zeros((seq, ROT_HALF), F32)
    cos64 = jnp.concatenate([cos, cos, ones], axis=1)
    sa64 = jnp.concatenate([-sin, zh, zeros], axis=1)
    sb64 = jnp.concatenate([zh, sin, zeros], axis=1)
    rep = LANES // QK_DIM
    return jnp.tile(cos64, (1, rep)), jnp.tile(sa64, (1, rep)), jnp.tile(sb64, (1, rep))


def _route(logits, tm):
    n = logits.shape[0]
    nk = n * TOP_K
    p = nk + N_EXPERTS * tm
    nb = p // tm
    top_vals, top_idx = lax.top_k(logits, TOP_K)
    gates = jax.nn.softmax(top_vals, axis=-1)
    flat_e = top_idx.reshape(-1).astype(jnp.int32)
    order = jnp.argsort(flat_e)
    sorted_e = flat_e[order]
    counts = jnp.bincount(flat_e, length=N_EXPERTS)
    starts = jnp.cumsum(counts) - counts
    padded = ((counts + tm - 1) // tm) * tm
    pad_ends = jnp.cumsum(padded)
    pad_starts = pad_ends - padded
    dest_sorted = (pad_starts[sorted_e] + jnp.arange(nk, dtype=jnp.int32) - starts[sorted_e]).astype(jnp.int32)
    slot_tok = jnp.zeros((p,), jnp.int32).at[dest_sorted].set((order // TOP_K).astype(jnp.int32))
    slot_w = jnp.zeros((p,), F32).at[dest_sorted].set(gates.reshape(-1)[order])
    dest = jnp.zeros((nk,), jnp.int32).at[order].set(dest_sorted)
    block_e = jnp.minimum(jnp.searchsorted(pad_ends, jnp.arange(nb) * tm, side='right'),
                          N_EXPERTS - 1).astype(jnp.int32)
    n_used = (pad_ends[-1] // tm).astype(jnp.int32).reshape(1)
    return slot_tok, slot_w, dest.reshape(n, TOP_K), block_e, n_used


def _trunk(x, prm):
    b, s, d = x.shape
    n = b * s
    x2d = x.reshape(n, d)
    cos_t, sa_t, sb_t = _rotary_tables(s)
    h = _inproj(x2d, prm["w_in"], cos_t, sa_t, sb_t, s)
    a = _attention(h.reshape(b, s, IN_COLS), prm["lam"], prm["subln_g"], prm["lam_init"])
    x1, x1b, logits = _mix(a.reshape(n, d), h, x2d, s, prm["b_gate"], prm["conv_w"], prm["w_attn_br"],
                           prm["w_conv_br"], prm["w_o"], prm["ln1_g"], prm["ln1_b"],
                           prm["w_router"], prm["b_router"])
    slot_tok, slot_w, dest, block_e, n_used = _route(logits[:, :N_EXPERTS], MOE_TM)
    xs = x1b[slot_tok]
    ys = _moe_experts(block_e, n_used, xs, slot_w[:, None], prm["w_exp_gate"], prm["b_exp_gate"],
                      prm["w_exp_up"], prm["b_exp_up"], prm["w_exp_down"], prm["b_exp_down"])
    yg = ys[dest.T]
    y = _combine_ln2(x1, yg, prm["ln2_g"], prm["ln2_b"])
    return y.reshape(b, s, d)


def kernel(x_prompt, x_sample, w_in, b_branch_gate, lambda_q1, lambda_k1, lambda_q2, lambda_k2, subln_g,
           conv_w, w_attn_br, w_conv_br, w_o, ln1_g, ln1_b, w_router, b_router, w_exp_gate, b_exp_gate,
           w_exp_up, b_exp_up, w_exp_down, b_exp_down, ln2_g, ln2_b):
    l = 0
    lam_init = 0.8 - 0.6 * math.exp(-0.3 * l)
    lam = (jnp.exp(jnp.sum(lambda_q1[l].astype(F32) * lambda_k1[l].astype(F32)))
           - jnp.exp(jnp.sum(lambda_q2[l].astype(F32) * lambda_k2[l].astype(F32))) + lam_init)
    prm = {
        "lam_init": lam_init,
        "lam": lam.reshape(1).astype(F32),
        "w_in": w_in[l].astype(BF16),
        "b_gate": b_branch_gate[l].reshape(1, 2 * D_MODEL),
        "subln_g": subln_g[l].reshape(1, V_DIM),
        "conv_w": jnp.pad(conv_w[l], ((0, SUBLANES - conv_w.shape[1]), (0, 0))),
        "w_attn_br": w_attn_br[l].astype(BF16),
        "w_conv_br": w_conv_br[l].astype(BF16),
        "w_o": w_o[l].astype(BF16),
        "ln1_g": ln1_g[l].reshape(1, D_MODEL),
        "ln1_b": ln1_b[l].reshape(1, D_MODEL),
        "w_router": jnp.pad(w_router[l], ((0, 0), (0, ROUTER_COLS - N_EXPERTS))).astype(BF16),
        "b_router": jnp.pad(b_router[l], (0, ROUTER_COLS - N_EXPERTS)).reshape(1, ROUTER_COLS),
        "w_exp_gate": w_exp_gate[l].astype(BF16),
        "b_exp_gate": b_exp_gate[l].reshape(N_EXPERTS, 1, D_MODEL),
        "w_exp_up": w_exp_up[l].astype(BF16),
        "b_exp_up": b_exp_up[l].reshape(N_EXPERTS, 1, D_MODEL),
        "w_exp_down": w_exp_down[l].astype(BF16),
        "b_exp_down": b_exp_down[l].reshape(N_EXPERTS, 1, D_MODEL),
        "ln2_g": ln2_g[l].reshape(1, D_MODEL),
        "ln2_b": ln2_b[l].reshape(1, D_MODEL),
    }
    return _trunk(x_prompt, prm), _trunk(x_sample, prm)
```

```python
import functools
import math

import jax
import jax.numpy as jnp
from jax import lax
from jax.experimental import pallas as pl
from jax.experimental.pallas import tpu as pltpu

F32 = jnp.float32
BF16 = jnp.bfloat16

D_MODEL = 1024
N_HEADS = 8
QK_DIM = 64
V_DIM = 128
HEAD_COLS = 2 * QK_DIM
ROT_DIM = QK_DIM // 4
ROT_HALF = ROT_DIM // 2
ROPE_THETA = 500000.0
SUBLN_EPS = 1e-5
IN_COLS = 8192
COL_BLOCK = 1024
CB_BLK, CC_BLK, CX_BLK, GA_BLK, GC_BLK = 3, 4, 5, 6, 7
K_HEAD0, V_HEAD0 = 8, 16
N_EXPERTS = 32
TOP_K = 4
SWIGLU_LIMIT = 7.0
SWIGLU_ALPHA = 1.702
LN_EPS = 1e-5
DEPTH = 1
DEEPNORM_ALPHA = (2 * DEPTH) ** 0.25
LANES = 128
SUBLANES = 8
ROUTER_COLS = LANES

VMEM_LIMIT = 56 * 1024 * 1024

PROJ_TM = 1024
ATTN_TQ = 512
ATTN_TK = 512
MIX_TM = 256
MOE_TM = 512
LN2_TM = 512


def _inproj_kernel(x_ref, w_ref, cos_ref, sa_ref, sb_ref, o_ref):
    j = pl.program_id(1)
    acc = jnp.dot(x_ref[...].astype(BF16), w_ref[...], preferred_element_type=F32)

    @pl.when(j >= 2)
    def _():
        o_ref[...] = acc.astype(o_ref.dtype)

    @pl.when(j < 2)
    def _():
        scale = jnp.where(j == 0, QK_DIM ** -0.5, 1.0).astype(F32)
        cos = cos_ref[...] * scale
        sa = sa_ref[...] * scale
        sb = sb_ref[...] * scale
        for c in range(COL_BLOCK // LANES):
            a = acc[:, c * LANES:(c + 1) * LANES]
            r = (a * cos + pltpu.roll(a, LANES - ROT_HALF, 1) * sa + pltpu.roll(a, ROT_HALF, 1) * sb)
            o_ref[:, c * LANES:(c + 1) * LANES] = r.astype(o_ref.dtype)


def _inproj(x2d, w_in_b, cos_t, sa_t, sb_t, seq):
    n = x2d.shape[0]
    tm = PROJ_TM
    pos_blocks = seq // tm
    tab_spec = pl.BlockSpec((tm, LANES), lambda i, j: (i % pos_blocks, 0))
    return pl.pallas_call(
        _inproj_kernel,
        out_shape=jax.ShapeDtypeStruct((n, IN_COLS), BF16),
        grid=(n // tm, IN_COLS // COL_BLOCK),
        in_specs=[
            pl.BlockSpec((tm, D_MODEL), lambda i, j: (i, 0)),
            pl.BlockSpec((D_MODEL, COL_BLOCK), lambda i, j: (0, j)),
            tab_spec, tab_spec, tab_spec,
        ],
        out_specs=pl.BlockSpec((tm, COL_BLOCK), lambda i, j: (i, j)),
        compiler_params=pltpu.CompilerParams(
            dimension_semantics=("parallel", "arbitrary"), vmem_limit_bytes=VMEM_LIMIT),
        name="inproj",
    )(x2d, w_in_b, cos_t, sa_t, sb_t)


def _attn_kernel(lam_ref, q_ref, k_ref, vt_ref, g_ref, o_ref, qt_s, acc_s, *, out_scale):
    tq = q_ref.shape[0]
    n_chunks, _, tk = vt_ref.shape
    qf = q_ref[...].astype(F32)
    lane = lax.broadcasted_iota(jnp.int32, qf.shape, 1)
    qt_s[0] = jnp.where(lane < QK_DIM, qf, 0.0).T.astype(BF16)
    qt_s[1] = jnp.where(lane >= QK_DIM, qf, 0.0).T.astype(BF16)
    acc_s[...] = jnp.zeros(acc_s.shape, F32)

    def chunk(c, carry):
        k = k_ref[pl.ds(pl.multiple_of(c * tk, tk), tk), :]
        vt = vt_ref[c]
        new = []
        for mp in range(2):
            m_prev, l_prev = carry[2 * mp], carry[2 * mp + 1]
            s = jnp.dot(k, qt_s[mp], preferred_element_type=F32)
            m_new = jnp.maximum(m_prev, jnp.max(s, axis=0, keepdims=True))
            alpha = jnp.exp(m_prev - m_new)
            p = jnp.exp(s - m_new)
            l_new = alpha * l_prev + jnp.sum(p, axis=0, keepdims=True)
            acc_s[mp] = alpha * acc_s[mp] + jnp.dot(vt, p.astype(BF16), preferred_element_type=F32)
            new += [m_new, l_new]
        return tuple(new)

    m0 = jnp.full((1, tq), -jnp.inf, F32)
    l0 = jnp.zeros((1, tq), F32)
    _, l1, _, l2 = lax.fori_loop(0, n_chunks, chunk, (m0, l0, m0, l0))

    lam = lam_ref[0]
    ot = acc_s[0] / l1 - lam * (acc_s[1] / l2)
    ot = ot * lax.rsqrt(jnp.mean(ot * ot, axis=0, keepdims=True) + SUBLN_EPS)
    o_ref[...] = (ot.T * (g_ref[...] * out_scale)).astype(o_ref.dtype)


def _attention(h3, lam, subln_g, lam_init):
    b, s, _ = h3.shape
    tq, tk = ATTN_TQ, ATTN_TK
    n_chunks = s // tk
    v = h3[:, :, V_HEAD0 * LANES:(V_HEAD0 + N_HEADS) * LANES]
    vt = v.reshape(b, n_chunks, tk, N_HEADS, V_DIM).transpose(0, 3, 1, 4, 2)
    kern = functools.partial(_attn_kernel, out_scale=1.0 - lam_init)
    return pl.pallas_call(
        kern,
        out_shape=jax.ShapeDtypeStruct((b, s, N_HEADS * V_DIM), BF16),
        grid=(b, N_HEADS, s // tq),
        in_specs=[
            pl.BlockSpec(memory_space=pltpu.SMEM),
            pl.BlockSpec((None, tq, HEAD_COLS), lambda bi, hi, qi: (bi, qi, hi)),
            pl.BlockSpec((None, s, HEAD_COLS), lambda bi, hi, qi: (bi, 0, K_HEAD0 + hi)),
            pl.BlockSpec((None, None, n_chunks, V_DIM, tk), lambda bi, hi, qi: (bi, hi, 0, 0, 0)),
            pl.BlockSpec((1, V_DIM), lambda bi, hi, qi: (0, 0)),
        ],
        out_specs=pl.BlockSpec((None, tq, V_DIM), lambda bi, hi, qi: (bi, qi, hi)),
        scratch_shapes=[
            pltpu.VMEM((2, HEAD_COLS, tq), BF16),
            pltpu.VMEM((2, V_DIM, tq), F32),
        ],
        compiler_params=pltpu.CompilerParams(
            dimension_semantics=("parallel", "parallel", "arbitrary"),
            vmem_limit_bytes=VMEM_LIMIT),
        name="diff_attention",
    )(lam, h3, h3, vt, subln_g)


def _layer_norm(y, g, b):
    mu = jnp.mean(y, axis=-1, keepdims=True)
    d = y - mu
    var = jnp.mean(d * d, axis=-1, keepdims=True)
    return d * lax.rsqrt(var + LN_EPS) * g + b


def _mix_kernel(a_ref, cb_ref, cc_ref, cx_ref, ga_ref, gc_ref, ccp_ref, cxp_ref, ccn_ref, cxn_ref,
                x_ref, bg_ref, cw_ref, wa_ref, wc_ref, wo_ref, g1_ref, b1_ref, wr_ref, br_ref,
                x1_ref, x1b_ref, lg_ref, *, tiles_per_seq):
    i = pl.program_id(0)
    tm = x_ref.shape[0]
    u = cc_ref[...].astype(F32) * cx_ref[...].astype(F32)
    first = (i % tiles_per_seq) == 0
    last = (i % tiles_per_seq) == tiles_per_seq - 1
    up = ccp_ref[SUBLANES - 1:SUBLANES, :].astype(F32) * cxp_ref[SUBLANES - 1:SUBLANES, :].astype(F32)
    un = ccn_ref[0:1, :].astype(F32) * cxn_ref[0:1, :].astype(F32)
    up = jnp.where(first, 0.0, up)
    un = jnp.where(last, 0.0, un)
    row = lax.broadcasted_iota(jnp.int32, u.shape, 0)
    u_prev = jnp.where(row == 0, up, pltpu.roll(u, 1, 0))
    u_next = jnp.where(row == tm - 1, un, pltpu.roll(u, tm - 1, 0))
    conv = cw_ref[0:1, :] * u_prev + cw_ref[1:2, :] * u + cw_ref[2:3, :] * u_next
    c = (cb_ref[...].astype(F32) * conv).astype(BF16)
    g_a = jax.nn.sigmoid(ga_ref[...].astype(F32) + bg_ref[:, :D_MODEL])
    g_c = jax.nn.sigmoid(gc_ref[...].astype(F32) + bg_ref[:, D_MODEL:])
    merged = (g_a * jnp.dot(a_ref[...], wa_ref[...], preferred_element_type=F32)
              + g_c * jnp.dot(c, wc_ref[...], preferred_element_type=F32))
    m = jnp.dot(merged.astype(BF16), wo_ref[...], preferred_element_type=F32)
    x1 = _layer_norm(DEEPNORM_ALPHA * x_ref[...] + m, g1_ref[...], b1_ref[...])
    x1_ref[...] = x1
    x1b = x1.astype(BF16)
    x1b_ref[...] = x1b
    lg_ref[...] = jnp.dot(x1b, wr_ref[...], preferred_element_type=F32) + br_ref[...]


def _mix(a2d, h2d, x2d, seq, bg, cw, wa, wc, wo, g1, b1, wr, br):
    n = x2d.shape[0]
    tm = MIX_TM
    tiles_per_seq = seq // tm
    halo_blocks = tm // SUBLANES
    n_halo = n // SUBLANES

    def col(blk):
        return pl.BlockSpec((tm, COL_BLOCK), lambda i: (i, blk))

    def prev(blk):
        return pl.BlockSpec((SUBLANES, COL_BLOCK), lambda i: (jnp.maximum(i * halo_blocks - 1, 0), blk))

    def nxt(blk):
        return pl.BlockSpec((SUBLANES, COL_BLOCK),
                            lambda i: (jnp.minimum((i + 1) * halo_blocks, n_halo - 1), blk))

    def full(shape):
        return pl.BlockSpec(shape, lambda i: (0,) * len(shape))

    kern = functools.partial(_mix_kernel, tiles_per_seq=tiles_per_seq)
    return pl.pallas_call(
        kern,
        out_shape=(jax.ShapeDtypeStruct((n, D_MODEL), F32),
                   jax.ShapeDtypeStruct((n, D_MODEL), BF16),
                   jax.ShapeDtypeStruct((n, ROUTER_COLS), F32)),
        grid=(n // tm,),
        in_specs=[
            pl.BlockSpec((tm, D_MODEL), lambda i: (i, 0)),
            col(CB_BLK), col(CC_BLK), col(CX_BLK), col(GA_BLK), col(GC_BLK),
            prev(CC_BLK), prev(CX_BLK), nxt(CC_BLK), nxt(CX_BLK),
            pl.BlockSpec((tm, D_MODEL), lambda i: (i, 0)),
            full((1, 2 * D_MODEL)), full((SUBLANES, D_MODEL)),
            full((D_MODEL, D_MODEL)), full((D_MODEL, D_MODEL)), full((D_MODEL, D_MODEL)),
            full((1, D_MODEL)), full((1, D_MODEL)),
            full((D_MODEL, ROUTER_COLS)), full((1, ROUTER_COLS)),
        ],
        out_specs=(pl.BlockSpec((tm, D_MODEL), lambda i: (i, 0)),
                   pl.BlockSpec((tm, D_MODEL), lambda i: (i, 0)),
                   pl.BlockSpec((tm, ROUTER_COLS), lambda i: (i, 0))),
        compiler_params=pltpu.CompilerParams(
            dimension_semantics=("parallel",), vmem_limit_bytes=VMEM_LIMIT),
        name="mix_ln1_router",
    )(a2d, h2d, h2d, h2d, h2d, h2d, h2d, h2d, h2d, h2d, x2d, bg, cw, wa, wc, wo, g1, b1, wr, br)


def _moe_kernel(be_ref, nu_ref, xs_ref, sw_ref, wg_ref, bg_ref, wu_ref, bu_ref, wd_ref, bd_ref, o_ref):
    i = pl.program_id(0)

    @pl.when(i < nu_ref[0])
    def _():
        x = xs_ref[...]
        hg = jnp.minimum(jnp.dot(x, wg_ref[...], preferred_element_type=F32) + bg_ref[...], SWIGLU_LIMIT)
        hu = jnp.clip(jnp.dot(x, wu_ref[...], preferred_element_type=F32) + bu_ref[...],
                      -SWIGLU_LIMIT, SWIGLU_LIMIT)
        act = hg * jax.nn.sigmoid(SWIGLU_ALPHA * hg) * (hu + 1.0)
        y = jnp.dot(act.astype(BF16), wd_ref[...], preferred_element_type=F32) + bd_ref[...]
        o_ref[...] = (y * sw_ref[...]).astype(o_ref.dtype)

    @pl.when(i >= nu_ref[0])
    def _():
        o_ref[...] = jnp.zeros(o_ref.shape, o_ref.dtype)


def _moe_experts(block_e, n_used, xs, slot_w, wg, bg, wu, bu, wd, bd):
    p = xs.shape[0]
    tm = MOE_TM
    nb = p // tm

    def row_map(i, be, nu):
        return (jnp.minimum(i, nu[0] - 1), 0)

    def w_map(i, be, nu):
        return (be[jnp.minimum(i, nu[0] - 1)], 0, 0)

    w_spec = pl.BlockSpec((None, D_MODEL, D_MODEL), w_map)
    b_spec = pl.BlockSpec((None, 1, D_MODEL), w_map)
    return pl.pallas_call(
        _moe_kernel,
        out_shape=jax.ShapeDtypeStruct((p, D_MODEL), BF16),
        grid_spec=pltpu.PrefetchScalarGridSpec(
            num_scalar_prefetch=2,
            grid=(nb,),
            in_specs=[
                pl.BlockSpec((tm, D_MODEL), row_map),
                pl.BlockSpec((tm, 1), row_map),
                w_spec, b_spec, w_spec, b_spec, w_spec, b_spec,
            ],
            out_specs=pl.BlockSpec((tm, D_MODEL), lambda i, be, nu: (i, 0)),
        ),
        compiler_params=pltpu.CompilerParams(
            dimension_semantics=("arbitrary",), vmem_limit_bytes=VMEM_LIMIT),
        name="moe_experts",
    )(block_e, n_used, xs, slot_w, wg, bg, wu, bu, wd, bd)


def _ln2_kernel(x1_ref, yg_ref, g_ref, b_ref, o_ref):
    f = (yg_ref[0].astype(F32) + yg_ref[1].astype(F32)) + (yg_ref[2].astype(F32) + yg_ref[3].astype(F32))
    o_ref[...] = _layer_norm(DEEPNORM_ALPHA * x1_ref[...] + f, g_ref[...], b_ref[...])


def _combine_ln2(x1, yg, g2, b2):
    n = x1.shape[0]
    tm = LN2_TM
    return pl.pallas_call(
        _ln2_kernel,
        out_shape=jax.ShapeDtypeStruct((n, D_MODEL), F32),
        grid=(n // tm,),
        in_specs=[
            pl.BlockSpec((tm, D_MODEL), lambda i: (i, 0)),
            pl.BlockSpec((TOP_K, tm, D_MODEL), lambda i: (0, i, 0)),
            pl.BlockSpec((1, D_MODEL), lambda i: (0, 0)),
            pl.BlockSpec((1, D_MODEL), lambda i: (0, 0)),
        ],
        out_specs=pl.BlockSpec((tm, D_MODEL), lambda i: (i, 0)),
        compiler_params=pltpu.CompilerParams(
            dimension_semantics=("parallel",), vmem_limit_bytes=VMEM_LIMIT),
        name="combine_ln2",
    )(x1, yg, g2, b2)


def _rotary_tables(seq):
    inv = ROPE_THETA ** (-jnp.arange(0, ROT_DIM, 2, dtype=F32) / ROT_DIM)
    ang = jnp.arange(seq, dtype=F32)[:, None] * inv[None, :]
    cos, sin = jnp.cos(ang), jnp.sin(ang)
    ones = jnp.ones((seq, QK_DIM - ROT_DIM), F32)
    zeros = jnp.zeros((seq, QK_DIM - ROT_DIM), F32)
    zh = jnp.zeros((seq, ROT_HALF), F32)
    cos64 = jnp.concatenate([cos, cos, ones], axis=1)
    sa64 = jnp.concatenate([-sin, zh, zeros], axis=1)
    sb64 = jnp.concatenate([zh, sin, zeros], axis=1)
    rep = LANES // QK_DIM
    return jnp.tile(cos64, (1, rep)), jnp.tile(sa64, (1, rep)), jnp.tile(sb64, (1, rep))


def _route(logits, tm):
    n = logits.shape[0]
    nk = n * TOP_K
    p = nk + N_EXPERTS * tm
    nb = p // tm
    top_vals, top_idx = lax.top_k(logits, TOP_K)
    gates = jax.nn.softmax(top_vals, axis=-1)
    flat_e = top_idx.reshape(-1).astype(jnp.int32)
    order = jnp.argsort(flat_e)
    sorted_e = flat_e[order]
    counts = jnp.bincount(flat_e, length=N_EXPERTS)
    starts = jnp.cumsum(counts) - counts
    padded = ((counts + tm - 1) // tm) * tm
    pad_ends = jnp.cumsum(padded)
    pad_starts = pad_ends - padded
    dest_sorted = (pad_starts[sorted_e] + jnp.arange(nk, dtype=jnp.int32) - starts[sorted_e]).astype(jnp.int32)
    slot_tok = jnp.zeros((p,), jnp.int32).at[dest_sorted].set((order // TOP_K).astype(jnp.int32))
    slot_w = jnp.zeros((p,), F32).at[dest_sorted].set(gates.reshape(-1)[order])
    dest = jnp.zeros((nk,), jnp.int32).at[order].set(dest_sorted)
    block_e = jnp.minimum(jnp.searchsorted(pad_ends, jnp.arange(nb) * tm, side='right'),
                          N_EXPERTS - 1).astype(jnp.int32)
    n_used = (pad_ends[-1] // tm).astype(jnp.int32).reshape(1)
    return slot_tok, slot_w, dest.reshape(n, TOP_K), block_e, n_used


def _trunk(x, prm):
    b, s, d = x.shape
    n = b * s
    x2d = x.reshape(n, d)
    cos_t, sa_t, sb_t = _rotary_tables(s)
    h = _inproj(x2d, prm["w_in"], cos_t, sa_t, sb_t, s)
    a = _attention(h.reshape(b, s, IN_COLS), prm["lam"], prm["subln_g"], prm["lam_init"])
    x1, x1b, logits = _mix(a.reshape(n, d), h, x2d, s, prm["b_gate"], prm["conv_w"], prm["w_attn_br"],
                           prm["w_conv_br"], prm["w_o"], prm["ln1_g"], prm["ln1_b"],
                           prm["w_router"], prm["b_router"])
    slot_tok, slot_w, dest, block_e, n_used = _route(logits[:, :N_EXPERTS], MOE_TM)
    xs = x1b[slot_tok]
    ys = _moe_experts(block_e, n_used, xs, slot_w[:, None], prm["w_exp_gate"], prm["b_exp_gate"],
                      prm["w_exp_up"], prm["b_exp_up"], prm["w_exp_down"], prm["b_exp_down"])
    yg = ys[dest.T]
    y = _combine_ln2(x1, yg, prm["ln2_g"], prm["ln2_b"])
    return y.reshape(b, s, d)


def kernel(x_prompt, x_sample, w_in, b_branch_gate, lambda_q1, lambda_k1, lambda_q2, lambda_k2, subln_g,
           conv_w, w_attn_br, w_conv_br, w_o, ln1_g, ln1_b, w_router, b_router, w_exp_gate, b_exp_gate,
           w_exp_up, b_exp_up, w_exp_down, b_exp_down, ln2_g, ln2_b):
    l = 0
    lam_init = 0.8 - 0.6 * math.exp(-0.3 * l)
    lam = (jnp.exp(jnp.sum(lambda_q1[l].astype(F32) * lambda_k1[l].astype(F32)))
           - jnp.exp(jnp.sum(lambda_q2[l].astype(F32) * lambda_k2[l].astype(F32))) + lam_init)
    prm = {
        "lam_init": lam_init,
        "lam": lam.reshape(1).astype(F32),
        "w_in": w_in[l].astype(BF16),
        "b_gate": b_branch_gate[l].reshape(1, 2 * D_MODEL),
        "subln_g": subln_g[l].reshape(1, V_DIM),
        "conv_w": jnp.pad(conv_w[l], ((0, SUBLANES - conv_w.shape[1]), (0, 0))),
        "w_attn_br": w_attn_br[l].astype(BF16),
        "w_conv_br": w_conv_br[l].astype(BF16),
        "w_o": w_o[l].astype(BF16),
        "ln1_g": ln1_g[l].reshape(1, D_MODEL),
        "ln1_b": ln1_b[l].reshape(1, D_MODEL),
        "w_router": jnp.pad(w_router[l], ((0, 0), (0, ROUTER_COLS - N_EXPERTS))).astype(BF16),
        "b_router": jnp.pad(b_router[l], (0, ROUTER_COLS - N_EXPERTS)).reshape(1, ROUTER_COLS),
        "w_exp_gate": w_exp_gate[l].astype(BF16),
        "b_exp_gate": b_exp_gate[l].reshape(N_EXPERTS, 1, D_MODEL),
        "w_exp_up": w_exp_up[l].astype(BF16),
        "b_exp_up": b_exp_up[l].reshape(N_EXPERTS, 1, D_MODEL),
        "w_exp_down": w_exp_down[l].astype(BF16),
        "b_exp_down": b_exp_down[l].reshape(N_EXPERTS, 1, D_MODEL),
        "ln2_g": ln2_g[l].reshape(1, D_MODEL),
        "ln2_b": ln2_b[l].reshape(1, D_MODEL),
    }
    return _trunk(x_prompt, prm), _trunk(x_sample, prm)
```

```python
import functools
import math

import jax
import jax.numpy as jnp
from jax import lax
from jax.experimental import pallas as pl
from jax.experimental.pallas import tpu as pltpu

F32 = jnp.float32
BF16 = jnp.bfloat16

D_MODEL = 1024
N_HEADS = 8
QK_DIM = 64
V_DIM = 128
HEAD_COLS = 2 * QK_DIM
ROT_DIM = QK_DIM // 4
ROT_HALF = ROT_DIM // 2
ROPE_THETA = 500000.0
SUBLN_EPS = 1e-5
IN_COLS = 8192
COL_BLOCK = 1024
CB_BLK, CC_BLK, CX_BLK, GA_BLK, GC_BLK = 3, 4, 5, 6, 7
K_HEAD0, V_HEAD0 = 8, 16
N_EXPERTS = 32
TOP_K = 4
SWIGLU_LIMIT = 7.0
SWIGLU_ALPHA = 1.702
LN_EPS = 1e-5
DEPTH = 1
DEEPNORM_ALPHA = (2 * DEPTH) ** 0.25
LANES = 128
SUBLANES = 8
ROUTER_COLS = LANES
ONES_ROWS = 16
Q_SCALE = math.log2(math.e) * QK_DIM ** -0.5

VMEM_LIMIT = 56 * 1024 * 1024

PROJ_TM = 1024
ATTN_TQ = 512
ATTN_TK = 512
MIX_TM = 256
MOE_TM = 512
LN2_TM = 512


def _inproj_kernel(x_ref, w_ref, cos_ref, sa_ref, sb_ref, o_ref):
    j = pl.program_id(1)
    acc = jnp.dot(x_ref[...].astype(BF16), w_ref[...], preferred_element_type=F32)

    @pl.when(j >= 2)
    def _():
        o_ref[...] = acc.astype(o_ref.dtype)

    @pl.when(j < 2)
    def _():
        scale = jnp.where(j == 0, Q_SCALE, 1.0).astype(F32)
        cos = cos_ref[...] * scale
        sa = sa_ref[...] * scale
        sb = sb_ref[...] * scale
        for c in range(COL_BLOCK // LANES):
            a = acc[:, c * LANES:(c + 1) * LANES]
            r = (a * cos + pltpu.roll(a, LANES - ROT_HALF, 1) * sa + pltpu.roll(a, ROT_HALF, 1) * sb)
            o_ref[:, c * LANES:(c + 1) * LANES] = r.astype(o_ref.dtype)


def _inproj(x2d, w_in_b, cos_t, sa_t, sb_t, seq):
    n = x2d.shape[0]
    tm = PROJ_TM
    pos_blocks = seq // tm
    tab_spec = pl.BlockSpec((tm, LANES), lambda i, j: (i % pos_blocks, 0))
    return pl.pallas_call(
        _inproj_kernel,
        out_shape=jax.ShapeDtypeStruct((n, IN_COLS), BF16),
        grid=(n // tm, IN_COLS // COL_BLOCK),
        in_specs=[
            pl.BlockSpec((tm, D_MODEL), lambda i, j: (i, 0)),
            pl.BlockSpec((D_MODEL, COL_BLOCK), lambda i, j: (0, j)),
            tab_spec, tab_spec, tab_spec,
        ],
        out_specs=pl.BlockSpec((tm, COL_BLOCK), lambda i, j: (i, j)),
        compiler_params=pltpu.CompilerParams(
            dimension_semantics=("parallel", "arbitrary"), vmem_limit_bytes=VMEM_LIMIT),
        name="inproj",
    )(x2d, w_in_b, cos_t, sa_t, sb_t)


def _attn_kernel(lam_ref, q_ref, k_ref, vt_ref, g_ref, o_ref, qt_s, acc_s, sa_s, sb_s, *, out_scale):
    tq = q_ref.shape[0]
    n_chunks, _, tk = vt_ref.shape
    qf = q_ref[...].astype(F32)
    lane = lax.broadcasted_iota(jnp.int32, qf.shape, 1)
    qt_s[0] = jnp.where(lane < QK_DIM, qf, 0.0).T.astype(BF16)
    qt_s[1] = jnp.where(lane >= QK_DIM, qf, 0.0).T.astype(BF16)
    acc_s[...] = jnp.zeros(acc_s.shape, F32)

    def scores(c, s_ref):
        k = k_ref[pl.ds(pl.multiple_of(c * tk, tk), tk), :]
        for mp in range(2):
            s_ref[mp] = jnp.dot(k, qt_s[mp], preferred_element_type=F32)

    def update(c, s_ref, ms):
        vt = vt_ref[c]
        new = []
        for mp in range(2):
            m_new = jnp.maximum(ms[mp], jnp.max(s_ref[mp], axis=0, keepdims=True))
            alpha = jnp.exp2(ms[mp] - m_new)
            p = jnp.exp2(s_ref[mp] - m_new).astype(BF16)
            acc_s[mp] = alpha * acc_s[mp] + jnp.dot(vt, p, preferred_element_type=F32)
            new.append(m_new)
        return tuple(new)

    def pair(i, ms):
        c = 2 * i
        scores(c + 1, sb_s)
        ms = update(c, sa_s, ms)
        scores(c + 2, sa_s)
        return update(c + 1, sb_s, ms)

    m0 = jnp.full((1, tq), -jnp.inf, F32)
    scores(0, sa_s)
    ms = lax.fori_loop(0, n_chunks // 2 - 1, pair, (m0, m0))
    scores(n_chunks - 1, sb_s)
    ms = update(n_chunks - 2, sa_s, ms)
    update(n_chunks - 1, sb_s, ms)

    lam = lam_ref[0]
    o1 = acc_s[0, :V_DIM, :] / acc_s[0, V_DIM:V_DIM + 1, :]
    o2 = acc_s[1, :V_DIM, :] / acc_s[1, V_DIM:V_DIM + 1, :]
    ot = o1 - lam * o2
    ot = ot * lax.rsqrt(jnp.mean(ot * ot, axis=0, keepdims=True) + SUBLN_EPS)
    o_ref[...] = (ot.T * (g_ref[...] * out_scale)).astype(o_ref.dtype)


def _attention(h3, lam, subln_g, lam_init):
    b, s, _ = h3.shape
    tq, tk = ATTN_TQ, ATTN_TK
    n_chunks = s // tk
    assert n_chunks % 2 == 0 and n_chunks >= 2
    v = h3[:, :, V_HEAD0 * LANES:(V_HEAD0 + N_HEADS) * LANES]
    vt = v.reshape(b, n_chunks, tk, N_HEADS, V_DIM).transpose(0, 3, 1, 4, 2)
    vt = jnp.concatenate([vt, jnp.ones((b, N_HEADS, n_chunks, ONES_ROWS, tk), BF16)], axis=3)
    vt_rows = V_DIM + ONES_ROWS
    kern = functools.partial(_attn_kernel, out_scale=1.0 - lam_init)
    return pl.pallas_call(
        kern,
        out_shape=jax.ShapeDtypeStruct((b, s, N_HEADS * V_DIM), BF16),
        grid=(b, N_HEADS, s // tq),
        in_specs=[
            pl.BlockSpec(memory_space=pltpu.SMEM),
            pl.BlockSpec((None, tq, HEAD_COLS), lambda bi, hi, qi: (bi, qi, hi)),
            pl.BlockSpec((None, s, HEAD_COLS), lambda bi, hi, qi: (bi, 0, K_HEAD0 + hi)),
            pl.BlockSpec((None, None, n_chunks, vt_rows, tk), lambda bi, hi, qi: (bi, hi, 0, 0, 0)),
            pl.BlockSpec((1, V_DIM), lambda bi, hi, qi: (0, 0)),
        ],
        out_specs=pl.BlockSpec((None, tq, V_DIM), lambda bi, hi, qi: (bi, qi, hi)),
        scratch_shapes=[
            pltpu.VMEM((2, HEAD_COLS, tq), BF16),
            pltpu.VMEM((2, vt_rows, tq), F32),
            pltpu.VMEM((2, tk, tq), F32),
            pltpu.VMEM((2, tk, tq), F32),
        ],
        compiler_params=pltpu.CompilerParams(
            dimension_semantics=("parallel", "parallel", "arbitrary"),
            vmem_limit_bytes=VMEM_LIMIT),
        name="diff_attention",
    )(lam, h3, h3, vt, subln_g)


def _layer_norm(y, g, b):
    mu = jnp.mean(y, axis=-1, keepdims=True)
    d = y - mu
    var = jnp.mean(d * d, axis=-1, keepdims=True)
    return d * lax.rsqrt(var + LN_EPS) * g + b


def _mix_kernel(a_ref, cb_ref, cc_ref, cx_ref, ga_ref, gc_ref, ccp_ref, cxp_ref, ccn_ref, cxn_ref,
                x_ref, bg_ref, cw_ref, wa_ref, wc_ref, wo_ref, g1_ref, b1_ref, wr_ref, br_ref,
                x1_ref, x1b_ref, lg_ref, *, tiles_per_seq):
    i = pl.program_id(0)
    tm = x_ref.shape[0]
    u = cc_ref[...].astype(F32) * cx_ref[...].astype(F32)
    first = (i % tiles_per_seq) == 0
    last = (i % tiles_per_seq) == tiles_per_seq - 1
    up = ccp_ref[SUBLANES - 1:SUBLANES, :].astype(F32) * cxp_ref[SUBLANES - 1:SUBLANES, :].astype(F32)
    un = ccn_ref[0:1, :].astype(F32) * cxn_ref[0:1, :].astype(F32)
    up = jnp.where(first, 0.0, up)
    un = jnp.where(last, 0.0, un)
    row = lax.broadcasted_iota(jnp.int32, u.shape, 0)
    u_prev = jnp.where(row == 0, up, pltpu.roll(u, 1, 0))
    u_next = jnp.where(row == tm - 1, un, pltpu.roll(u, tm - 1, 0))
    conv = cw_ref[0:1, :] * u_prev + cw_ref[1:2, :] * u + cw_ref[2:3, :] * u_next
    c = (cb_ref[...].astype(F32) * conv).astype(BF16)
    g_a = jax.nn.sigmoid(ga_ref[...].astype(F32) + bg_ref[:, :D_MODEL])
    g_c = jax.nn.sigmoid(gc_ref[...].astype(F32) + bg_ref[:, D_MODEL:])
    merged = (g_a * jnp.dot(a_ref[...], wa_ref[...], preferred_element_type=F32)
              + g_c * jnp.dot(c, wc_ref[...], preferred_element_type=F32))
    m = jnp.dot(merged.astype(BF16), wo_ref[...], preferred_element_type=F32)
    x1 = _layer_norm(DEEPNORM_ALPHA * x_ref[...] + m, g1_ref[...], b1_ref[...])
    x1_ref[...] = x1
    x1b = x1.astype(BF16)
    x1b_ref[...] = x1b
    lg_ref[...] = jnp.dot(x1b, wr_ref[...], preferred_element_type=F32) + br_ref[...]


def _mix(a2d, h2d, x2d, seq, bg, cw, wa, wc, wo, g1, b1, wr, br):
    n = x2d.shape[0]
    tm = MIX_TM
    tiles_per_seq = seq // tm
    halo_blocks = tm // SUBLANES
    n_halo = n // SUBLANES

    def col(blk):
        return pl.BlockSpec((tm, COL_BLOCK), lambda i: (i, blk))

    def prev(blk):
        return pl.BlockSpec((SUBLANES, COL_BLOCK), lambda i: (jnp.maximum(i * halo_blocks - 1, 0), blk))

    def nxt(blk):
        return pl.BlockSpec((SUBLANES, COL_BLOCK),
                            lambda i: (jnp.minimum((i + 1) * halo_blocks, n_halo - 1), blk))

    def full(shape):
        return pl.BlockSpec(shape, lambda i: (0,) * len(shape))

    kern = functools.partial(_mix_kernel, tiles_per_seq=tiles_per_seq)
    return pl.pallas_call(
        kern,
        out_shape=(jax.ShapeDtypeStruct((n, D_MODEL), F32),
                   jax.ShapeDtypeStruct((n, D_MODEL), BF16),
                   jax.ShapeDtypeStruct((n, ROUTER_COLS), F32)),
        grid=(n // tm,),
        in_specs=[
            pl.BlockSpec((tm, D_MODEL), lambda i: (i, 0)),
            col(CB_BLK), col(CC_BLK), col(CX_BLK), col(GA_BLK), col(GC_BLK),
            prev(CC_BLK), prev(CX_BLK), nxt(CC_BLK), nxt(CX_BLK),
            pl.BlockSpec((tm, D_MODEL), lambda i: (i, 0)),
            full((1, 2 * D_MODEL)), full((SUBLANES, D_MODEL)),
            full((D_MODEL, D_MODEL)), full((D_MODEL, D_MODEL)), full((D_MODEL, D_MODEL)),
            full((1, D_MODEL)), full((1, D_MODEL)),
            full((D_MODEL, ROUTER_COLS)), full((1, ROUTER_COLS)),
        ],
        out_specs=(pl.BlockSpec((tm, D_MODEL), lambda i: (i, 0)),
                   pl.BlockSpec((tm, D_MODEL), lambda i: (i, 0)),
                   pl.BlockSpec((tm, ROUTER_COLS), lambda i: (i, 0))),
        compiler_params=pltpu.CompilerParams(
            dimension_semantics=("parallel",), vmem_limit_bytes=VMEM_LIMIT),
        name="mix_ln1_router",
    )(a2d, h2d, h2d, h2d, h2d, h2d, h2d, h2d, h2d, h2d, x2d, bg, cw, wa, wc, wo, g1, b1, wr, br)


def _moe_kernel(be_ref, nu_ref, xs_ref, sw_ref, wg_ref, bg_ref, wu_ref, bu_ref, wd_ref, bd_ref, o_ref):
    i = pl.program_id(0)

    @pl.when(i < nu_ref[0])
    def _():
        x = xs_ref[...]
        hg = jnp.minimum(jnp.dot(x, wg_ref[...], preferred_element_type=F32) + bg_ref[...], SWIGLU_LIMIT)
        hu = jnp.clip(jnp.dot(x, wu_ref[...], preferred_element_type=F32) + bu_ref[...],
                      -SWIGLU_LIMIT, SWIGLU_LIMIT)
        act = hg * jax.nn.sigmoid(SWIGLU_ALPHA * hg) * (hu + 1.0)
        y = jnp.dot(act.astype(BF16), wd_ref[...], preferred_element_type=F32) + bd_ref[...]
        o_ref[...] = (y * sw_ref[...]).astype(o_ref.dtype)

    @pl.when(i >= nu_ref[0])
    def _():
        o_ref[...] = jnp.zeros(o_ref.shape, o_ref.dtype)


def _moe_experts(block_e, n_used, xs, slot_w, wg, bg, wu, bu, wd, bd):
    p = xs.shape[0]
    tm = MOE_TM
    nb = p // tm

    def row_map(i, be, nu):
        return (jnp.minimum(i, nu[0] - 1), 0)

    def w_map(i, be, nu):
        return (be[jnp.minimum(i, nu[0] - 1)], 0, 0)

    w_spec = pl.BlockSpec((None, D_MODEL, D_MODEL), w_map)
    b_spec = pl.BlockSpec((None, 1, D_MODEL), w_map)
    return pl.pallas_call(
        _moe_kernel,
        out_shape=jax.ShapeDtypeStruct((p, D_MODEL), BF16),
        grid_spec=pltpu.PrefetchScalarGridSpec(
            num_scalar_prefetch=2,
            grid=(nb,),
            in_specs=[
                pl.BlockSpec((tm, D_MODEL), row_map),
                pl.BlockSpec((tm, 1), row_map),
                w_spec, b_spec, w_spec, b_spec, w_spec, b_spec,
            ],
            out_specs=pl.BlockSpec((tm, D_MODEL), lambda i, be, nu: (i, 0)),
        ),
        compiler_params=pltpu.CompilerParams(
            dimension_semantics=("arbitrary",), vmem_limit_bytes=VMEM_LIMIT),
        name="moe_experts",
    )(block_e, n_used, xs, slot_w, wg, bg, wu, bu, wd, bd)


def _ln2_kernel(x1_ref, yg_ref, g_ref, b_ref, o_ref):
    f = (yg_ref[0].astype(F32) + yg_ref[1].astype(F32)) + (yg_ref[2].astype(F32) + yg_ref[3].astype(F32))
    o_ref[...] = _layer_norm(DEEPNORM_ALPHA * x1_ref[...] + f, g_ref[...], b_ref[...])


def _combine_ln2(x1, yg, g2, b2):
    n = x1.shape[0]
    tm = LN2_TM
    return pl.pallas_call(
        _ln2_kernel,
        out_shape=jax.ShapeDtypeStruct((n, D_MODEL), F32),
        grid=(n // tm,),
        in_specs=[
            pl.BlockSpec((tm, D_MODEL), lambda i: (i, 0)),
            pl.BlockSpec((TOP_K, tm, D_MODEL), lambda i: (0, i, 0)),
            pl.BlockSpec((1, D_MODEL), lambda i: (0, 0)),
            pl.BlockSpec((1, D_MODEL), lambda i: (0, 0)),
        ],
        out_specs=pl.BlockSpec((tm, D_MODEL), lambda i: (i, 0)),
        compiler_params=pltpu.CompilerParams(
            dimension_semantics=("parallel",), vmem_limit_bytes=VMEM_LIMIT),
        name="combine_ln2",
    )(x1, yg, g2, b2)


def _rotary_tables(seq):
    inv = ROPE_THETA ** (-jnp.arange(0, ROT_DIM, 2, dtype=F32) / ROT_DIM)
    ang = jnp.arange(seq, dtype=F32)[:, None] * inv[None, :]
    cos, sin = jnp.cos(ang), jnp.sin(ang)
    ones = jnp.ones((seq, QK_DIM - ROT_DIM), F32)
    zeros = jnp.zeros((seq, QK_DIM - ROT_DIM), F32)
    zh = jnp.zeros((seq, ROT_HALF), F32)
    cos64 = jnp.concatenate([cos, cos, ones], axis=1)
    sa64 = jnp.concatenate([-sin, zh, zeros], axis=1)
    sb64 = jnp.concatenate([zh, sin, zeros], axis=1)
    rep = LANES // QK_DIM
    return jnp.tile(cos64, (1, rep)), jnp.tile(sa64, (1, rep)), jnp.tile(sb64, (1, rep))


def _route(logits, tm):
    n = logits.shape[0]
    nk = n * TOP_K
    p = nk + N_EXPERTS * tm
    nb = p // tm
    top_vals, top_idx = lax.top_k(logits, TOP_K)
    gates = jax.nn.softmax(top_vals, axis=-1)
    flat_e = top_idx.reshape(-1).astype(jnp.int32)
    order = jnp.argsort(flat_e)
    sorted_e = flat_e[order]
    counts = jnp.bincount(flat_e, length=N_EXPERTS)
    starts = jnp.cumsum(counts) - counts
    padded = ((counts + tm - 1) // tm) * tm
    pad_ends = jnp.cumsum(padded)
    pad_starts = pad_ends - padded
    dest_sorted = (pad_starts[sorted_e] + jnp.arange(nk, dtype=jnp.int32) - starts[sorted_e]).astype(jnp.int32)
    slot_tok = jnp.zeros((p,), jnp.int32).at[dest_sorted].set((order // TOP_K).astype(jnp.int32))
    slot_w = jnp.zeros((p,), F32).at[dest_sorted].set(gates.reshape(-1)[order])
    dest = jnp.zeros((nk,), jnp.int32).at[order].set(dest_sorted)
    block_e = jnp.minimum(jnp.searchsorted(pad_ends, jnp.arange(nb) * tm, side='right'),
                          N_EXPERTS - 1).astype(jnp.int32)
    n_used = (pad_ends[-1] // tm).astype(jnp.int32).reshape(1)
    return slot_tok, slot_w, dest.reshape(n, TOP_K), block_e, n_used


def _trunk(x, prm):
    b, s, d = x.shape
    n = b * s
    x2d = x.reshape(n, d)
    cos_t, sa_t, sb_t = _rotary_tables(s)
    h = _inproj(x2d, prm["w_in"], cos_t, sa_t, sb_t, s)
    a = _attention(h.reshape(b, s, IN_COLS), prm["lam"], prm["subln_g"], prm["lam_init"])
    x1, x1b, logits = _mix(a.reshape(n, d), h, x2d, s, prm["b_gate"], prm["conv_w"], prm["w_attn_br"],
                           prm["w_conv_br"], prm["w_o"], prm["ln1_g"], prm["ln1_b"],
                           prm["w_router"], prm["b_router"])
    slot_tok, slot_w, dest, block_e, n_used = _route(logits[:, :N_EXPERTS], MOE_TM)
    xs = x1b[slot_tok]
    ys = _moe_experts(block_e, n_used, xs, slot_w[:, None], prm["w_exp_gate"], prm["b_exp_gate"],
                      prm["w_exp_up"], prm["b_exp_up"], prm["w_exp_down"], prm["b_exp_down"])
    yg = ys[dest.T]
    y = _combine_ln2(x1, yg, prm["ln2_g"], prm["ln2_b"])
    return y.reshape(b, s, d)


def kernel(x_prompt, x_sample, w_in, b_branch_gate, lambda_q1, lambda_k1, lambda_q2, lambda_k2, subln_g,
           conv_w, w_attn_br, w_conv_br, w_o, ln1_g, ln1_b, w_router, b_router, w_exp_gate, b_exp_gate,
           w_exp_up, b_exp_up, w_exp_down, b_exp_down, ln2_g, ln2_b):
    l = 0
    lam_init = 0.8 - 0.6 * math.exp(-0.3 * l)
    lam = (jnp.exp(jnp.sum(lambda_q1[l].astype(F32) * lambda_k1[l].astype(F32)))
           - jnp.exp(jnp.sum(lambda_q2[l].astype(F32) * lambda_k2[l].astype(F32))) + lam_init)
    prm = {
        "lam_init": lam_init,
        "lam": lam.reshape(1).astype(F32),
        "w_in": w_in[l].astype(BF16),
        "b_gate": b_branch_gate[l].reshape(1, 2 * D_MODEL),
        "subln_g": subln_g[l].reshape(1, V_DIM),
        "conv_w": jnp.pad(conv_w[l], ((0, SUBLANES - conv_w.shape[1]), (0, 0))),
        "w_attn_br": w_attn_br[l].astype(BF16),
        "w_conv_br": w_conv_br[l].astype(BF16),
        "w_o": w_o[l].astype(BF16),
        "ln1_g": ln1_g[l].reshape(1, D_MODEL),
        "ln1_b": ln1_b[l].reshape(1, D_MODEL),
        "w_router": jnp.pad(w_router[l], ((0, 0), (0, ROUTER_COLS - N_EXPERTS))).astype(BF16),
        "b_router": jnp.pad(b_router[l], (0, ROUTER_COLS - N_EXPERTS)).reshape(1, ROUTER_COLS),
        "w_exp_gate": w_exp_gate[l].astype(BF16),
        "b_exp_gate": b_exp_gate[l].reshape(N_EXPERTS, 1, D_MODEL),
        "w_exp_up": w_exp_up[l].astype(BF16),
        "b_exp_up": b_exp_up[l].reshape(N_EXPERTS, 1, D_MODEL),
        "w_exp_down": w_exp_down[l].astype(BF16),
        "b_exp_down": b_exp_down[l].reshape(N_EXPERTS, 1, D_MODEL),
        "ln2_g": ln2_g[l].reshape(1, D_MODEL),
        "ln2_b": ln2_b[l].reshape(1, D_MODEL),
    }
    return _trunk(x_prompt, prm), _trunk(x_sample, prm)
```

```python
import functools
import math

import jax
import jax.numpy as jnp
from jax import lax
from jax.experimental import pallas as pl
from jax.experimental.pallas import tpu as pltpu
from jax.experimental.pallas import tpu_sc as plsc

F32 = jnp.float32
BF16 = jnp.bfloat16

D_MODEL = 1024
N_HEADS = 8
QK_DIM = 64
V_DIM = 128
HEAD_COLS = 2 * QK_DIM
ROT_DIM = QK_DIM // 4
ROT_HALF = ROT_DIM // 2
ROPE_THETA = 500000.0
SUBLN_EPS = 1e-5
IN_COLS = 8192
COL_BLOCK = 1024
CB_BLK, CC_BLK, CX_BLK, GA_BLK, GC_BLK = 3, 4, 5, 6, 7
K_HEAD0, V_HEAD0 = 8, 16
N_EXPERTS = 32
TOP_K = 4
SWIGLU_LIMIT = 7.0
SWIGLU_ALPHA = 1.702
LN_EPS = 1e-5
DEPTH = 1
DEEPNORM_ALPHA = (2 * DEPTH) ** 0.25
LANES = 128
SUBLANES = 8
ROUTER_COLS = LANES
RT_EXPERT, RT_RANK, RT_GATE = 0, 4, 8
SC_WINDOW = 32
ONES_ROWS = 16
Q_SCALE = math.log2(math.e) * QK_DIM ** -0.5

VMEM_LIMIT = 56 * 1024 * 1024

PROJ_TM = 1024
ATTN_TQ = 512
ATTN_TK = 512
MIX_TM = 256
MOE_TM = 512
LN2_TM = 512


def _inproj_kernel(x_ref, w_ref, cos_ref, sa_ref, sb_ref, o_ref):
    j = pl.program_id(1)
    acc = jnp.dot(x_ref[...].astype(BF16), w_ref[...], preferred_element_type=F32)

    @pl.when(j >= 2)
    def _():
        o_ref[...] = acc.astype(o_ref.dtype)

    @pl.when(j < 2)
    def _():
        scale = jnp.where(j == 0, Q_SCALE, 1.0).astype(F32)
        cos = cos_ref[...] * scale
        sa = sa_ref[...] * scale
        sb = sb_ref[...] * scale
        for c in range(COL_BLOCK // LANES):
            a = acc[:, c * LANES:(c + 1) * LANES]
            r = (a * cos + pltpu.roll(a, LANES - ROT_HALF, 1) * sa + pltpu.roll(a, ROT_HALF, 1) * sb)
            o_ref[:, c * LANES:(c + 1) * LANES] = r.astype(o_ref.dtype)


def _inproj(x2d, w_in_b, cos_t, sa_t, sb_t, seq):
    n = x2d.shape[0]
    tm = PROJ_TM
    pos_blocks = seq // tm
    tab_spec = pl.BlockSpec((tm, LANES), lambda i, j: (i % pos_blocks, 0))
    return pl.pallas_call(
        _inproj_kernel,
        out_shape=jax.ShapeDtypeStruct((n, IN_COLS), BF16),
        grid=(n // tm, IN_COLS // COL_BLOCK),
        in_specs=[
            pl.BlockSpec((tm, D_MODEL), lambda i, j: (i, 0)),
            pl.BlockSpec((D_MODEL, COL_BLOCK), lambda i, j: (0, j)),
            tab_spec, tab_spec, tab_spec,
        ],
        out_specs=pl.BlockSpec((tm, COL_BLOCK), lambda i, j: (i, j)),
        compiler_params=pltpu.CompilerParams(
            dimension_semantics=("parallel", "arbitrary"), vmem_limit_bytes=VMEM_LIMIT),
        name="inproj",
    )(x2d, w_in_b, cos_t, sa_t, sb_t)


def _attn_kernel(lam_ref, q_ref, k_ref, vt_ref, g_ref, o_ref, qt_s, acc_s, sa_s, sb_s, *, out_scale):
    tq = q_ref.shape[0]
    n_chunks, _, tk = vt_ref.shape
    qf = q_ref[...].astype(F32)
    lane = lax.broadcasted_iota(jnp.int32, qf.shape, 1)
    qt_s[0] = jnp.where(lane < QK_DIM, qf, 0.0).T.astype(BF16)
    qt_s[1] = jnp.where(lane >= QK_DIM, qf, 0.0).T.astype(BF16)
    acc_s[...] = jnp.zeros(acc_s.shape, F32)

    def scores(c, s_ref):
        k = k_ref[pl.ds(pl.multiple_of(c * tk, tk), tk), :]
        for mp in range(2):
            s_ref[mp] = jnp.dot(k, qt_s[mp], preferred_element_type=F32)

    def update(c, s_ref, ms):
        vt = vt_ref[c]
        new = []
        for mp in range(2):
            m_new = jnp.maximum(ms[mp], jnp.max(s_ref[mp], axis=0, keepdims=True))
            alpha = jnp.exp2(ms[mp] - m_new)
            p = jnp.exp2(s_ref[mp] - m_new).astype(BF16)
            acc_s[mp] = alpha * acc_s[mp] + jnp.dot(vt, p, preferred_element_type=F32)
            new.append(m_new)
        return tuple(new)

    def pair(i, ms):
        c = 2 * i
        scores(c + 1, sb_s)
        ms = update(c, sa_s, ms)
        scores(c + 2, sa_s)
        return update(c + 1, sb_s, ms)

    m0 = jnp.full((1, tq), -jnp.inf, F32)
    scores(0, sa_s)
    ms = lax.fori_loop(0, n_chunks // 2 - 1, pair, (m0, m0))
    scores(n_chunks - 1, sb_s)
    ms = update(n_chunks - 2, sa_s, ms)
    update(n_chunks - 1, sb_s, ms)

    lam = lam_ref[0]
    o1 = acc_s[0, :V_DIM, :] / acc_s[0, V_DIM:V_DIM + 1, :]
    o2 = acc_s[1, :V_DIM, :] / acc_s[1, V_DIM:V_DIM + 1, :]
    ot = o1 - lam * o2
    ot = ot * lax.rsqrt(jnp.mean(ot * ot, axis=0, keepdims=True) + SUBLN_EPS)
    o_ref[...] = (ot.T * (g_ref[...] * out_scale)).astype(o_ref.dtype)


def _attention(h3, lam, subln_g, lam_init):
    b, s, _ = h3.shape
    tq, tk = ATTN_TQ, ATTN_TK
    n_chunks = s // tk
    assert n_chunks % 2 == 0 and n_chunks >= 2
    v = h3[:, :, V_HEAD0 * LANES:(V_HEAD0 + N_HEADS) * LANES]
    vt = v.reshape(b, n_chunks, tk, N_HEADS, V_DIM).transpose(0, 3, 1, 4, 2)
    vt = jnp.concatenate([vt, jnp.ones((b, N_HEADS, n_chunks, ONES_ROWS, tk), BF16)], axis=3)
    vt_rows = V_DIM + ONES_ROWS
    kern = functools.partial(_attn_kernel, out_scale=1.0 - lam_init)
    return pl.pallas_call(
        kern,
        out_shape=jax.ShapeDtypeStruct((b, s, N_HEADS * V_DIM), BF16),
        grid=(b, N_HEADS, s // tq),
        in_specs=[
            pl.BlockSpec(memory_space=pltpu.SMEM),
            pl.BlockSpec((None, tq, HEAD_COLS), lambda bi, hi, qi: (bi, qi, hi)),
            pl.BlockSpec((None, s, HEAD_COLS), lambda bi, hi, qi: (bi, 0, K_HEAD0 + hi)),
            pl.BlockSpec((None, None, n_chunks, vt_rows, tk), lambda bi, hi, qi: (bi, hi, 0, 0, 0)),
            pl.BlockSpec((1, V_DIM), lambda bi, hi, qi: (0, 0)),
        ],
        out_specs=pl.BlockSpec((None, tq, V_DIM), lambda bi, hi, qi: (bi, qi, hi)),
        scratch_shapes=[
            pltpu.VMEM((2, HEAD_COLS, tq), BF16),
            pltpu.VMEM((2, vt_rows, tq), F32),
            pltpu.VMEM((2, tk, tq), F32),
            pltpu.VMEM((2, tk, tq), F32),
        ],
        compiler_params=pltpu.CompilerParams(
            dimension_semantics=("parallel", "parallel", "arbitrary"),
            vmem_limit_bytes=VMEM_LIMIT),
        name="diff_attention",
    )(lam, h3, h3, vt, subln_g)


def _layer_norm(y, g, b):
    mu = jnp.mean(y, axis=-1, keepdims=True)
    d = y - mu
    var = jnp.mean(d * d, axis=-1, keepdims=True)
    return d * lax.rsqrt(var + LN_EPS) * g + b


def _mix_kernel(a_ref, cb_ref, cc_ref, cx_ref, ga_ref, gc_ref, ccp_ref, cxp_ref, ccn_ref, cxn_ref,
                x_ref, bg_ref, cw_ref, wa_ref, wc_ref, wo_ref, g1_ref, b1_ref, wr_ref, br_ref, tri_ref,
                x1_ref, rt_ref, cnt_ref, *, tiles_per_seq):
    i = pl.program_id(0)

    @pl.when(i == 0)
    def _():
        cnt_ref[...] = jnp.zeros(cnt_ref.shape, F32)

    tm = x_ref.shape[0]
    u = cc_ref[...].astype(F32) * cx_ref[...].astype(F32)
    first = (i % tiles_per_seq) == 0
    last = (i % tiles_per_seq) == tiles_per_seq - 1
    up = ccp_ref[SUBLANES - 1:SUBLANES, :].astype(F32) * cxp_ref[SUBLANES - 1:SUBLANES, :].astype(F32)
    un = ccn_ref[0:1, :].astype(F32) * cxn_ref[0:1, :].astype(F32)
    up = jnp.where(first, 0.0, up)
    un = jnp.where(last, 0.0, un)
    row = lax.broadcasted_iota(jnp.int32, u.shape, 0)
    u_prev = jnp.where(row == 0, up, pltpu.roll(u, 1, 0))
    u_next = jnp.where(row == tm - 1, un, pltpu.roll(u, tm - 1, 0))
    conv = cw_ref[0:1, :] * u_prev + cw_ref[1:2, :] * u + cw_ref[2:3, :] * u_next
    c = (cb_ref[...].astype(F32) * conv).astype(BF16)
    g_a = jax.nn.sigmoid(ga_ref[...].astype(F32) + bg_ref[:, :D_MODEL])
    g_c = jax.nn.sigmoid(gc_ref[...].astype(F32) + bg_ref[:, D_MODEL:])
    merged = (g_a * jnp.dot(a_ref[...], wa_ref[...], preferred_element_type=F32)
              + g_c * jnp.dot(c, wc_ref[...], preferred_element_type=F32))
    m = jnp.dot(merged.astype(BF16), wo_ref[...], preferred_element_type=F32)
    x1 = _layer_norm(DEEPNORM_ALPHA * x_ref[...] + m, g1_ref[...], b1_ref[...])
    x1_ref[...] = x1
    logits = jnp.dot(x1.astype(BF16), wr_ref[...], preferred_element_type=F32) + br_ref[...]

    lane = lax.broadcasted_iota(jnp.int32, logits.shape, 1)
    work = jnp.where(lane < N_EXPERTS, logits, -jnp.inf)
    hots, vals, idxs = [], [], []
    for _ in range(TOP_K):
        mx = jnp.max(work, axis=-1, keepdims=True)
        ix = jnp.min(jnp.where(work == mx, lane, ROUTER_COLS), axis=-1, keepdims=True)
        hot = lane == ix
        work = jnp.where(hot, -jnp.inf, work)
        hots.append(hot)
        vals.append(mx)
        idxs.append(ix)
    exps = [jnp.exp(v - vals[0]) for v in vals]
    denom = exps[0] + exps[1] + exps[2] + exps[3]
    picked = jnp.zeros(logits.shape, F32)
    for hot in hots:
        picked = picked + hot.astype(F32)
    before = cnt_ref[...] + jnp.dot(tri_ref[...], picked.astype(BF16), preferred_element_type=F32)
    out = jnp.zeros(logits.shape, F32)
    for j in range(TOP_K):
        rank = jnp.sum(jnp.where(hots[j], before, 0.0), axis=-1, keepdims=True)
        out = jnp.where(lane == RT_EXPERT + j, idxs[j].astype(F32), out)
        out = jnp.where(lane == RT_RANK + j, rank, out)
        out = jnp.where(lane == RT_GATE + j, exps[j] / denom, out)
    rt_ref[...] = out
    cnt_ref[...] = cnt_ref[...] + jnp.sum(picked, axis=0, keepdims=True)


def _mix(a2d, h2d, x2d, seq, bg, cw, wa, wc, wo, g1, b1, wr, br):
    n = x2d.shape[0]
    tm = MIX_TM
    tiles_per_seq = seq // tm
    halo_blocks = tm // SUBLANES
    n_halo = n // SUBLANES

    def col(blk):
        return pl.BlockSpec((tm, COL_BLOCK), lambda i: (i, blk))

    def prev(blk):
        return pl.BlockSpec((SUBLANES, COL_BLOCK), lambda i: (jnp.maximum(i * halo_blocks - 1, 0), blk))

    def nxt(blk):
        return pl.BlockSpec((SUBLANES, COL_BLOCK),
                            lambda i: (jnp.minimum((i + 1) * halo_blocks, n_halo - 1), blk))

    def full(shape):
        return pl.BlockSpec(shape, lambda i: (0,) * len(shape))

    tri = (lax.broadcasted_iota(jnp.int32, (tm, tm), 1)
           < lax.broadcasted_iota(jnp.int32, (tm, tm), 0)).astype(BF16)
    kern = functools.partial(_mix_kernel, tiles_per_seq=tiles_per_seq)
    return pl.pallas_call(
        kern,
        out_shape=(jax.ShapeDtypeStruct((n, D_MODEL), F32),
                   jax.ShapeDtypeStruct((n, ROUTER_COLS), F32),
                   jax.ShapeDtypeStruct((1, ROUTER_COLS), F32)),
        grid=(n // tm,),
        in_specs=[
            pl.BlockSpec((tm, D_MODEL), lambda i: (i, 0)),
            col(CB_BLK), col(CC_BLK), col(CX_BLK), col(GA_BLK), col(GC_BLK),
            prev(CC_BLK), prev(CX_BLK), nxt(CC_BLK), nxt(CX_BLK),
            pl.BlockSpec((tm, D_MODEL), lambda i: (i, 0)),
            full((1, 2 * D_MODEL)), full((SUBLANES, D_MODEL)),
            full((D_MODEL, D_MODEL)), full((D_MODEL, D_MODEL)), full((D_MODEL, D_MODEL)),
            full((1, D_MODEL)), full((1, D_MODEL)),
            full((D_MODEL, ROUTER_COLS)), full((1, ROUTER_COLS)), full((tm, tm)),
        ],
        out_specs=(pl.BlockSpec((tm, D_MODEL), lambda i: (i, 0)),
                   pl.BlockSpec((tm, ROUTER_COLS), lambda i: (i, 0)),
                   pl.BlockSpec((1, ROUTER_COLS), lambda i: (0, 0))),
        compiler_params=pltpu.CompilerParams(
            dimension_semantics=("arbitrary",), vmem_limit_bytes=VMEM_LIMIT),
        name="mix_ln1_router",
    )(a2d, h2d, h2d, h2d, h2d, h2d, h2d, h2d, h2d, h2d, x2d, bg, cw, wa, wc, wo, g1, b1, wr, br, tri)


def _moe_kernel(be_ref, nv_ref, nu_ref, xs_ref, wg_ref, bg_ref, wu_ref, bu_ref, wd_ref, bd_ref, o_ref):
    i = pl.program_id(0)

    @pl.when(i < nu_ref[0])
    def _():
        row = lax.broadcasted_iota(jnp.int32, xs_ref.shape, 0)
        x = jnp.where(row < nv_ref[i], xs_ref[...], 0.0).astype(BF16)
        hg = jnp.minimum(jnp.dot(x, wg_ref[...], preferred_element_type=F32) + bg_ref[...], SWIGLU_LIMIT)
        hu = jnp.clip(jnp.dot(x, wu_ref[...], preferred_element_type=F32) + bu_ref[...],
                      -SWIGLU_LIMIT, SWIGLU_LIMIT)
        act = hg * jax.nn.sigmoid(SWIGLU_ALPHA * hg) * (hu + 1.0)
        o_ref[...] = jnp.dot(act.astype(BF16), wd_ref[...], preferred_element_type=F32) + bd_ref[...]

    @pl.when(i >= nu_ref[0])
    def _():
        o_ref[...] = jnp.zeros(o_ref.shape, o_ref.dtype)


def _moe_experts(block_e, n_valid, n_used, xs, wg, bg, wu, bu, wd, bd):
    p = xs.shape[0]
    tm = MOE_TM
    nb = p // tm

    def row_map(i, be, nv, nu):
        return (jnp.minimum(i, nu[0] - 1), 0)

    def w_map(i, be, nv, nu):
        return (be[jnp.minimum(i, nu[0] - 1)], 0, 0)

    w_spec = pl.BlockSpec((None, D_MODEL, D_MODEL), w_map)
    b_spec = pl.BlockSpec((None, 1, D_MODEL), w_map)
    return pl.pallas_call(
        _moe_kernel,
        out_shape=jax.ShapeDtypeStruct((p, D_MODEL), F32),
        grid_spec=pltpu.PrefetchScalarGridSpec(
            num_scalar_prefetch=3,
            grid=(nb,),
            in_specs=[
                pl.BlockSpec((tm, D_MODEL), row_map),
                w_spec, b_spec, w_spec, b_spec, w_spec, b_spec,
            ],
            out_specs=pl.BlockSpec((tm, D_MODEL), lambda i, be, nv, nu: (i, 0)),
        ),
        compiler_params=pltpu.CompilerParams(
            dimension_semantics=("arbitrary",), vmem_limit_bytes=VMEM_LIMIT),
        name="moe_experts",
    )(block_e, n_valid, n_used, xs, wg, bg, wu, bu, wd, bd)


def _sc_mesh():
    return plsc.VectorSubcoreMesh(core_axis_name="core", subcore_axis_name="subcore")


def _sc_dispatch(x1, dest_t, p):
    n, d = x1.shape
    w = SC_WINDOW
    assert TOP_K * w == LANES
    idx = dest_t.reshape(TOP_K, n // w, w).transpose(1, 0, 2).reshape(n // w, LANES)

    @pl.kernel(out_type=jax.ShapeDtypeStruct((p, d), x1.dtype), mesh=_sc_mesh(), scratch_types=[],
               name="sc_dispatch")
    def run(x_hbm, d_hbm, o_hbm):
        def body(x_vmem, d_vmem):
            for k in range(TOP_K):
                pltpu.sync_copy(x_vmem, o_hbm.at[d_vmem.at[0, pl.ds(k * w, w)]])

        pltpu.emit_pipeline(
            body,
            grid=(n // w,),
            in_specs=[pl.BlockSpec((w, d), lambda i: (i, 0)),
                      pl.BlockSpec((1, LANES), lambda i: (i, 0))],
            out_specs=[],
            core_axis_name=("core", "subcore"),
            dimension_semantics=(pltpu.PARALLEL,),
        )(x_hbm, d_hbm)

    return run(x1, idx)


def _sc_gather(ys, idx):
    r = idx.shape[0]
    d = ys.shape[1]
    w = SC_WINDOW
    idx = jnp.pad(idx.reshape(r // w, w), ((0, 0), (0, LANES - w)))

    @pl.kernel(out_type=jax.ShapeDtypeStruct((r, d), ys.dtype), mesh=_sc_mesh(), scratch_types=[],
               name="sc_gather")
    def run(y_hbm, i_hbm, o_hbm):
        def body(i_vmem, o_vmem):
            pltpu.sync_copy(y_hbm.at[i_vmem.at[0, pl.ds(0, w)]], o_vmem)

        pltpu.emit_pipeline(
            body,
            grid=(r // w,),
            in_specs=[pl.BlockSpec((1, LANES), lambda i: (i, 0))],
            out_specs=[pl.BlockSpec((w, d), lambda i: (i, 0))],
            core_axis_name=("core", "subcore"),
            dimension_semantics=(pltpu.PARALLEL,),
        )(i_hbm, o_hbm)

    return run(ys, idx)


def _ln2_kernel(x1_ref, yg_ref, rt_ref, g_ref, b_ref, o_ref):
    f = jnp.zeros(x1_ref.shape, F32)
    for k in range(TOP_K):
        f = f + yg_ref[k] * rt_ref[:, RT_GATE + k:RT_GATE + k + 1]
    o_ref[...] = _layer_norm(DEEPNORM_ALPHA * x1_ref[...] + f, g_ref[...], b_ref[...])


def _combine_ln2(x1, yg, rt, g2, b2):
    n = x1.shape[0]
    tm = LN2_TM
    return pl.pallas_call(
        _ln2_kernel,
        out_shape=jax.ShapeDtypeStruct((n, D_MODEL), F32),
        grid=(n // tm,),
        in_specs=[
            pl.BlockSpec((tm, D_MODEL), lambda i: (i, 0)),
            pl.BlockSpec((TOP_K, tm, D_MODEL), lambda i: (0, i, 0)),
            pl.BlockSpec((tm, ROUTER_COLS), lambda i: (i, 0)),
            pl.BlockSpec((1, D_MODEL), lambda i: (0, 0)),
            pl.BlockSpec((1, D_MODEL), lambda i: (0, 0)),
        ],
        out_specs=pl.BlockSpec((tm, D_MODEL), lambda i: (i, 0)),
        compiler_params=pltpu.CompilerParams(
            dimension_semantics=("parallel",), vmem_limit_bytes=VMEM_LIMIT),
        name="combine_ln2",
    )(x1, yg, rt, g2, b2)


def _rotary_tables(seq):
    inv = ROPE_THETA ** (-jnp.arange(0, ROT_DIM, 2, dtype=F32) / ROT_DIM)
    ang = jnp.arange(seq, dtype=F32)[:, None] * inv[None, :]
    cos, sin = jnp.cos(ang), jnp.sin(ang)
    ones = jnp.ones((seq, QK_DIM - ROT_DIM), F32)
    zeros = jnp.zeros((seq, QK_DIM - ROT_DIM), F32)
    zh = jnp.zeros((seq, ROT_HALF), F32)
    cos64 = jnp.concatenate([cos, cos, ones], axis=1)
    sa64 = jnp.concatenate([-sin, zh, zeros], axis=1)
    sb64 = jnp.concatenate([zh, sin, zeros], axis=1)
    rep = LANES // QK_DIM
    return jnp.tile(cos64, (1, rep)), jnp.tile(sa64, (1, rep)), jnp.tile(sb64, (1, rep))


def _slot_layout(rt, counts, tm):
    n = rt.shape[0]
    p = n * TOP_K + N_EXPERTS * tm
    nb = p // tm
    expert = rt[:, RT_EXPERT:RT_EXPERT + TOP_K].astype(jnp.int32)
    rank = rt[:, RT_RANK:RT_RANK + TOP_K].astype(jnp.int32)
    padded = ((counts + tm - 1) // tm) * tm
    pad_ends = jnp.cumsum(padded)
    pad_starts = pad_ends - padded
    dest_t = (pad_starts[expert] + rank).T
    block_start = jnp.arange(nb, dtype=jnp.int32) * tm
    block_e = jnp.minimum(jnp.searchsorted(pad_ends, block_start, side='right'),
                          N_EXPERTS - 1).astype(jnp.int32)
    n_valid = jnp.clip(pad_starts[block_e] + counts[block_e] - block_start, 0, tm).astype(jnp.int32)
    n_used = (pad_ends[-1] // tm).astype(jnp.int32).reshape(1)
    return dest_t.astype(jnp.int32), block_e, n_valid, n_used, p


def _trunk(x, prm):
    b, s, d = x.shape
    n = b * s
    x2d = x.reshape(n, d)
    cos_t, sa_t, sb_t = _rotary_tables(s)
    h = _inproj(x2d, prm["w_in"], cos_t, sa_t, sb_t, s)
    a = _attention(h.reshape(b, s, IN_COLS), prm["lam"], prm["subln_g"], prm["lam_init"])
    x1, rt, cnt = _mix(a.reshape(n, d), h, x2d, s, prm["b_gate"], prm["conv_w"], prm["w_attn_br"],
                       prm["w_conv_br"], prm["w_o"], prm["ln1_g"], prm["ln1_b"],
                       prm["w_router"], prm["b_router"])
    counts = cnt[0, :N_EXPERTS].astype(jnp.int32)
    dest_t, block_e, n_valid, n_used, p = _slot_layout(rt, counts, MOE_TM)
    xs = _sc_dispatch(x1, dest_t, p)
    ys = _moe_experts(block_e, n_valid, n_used, xs, prm["w_exp_gate"], prm["b_exp_gate"],
                      prm["w_exp_up"], prm["b_exp_up"], prm["w_exp_down"], prm["b_exp_down"])
    yg = _sc_gather(ys, dest_t.reshape(-1)).reshape(TOP_K, n, d)
    y = _combine_ln2(x1, yg, rt, prm["ln2_g"], prm["ln2_b"])
    return y.reshape(b, s, d)


def kernel(x_prompt, x_sample, w_in, b_branch_gate, lambda_q1, lambda_k1, lambda_q2, lambda_k2, subln_g,
           conv_w, w_attn_br, w_conv_br, w_o, ln1_g, ln1_b, w_router, b_router, w_exp_gate, b_exp_gate,
           w_exp_up, b_exp_up, w_exp_down, b_exp_down, ln2_g, ln2_b):
    l = 0
    lam_init = 0.8 - 0.6 * math.exp(-0.3 * l)
    lam = (jnp.exp(jnp.sum(lambda_q1[l].astype(F32) * lambda_k1[l].astype(F32)))
           - jnp.exp(jnp.sum(lambda_q2[l].astype(F32) * lambda_k2[l].astype(F32))) + lam_init)
    prm = {
        "lam_init": lam_init,
        "lam": lam.reshape(1).astype(F32),
        "w_in": w_in[l].astype(BF16),
        "b_gate": b_branch_gate[l].reshape(1, 2 * D_MODEL),
        "subln_g": subln_g[l].reshape(1, V_DIM),
        "conv_w": jnp.pad(conv_w[l], ((0, SUBLANES - conv_w.shape[1]), (0, 0))),
        "w_attn_br": w_attn_br[l].astype(BF16),
        "w_conv_br": w_conv_br[l].astype(BF16),
        "w_o": w_o[l].astype(BF16),
        "ln1_g": ln1_g[l].reshape(1, D_MODEL),
        "ln1_b": ln1_b[l].reshape(1, D_MODEL),
        "w_router": jnp.pad(w_router[l], ((0, 0), (0, ROUTER_COLS - N_EXPERTS))).astype(BF16),
        "b_router": jnp.pad(b_router[l], (0, ROUTER_COLS - N_EXPERTS)).reshape(1, ROUTER_COLS),
        "w_exp_gate": w_exp_gate[l].astype(BF16),
        "b_exp_gate": b_exp_gate[l].reshape(N_EXPERTS, 1, D_MODEL),
        "w_exp_up": w_exp_up[l].astype(BF16),
        "b_exp_up": b_exp_up[l].reshape(N_EXPERTS, 1, D_MODEL),
        "w_exp_down": w_exp_down[l].astype(BF16),
        "b_exp_down": b_exp_down[l].reshape(N_EXPERTS, 1, D_MODEL),
        "ln2_g": ln2_g[l].reshape(1, D_MODEL),
        "ln2_b": ln2_b[l].reshape(1, D_MODEL),
    }
    return _trunk(x_prompt, prm), _trunk(x_sample, prm)
```

```python
import functools
import math

import jax
import jax.numpy as jnp
from jax import lax
from jax.experimental import pallas as pl
from jax.experimental.pallas import tpu as pltpu
from jax.experimental.pallas import tpu_sc as plsc

F32 = jnp.float32
BF16 = jnp.bfloat16

D_MODEL = 1024
N_HEADS = 8
QK_DIM = 64
V_DIM = 128
HEAD_COLS = 2 * QK_DIM
ROT_DIM = QK_DIM // 4
ROT_HALF = ROT_DIM // 2
ROPE_THETA = 500000.0
SUBLN_EPS = 1e-5
IN_COLS = 8192
COL_BLOCK = 1024
CB_BLK, CC_BLK, CX_BLK, GA_BLK, GC_BLK = 3, 4, 5, 6, 7
K_HEAD0, V_HEAD0 = 8, 16
N_EXPERTS = 32
TOP_K = 4
SWIGLU_LIMIT = 7.0
SWIGLU_ALPHA = 1.702
LN_EPS = 1e-5
DEPTH = 1
DEEPNORM_ALPHA = (2 * DEPTH) ** 0.25
LANES = 128
SUBLANES = 8
ROUTER_COLS = LANES
RT_EXPERT, RT_RANK, RT_GATE = 0, 4, 8
SC_WINDOW = 32
ONES_ROWS = 16
Q_SCALE = math.log2(math.e) * QK_DIM ** -0.5

VMEM_LIMIT = 56 * 1024 * 1024

PROJ_TM = 1024
ATTN_TQ = 1024
ATTN_TK = 512
MIX_TM = 256
MOE_TM = 512
LN2_TM = 512


def _inproj_kernel(x_ref, w_ref, cos_ref, sa_ref, sb_ref, o_ref):
    j = pl.program_id(1)
    acc = jnp.dot(x_ref[...].astype(BF16), w_ref[...], preferred_element_type=F32)

    @pl.when(j >= 2)
    def _():
        o_ref[...] = acc.astype(o_ref.dtype)

    @pl.when(j < 2)
    def _():
        scale = jnp.where(j == 0, Q_SCALE, 1.0).astype(F32)
        cos = cos_ref[...] * scale
        sa = sa_ref[...] * scale
        sb = sb_ref[...] * scale
        for c in range(COL_BLOCK // LANES):
            a = acc[:, c * LANES:(c + 1) * LANES]
            r = (a * cos + pltpu.roll(a, LANES - ROT_HALF, 1) * sa + pltpu.roll(a, ROT_HALF, 1) * sb)
            o_ref[:, c * LANES:(c + 1) * LANES] = r.astype(o_ref.dtype)


def _inproj(x2d, w_in_b, cos_t, sa_t, sb_t, seq):
    n = x2d.shape[0]
    tm = PROJ_TM
    pos_blocks = seq // tm
    tab_spec = pl.BlockSpec((tm, LANES), lambda i, j: (i % pos_blocks, 0))
    return pl.pallas_call(
        _inproj_kernel,
        out_shape=jax.ShapeDtypeStruct((n, IN_COLS), BF16),
        grid=(n // tm, IN_COLS // COL_BLOCK),
        in_specs=[
            pl.BlockSpec((tm, D_MODEL), lambda i, j: (i, 0)),
            pl.BlockSpec((D_MODEL, COL_BLOCK), lambda i, j: (0, j)),
            tab_spec, tab_spec, tab_spec,
        ],
        out_specs=pl.BlockSpec((tm, COL_BLOCK), lambda i, j: (i, j)),
        compiler_params=pltpu.CompilerParams(
            dimension_semantics=("parallel", "arbitrary"), vmem_limit_bytes=VMEM_LIMIT),
        name="inproj",
    )(x2d, w_in_b, cos_t, sa_t, sb_t)


def _attn_kernel(lam_ref, q_ref, k_ref, vt_ref, g_ref, o_ref, qt_s, acc_s, sa0, sa1, sb0, sb1, *, out_scale):
    tq = q_ref.shape[0]
    n_chunks, _, tk = vt_ref.shape
    qf = q_ref[...].astype(F32)
    lane = lax.broadcasted_iota(jnp.int32, qf.shape, 1)
    qt_s[0] = jnp.where(lane < QK_DIM, qf, 0.0).T.astype(BF16)
    qt_s[1] = jnp.where(lane >= QK_DIM, qf, 0.0).T.astype(BF16)
    acc_s[...] = jnp.zeros(acc_s.shape, F32)
    sa_s, sb_s = (sa0, sa1), (sb0, sb1)

    def scores(c, s_refs, mp):
        k = k_ref[pl.ds(pl.multiple_of(c * tk, tk), tk), :]
        s_refs[mp][...] = jnp.dot(k, qt_s[mp], preferred_element_type=F32)

    def update(c, s_refs, mp, m_old):
        s_ref = s_refs[mp]
        m_new = jnp.maximum(m_old, jnp.max(s_ref[...], axis=0, keepdims=True))
        alpha = jnp.exp2(m_old - m_new)
        p = jnp.exp2(s_ref[...] - m_new).astype(BF16)
        acc_s[mp] = alpha * acc_s[mp] + jnp.dot(vt_ref[c], p, preferred_element_type=F32)
        return m_new

    def step(c, cur, nxt, ms, prefetch):
        new = []
        for mp in range(2):
            if prefetch:
                scores(c + 1, nxt, mp)
            new.append(update(c, cur, mp, ms[mp]))
        return tuple(new)

    def pair(i, ms):
        ms = step(2 * i, sa_s, sb_s, ms, True)
        return step(2 * i + 1, sb_s, sa_s, ms, True)

    m0 = jnp.full((1, tq), -jnp.inf, F32)
    for mp in range(2):
        scores(0, sa_s, mp)
    ms = lax.fori_loop(0, n_chunks // 2 - 1, pair, (m0, m0))
    ms = step(n_chunks - 2, sa_s, sb_s, ms, True)
    step(n_chunks - 1, sb_s, sa_s, ms, False)

    lam = lam_ref[0]
    o1 = acc_s[0, :V_DIM, :] / acc_s[0, V_DIM:V_DIM + 1, :]
    o2 = acc_s[1, :V_DIM, :] / acc_s[1, V_DIM:V_DIM + 1, :]
    ot = o1 - lam * o2
    ot = ot * lax.rsqrt(jnp.mean(ot * ot, axis=0, keepdims=True) + SUBLN_EPS)
    o_ref[...] = (ot.T * (g_ref[...] * out_scale)).astype(o_ref.dtype)


def _attention(h3, lam, subln_g, lam_init):
    b, s, _ = h3.shape
    tq, tk = ATTN_TQ, ATTN_TK
    n_chunks = s // tk
    assert n_chunks % 2 == 0 and n_chunks >= 2
    v = h3[:, :, V_HEAD0 * LANES:(V_HEAD0 + N_HEADS) * LANES]
    vt = v.reshape(b, n_chunks, tk, N_HEADS, V_DIM).transpose(0, 3, 1, 4, 2)
    vt = jnp.concatenate([vt, jnp.ones((b, N_HEADS, n_chunks, ONES_ROWS, tk), BF16)], axis=3)
    vt_rows = V_DIM + ONES_ROWS
    kern = functools.partial(_attn_kernel, out_scale=1.0 - lam_init)
    return pl.pallas_call(
        kern,
        out_shape=jax.ShapeDtypeStruct((b, s, N_HEADS * V_DIM), BF16),
        grid=(b, N_HEADS, s // tq),
        in_specs=[
            pl.BlockSpec(memory_space=pltpu.SMEM),
            pl.BlockSpec((None, tq, HEAD_COLS), lambda bi, hi, qi: (bi, qi, hi)),
            pl.BlockSpec((None, s, HEAD_COLS), lambda bi, hi, qi: (bi, 0, K_HEAD0 + hi)),
            pl.BlockSpec((None, None, n_chunks, vt_rows, tk), lambda bi, hi, qi: (bi, hi, 0, 0, 0)),
            pl.BlockSpec((1, V_DIM), lambda bi, hi, qi: (0, 0)),
        ],
        out_specs=pl.BlockSpec((None, tq, V_DIM), lambda bi, hi, qi: (bi, qi, hi)),
        scratch_shapes=[
            pltpu.VMEM((2, HEAD_COLS, tq), BF16),
            pltpu.VMEM((2, vt_rows, tq), F32),
            pltpu.VMEM((tk, tq), F32), pltpu.VMEM((tk, tq), F32),
            pltpu.VMEM((tk, tq), F32), pltpu.VMEM((tk, tq), F32),
        ],
        compiler_params=pltpu.CompilerParams(
            dimension_semantics=("parallel", "parallel", "arbitrary"),
            vmem_limit_bytes=VMEM_LIMIT),
        name="diff_attention",
    )(lam, h3, h3, vt, subln_g)


def _layer_norm(y, g, b):
    mu = jnp.mean(y, axis=-1, keepdims=True)
    d = y - mu
    var = jnp.mean(d * d, axis=-1, keepdims=True)
    return d * lax.rsqrt(var + LN_EPS) * g + b


def _mix_kernel(a_ref, cb_ref, cc_ref, cx_ref, ga_ref, gc_ref, ccp_ref, cxp_ref, ccn_ref, cxn_ref,
                x_ref, bg_ref, cw_ref, wa_ref, wc_ref, wo_ref, g1_ref, b1_ref, wr_ref, br_ref, tri_ref,
                x1_ref, rt_ref, cnt_ref, *, tiles_per_seq):
    i = pl.program_id(0)

    @pl.when(i == 0)
    def _():
        cnt_ref[...] = jnp.zeros(cnt_ref.shape, F32)

    tm = x_ref.shape[0]
    u = cc_ref[...].astype(F32) * cx_ref[...].astype(F32)
    first = (i % tiles_per_seq) == 0
    last = (i % tiles_per_seq) == tiles_per_seq - 1
    up = ccp_ref[SUBLANES - 1:SUBLANES, :].astype(F32) * cxp_ref[SUBLANES - 1:SUBLANES, :].astype(F32)
    un = ccn_ref[0:1, :].astype(F32) * cxn_ref[0:1, :].astype(F32)
    up = jnp.where(first, 0.0, up)
    un = jnp.where(last, 0.0, un)
    row = lax.broadcasted_iota(jnp.int32, u.shape, 0)
    u_prev = jnp.where(row == 0, up, pltpu.roll(u, 1, 0))
    u_next = jnp.where(row == tm - 1, un, pltpu.roll(u, tm - 1, 0))
    conv = cw_ref[0:1, :] * u_prev + cw_ref[1:2, :] * u + cw_ref[2:3, :] * u_next
    c = (cb_ref[...].astype(F32) * conv).astype(BF16)
    g_a = jax.nn.sigmoid(ga_ref[...].astype(F32) + bg_ref[:, :D_MODEL])
    g_c = jax.nn.sigmoid(gc_ref[...].astype(F32) + bg_ref[:, D_MODEL:])
    merged = (g_a * jnp.dot(a_ref[...], wa_ref[...], preferred_element_type=F32)
              + g_c * jnp.dot(c, wc_ref[...], preferred_element_type=F32))
    m = jnp.dot(merged.astype(BF16), wo_ref[...], preferred_element_type=F32)
    x1 = _layer_norm(DEEPNORM_ALPHA * x_ref[...] + m, g1_ref[...], b1_ref[...])
    x1_ref[...] = x1
    logits = jnp.dot(x1.astype(BF16), wr_ref[...], preferred_element_type=F32) + br_ref[...]

    lane = lax.broadcasted_iota(jnp.int32, logits.shape, 1)
    work = jnp.where(lane < N_EXPERTS, logits, -jnp.inf)
    hots, vals, idxs = [], [], []
    for _ in range(TOP_K):
        mx = jnp.max(work, axis=-1, keepdims=True)
        ix = jnp.min(jnp.where(work == mx, lane, ROUTER_COLS), axis=-1, keepdims=True)
        hot = lane == ix
        work = jnp.where(hot, -jnp.inf, work)
        hots.append(hot)
        vals.append(mx)
        idxs.append(ix)
    exps = [jnp.exp(v - vals[0]) for v in vals]
    denom = exps[0] + exps[1] + exps[2] + exps[3]
    picked = jnp.zeros(logits.shape, F32)
    for hot in hots:
        picked = picked + hot.astype(F32)
    before = cnt_ref[...] + jnp.dot(tri_ref[...], picked.astype(BF16), preferred_element_type=F32)
    out = jnp.zeros(logits.shape, F32)
    for j in range(TOP_K):
        rank = jnp.sum(jnp.where(hots[j], before, 0.0), axis=-1, keepdims=True)
        out = jnp.where(lane == RT_EXPERT + j, idxs[j].astype(F32), out)
        out = jnp.where(lane == RT_RANK + j, rank, out)
        out = jnp.where(lane == RT_GATE + j, exps[j] / denom, out)
    rt_ref[...] = out
    cnt_ref[...] = cnt_ref[...] + jnp.sum(picked, axis=0, keepdims=True)


def _mix(a2d, h2d, x2d, seq, bg, cw, wa, wc, wo, g1, b1, wr, br):
    n = x2d.shape[0]
    tm = MIX_TM
    tiles_per_seq = seq // tm
    halo_blocks = tm // SUBLANES
    n_halo = n // SUBLANES

    def col(blk):
        return pl.BlockSpec((tm, COL_BLOCK), lambda i: (i, blk))

    def prev(blk):
        return pl.BlockSpec((SUBLANES, COL_BLOCK), lambda i: (jnp.maximum(i * halo_blocks - 1, 0), blk))

    def nxt(blk):
        return pl.BlockSpec((SUBLANES, COL_BLOCK),
                            lambda i: (jnp.minimum((i + 1) * halo_blocks, n_halo - 1), blk))

    def full(shape):
        return pl.BlockSpec(shape, lambda i: (0,) * len(shape))

    tri = (lax.broadcasted_iota(jnp.int32, (tm, tm), 1)
           < lax.broadcasted_iota(jnp.int32, (tm, tm), 0)).astype(BF16)
    kern = functools.partial(_mix_kernel, tiles_per_seq=tiles_per_seq)
    return pl.pallas_call(
        kern,
        out_shape=(jax.ShapeDtypeStruct((n, D_MODEL), F32),
                   jax.ShapeDtypeStruct((n, ROUTER_COLS), F32),
                   jax.ShapeDtypeStruct((1, ROUTER_COLS), F32)),
        grid=(n // tm,),
        in_specs=[
            pl.BlockSpec((tm, D_MODEL), lambda i: (i, 0)),
            col(CB_BLK), col(CC_BLK), col(CX_BLK), col(GA_BLK), col(GC_BLK),
            prev(CC_BLK), prev(CX_BLK), nxt(CC_BLK), nxt(CX_BLK),
            pl.BlockSpec((tm, D_MODEL), lambda i: (i, 0)),
            full((1, 2 * D_MODEL)), full((SUBLANES, D_MODEL)),
            full((D_MODEL, D_MODEL)), full((D_MODEL, D_MODEL)), full((D_MODEL, D_MODEL)),
            full((1, D_MODEL)), full((1, D_MODEL)),
            full((D_MODEL, ROUTER_COLS)), full((1, ROUTER_COLS)), full((tm, tm)),
        ],
        out_specs=(pl.BlockSpec((tm, D_MODEL), lambda i: (i, 0)),
                   pl.BlockSpec((tm, ROUTER_COLS), lambda i: (i, 0)),
                   pl.BlockSpec((1, ROUTER_COLS), lambda i: (0, 0))),
        compiler_params=pltpu.CompilerParams(
            dimension_semantics=("arbitrary",), vmem_limit_bytes=VMEM_LIMIT),
        name="mix_ln1_router",
    )(a2d, h2d, h2d, h2d, h2d, h2d, h2d, h2d, h2d, h2d, x2d, bg, cw, wa, wc, wo, g1, b1, wr, br, tri)


def _moe_kernel(be_ref, nv_ref, nu_ref, xs_ref, wg_ref, bg_ref, wu_ref, bu_ref, wd_ref, bd_ref, o_ref):
    i = pl.program_id(0)

    @pl.when(i < nu_ref[0])
    def _():
        row = lax.broadcasted_iota(jnp.int32, xs_ref.shape, 0)
        x = jnp.where(row < nv_ref[i], xs_ref[...], 0.0).astype(BF16)
        hg = jnp.minimum(jnp.dot(x, wg_ref[...], preferred_element_type=F32) + bg_ref[...], SWIGLU_LIMIT)
        hu = jnp.clip(jnp.dot(x, wu_ref[...], preferred_element_type=F32) + bu_ref[...],
                      -SWIGLU_LIMIT, SWIGLU_LIMIT)
        act = hg * jax.nn.sigmoid(SWIGLU_ALPHA * hg) * (hu + 1.0)
        o_ref[...] = jnp.dot(act.astype(BF16), wd_ref[...], preferred_element_type=F32) + bd_ref[...]

    @pl.when(i >= nu_ref[0])
    def _():
        o_ref[...] = jnp.zeros(o_ref.shape, o_ref.dtype)


def _moe_experts(block_e, n_valid, n_used, xs, wg, bg, wu, bu, wd, bd):
    p = xs.shape[0]
    tm = MOE_TM
    nb = p // tm

    def row_map(i, be, nv, nu):
        return (jnp.minimum(i, nu[0] - 1), 0)

    def w_map(i, be, nv, nu):
        return (be[jnp.minimum(i, nu[0] - 1)], 0, 0)

    w_spec = pl.BlockSpec((None, D_MODEL, D_MODEL), w_map)
    b_spec = pl.BlockSpec((None, 1, D_MODEL), w_map)
    return pl.pallas_call(
        _moe_kernel,
        out_shape=jax.ShapeDtypeStruct((p, D_MODEL), F32),
        grid_spec=pltpu.PrefetchScalarGridSpec(
            num_scalar_prefetch=3,
            grid=(nb,),
            in_specs=[
                pl.BlockSpec((tm, D_MODEL), row_map),
                w_spec, b_spec, w_spec, b_spec, w_spec, b_spec,
            ],
            out_specs=pl.BlockSpec((tm, D_MODEL), lambda i, be, nv, nu: (i, 0)),
        ),
        compiler_params=pltpu.CompilerParams(
            dimension_semantics=("arbitrary",), vmem_limit_bytes=VMEM_LIMIT),
        name="moe_experts",
    )(block_e, n_valid, n_used, xs, wg, bg, wu, bu, wd, bd)


def _sc_mesh():
    return plsc.VectorSubcoreMesh(core_axis_name="core", subcore_axis_name="subcore")


def _sc_dispatch(x1, dest_t, p):
    n, d = x1.shape
    w = SC_WINDOW
    assert TOP_K * w == LANES
    idx = dest_t.reshape(TOP_K, n // w, w).transpose(1, 0, 2).reshape(n // w, LANES)

    @pl.kernel(out_type=jax.ShapeDtypeStruct((p, d), x1.dtype), mesh=_sc_mesh(), scratch_types=[],
               name="sc_dispatch")
    def run(x_hbm, d_hbm, o_hbm):
        def body(x_vmem, d_vmem):
            for k in range(TOP_K):
                pltpu.sync_copy(x_vmem, o_hbm.at[d_vmem.at[0, pl.ds(k * w, w)]])

        pltpu.emit_pipeline(
            body,
            grid=(n // w,),
            in_specs=[pl.BlockSpec((w, d), lambda i: (i, 0)),
                      pl.BlockSpec((1, LANES), lambda i: (i, 0))],
            out_specs=[],
            core_axis_name=("core", "subcore"),
            dimension_semantics=(pltpu.PARALLEL,),
        )(x_hbm, d_hbm)

    return run(x1, idx)


def _sc_gather(ys, idx):
    r = idx.shape[0]
    d = ys.shape[1]
    w = SC_WINDOW
    idx = jnp.pad(idx.reshape(r // w, w), ((0, 0), (0, LANES - w)))

    @pl.kernel(out_type=jax.ShapeDtypeStruct((r, d), ys.dtype), mesh=_sc_mesh(), scratch_types=[],
               name="sc_gather")
    def run(y_hbm, i_hbm, o_hbm):
        def body(i_vmem, o_vmem):
            pltpu.sync_copy(y_hbm.at[i_vmem.at[0, pl.ds(0, w)]], o_vmem)

        pltpu.emit_pipeline(
            body,
            grid=(r // w,),
            in_specs=[pl.BlockSpec((1, LANES), lambda i: (i, 0))],
            out_specs=[pl.BlockSpec((w, d), lambda i: (i, 0))],
            core_axis_name=("core", "subcore"),
            dimension_semantics=(pltpu.PARALLEL,),
        )(i_hbm, o_hbm)

    return run(ys, idx)


def _ln2_kernel(x1_ref, yg_ref, rt_ref, g_ref, b_ref, o_ref):
    f = jnp.zeros(x1_ref.shape, F32)
    for k in range(TOP_K):
        f = f + yg_ref[k] * rt_ref[:, RT_GATE + k:RT_GATE + k + 1]
    o_ref[...] = _layer_norm(DEEPNORM_ALPHA * x1_ref[...] + f, g_ref[...], b_ref[...])


def _combine_ln2(x1, yg, rt, g2, b2):
    n = x1.shape[0]
    tm = LN2_TM
    return pl.pallas_call(
        _ln2_kernel,
        out_shape=jax.ShapeDtypeStruct((n, D_MODEL), F32),
        grid=(n // tm,),
        in_specs=[
            pl.BlockSpec((tm, D_MODEL), lambda i: (i, 0)),
            pl.BlockSpec((TOP_K, tm, D_MODEL), lambda i: (0, i, 0)),
            pl.BlockSpec((tm, ROUTER_COLS), lambda i: (i, 0)),
            pl.BlockSpec((1, D_MODEL), lambda i: (0, 0)),
            pl.BlockSpec((1, D_MODEL), lambda i: (0, 0)),
        ],
        out_specs=pl.BlockSpec((tm, D_MODEL), lambda i: (i, 0)),
        compiler_params=pltpu.CompilerParams(
            dimension_semantics=("parallel",), vmem_limit_bytes=VMEM_LIMIT),
        name="combine_ln2",
    )(x1, yg, rt, g2, b2)


def _rotary_tables(seq):
    inv = ROPE_THETA ** (-jnp.arange(0, ROT_DIM, 2, dtype=F32) / ROT_DIM)
    ang = jnp.arange(seq, dtype=F32)[:, None] * inv[None, :]
    cos, sin = jnp.cos(ang), jnp.sin(ang)
    ones = jnp.ones((seq, QK_DIM - ROT_DIM), F32)
    zeros = jnp.zeros((seq, QK_DIM - ROT_DIM), F32)
    zh = jnp.zeros((seq, ROT_HALF), F32)
    cos64 = jnp.concatenate([cos, cos, ones], axis=1)
    sa64 = jnp.concatenate([-sin, zh, zeros], axis=1)
    sb64 = jnp.concatenate([zh, sin, zeros], axis=1)
    rep = LANES // QK_DIM
    return jnp.tile(cos64, (1, rep)), jnp.tile(sa64, (1, rep)), jnp.tile(sb64, (1, rep))


def _slot_layout(rt, counts, tm):
    n = rt.shape[0]
    p = n * TOP_K + N_EXPERTS * tm
    nb = p // tm
    expert = rt[:, RT_EXPERT:RT_EXPERT + TOP_K].astype(jnp.int32)
    rank = rt[:, RT_RANK:RT_RANK + TOP_K].astype(jnp.int32)
    padded = ((counts + tm - 1) // tm) * tm
    pad_ends = jnp.cumsum(padded)
    pad_starts = pad_ends - padded
    dest_t = (pad_starts[expert] + rank).T
    block_start = jnp.arange(nb, dtype=jnp.int32) * tm
    block_e = jnp.minimum(jnp.sum(pad_ends[None, :] <= block_start[:, None], axis=1),
                          N_EXPERTS - 1).astype(jnp.int32)
    n_valid = jnp.clip(pad_starts[block_e] + counts[block_e] - block_start, 0, tm).astype(jnp.int32)
    n_used = (pad_ends[-1] // tm).astype(jnp.int32).reshape(1)
    return dest_t.astype(jnp.int32), block_e, n_valid, n_used, p


def _trunk(x, prm):
    b, s, d = x.shape
    n = b * s
    x2d = x.reshape(n, d)
    cos_t, sa_t, sb_t = _rotary_tables(s)
    h = _inproj(x2d, prm["w_in"], cos_t, sa_t, sb_t, s)
    a = _attention(h.reshape(b, s, IN_COLS), prm["lam"], prm["subln_g"], prm["lam_init"])
    x1, rt, cnt = _mix(a.reshape(n, d), h, x2d, s, prm["b_gate"], prm["conv_w"], prm["w_attn_br"],
                       prm["w_conv_br"], prm["w_o"], prm["ln1_g"], prm["ln1_b"],
                       prm["w_router"], prm["b_router"])
    counts = cnt[0, :N_EXPERTS].astype(jnp.int32)
    dest_t, block_e, n_valid, n_used, p = _slot_layout(rt, counts, MOE_TM)
    xs = _sc_dispatch(x1, dest_t, p)
    ys = _moe_experts(block_e, n_valid, n_used, xs, prm["w_exp_gate"], prm["b_exp_gate"],
                      prm["w_exp_up"], prm["b_exp_up"], prm["w_exp_down"], prm["b_exp_down"])
    yg = _sc_gather(ys, dest_t.reshape(-1)).reshape(TOP_K, n, d)
    y = _combine_ln2(x1, yg, rt, prm["ln2_g"], prm["ln2_b"])
    return y.reshape(b, s, d)


def kernel(x_prompt, x_sample, w_in, b_branch_gate, lambda_q1, lambda_k1, lambda_q2, lambda_k2, subln_g,
           conv_w, w_attn_br, w_conv_br, w_o, ln1_g, ln1_b, w_router, b_router, w_exp_gate, b_exp_gate,
           w_exp_up, b_exp_up, w_exp_down, b_exp_down, ln2_g, ln2_b):
    l = 0
    lam_init = 0.8 - 0.6 * math.exp(-0.3 * l)
    lam = (jnp.exp(jnp.sum(lambda_q1[l].astype(F32) * lambda_k1[l].astype(F32)))
           - jnp.exp(jnp.sum(lambda_q2[l].astype(F32) * lambda_k2[l].astype(F32))) + lam_init)
    prm = {
        "lam_init": lam_init,
        "lam": lam.reshape(1).astype(F32),
        "w_in": w_in[l].astype(BF16),
        "b_gate": b_branch_gate[l].reshape(1, 2 * D_MODEL),
        "subln_g": subln_g[l].reshape(1, V_DIM),
        "conv_w": jnp.pad(conv_w[l], ((0, SUBLANES - conv_w.shape[1]), (0, 0))),
        "w_attn_br": w_attn_br[l].astype(BF16),
        "w_conv_br": w_conv_br[l].astype(BF16),
        "w_o": w_o[l].astype(BF16),
        "ln1_g": ln1_g[l].reshape(1, D_MODEL),
        "ln1_b": ln1_b[l].reshape(1, D_MODEL),
        "w_router": jnp.pad(w_router[l], ((0, 0), (0, ROUTER_COLS - N_EXPERTS))).astype(BF16),
        "b_router": jnp.pad(b_router[l], (0, ROUTER_COLS - N_EXPERTS)).reshape(1, ROUTER_COLS),
        "w_exp_gate": w_exp_gate[l].astype(BF16),
        "b_exp_gate": b_exp_gate[l].reshape(N_EXPERTS, 1, D_MODEL),
        "w_exp_up": w_exp_up[l].astype(BF16),
        "b_exp_up": b_exp_up[l].reshape(N_EXPERTS, 1, D_MODEL),
        "w_exp_down": w_exp_down[l].astype(BF16),
        "b_exp_down": b_exp_down[l].reshape(N_EXPERTS, 1, D_MODEL),
        "ln2_g": ln2_g[l].reshape(1, D_MODEL),
        "ln2_b": ln2_b[l].reshape(1, D_MODEL),
    }
    return _trunk(x_prompt, prm), _trunk(x_sample, prm)
```

```python
import functools
import math

import jax
import jax.numpy as jnp
from jax import lax
from jax.experimental import pallas as pl
from jax.experimental.pallas import tpu as pltpu
from jax.experimental.pallas import tpu_sc as plsc

F32 = jnp.float32
BF16 = jnp.bfloat16

D_MODEL = 1024
N_HEADS = 8
QK_DIM = 64
V_DIM = 128
HEAD_COLS = 2 * QK_DIM
ROT_DIM = QK_DIM // 4
ROT_HALF = ROT_DIM // 2
ROPE_THETA = 500000.0
SUBLN_EPS = 1e-5
IN_COLS = 8192
COL_BLOCK = 1024
CB_BLK, CC_BLK, CX_BLK, GA_BLK, GC_BLK = 3, 4, 5, 6, 7
K_HEAD0, V_HEAD0 = 8, 16
N_EXPERTS = 32
TOP_K = 4
SWIGLU_LIMIT = 7.0
SWIGLU_ALPHA = 1.702
LN_EPS = 1e-5
DEPTH = 1
DEEPNORM_ALPHA = (2 * DEPTH) ** 0.25
LANES = 128
SUBLANES = 8
ROUTER_COLS = LANES
RT_EXPERT, RT_RANK, RT_GATE = 0, 4, 8
SC_WINDOW = 32
ONES_ROWS = 16
Q_SCALE = math.log2(math.e) * QK_DIM ** -0.5

VMEM_LIMIT = 56 * 1024 * 1024

PROJ_TM = 1024
ATTN_TQ = 1024
ATTN_TK = 512
MIX_TM = 256
MOE_TM = 512
LN2_TM = 512


def _inproj_kernel(x_ref, w_ref, cos_ref, sa_ref, sb_ref, o_ref):
    j = pl.program_id(1)
    acc = jnp.dot(x_ref[...].astype(BF16), w_ref[...], preferred_element_type=F32)

    @pl.when(j >= 2)
    def _():
        o_ref[...] = acc.astype(o_ref.dtype)

    @pl.when(j < 2)
    def _():
        scale = jnp.where(j == 0, Q_SCALE, 1.0).astype(F32)
        cos = cos_ref[...] * scale
        sa = sa_ref[...] * scale
        sb = sb_ref[...] * scale
        for c in range(COL_BLOCK // LANES):
            a = acc[:, c * LANES:(c + 1) * LANES]
            r = (a * cos + pltpu.roll(a, LANES - ROT_HALF, 1) * sa + pltpu.roll(a, ROT_HALF, 1) * sb)
            o_ref[:, c * LANES:(c + 1) * LANES] = r.astype(o_ref.dtype)


def _inproj(x2d, w_in_b, cos_t, sa_t, sb_t, seq):
    n = x2d.shape[0]
    tm = PROJ_TM
    pos_blocks = seq // tm
    tab_spec = pl.BlockSpec((tm, LANES), lambda i, j: (i % pos_blocks, 0))
    return pl.pallas_call(
        _inproj_kernel,
        out_shape=jax.ShapeDtypeStruct((n, IN_COLS), BF16),
        grid=(n // tm, IN_COLS // COL_BLOCK),
        in_specs=[
            pl.BlockSpec((tm, D_MODEL), lambda i, j: (i, 0)),
            pl.BlockSpec((D_MODEL, COL_BLOCK), lambda i, j: (0, j)),
            tab_spec, tab_spec, tab_spec,
        ],
        out_specs=pl.BlockSpec((tm, COL_BLOCK), lambda i, j: (i, j)),
        compiler_params=pltpu.CompilerParams(
            dimension_semantics=("parallel", "arbitrary"), vmem_limit_bytes=VMEM_LIMIT),
        name="inproj",
    )(x2d, w_in_b, cos_t, sa_t, sb_t)


def _attn_kernel(lam_ref, q_ref, k_ref, vt_ref, g_ref, o_ref, qt_s, acc_s, sa0, sa1, sb0, sb1, *, out_scale):
    tq = q_ref.shape[0]
    n_chunks, _, tk = vt_ref.shape
    qf = q_ref[...].astype(F32)
    lane = lax.broadcasted_iota(jnp.int32, qf.shape, 1)
    qt_s[0] = jnp.where(lane < QK_DIM, qf, 0.0).T.astype(BF16)
    qt_s[1] = jnp.where(lane >= QK_DIM, qf, 0.0).T.astype(BF16)
    acc_s[...] = jnp.zeros(acc_s.shape, F32)
    sa_s, sb_s = (sa0, sa1), (sb0, sb1)

    def scores(c, s_refs, mp):
        k = k_ref[pl.ds(pl.multiple_of(c * tk, tk), tk), :]
        s = jnp.dot(k, qt_s[mp], preferred_element_type=F32)
        s_refs[mp][...] = s
        return jnp.max(s, axis=0, keepdims=True)

    def update(c, s_refs, mp, m_old, c_max):
        m_new = jnp.maximum(m_old, c_max)
        alpha = jnp.exp2(m_old - m_new)
        p = jnp.exp2(s_refs[mp][...] - m_new).astype(BF16)
        acc_s[mp] = alpha * acc_s[mp] + jnp.dot(vt_ref[c], p, preferred_element_type=F32)
        return m_new

    def step(c, cur, nxt, ms, cur_max, prefetch):
        new_ms, nxt_max = [], []
        for mp in range(2):
            if prefetch:
                nxt_max.append(scores(c + 1, nxt, mp))
            new_ms.append(update(c, cur, mp, ms[mp], cur_max[mp]))
        return tuple(new_ms), tuple(nxt_max)

    def pair(i, carry):
        ms, a_max = carry
        ms, b_max = step(2 * i, sa_s, sb_s, ms, a_max, True)
        return step(2 * i + 1, sb_s, sa_s, ms, b_max, True)

    m0 = jnp.full((1, tq), -jnp.inf, F32)
    a_max = tuple(scores(0, sa_s, mp) for mp in range(2))
    ms, a_max = lax.fori_loop(0, n_chunks // 2 - 1, pair, ((m0, m0), a_max))
    ms, b_max = step(n_chunks - 2, sa_s, sb_s, ms, a_max, True)
    step(n_chunks - 1, sb_s, sa_s, ms, b_max, False)

    lam = lam_ref[0]
    o1 = acc_s[0, :V_DIM, :] / acc_s[0, V_DIM:V_DIM + 1, :]
    o2 = acc_s[1, :V_DIM, :] / acc_s[1, V_DIM:V_DIM + 1, :]
    ot = o1 - lam * o2
    ot = ot * lax.rsqrt(jnp.mean(ot * ot, axis=0, keepdims=True) + SUBLN_EPS)
    o_ref[...] = (ot.T * (g_ref[...] * out_scale)).astype(o_ref.dtype)


def _attention(h3, lam, subln_g, lam_init):
    b, s, _ = h3.shape
    tq, tk = ATTN_TQ, ATTN_TK
    n_chunks = s // tk
    assert n_chunks % 2 == 0 and n_chunks >= 2
    v = h3[:, :, V_HEAD0 * LANES:(V_HEAD0 + N_HEADS) * LANES]
    vt = v.reshape(b, n_chunks, tk, N_HEADS, V_DIM).transpose(0, 3, 1, 4, 2)
    vt = jnp.concatenate([vt, jnp.ones((b, N_HEADS, n_chunks, ONES_ROWS, tk), BF16)], axis=3)
    vt_rows = V_DIM + ONES_ROWS
    kern = functools.partial(_attn_kernel, out_scale=1.0 - lam_init)
    return pl.pallas_call(
        kern,
        out_shape=jax.ShapeDtypeStruct((b, s, N_HEADS * V_DIM), BF16),
        grid=(b, N_HEADS, s // tq),
        in_specs=[
            pl.BlockSpec(memory_space=pltpu.SMEM),
            pl.BlockSpec((None, tq, HEAD_COLS), lambda bi, hi, qi: (bi, qi, hi)),
            pl.BlockSpec((None, s, HEAD_COLS), lambda bi, hi, qi: (bi, 0, K_HEAD0 + hi)),
            pl.BlockSpec((None, None, n_chunks, vt_rows, tk), lambda bi, hi, qi: (bi, hi, 0, 0, 0)),
            pl.BlockSpec((1, V_DIM), lambda bi, hi, qi: (0, 0)),
        ],
        out_specs=pl.BlockSpec((None, tq, V_DIM), lambda bi, hi, qi: (bi, qi, hi)),
        scratch_shapes=[
            pltpu.VMEM((2, HEAD_COLS, tq), BF16),
            pltpu.VMEM((2, vt_rows, tq), F32),
            pltpu.VMEM((tk, tq), F32), pltpu.VMEM((tk, tq), F32),
            pltpu.VMEM((tk, tq), F32), pltpu.VMEM((tk, tq), F32),
        ],
        compiler_params=pltpu.CompilerParams(
            dimension_semantics=("parallel", "parallel", "arbitrary"),
            vmem_limit_bytes=VMEM_LIMIT),
        name="diff_attention",
    )(lam, h3, h3, vt, subln_g)


def _layer_norm(y, g, b):
    mu = jnp.mean(y, axis=-1, keepdims=True)
    d = y - mu
    var = jnp.mean(d * d, axis=-1, keepdims=True)
    return d * lax.rsqrt(var + LN_EPS) * g + b


def _mix_kernel(a_ref, cb_ref, cc_ref, cx_ref, ga_ref, gc_ref, ccp_ref, cxp_ref, ccn_ref, cxn_ref,
                x_ref, bg_ref, cw_ref, wa_ref, wc_ref, wo_ref, g1_ref, b1_ref, wr_ref, br_ref, tri_ref,
                x1_ref, rt_ref, cnt_ref, *, tiles_per_seq):
    i = pl.program_id(0)

    @pl.when(i == 0)
    def _():
        cnt_ref[...] = jnp.zeros(cnt_ref.shape, F32)

    tm = x_ref.shape[0]
    u = cc_ref[...].astype(F32) * cx_ref[...].astype(F32)
    first = (i % tiles_per_seq) == 0
    last = (i % tiles_per_seq) == tiles_per_seq - 1
    up = ccp_ref[SUBLANES - 1:SUBLANES, :].astype(F32) * cxp_ref[SUBLANES - 1:SUBLANES, :].astype(F32)
    un = ccn_ref[0:1, :].astype(F32) * cxn_ref[0:1, :].astype(F32)
    up = jnp.where(first, 0.0, up)
    un = jnp.where(last, 0.0, un)
    row = lax.broadcasted_iota(jnp.int32, u.shape, 0)
    u_prev = jnp.where(row == 0, up, pltpu.roll(u, 1, 0))
    u_next = jnp.where(row == tm - 1, un, pltpu.roll(u, tm - 1, 0))
    conv = cw_ref[0:1, :] * u_prev + cw_ref[1:2, :] * u + cw_ref[2:3, :] * u_next
    c = (cb_ref[...].astype(F32) * conv).astype(BF16)
    g_a = jax.nn.sigmoid(ga_ref[...].astype(F32) + bg_ref[:, :D_MODEL])
    g_c = jax.nn.sigmoid(gc_ref[...].astype(F32) + bg_ref[:, D_MODEL:])
    merged = (g_a * jnp.dot(a_ref[...], wa_ref[...], preferred_element_type=F32)
              + g_c * jnp.dot(c, wc_ref[...], preferred_element_type=F32))
    m = jnp.dot(merged.astype(BF16), wo_ref[...], preferred_element_type=F32)
    x1 = _layer_norm(DEEPNORM_ALPHA * x_ref[...] + m, g1_ref[...], b1_ref[...])
    x1_ref[...] = x1
    logits = jnp.dot(x1.astype(BF16), wr_ref[...], preferred_element_type=F32) + br_ref[...]

    lane = lax.broadcasted_iota(jnp.int32, logits.shape, 1)
    work = jnp.where(lane < N_EXPERTS, logits, -jnp.inf)
    hots, vals, idxs = [], [], []
    for _ in range(TOP_K):
        mx = jnp.max(work, axis=-1, keepdims=True)
        ix = jnp.min(jnp.where(work == mx, lane, ROUTER_COLS), axis=-1, keepdims=True)
        hot = lane == ix
        work = jnp.where(hot, -jnp.inf, work)
        hots.append(hot)
        vals.append(mx)
        idxs.append(ix)
    exps = [jnp.exp(v - vals[0]) for v in vals]
    denom = exps[0] + exps[1] + exps[2] + exps[3]
    picked = jnp.zeros(logits.shape, F32)
    for hot in hots:
        picked = picked + hot.astype(F32)
    before = cnt_ref[...] + jnp.dot(tri_ref[...], picked.astype(BF16), preferred_element_type=F32)
    out = jnp.zeros(logits.shape, F32)
    for j in range(TOP_K):
        rank = jnp.sum(jnp.where(hots[j], before, 0.0), axis=-1, keepdims=True)
        out = jnp.where(lane == RT_EXPERT + j, idxs[j].astype(F32), out)
        out = jnp.where(lane == RT_RANK + j, rank, out)
        out = jnp.where(lane == RT_GATE + j, exps[j] / denom, out)
    rt_ref[...] = out
    cnt_ref[...] = cnt_ref[...] + jnp.sum(picked, axis=0, keepdims=True)


def _mix(a2d, h2d, x2d, seq, bg, cw, wa, wc, wo, g1, b1, wr, br):
    n = x2d.shape[0]
    tm = MIX_TM
    tiles_per_seq = seq // tm
    halo_blocks = tm // SUBLANES
    n_halo = n // SUBLANES

    def col(blk):
        return pl.BlockSpec((tm, COL_BLOCK), lambda i: (i, blk))

    def prev(blk):
        return pl.BlockSpec((SUBLANES, COL_BLOCK), lambda i: (jnp.maximum(i * halo_blocks - 1, 0), blk))

    def nxt(blk):
        return pl.BlockSpec((SUBLANES, COL_BLOCK),
                            lambda i: (jnp.minimum((i + 1) * halo_blocks, n_halo - 1), blk))

    def full(shape):
        return pl.BlockSpec(shape, lambda i: (0,) * len(shape))

    tri = (lax.broadcasted_iota(jnp.int32, (tm, tm), 1)
           < lax.broadcasted_iota(jnp.int32, (tm, tm), 0)).astype(BF16)
    kern = functools.partial(_mix_kernel, tiles_per_seq=tiles_per_seq)
    return pl.pallas_call(
        kern,
        out_shape=(jax.ShapeDtypeStruct((n, D_MODEL), F32),
                   jax.ShapeDtypeStruct((n, ROUTER_COLS), F32),
                   jax.ShapeDtypeStruct((1, ROUTER_COLS), F32)),
        grid=(n // tm,),
        in_specs=[
            pl.BlockSpec((tm, D_MODEL), lambda i: (i, 0)),
            col(CB_BLK), col(CC_BLK), col(CX_BLK), col(GA_BLK), col(GC_BLK),
            prev(CC_BLK), prev(CX_BLK), nxt(CC_BLK), nxt(CX_BLK),
            pl.BlockSpec((tm, D_MODEL), lambda i: (i, 0)),
            full((1, 2 * D_MODEL)), full((SUBLANES, D_MODEL)),
            full((D_MODEL, D_MODEL)), full((D_MODEL, D_MODEL)), full((D_MODEL, D_MODEL)),
            full((1, D_MODEL)), full((1, D_MODEL)),
            full((D_MODEL, ROUTER_COLS)), full((1, ROUTER_COLS)), full((tm, tm)),
        ],
        out_specs=(pl.BlockSpec((tm, D_MODEL), lambda i: (i, 0)),
                   pl.BlockSpec((tm, ROUTER_COLS), lambda i: (i, 0)),
                   pl.BlockSpec((1, ROUTER_COLS), lambda i: (0, 0))),
        compiler_params=pltpu.CompilerParams(
            dimension_semantics=("arbitrary",), vmem_limit_bytes=VMEM_LIMIT),
        name="mix_ln1_router",
    )(a2d, h2d, h2d, h2d, h2d, h2d, h2d, h2d, h2d, h2d, x2d, bg, cw, wa, wc, wo, g1, b1, wr, br, tri)


def _moe_kernel(be_ref, nv_ref, nu_ref, xs_ref, wg_ref, bg_ref, wu_ref, bu_ref, wd_ref, bd_ref, o_ref):
    i = pl.program_id(0)

    @pl.when(i < nu_ref[0])
    def _():
        row = lax.broadcasted_iota(jnp.int32, xs_ref.shape, 0)
        x = jnp.where(row < nv_ref[i], xs_ref[...], 0.0).astype(BF16)
        hg = jnp.minimum(jnp.dot(x, wg_ref[...], preferred_element_type=F32) + bg_ref[...], SWIGLU_LIMIT)
        hu = jnp.clip(jnp.dot(x, wu_ref[...], preferred_element_type=F32) + bu_ref[...],
                      -SWIGLU_LIMIT, SWIGLU_LIMIT)
        act = hg * jax.nn.sigmoid(SWIGLU_ALPHA * hg) * (hu + 1.0)
        o_ref[...] = jnp.dot(act.astype(BF16), wd_ref[...], preferred_element_type=F32) + bd_ref[...]

    @pl.when(i >= nu_ref[0])
    def _():
        o_ref[...] = jnp.zeros(o_ref.shape, o_ref.dtype)


def _moe_experts(block_e, n_valid, n_used, xs, wg, bg, wu, bu, wd, bd):
    p = xs.shape[0]
    tm = MOE_TM
    nb = p // tm

    def row_map(i, be, nv, nu):
        return (jnp.minimum(i, nu[0] - 1), 0)

    def w_map(i, be, nv, nu):
        return (be[jnp.minimum(i, nu[0] - 1)], 0, 0)

    w_spec = pl.BlockSpec((None, D_MODEL, D_MODEL), w_map)
    b_spec = pl.BlockSpec((None, 1, D_MODEL), w_map)
    return pl.pallas_call(
        _moe_kernel,
        out_shape=jax.ShapeDtypeStruct((p, D_MODEL), F32),
        grid_spec=pltpu.PrefetchScalarGridSpec(
            num_scalar_prefetch=3,
            grid=(nb,),
            in_specs=[
                pl.BlockSpec((tm, D_MODEL), row_map),
                w_spec, b_spec, w_spec, b_spec, w_spec, b_spec,
            ],
            out_specs=pl.BlockSpec((tm, D_MODEL), lambda i, be, nv, nu: (i, 0)),
        ),
        compiler_params=pltpu.CompilerParams(
            dimension_semantics=("arbitrary",), vmem_limit_bytes=VMEM_LIMIT),
        name="moe_experts",
    )(block_e, n_valid, n_used, xs, wg, bg, wu, bu, wd, bd)


def _sc_mesh():
    return plsc.VectorSubcoreMesh(core_axis_name="core", subcore_axis_name="subcore")


def _sc_dispatch(x1, dest_t, p):
    n, d = x1.shape
    w = SC_WINDOW
    assert TOP_K * w == LANES
    idx = dest_t.reshape(TOP_K, n // w, w).transpose(1, 0, 2).reshape(n // w, LANES)

    @pl.kernel(out_type=jax.ShapeDtypeStruct((p, d), x1.dtype), mesh=_sc_mesh(), scratch_types=[],
               name="sc_dispatch")
    def run(x_hbm, d_hbm, o_hbm):
        def body(x_vmem, d_vmem):
            for k in range(TOP_K):
                pltpu.sync_copy(x_vmem, o_hbm.at[d_vmem.at[0, pl.ds(k * w, w)]])

        pltpu.emit_pipeline(
            body,
            grid=(n // w,),
            in_specs=[pl.BlockSpec((w, d), lambda i: (i, 0)),
                      pl.BlockSpec((1, LANES), lambda i: (i, 0))],
            out_specs=[],
            core_axis_name=("core", "subcore"),
            dimension_semantics=(pltpu.PARALLEL,),
        )(x_hbm, d_hbm)

    return run(x1, idx)


def _sc_gather(ys, idx):
    r = idx.shape[0]
    d = ys.shape[1]
    w = SC_WINDOW
    idx = jnp.pad(idx.reshape(r // w, w), ((0, 0), (0, LANES - w)))

    @pl.kernel(out_type=jax.ShapeDtypeStruct((r, d), ys.dtype), mesh=_sc_mesh(), scratch_types=[],
               name="sc_gather")
    def run(y_hbm, i_hbm, o_hbm):
        def body(i_vmem, o_vmem):
            pltpu.sync_copy(y_hbm.at[i_vmem.at[0, pl.ds(0, w)]], o_vmem)

        pltpu.emit_pipeline(
            body,
            grid=(r // w,),
            in_specs=[pl.BlockSpec((1, LANES), lambda i: (i, 0))],
            out_specs=[pl.BlockSpec((w, d), lambda i: (i, 0))],
            core_axis_name=("core", "subcore"),
            dimension_semantics=(pltpu.PARALLEL,),
        )(i_hbm, o_hbm)

    return run(ys, idx)


def _ln2_kernel(x1_ref, yg_ref, rt_ref, g_ref, b_ref, o_ref):
    f = jnp.zeros(x1_ref.shape, F32)
    for k in range(TOP_K):
        f = f + yg_ref[k] * rt_ref[:, RT_GATE + k:RT_GATE + k + 1]
    o_ref[...] = _layer_norm(DEEPNORM_ALPHA * x1_ref[...] + f, g_ref[...], b_ref[...])


def _combine_ln2(x1, yg, rt, g2, b2):
    n = x1.shape[0]
    tm = LN2_TM
    return pl.pallas_call(
        _ln2_kernel,
        out_shape=jax.ShapeDtypeStruct((n, D_MODEL), F32),
        grid=(n // tm,),
        in_specs=[
            pl.BlockSpec((tm, D_MODEL), lambda i: (i, 0)),
            pl.BlockSpec((TOP_K, tm, D_MODEL), lambda i: (0, i, 0)),
            pl.BlockSpec((tm, ROUTER_COLS), lambda i: (i, 0)),
            pl.BlockSpec((1, D_MODEL), lambda i: (0, 0)),
            pl.BlockSpec((1, D_MODEL), lambda i: (0, 0)),
        ],
        out_specs=pl.BlockSpec((tm, D_MODEL), lambda i: (i, 0)),
        compiler_params=pltpu.CompilerParams(
            dimension_semantics=("parallel",), vmem_limit_bytes=VMEM_LIMIT),
        name="combine_ln2",
    )(x1, yg, rt, g2, b2)


def _rotary_tables(seq):
    inv = ROPE_THETA ** (-jnp.arange(0, ROT_DIM, 2, dtype=F32) / ROT_DIM)
    ang = jnp.arange(seq, dtype=F32)[:, None] * inv[None, :]
    cos, sin = jnp.cos(ang), jnp.sin(ang)
    ones = jnp.ones((seq, QK_DIM - ROT_DIM), F32)
    zeros = jnp.zeros((seq, QK_DIM - ROT_DIM), F32)
    zh = jnp.zeros((seq, ROT_HALF), F32)
    cos64 = jnp.concatenate([cos, cos, ones], axis=1)
    sa64 = jnp.concatenate([-sin, zh, zeros], axis=1)
    sb64 = jnp.concatenate([zh, sin, zeros], axis=1)
    rep = LANES // QK_DIM
    return jnp.tile(cos64, (1, rep)), jnp.tile(sa64, (1, rep)), jnp.tile(sb64, (1, rep))


def _slot_layout(rt, counts, tm):
    n = rt.shape[0]
    p = n * TOP_K + N_EXPERTS * tm
    nb = p // tm
    expert = rt[:, RT_EXPERT:RT_EXPERT + TOP_K].astype(jnp.int32)
    rank = rt[:, RT_RANK:RT_RANK + TOP_K].astype(jnp.int32)
    padded = ((counts + tm - 1) // tm) * tm
    pad_ends = jnp.cumsum(padded)
    pad_starts = pad_ends - padded
    dest_t = (pad_starts[expert] + rank).T
    block_start = jnp.arange(nb, dtype=jnp.int32) * tm
    block_e = jnp.minimum(jnp.sum(pad_ends[None, :] <= block_start[:, None], axis=1),
                          N_EXPERTS - 1).astype(jnp.int32)
    n_valid = jnp.clip(pad_starts[block_e] + counts[block_e] - block_start, 0, tm).astype(jnp.int32)
    n_used = (pad_ends[-1] // tm).astype(jnp.int32).reshape(1)
    return dest_t.astype(jnp.int32), block_e, n_valid, n_used, p


def _trunk(x, prm):
    b, s, d = x.shape
    n = b * s
    x2d = x.reshape(n, d)
    cos_t, sa_t, sb_t = _rotary_tables(s)
    h = _inproj(x2d, prm["w_in"], cos_t, sa_t, sb_t, s)
    a = _attention(h.reshape(b, s, IN_COLS), prm["lam"], prm["subln_g"], prm["lam_init"])
    x1, rt, cnt = _mix(a.reshape(n, d), h, x2d, s, prm["b_gate"], prm["conv_w"], prm["w_attn_br"],
                       prm["w_conv_br"], prm["w_o"], prm["ln1_g"], prm["ln1_b"],
                       prm["w_router"], prm["b_router"])
    counts = cnt[0, :N_EXPERTS].astype(jnp.int32)
    dest_t, block_e, n_valid, n_used, p = _slot_layout(rt, counts, MOE_TM)
    xs = _sc_dispatch(x1, dest_t, p)
    ys = _moe_experts(block_e, n_valid, n_used, xs, prm["w_exp_gate"], prm["b_exp_gate"],
                      prm["w_exp_up"], prm["b_exp_up"], prm["w_exp_down"], prm["b_exp_down"])
    yg = _sc_gather(ys, dest_t.reshape(-1)).reshape(TOP_K, n, d)
    y = _combine_ln2(x1, yg, rt, prm["ln2_g"], prm["ln2_b"])
    return y.reshape(b, s, d)


def kernel(x_prompt, x_sample, w_in, b_branch_gate, lambda_q1, lambda_k1, lambda_q2, lambda_k2, subln_g,
           conv_w, w_attn_br, w_conv_br, w_o, ln1_g, ln1_b, w_router, b_router, w_exp_gate, b_exp_gate,
           w_exp_up, b_exp_up, w_exp_down, b_exp_down, ln2_g, ln2_b):
    l = 0
    lam_init = 0.8 - 0.6 * math.exp(-0.3 * l)
    lam = (jnp.exp(jnp.sum(lambda_q1[l].astype(F32) * lambda_k1[l].astype(F32)))
           - jnp.exp(jnp.sum(lambda_q2[l].astype(F32) * lambda_k2[l].astype(F32))) + lam_init)
    prm = {
        "lam_init": lam_init,
        "lam": lam.reshape(1).astype(F32),
        "w_in": w_in[l].astype(BF16),
        "b_gate": b_branch_gate[l].reshape(1, 2 * D_MODEL),
        "subln_g": subln_g[l].reshape(1, V_DIM),
        "conv_w": jnp.pad(conv_w[l], ((0, SUBLANES - conv_w.shape[1]), (0, 0))),
        "w_attn_br": w_attn_br[l].astype(BF16),
        "w_conv_br": w_conv_br[l].astype(BF16),
        "w_o": w_o[l].astype(BF16),
        "ln1_g": ln1_g[l].reshape(1, D_MODEL),
        "ln1_b": ln1_b[l].reshape(1, D_MODEL),
        "w_router": jnp.pad(w_router[l], ((0, 0), (0, ROUTER_COLS - N_EXPERTS))).astype(BF16),
        "b_router": jnp.pad(b_router[l], (0, ROUTER_COLS - N_EXPERTS)).reshape(1, ROUTER_COLS),
        "w_exp_gate": w_exp_gate[l].astype(BF16),
        "b_exp_gate": b_exp_gate[l].reshape(N_EXPERTS, 1, D_MODEL),
        "w_exp_up": w_exp_up[l].astype(BF16),
        "b_exp_up": b_exp_up[l].reshape(N_EXPERTS, 1, D_MODEL),
        "w_exp_down": w_exp_down[l].astype(BF16),
        "b_exp_down": b_exp_down[l].reshape(N_EXPERTS, 1, D_MODEL),
        "ln2_g": ln2_g[l].reshape(1, D_MODEL),
        "ln2_b": ln2_b[l].reshape(1, D_MODEL),
    }
    return _trunk(x_prompt, prm), _trunk(x_sample, prm)
```

```python
import functools
import math

import jax
import jax.numpy as jnp
from jax import lax
from jax.experimental import pallas as pl
from jax.experimental.pallas import tpu as pltpu
from jax.experimental.pallas import tpu_sc as plsc

F32 = jnp.float32
BF16 = jnp.bfloat16

D_MODEL = 1024
N_HEADS = 8
QK_DIM = 64
V_DIM = 128
HEAD_COLS = 2 * QK_DIM
ROT_DIM = QK_DIM // 4
ROT_HALF = ROT_DIM // 2
ROPE_THETA = 500000.0
SUBLN_EPS = 1e-5
IN_COLS = 8192
COL_BLOCK = 1024
CB_BLK, CC_BLK, CX_BLK, GA_BLK, GC_BLK = 3, 4, 5, 6, 7
K_HEAD0, V_HEAD0 = 8, 16
N_EXPERTS = 32
TOP_K = 4
SWIGLU_LIMIT = 7.0
SWIGLU_ALPHA = 1.702
LN_EPS = 1e-5
DEPTH = 1
DEEPNORM_ALPHA = (2 * DEPTH) ** 0.25
LANES = 128
SUBLANES = 8
ROUTER_COLS = LANES
RT_EXPERT, RT_RANK, RT_GATE = 0, 4, 8
SC_WINDOW = 32
ONES_ROWS = 16
Q_SCALE = math.log2(math.e) * QK_DIM ** -0.5

VMEM_LIMIT = 56 * 1024 * 1024

PROJ_TM = 1024
ATTN_TQ = 1024
ATTN_TK = 512
MIX_TM = 256
MOE_TM = 512
LN2_TM = 512


def _inproj_kernel(x_ref, w_ref, cos_ref, sa_ref, sb_ref, o_ref):
    j = pl.program_id(1)
    acc = jnp.dot(x_ref[...].astype(BF16), w_ref[...], preferred_element_type=F32)

    @pl.when(j >= 2)
    def _():
        o_ref[...] = acc.astype(o_ref.dtype)

    @pl.when(j < 2)
    def _():
        scale = jnp.where(j == 0, Q_SCALE, 1.0).astype(F32)
        cos = cos_ref[...] * scale
        sa = sa_ref[...] * scale
        sb = sb_ref[...] * scale
        for c in range(COL_BLOCK // LANES):
            a = acc[:, c * LANES:(c + 1) * LANES]
            r = (a * cos + pltpu.roll(a, LANES - ROT_HALF, 1) * sa + pltpu.roll(a, ROT_HALF, 1) * sb)
            o_ref[:, c * LANES:(c + 1) * LANES] = r.astype(o_ref.dtype)


def _inproj(x2d, w_in_b, cos_t, sa_t, sb_t, seq):
    n = x2d.shape[0]
    tm = PROJ_TM
    pos_blocks = seq // tm
    tab_spec = pl.BlockSpec((tm, LANES), lambda i, j: (i % pos_blocks, 0))
    return pl.pallas_call(
        _inproj_kernel,
        out_shape=jax.ShapeDtypeStruct((n, IN_COLS), BF16),
        grid=(n // tm, IN_COLS // COL_BLOCK),
        in_specs=[
            pl.BlockSpec((tm, D_MODEL), lambda i, j: (i, 0)),
            pl.BlockSpec((D_MODEL, COL_BLOCK), lambda i, j: (0, j)),
            tab_spec, tab_spec, tab_spec,
        ],
        out_specs=pl.BlockSpec((tm, COL_BLOCK), lambda i, j: (i, j)),
        compiler_params=pltpu.CompilerParams(
            dimension_semantics=("parallel", "arbitrary"), vmem_limit_bytes=VMEM_LIMIT),
        name="inproj",
    )(x2d, w_in_b, cos_t, sa_t, sb_t)


def _attn_kernel(lam_ref, q_ref, k_ref, vt_ref, g_ref, o_ref, qt_s, acc_s, sa0, sa1, sb0, sb1, *, out_scale):
    tq = q_ref.shape[0]
    n_chunks, _, tk = vt_ref.shape
    qf = q_ref[...].astype(F32)
    lane = lax.broadcasted_iota(jnp.int32, qf.shape, 1)
    qt_s[0] = jnp.where(lane < QK_DIM, qf, 0.0).T.astype(BF16)
    qt_s[1] = jnp.where(lane >= QK_DIM, qf, 0.0).T.astype(BF16)
    acc_s[...] = jnp.zeros(acc_s.shape, F32)
    sa_s, sb_s = (sa0, sa1), (sb0, sb1)

    def scores(c, s_refs, mp):
        k = k_ref[pl.ds(pl.multiple_of(c * tk, tk), tk), :]
        s = jnp.dot(k, qt_s[mp], preferred_element_type=F32)
        s_refs[mp][...] = s
        return jnp.max(s, axis=0, keepdims=True)

    def update(c, s_refs, mp, m_old, c_max):
        m_new = jnp.maximum(m_old, c_max)
        alpha = jnp.exp2(m_old - m_new)
        p = jnp.exp2(s_refs[mp][...] - m_new).astype(BF16)
        acc_s[mp] = alpha * acc_s[mp] + jnp.dot(vt_ref[c], p, preferred_element_type=F32)
        return m_new

    def step(c, cur, nxt, ms, cur_max, prefetch):
        new_ms, nxt_max = [], []
        for mp in range(2):
            if prefetch:
                nxt_max.append(scores(c + 1, nxt, mp))
            new_ms.append(update(c, cur, mp, ms[mp], cur_max[mp]))
        return tuple(new_ms), tuple(nxt_max)

    def pair(i, carry):
        ms, a_max = carry
        ms, b_max = step(2 * i, sa_s, sb_s, ms, a_max, True)
        return step(2 * i + 1, sb_s, sa_s, ms, b_max, True)

    m0 = jnp.full((1, tq), -jnp.inf, F32)
    a_max = tuple(scores(0, sa_s, mp) for mp in range(2))
    ms, a_max = lax.fori_loop(0, n_chunks // 2 - 1, pair, ((m0, m0), a_max))
    ms, b_max = step(n_chunks - 2, sa_s, sb_s, ms, a_max, True)
    step(n_chunks - 1, sb_s, sa_s, ms, b_max, False)

    lam = lam_ref[0]
    o1 = acc_s[0, :V_DIM, :] / acc_s[0, V_DIM:V_DIM + 1, :]
    o2 = acc_s[1, :V_DIM, :] / acc_s[1, V_DIM:V_DIM + 1, :]
    ot = o1 - lam * o2
    ot = ot * lax.rsqrt(jnp.mean(ot * ot, axis=0, keepdims=True) + SUBLN_EPS)
    o_ref[...] = (ot.T * (g_ref[...] * out_scale)).astype(o_ref.dtype)


def _attention(h3, lam, subln_g, lam_init):
    b, s, _ = h3.shape
    tq, tk = ATTN_TQ, ATTN_TK
    n_chunks = s // tk
    assert n_chunks % 2 == 0 and n_chunks >= 2
    v = h3[:, :, V_HEAD0 * LANES:(V_HEAD0 + N_HEADS) * LANES]
    vt = v.reshape(b, n_chunks, tk, N_HEADS, V_DIM).transpose(0, 3, 1, 4, 2)
    vt = jnp.concatenate([vt, jnp.ones((b, N_HEADS, n_chunks, ONES_ROWS, tk), BF16)], axis=3)
    vt_rows = V_DIM + ONES_ROWS
    kern = functools.partial(_attn_kernel, out_scale=1.0 - lam_init)
    return pl.pallas_call(
        kern,
        out_shape=jax.ShapeDtypeStruct((b, s, N_HEADS * V_DIM), BF16),
        grid=(b, N_HEADS, s // tq),
        in_specs=[
            pl.BlockSpec(memory_space=pltpu.SMEM),
            pl.BlockSpec((None, tq, HEAD_COLS), lambda bi, hi, qi: (bi, qi, hi)),
            pl.BlockSpec((None, s, HEAD_COLS), lambda bi, hi, qi: (bi, 0, K_HEAD0 + hi)),
            pl.BlockSpec((None, None, n_chunks, vt_rows, tk), lambda bi, hi, qi: (bi, hi, 0, 0, 0)),
            pl.BlockSpec((1, V_DIM), lambda bi, hi, qi: (0, 0)),
        ],
        out_specs=pl.BlockSpec((None, tq, V_DIM), lambda bi, hi, qi: (bi, qi, hi)),
        scratch_shapes=[
            pltpu.VMEM((2, HEAD_COLS, tq), BF16),
            pltpu.VMEM((2, vt_rows, tq), F32),
            pltpu.VMEM((tk, tq), F32), pltpu.VMEM((tk, tq), F32),
            pltpu.VMEM((tk, tq), F32), pltpu.VMEM((tk, tq), F32),
        ],
        compiler_params=pltpu.CompilerParams(
            dimension_semantics=("parallel", "parallel", "arbitrary"),
            vmem_limit_bytes=VMEM_LIMIT),
        name="diff_attention",
    )(lam, h3, h3, vt, subln_g)


HALF = D_MODEL // 2


def _pack_row(x):
    return pltpu.pack_elementwise([x[:, :HALF], x[:, HALF:]], packed_dtype=BF16)


def _unpack_row(w):
    lo = pltpu.unpack_elementwise(w, index=0, packed_dtype=BF16, unpacked_dtype=F32)
    hi = pltpu.unpack_elementwise(w, index=1, packed_dtype=BF16, unpacked_dtype=F32)
    return jnp.concatenate([lo, hi], axis=1)


def _layer_norm(y, g, b):
    mu = jnp.mean(y, axis=-1, keepdims=True)
    d = y - mu
    var = jnp.mean(d * d, axis=-1, keepdims=True)
    return d * lax.rsqrt(var + LN_EPS) * g + b


def _mix_kernel(a_ref, cb_ref, cc_ref, cx_ref, ga_ref, gc_ref, ccp_ref, cxp_ref, ccn_ref, cxn_ref,
                x_ref, bg_ref, cw_ref, wa_ref, wc_ref, wo_ref, g1_ref, b1_ref, wr_ref, br_ref, tri_ref,
                x1_ref, x1p_ref, rt_ref, cnt_ref, *, tiles_per_seq):
    i = pl.program_id(0)

    @pl.when(i == 0)
    def _():
        cnt_ref[...] = jnp.zeros(cnt_ref.shape, F32)

    tm = x_ref.shape[0]
    u = cc_ref[...].astype(F32) * cx_ref[...].astype(F32)
    first = (i % tiles_per_seq) == 0
    last = (i % tiles_per_seq) == tiles_per_seq - 1
    up = ccp_ref[SUBLANES - 1:SUBLANES, :].astype(F32) * cxp_ref[SUBLANES - 1:SUBLANES, :].astype(F32)
    un = ccn_ref[0:1, :].astype(F32) * cxn_ref[0:1, :].astype(F32)
    up = jnp.where(first, 0.0, up)
    un = jnp.where(last, 0.0, un)
    row = lax.broadcasted_iota(jnp.int32, u.shape, 0)
    u_prev = jnp.where(row == 0, up, pltpu.roll(u, 1, 0))
    u_next = jnp.where(row == tm - 1, un, pltpu.roll(u, tm - 1, 0))
    conv = cw_ref[0:1, :] * u_prev + cw_ref[1:2, :] * u + cw_ref[2:3, :] * u_next
    c = (cb_ref[...].astype(F32) * conv).astype(BF16)
    g_a = jax.nn.sigmoid(ga_ref[...].astype(F32) + bg_ref[:, :D_MODEL])
    g_c = jax.nn.sigmoid(gc_ref[...].astype(F32) + bg_ref[:, D_MODEL:])
    merged = (g_a * jnp.dot(a_ref[...], wa_ref[...], preferred_element_type=F32)
              + g_c * jnp.dot(c, wc_ref[...], preferred_element_type=F32))
    m = jnp.dot(merged.astype(BF16), wo_ref[...], preferred_element_type=F32)
    x1 = _layer_norm(DEEPNORM_ALPHA * x_ref[...] + m, g1_ref[...], b1_ref[...])
    x1_ref[...] = x1
    x1p_ref[...] = _pack_row(x1)
    logits =jnp.dot(x1.astype(BF16), wr_ref[...], preferred_element_type=F32) + br_ref[...]

    lane = lax.broadcasted_iota(jnp.int32, logits.shape, 1)
    work = jnp.where(lane < N_EXPERTS, logits, -jnp.inf)
    hots, vals, idxs = [], [], []
    for _ in range(TOP_K):
        mx = jnp.max(work, axis=-1, keepdims=True)
        ix = jnp.min(jnp.where(work == mx, lane, ROUTER_COLS), axis=-1, keepdims=True)
        hot = lane == ix
        work = jnp.where(hot, -jnp.inf, work)
        hots.append(hot)
        vals.append(mx)
        idxs.append(ix)
    exps = [jnp.exp(v - vals[0]) for v in vals]
    denom = exps[0] + exps[1] + exps[2] + exps[3]
    picked = jnp.zeros(logits.shape, F32)
    for hot in hots:
        picked = picked + hot.astype(F32)
    before = cnt_ref[...] + jnp.dot(tri_ref[...], picked.astype(BF16), preferred_element_type=F32)
    out = jnp.zeros(logits.shape, F32)
    for j in range(TOP_K):
        rank = jnp.sum(jnp.where(hots[j], before, 0.0), axis=-1, keepdims=True)
        out = jnp.where(lane == RT_EXPERT + j, idxs[j].astype(F32), out)
        out = jnp.where(lane == RT_RANK + j, rank, out)
        out = jnp.where(lane == RT_GATE + j, exps[j] / denom, out)
    rt_ref[...] = out
    cnt_ref[...] = cnt_ref[...] + jnp.sum(picked, axis=0, keepdims=True)


def _mix(a2d, h2d, x2d, seq, bg, cw, wa, wc, wo, g1, b1, wr, br):
    n = x2d.shape[0]
    tm = MIX_TM
    tiles_per_seq = seq // tm
    halo_blocks = tm // SUBLANES
    n_halo = n // SUBLANES

    def col(blk):
        return pl.BlockSpec((tm, COL_BLOCK), lambda i: (i, blk))

    def prev(blk):
        return pl.BlockSpec((SUBLANES, COL_BLOCK), lambda i: (jnp.maximum(i * halo_blocks - 1, 0), blk))

    def nxt(blk):
        return pl.BlockSpec((SUBLANES, COL_BLOCK),
                            lambda i: (jnp.minimum((i + 1) * halo_blocks, n_halo - 1), blk))

    def full(shape):
        return pl.BlockSpec(shape, lambda i: (0,) * len(shape))

    tri = (lax.broadcasted_iota(jnp.int32, (tm, tm), 1)
           < lax.broadcasted_iota(jnp.int32, (tm, tm), 0)).astype(BF16)
    kern = functools.partial(_mix_kernel, tiles_per_seq=tiles_per_seq)
    return pl.pallas_call(
        kern,
        out_shape=(jax.ShapeDtypeStruct((n, D_MODEL), F32),
                   jax.ShapeDtypeStruct((n, HALF), jnp.int32),
                   jax.ShapeDtypeStruct((n, ROUTER_COLS), F32),
                   jax.ShapeDtypeStruct((1, ROUTER_COLS), F32)),
        grid=(n // tm,),
        in_specs=[
            pl.BlockSpec((tm, D_MODEL), lambda i: (i, 0)),
            col(CB_BLK), col(CC_BLK), col(CX_BLK), col(GA_BLK), col(GC_BLK),
            prev(CC_BLK), prev(CX_BLK), nxt(CC_BLK), nxt(CX_BLK),
            pl.BlockSpec((tm, D_MODEL), lambda i: (i, 0)),
            full((1, 2 * D_MODEL)), full((SUBLANES, D_MODEL)),
            full((D_MODEL, D_MODEL)), full((D_MODEL, D_MODEL)), full((D_MODEL, D_MODEL)),
            full((1, D_MODEL)), full((1, D_MODEL)),
            full((D_MODEL, ROUTER_COLS)), full((1, ROUTER_COLS)), full((tm, tm)),
        ],
        out_specs=(pl.BlockSpec((tm, D_MODEL), lambda i: (i, 0)),
                   pl.BlockSpec((tm, HALF), lambda i: (i, 0)),
                   pl.BlockSpec((tm, ROUTER_COLS), lambda i: (i, 0)),
                   pl.BlockSpec((1, ROUTER_COLS), lambda i: (0, 0))),
        compiler_params=pltpu.CompilerParams(
            dimension_semantics=("arbitrary",), vmem_limit_bytes=VMEM_LIMIT),
        name="mix_ln1_router",
    )(a2d, h2d, h2d, h2d, h2d, h2d, h2d, h2d, h2d, h2d, x2d, bg, cw, wa, wc, wo, g1, b1, wr, br, tri)


def _moe_kernel(be_ref, nv_ref, nu_ref, xs_ref, wg_ref, bg_ref, wu_ref, bu_ref, wd_ref, bd_ref, o_ref,
                wg_s, wu_s, wd_s):
    i = pl.program_id(0)
    used = i < nu_ref[0]

    @pl.when(used & ((i == 0) | (be_ref[i] != be_ref[jnp.maximum(i - 1, 0)])))
    def _():
        wg_s[...] = wg_ref[...].astype(BF16)
        wu_s[...] = wu_ref[...].astype(BF16)
        wd_s[...] = wd_ref[...].astype(BF16)

    @pl.when(used)
    def _():
        row = lax.broadcasted_iota(jnp.int32, xs_ref.shape, 0)
        x = _unpack_row(jnp.where(row < nv_ref[i], xs_ref[...], 0)).astype(BF16)
        hg = jnp.minimum(jnp.dot(x, wg_s[...], preferred_element_type=F32) + bg_ref[...], SWIGLU_LIMIT)
        hu = jnp.clip(jnp.dot(x, wu_s[...], preferred_element_type=F32) + bu_ref[...],
                      -SWIGLU_LIMIT, SWIGLU_LIMIT)
        act = hg * jax.nn.sigmoid(SWIGLU_ALPHA * hg) * (hu + 1.0)
        y = jnp.dot(act.astype(BF16), wd_s[...], preferred_element_type=F32) + bd_ref[...]
        o_ref[...] = _pack_row(y)

    @pl.when(jnp.logical_not(used))
    def _():
        o_ref[...] = jnp.zeros(o_ref.shape, o_ref.dtype)


def _moe_experts(block_e, n_valid, n_used, xs, wg, bg, wu, bu, wd, bd):
    p = xs.shape[0]
    tm = MOE_TM
    nb = p // tm

    def row_map(i, be, nv, nu):
        return (jnp.minimum(i, nu[0] - 1), 0)

    def w_map(i, be, nv, nu):
        return (be[jnp.minimum(i, nu[0] - 1)], 0, 0)

    w_spec = pl.BlockSpec((None, D_MODEL, D_MODEL), w_map)
    b_spec = pl.BlockSpec((None, 1, D_MODEL), w_map)
    return pl.pallas_call(
        _moe_kernel,
        out_shape=jax.ShapeDtypeStruct((p, HALF), jnp.int32),
        grid_spec=pltpu.PrefetchScalarGridSpec(
            num_scalar_prefetch=3,
            grid=(nb,),
            in_specs=[
                pl.BlockSpec((tm, HALF), row_map),
                w_spec, b_spec, w_spec, b_spec, w_spec, b_spec,
            ],
            out_specs=pl.BlockSpec((tm, HALF), lambda i, be, nv, nu: (i, 0)),
            scratch_shapes=[pltpu.VMEM((D_MODEL, D_MODEL), BF16)] * 3,
        ),
        compiler_params=pltpu.CompilerParams(
            dimension_semantics=("arbitrary",), vmem_limit_bytes=VMEM_LIMIT),
        name="moe_experts",
    )(block_e, n_valid, n_used, xs, wg, bg, wu, bu, wd, bd)


def _sc_mesh():
    return plsc.VectorSubcoreMesh(core_axis_name="core", subcore_axis_name="subcore")


def _sc_dispatch(x1, dest_t, p):
    n, d = x1.shape
    w = SC_WINDOW
    assert TOP_K * w == LANES
    idx = dest_t.reshape(TOP_K, n // w, w).transpose(1, 0, 2).reshape(n // w, LANES)

    @pl.kernel(out_type=jax.ShapeDtypeStruct((p, d), x1.dtype), mesh=_sc_mesh(), scratch_types=[],
               name="sc_dispatch")
    def run(x_hbm, d_hbm, o_hbm):
        def body(x_vmem, d_vmem):
            for k in range(TOP_K):
                pltpu.sync_copy(x_vmem, o_hbm.at[d_vmem.at[0, pl.ds(k * w, w)]])

        pltpu.emit_pipeline(
            body,
            grid=(n // w,),
            in_specs=[pl.BlockSpec((w, d), lambda i: (i, 0)),
                      pl.BlockSpec((1, LANES), lambda i: (i, 0))],
            out_specs=[],
            core_axis_name=("core", "subcore"),
            dimension_semantics=(pltpu.PARALLEL,),
        )(x_hbm, d_hbm)

    return run(x1, idx)


def _sc_gather(ys, idx):
    r = idx.shape[0]
    d = ys.shape[1]
    w = SC_WINDOW
    idx = jnp.pad(idx.reshape(r // w, w), ((0, 0), (0, LANES - w)))

    @pl.kernel(out_type=jax.ShapeDtypeStruct((r, d), ys.dtype), mesh=_sc_mesh(), scratch_types=[],
               name="sc_gather")
    def run(y_hbm, i_hbm, o_hbm):
        def body(i_vmem, o_vmem):
            pltpu.sync_copy(y_hbm.at[i_vmem.at[0, pl.ds(0, w)]], o_vmem)

        pltpu.emit_pipeline(
            body,
            grid=(r // w,),
            in_specs=[pl.BlockSpec((1, LANES), lambda i: (i, 0))],
            out_specs=[pl.BlockSpec((w, d), lambda i: (i, 0))],
            core_axis_name=("core", "subcore"),
            dimension_semantics=(pltpu.PARALLEL,),
        )(i_hbm, o_hbm)

    return run(ys, idx)


def _ln2_kernel(x1_ref, yg_ref, rt_ref, g_ref, b_ref, o_ref):
    f = jnp.zeros(x1_ref.shape, F32)
    for k in range(TOP_K):
        f = f + _unpack_row(yg_ref[k]) * rt_ref[:, RT_GATE + k:RT_GATE + k + 1]
    o_ref[...] = _layer_norm(DEEPNORM_ALPHA * x1_ref[...] + f, g_ref[...], b_ref[...])


def _combine_ln2(x1, yg, rt, g2, b2):
    n = x1.shape[0]
    tm = LN2_TM
    return pl.pallas_call(
        _ln2_kernel,
        out_shape=jax.ShapeDtypeStruct((n, D_MODEL), F32),
        grid=(n // tm,),
        in_specs=[
            pl.BlockSpec((tm, D_MODEL), lambda i: (i, 0)),
            pl.BlockSpec((TOP_K, tm, HALF), lambda i: (0, i, 0)),
            pl.BlockSpec((tm, ROUTER_COLS), lambda i: (i, 0)),
            pl.BlockSpec((1, D_MODEL), lambda i: (0, 0)),
            pl.BlockSpec((1, D_MODEL), lambda i: (0, 0)),
        ],
        out_specs=pl.BlockSpec((tm, D_MODEL), lambda i: (i, 0)),
        compiler_params=pltpu.CompilerParams(
            dimension_semantics=("parallel",), vmem_limit_bytes=VMEM_LIMIT),
        name="combine_ln2",
    )(x1, yg, rt, g2, b2)


def _rotary_tables(seq):
    inv = ROPE_THETA ** (-jnp.arange(0, ROT_DIM, 2, dtype=F32) / ROT_DIM)
    ang = jnp.arange(seq, dtype=F32)[:, None] * inv[None, :]
    cos, sin = jnp.cos(ang), jnp.sin(ang)
    ones = jnp.ones((seq, QK_DIM - ROT_DIM), F32)
    zeros = jnp.zeros((seq, QK_DIM - ROT_DIM), F32)
    zh = jnp.zeros((seq, ROT_HALF), F32)
    cos64 = jnp.concatenate([cos, cos, ones], axis=1)
    sa64 = jnp.concatenate([-sin, zh, zeros], axis=1)
    sb64 = jnp.concatenate([zh, sin, zeros], axis=1)
    rep = LANES // QK_DIM
    return jnp.tile(cos64, (1, rep)), jnp.tile(sa64, (1, rep)), jnp.tile(sb64, (1, rep))


def _slot_layout(rt, counts, tm):
    n = rt.shape[0]
    p = n * TOP_K + N_EXPERTS * tm
    nb = p // tm
    expert = rt[:, RT_EXPERT:RT_EXPERT + TOP_K].astype(jnp.int32)
    rank = rt[:, RT_RANK:RT_RANK + TOP_K].astype(jnp.int32)
    padded = ((counts + tm - 1) // tm) * tm
    pad_ends = jnp.cumsum(padded)
    pad_starts = pad_ends - padded
    dest_t = (pad_starts[expert] + rank).T
    block_start = jnp.arange(nb, dtype=jnp.int32) * tm
    block_e = jnp.minimum(jnp.sum(pad_ends[None, :] <= block_start[:, None], axis=1),
                          N_EXPERTS - 1).astype(jnp.int32)
    n_valid = jnp.clip(pad_starts[block_e] + counts[block_e] - block_start, 0, tm).astype(jnp.int32)
    n_used = (pad_ends[-1] // tm).astype(jnp.int32).reshape(1)
    return dest_t.astype(jnp.int32), block_e, n_valid, n_used, p


def _trunk(x, prm):
    b, s, d = x.shape
    n = b * s
    x2d = x.reshape(n, d)
    cos_t, sa_t, sb_t = _rotary_tables(s)
    h = _inproj(x2d, prm["w_in"], cos_t, sa_t, sb_t, s)
    a = _attention(h.reshape(b, s, IN_COLS), prm["lam"], prm["subln_g"], prm["lam_init"])
    x1, x1p, rt, cnt = _mix(a.reshape(n, d), h, x2d, s, prm["b_gate"], prm["conv_w"], prm["w_attn_br"],
                            prm["w_conv_br"], prm["w_o"], prm["ln1_g"], prm["ln1_b"],
                            prm["w_router"], prm["b_router"])
    counts = cnt[0, :N_EXPERTS].astype(jnp.int32)
    dest_t, block_e, n_valid, n_used, p = _slot_layout(rt, counts, MOE_TM)
    xs = _sc_dispatch(x1p, dest_t, p)
    ys = _moe_experts(block_e, n_valid, n_used, xs, prm["w_exp_gate"], prm["b_exp_gate"],
                      prm["w_exp_up"], prm["b_exp_up"], prm["w_exp_down"], prm["b_exp_down"])
    yg = _sc_gather(ys, dest_t.reshape(-1)).reshape(TOP_K, n, HALF)
    y = _combine_ln2(x1, yg, rt, prm["ln2_g"], prm["ln2_b"])
    return y.reshape(b, s, d)


def kernel(x_prompt, x_sample, w_in, b_branch_gate, lambda_q1, lambda_k1, lambda_q2, lambda_k2, subln_g,
           conv_w, w_attn_br, w_conv_br, w_o, ln1_g, ln1_b, w_router, b_router, w_exp_gate, b_exp_gate,
           w_exp_up, b_exp_up, w_exp_down, b_exp_down, ln2_g, ln2_b):
    l = 0
    lam_init = 0.8 - 0.6 * math.exp(-0.3 * l)
    lam = (jnp.exp(jnp.sum(lambda_q1[l].astype(F32) * lambda_k1[l].astype(F32)))
           - jnp.exp(jnp.sum(lambda_q2[l].astype(F32) * lambda_k2[l].astype(F32))) + lam_init)
    prm = {
        "lam_init": lam_init,
        "lam": lam.reshape(1).astype(F32),
        "w_in": w_in[l].astype(BF16),
        "b_gate": b_branch_gate[l].reshape(1, 2 * D_MODEL),
        "subln_g": subln_g[l].reshape(1, V_DIM),
        "conv_w": jnp.pad(conv_w[l], ((0, SUBLANES - conv_w.shape[1]), (0, 0))),
        "w_attn_br": w_attn_br[l].astype(BF16),
        "w_conv_br": w_conv_br[l].astype(BF16),
        "w_o": w_o[l].astype(BF16),
        "ln1_g": ln1_g[l].reshape(1, D_MODEL),
        "ln1_b": ln1_b[l].reshape(1, D_MODEL),
        "w_router": jnp.pad(w_router[l], ((0, 0), (0, ROUTER_COLS - N_EXPERTS))).astype(BF16),
        "b_router": jnp.pad(b_router[l], (0, ROUTER_COLS - N_EXPERTS)).reshape(1, ROUTER_COLS),
        "w_exp_gate": w_exp_gate[l],
        "b_exp_gate": b_exp_gate[l].reshape(N_EXPERTS, 1, D_MODEL),
        "w_exp_up": w_exp_up[l],
        "b_exp_up": b_exp_up[l].reshape(N_EXPERTS, 1, D_MODEL),
        "w_exp_down": w_exp_down[l],
        "b_exp_down": b_exp_down[l].reshape(N_EXPERTS, 1, D_MODEL),
        "ln2_g": ln2_g[l].reshape(1, D_MODEL),
        "ln2_b": ln2_b[l].reshape(1, D_MODEL),
    }
    return _trunk(x_prompt, prm), _trunk(x_sample, prm)
```

```python
import functools
import math

import jax
import jax.numpy as jnp
from jax import lax
from jax.experimental import pallas as pl
from jax.experimental.pallas import tpu as pltpu
from jax.experimental.pallas import tpu_sc as plsc

F32 = jnp.float32
BF16 = jnp.bfloat16

D_MODEL = 1024
N_HEADS = 8
QK_DIM = 64
V_DIM = 128
HEAD_COLS = 2 * QK_DIM
ROT_DIM = QK_DIM // 4
ROT_HALF = ROT_DIM // 2
ROPE_THETA = 500000.0
SUBLN_EPS = 1e-5
IN_COLS = 8192
COL_BLOCK = 1024
V_STEP = 2
H_COLS = IN_COLS - COL_BLOCK
CB_BLK, CC_BLK, CX_BLK, GA_BLK, GC_BLK = 2, 3, 4, 5, 6
K_HEAD0 = 8
N_EXPERTS = 32
TOP_K = 4
SWIGLU_LIMIT = 7.0
SWIGLU_ALPHA = 1.702
LN_EPS = 1e-5
DEPTH = 1
DEEPNORM_ALPHA = (2 * DEPTH) ** 0.25
LANES = 128
SUBLANES = 8
ROUTER_COLS = LANES
RT_EXPERT, RT_RANK, RT_GATE = 0, 4, 8
SC_WINDOW = 32
ONES_ROWS = 16
Q_SCALE = math.log2(math.e) * QK_DIM ** -0.5

VMEM_LIMIT = 56 * 1024 * 1024

PROJ_TM = 1024
ATTN_TQ = 1024
ATTN_TK = 512
MIX_TM = 256
MOE_TM = 512
LN2_TM = 512


def _inproj_kernel(x_ref, w_ref, wvt_ref, cos_ref, sa_ref, sb_ref, o_ref, vt_ref):
    j = pl.program_id(1)
    xb = x_ref[...].astype(BF16)

    @pl.when(j == V_STEP)
    def _():
        acc_t = lax.dot_general(wvt_ref[...], xb, (((1,), (1,)), ((), ())), preferred_element_type=F32)
        n_heads, n_chunks, _, tk = vt_ref.shape
        for hh in range(n_heads):
            for cc in range(n_chunks):
                vt_ref[hh, cc, :V_DIM, :] = acc_t[hh * V_DIM:(hh + 1) * V_DIM,
                                                  cc * tk:(cc + 1) * tk].astype(BF16)
                vt_ref[hh, cc, V_DIM:, :] = jnp.ones((ONES_ROWS, tk), BF16)

    @pl.when(j > V_STEP)
    def _():
        o_ref[...] = jnp.dot(xb, w_ref[...], preferred_element_type=F32).astype(o_ref.dtype)

    @pl.when(j < V_STEP)
    def _():
        acc = jnp.dot(xb, w_ref[...], preferred_element_type=F32)
        scale = jnp.where(j == 0, Q_SCALE, 1.0).astype(F32)
        cos = cos_ref[...] * scale
        sa = sa_ref[...] * scale
        sb = sb_ref[...] * scale
        for c in range(COL_BLOCK // LANES):
            a = acc[:, c * LANES:(c + 1) * LANES]
            r = (a * cos + pltpu.roll(a, LANES - ROT_HALF, 1) * sa + pltpu.roll(a, ROT_HALF, 1) * sb)
            o_ref[:, c * LANES:(c + 1) * LANES] = r.astype(o_ref.dtype)


def _inproj(x2d, w_in_b, wvt_b, cos_t, sa_t, sb_t, batch, seq):
    n = x2d.shape[0]
    tm, tk = PROJ_TM, ATTN_TK
    pos_blocks = seq // tm
    tab_spec = pl.BlockSpec((tm, LANES), lambda i, j: (i % pos_blocks, 0))
    return pl.pallas_call(
        _inproj_kernel,
        out_shape=(jax.ShapeDtypeStruct((n, H_COLS), BF16),
                   jax.ShapeDtypeStruct((batch, N_HEADS, seq // tk, V_DIM + ONES_ROWS, tk), BF16)),
        grid=(n // tm, IN_COLS // COL_BLOCK),
        in_specs=[
            pl.BlockSpec((tm, D_MODEL), lambda i, j: (i, 0)),
            pl.BlockSpec((D_MODEL, COL_BLOCK), lambda i, j: (0, jnp.where(j == V_STEP, j - 1, j))),
            pl.BlockSpec((COL_BLOCK, D_MODEL), lambda i, j: (0, 0)),
            tab_spec, tab_spec, tab_spec,
        ],
        out_specs=(
            pl.BlockSpec((tm, COL_BLOCK), lambda i, j: (i, jnp.where(j >= V_STEP, j - 1, j))),
            pl.BlockSpec((None, N_HEADS, tm // tk, V_DIM + ONES_ROWS, tk),
                         lambda i, j: (i // pos_blocks, 0, i % pos_blocks, 0, 0)),
        ),
        compiler_params=pltpu.CompilerParams(
            dimension_semantics=("parallel", "arbitrary"), vmem_limit_bytes=VMEM_LIMIT),
        name="inproj",
    )(x2d, w_in_b, wvt_b, cos_t, sa_t, sb_t)


def _attn_kernel(lam_ref, q_ref, k_ref, vt_ref, g_ref, o_ref, qt_s, acc_s, sa0, sa1, sb0, sb1, *, out_scale):
    tq = q_ref.shape[0]
    n_chunks, _, tk = vt_ref.shape
    qf = q_ref[...].astype(F32)
    lane = lax.broadcasted_iota(jnp.int32, qf.shape, 1)
    qt_s[0] = jnp.where(lane < QK_DIM, qf, 0.0).T.astype(BF16)
    qt_s[1] = jnp.where(lane >= QK_DIM, qf, 0.0).T.astype(BF16)
    acc_s[...] = jnp.zeros(acc_s.shape, F32)
    sa_s, sb_s = (sa0, sa1), (sb0, sb1)

    def scores(c, s_refs, mp):
        k = k_ref[pl.ds(pl.multiple_of(c * tk, tk), tk), :]
        s = jnp.dot(k, qt_s[mp], preferred_element_type=F32)
        s_refs[mp][...] = s
        return jnp.max(s, axis=0, keepdims=True)

    def update(c, s_refs, mp, m_old, c_max):
        m_new = jnp.maximum(m_old, c_max)
        alpha = jnp.exp2(m_old - m_new)
        p = jnp.exp2(s_refs[mp][...] - m_new).astype(BF16)
        acc_s[mp] = alpha * acc_s[mp] + jnp.dot(vt_ref[c], p, preferred_element_type=F32)
        return m_new

    def step(c, cur, nxt, ms, cur_max, prefetch):
        new_ms, nxt_max = [], []
        for mp in range(2):
            if prefetch:
                nxt_max.append(scores(c + 1, nxt, mp))
            new_ms.append(update(c, cur, mp, ms[mp], cur_max[mp]))
        return tuple(new_ms), tuple(nxt_max)

    def pair(i, carry):
        ms, a_max = carry
        ms, b_max = step(2 * i, sa_s, sb_s, ms, a_max, True)
        return step(2 * i + 1, sb_s, sa_s, ms, b_max, True)

    m0 = jnp.full((1, tq), -jnp.inf, F32)
    a_max = tuple(scores(0, sa_s, mp) for mp in range(2))
    ms, a_max = lax.fori_loop(0, n_chunks // 2 - 1, pair, ((m0, m0), a_max))
    ms, b_max = step(n_chunks - 2, sa_s, sb_s, ms, a_max, True)
    step(n_chunks - 1, sb_s, sa_s, ms, b_max, False)

    lam = lam_ref[0]
    o1 = acc_s[0, :V_DIM, :] / acc_s[0, V_DIM:V_DIM + 1, :]
    o2 = acc_s[1, :V_DIM, :] / acc_s[1, V_DIM:V_DIM + 1, :]
    ot = o1 - lam * o2
    ot = ot * lax.rsqrt(jnp.mean(ot * ot, axis=0, keepdims=True) + SUBLN_EPS)
    o_ref[...] = (ot.T * (g_ref[...] * out_scale)).astype(o_ref.dtype)


def _attention(h3, vt, lam, subln_g, lam_init):
    b, s, _ = h3.shape
    tq, tk = ATTN_TQ, ATTN_TK
    n_chunks = s // tk
    assert n_chunks % 2 == 0 and n_chunks >= 2
    vt_rows = V_DIM + ONES_ROWS
    kern = functools.partial(_attn_kernel, out_scale=1.0 - lam_init)
    return pl.pallas_call(
        kern,
        out_shape=jax.ShapeDtypeStruct((b, s, N_HEADS * V_DIM), BF16),
        grid=(b, N_HEADS, s // tq),
        in_specs=[
            pl.BlockSpec(memory_space=pltpu.SMEM),
            pl.BlockSpec((None, tq, HEAD_COLS), lambda bi, hi, qi: (bi, qi, hi)),
            pl.BlockSpec((None, s, HEAD_COLS), lambda bi, hi, qi: (bi, 0, K_HEAD0 + hi)),
            pl.BlockSpec((None, None, n_chunks, vt_rows, tk), lambda bi, hi, qi: (bi, hi, 0, 0, 0)),
            pl.BlockSpec((1, V_DIM), lambda bi, hi, qi: (0, 0)),
        ],
        out_specs=pl.BlockSpec((None, tq, V_DIM), lambda bi, hi, qi: (bi, qi, hi)),
        scratch_shapes=[
            pltpu.VMEM((2, HEAD_COLS, tq), BF16),
            pltpu.VMEM((2, vt_rows, tq), F32),
            pltpu.VMEM((tk, tq), F32), pltpu.VMEM((tk, tq), F32),
            pltpu.VMEM((tk, tq), F32), pltpu.VMEM((tk, tq), F32),
        ],
        compiler_params=pltpu.CompilerParams(
            dimension_semantics=("parallel", "parallel", "arbitrary"),
            vmem_limit_bytes=VMEM_LIMIT),
        name="diff_attention",
    )(lam, h3, h3, vt, subln_g)


HALF = D_MODEL // 2


def _pack_row(x):
    return pltpu.pack_elementwise([x[:, :HALF], x[:, HALF:]], packed_dtype=BF16)


def _unpack_row(w):
    lo = pltpu.unpack_elementwise(w, index=0, packed_dtype=BF16, unpacked_dtype=F32)
    hi = pltpu.unpack_elementwise(w, index=1, packed_dtype=BF16, unpacked_dtype=F32)
    return jnp.concatenate([lo, hi], axis=1)


def _layer_norm(y, g, b):
    mu = jnp.mean(y, axis=-1, keepdims=True)
    d = y - mu
    var = jnp.mean(d * d, axis=-1, keepdims=True)
    return d * lax.rsqrt(var + LN_EPS) * g + b


def _mix_kernel(a_ref, cb_ref, cc_ref, cx_ref, ga_ref, gc_ref, ccp_ref, cxp_ref, ccn_ref, cxn_ref,
                x_ref, bg_ref, cw_ref, wa_ref, wc_ref, wo_ref, g1_ref, b1_ref, wr_ref, br_ref, tri_ref,
                x1_ref, x1p_ref, rt_ref, cnt_ref, *, tiles_per_seq):
    i = pl.program_id(0)

    @pl.when(i == 0)
    def _():
        cnt_ref[...] = jnp.zeros(cnt_ref.shape, F32)

    tm = x_ref.shape[0]
    u = cc_ref[...].astype(F32) * cx_ref[...].astype(F32)
    first = (i % tiles_per_seq) == 0
    last = (i % tiles_per_seq) == tiles_per_seq - 1
    up = ccp_ref[SUBLANES - 1:SUBLANES, :].astype(F32) * cxp_ref[SUBLANES - 1:SUBLANES, :].astype(F32)
    un = ccn_ref[0:1, :].astype(F32) * cxn_ref[0:1, :].astype(F32)
    up = jnp.where(first, 0.0, up)
    un = jnp.where(last, 0.0, un)
    row = lax.broadcasted_iota(jnp.int32, u.shape, 0)
    u_prev = jnp.where(row == 0, up, pltpu.roll(u, 1, 0))
    u_next = jnp.where(row == tm - 1, un, pltpu.roll(u, tm - 1, 0))
    conv = cw_ref[0:1, :] * u_prev + cw_ref[1:2, :] * u + cw_ref[2:3, :] * u_next
    c = (cb_ref[...].astype(F32) * conv).astype(BF16)
    g_a = jax.nn.sigmoid(ga_ref[...].astype(F32) + bg_ref[:, :D_MODEL])
    g_c = jax.nn.sigmoid(gc_ref[...].astype(F32) + bg_ref[:, D_MODEL:])
    merged = (g_a * jnp.dot(a_ref[...], wa_ref[...], preferred_element_type=F32)
              + g_c * jnp.dot(c, wc_ref[...], preferred_element_type=F32))
    m = jnp.dot(merged.astype(BF16), wo_ref[...], preferred_element_type=F32)
    x1 = _layer_norm(DEEPNORM_ALPHA * x_ref[...] + m, g1_ref[...], b1_ref[...])
    x1_ref[...] = x1
    x1p_ref[...] = _pack_row(x1)
    logits =jnp.dot(x1.astype(BF16), wr_ref[...], preferred_element_type=F32) + br_ref[...]

    lane = lax.broadcasted_iota(jnp.int32, logits.shape, 1)
    lane_f = lane.astype(F32)
    work = jnp.where(lane < N_EXPERTS, logits, -jnp.inf)
    hots, vals, idxs = [], [], []
    for _ in range(TOP_K):
        mx = jnp.max(work, axis=-1, keepdims=True)
        ix = jnp.min(jnp.where(work == mx, lane_f, float(ROUTER_COLS)), axis=-1, keepdims=True)
        hot = lane_f == ix
        work = jnp.where(hot, -jnp.inf, work)
        hots.append(hot)
        vals.append(mx)
        idxs.append(ix)
    exps = [jnp.exp(v - vals[0]) for v in vals]
    denom = exps[0] + exps[1] + exps[2] + exps[3]
    picked = jnp.zeros(logits.shape, F32)
    for hot in hots:
        picked = picked + hot.astype(F32)
    before = cnt_ref[...] + jnp.dot(tri_ref[...], picked.astype(BF16), preferred_element_type=F32)
    out = jnp.zeros(logits.shape, F32)
    for j in range(TOP_K):
        rank = jnp.sum(jnp.where(hots[j], before, 0.0), axis=-1, keepdims=True)
        out = jnp.where(lane == RT_EXPERT + j, idxs[j], out)
        out = jnp.where(lane == RT_RANK + j, rank, out)
        out = jnp.where(lane == RT_GATE + j, exps[j] / denom, out)
    rt_ref[...] = out
    cnt_ref[...] = cnt_ref[...] + jnp.sum(picked, axis=0, keepdims=True)


def _mix(a2d, h2d, x2d, seq, bg, cw, wa, wc, wo, g1, b1, wr, br):
    n = x2d.shape[0]
    tm = MIX_TM
    tiles_per_seq = seq // tm
    halo_blocks = tm // SUBLANES
    n_halo = n // SUBLANES

    def col(blk):
        return pl.BlockSpec((tm, COL_BLOCK), lambda i: (i, blk))

    def prev(blk):
        return pl.BlockSpec((SUBLANES, COL_BLOCK), lambda i: (jnp.maximum(i * halo_blocks - 1, 0), blk))

    def nxt(blk):
        return pl.BlockSpec((SUBLANES, COL_BLOCK),
                            lambda i: (jnp.minimum((i + 1) * halo_blocks, n_halo - 1), blk))

    def full(shape):
        return pl.BlockSpec(shape, lambda i: (0,) * len(shape))

    tri = (lax.broadcasted_iota(jnp.int32, (tm, tm), 1)
           < lax.broadcasted_iota(jnp.int32, (tm, tm), 0)).astype(BF16)
    kern = functools.partial(_mix_kernel, tiles_per_seq=tiles_per_seq)
    return pl.pallas_call(
        kern,
        out_shape=(jax.ShapeDtypeStruct((n, D_MODEL), F32),
                   jax.ShapeDtypeStruct((n, HALF), jnp.int32),
                   jax.ShapeDtypeStruct((n, ROUTER_COLS), F32),
                   jax.ShapeDtypeStruct((1, ROUTER_COLS), F32)),
        grid=(n // tm,),
        in_specs=[
            pl.BlockSpec((tm, D_MODEL), lambda i: (i, 0)),
            col(CB_BLK), col(CC_BLK), col(CX_BLK), col(GA_BLK), col(GC_BLK),
            prev(CC_BLK), prev(CX_BLK), nxt(CC_BLK), nxt(CX_BLK),
            pl.BlockSpec((tm, D_MODEL), lambda i: (i, 0)),
            full((1, 2 * D_MODEL)), full((SUBLANES, D_MODEL)),
            full((D_MODEL, D_MODEL)), full((D_MODEL, D_MODEL)), full((D_MODEL, D_MODEL)),
            full((1, D_MODEL)), full((1, D_MODEL)),
            full((D_MODEL, ROUTER_COLS)), full((1, ROUTER_COLS)), full((tm, tm)),
        ],
        out_specs=(pl.BlockSpec((tm, D_MODEL), lambda i: (i, 0)),
                   pl.BlockSpec((tm, HALF), lambda i: (i, 0)),
                   pl.BlockSpec((tm, ROUTER_COLS), lambda i: (i, 0)),
                   pl.BlockSpec((1, ROUTER_COLS), lambda i: (0, 0))),
        compiler_params=pltpu.CompilerParams(
            dimension_semantics=("arbitrary",), vmem_limit_bytes=VMEM_LIMIT),
        name="mix_ln1_router",
    )(a2d, h2d, h2d, h2d, h2d, h2d, h2d, h2d, h2d, h2d, x2d, bg, cw, wa, wc, wo, g1, b1, wr, br, tri)


def _moe_kernel(be_ref, nv_ref, nu_ref, xs_ref, wg_ref, bg_ref, wu_ref, bu_ref, wd_ref, bd_ref, o_ref,
                wg_s, wu_s, wd_s):
    i = pl.program_id(0)
    used = i < nu_ref[0]

    @pl.when(used & ((i == 0) | (be_ref[i] != be_ref[jnp.maximum(i - 1, 0)])))
    def _():
        wg_s[...] = wg_ref[...].astype(BF16)
        wu_s[...] = wu_ref[...].astype(BF16)
        wd_s[...] = wd_ref[...].astype(BF16)

    @pl.when(used)
    def _():
        row = lax.broadcasted_iota(jnp.int32, xs_ref.shape, 0)
        x = _unpack_row(jnp.where(row < nv_ref[i], xs_ref[...], 0)).astype(BF16)
        hg = jnp.minimum(jnp.dot(x, wg_s[...], preferred_element_type=F32) + bg_ref[...], SWIGLU_LIMIT)
        hu = jnp.clip(jnp.dot(x, wu_s[...], preferred_element_type=F32) + bu_ref[...],
                      -SWIGLU_LIMIT, SWIGLU_LIMIT)
        act = hg * jax.nn.sigmoid(SWIGLU_ALPHA * hg) * (hu + 1.0)
        y = jnp.dot(act.astype(BF16), wd_s[...], preferred_element_type=F32) + bd_ref[...]
        o_ref[...] = _pack_row(y)

    @pl.when(jnp.logical_not(used))
    def _():
        o_ref[...] = jnp.zeros(o_ref.shape, o_ref.dtype)


def _moe_experts(block_e, n_valid, n_used, xs, wg, bg, wu, bu, wd, bd):
    p = xs.shape[0]
    tm = MOE_TM
    nb = p // tm

    def row_map(i, be, nv, nu):
        return (jnp.minimum(i, nu[0] - 1), 0)

    def w_map(i, be, nv, nu):
        return (be[jnp.minimum(i, nu[0] - 1)], 0, 0)

    w_spec = pl.BlockSpec((None, D_MODEL, D_MODEL), w_map)
    b_spec = pl.BlockSpec((None, 1, D_MODEL), w_map)
    return pl.pallas_call(
        _moe_kernel,
        out_shape=jax.ShapeDtypeStruct((p, HALF), jnp.int32),
        grid_spec=pltpu.PrefetchScalarGridSpec(
            num_scalar_prefetch=3,
            grid=(nb,),
            in_specs=[
                pl.BlockSpec((tm, HALF), row_map),
                w_spec, b_spec, w_spec, b_spec, w_spec, b_spec,
            ],
            out_specs=pl.BlockSpec((tm, HALF), lambda i, be, nv, nu: (i, 0)),
            scratch_shapes=[pltpu.VMEM((D_MODEL, D_MODEL), BF16)] * 3,
        ),
        compiler_params=pltpu.CompilerParams(
            dimension_semantics=("arbitrary",), vmem_limit_bytes=VMEM_LIMIT),
        name="moe_experts",
    )(block_e, n_valid, n_used, xs, wg, bg, wu, bu, wd, bd)


def _sc_mesh():
    return plsc.VectorSubcoreMesh(core_axis_name="core", subcore_axis_name="subcore")


def _sc_dispatch(x1, dest_t, p):
    n, d = x1.shape
    w = SC_WINDOW
    assert TOP_K * w == LANES
    idx = dest_t.reshape(TOP_K, n // w, w).transpose(1, 0, 2).reshape(n // w, LANES)

    @pl.kernel(out_type=jax.ShapeDtypeStruct((p, d), x1.dtype), mesh=_sc_mesh(), scratch_types=[],
               name="sc_dispatch")
    def run(x_hbm, d_hbm, o_hbm):
        def body(x_vmem, d_vmem):
            for k in range(TOP_K):
                pltpu.sync_copy(x_vmem, o_hbm.at[d_vmem.at[0, pl.ds(k * w, w)]])

        pltpu.emit_pipeline(
            body,
            grid=(n // w,),
            in_specs=[pl.BlockSpec((w, d), lambda i: (i, 0)),
                      pl.BlockSpec((1, LANES), lambda i: (i, 0))],
            out_specs=[],
            core_axis_name=("core", "subcore"),
            dimension_semantics=(pltpu.PARALLEL,),
        )(x_hbm, d_hbm)

    return run(x1, idx)


def _sc_gather(ys, idx):
    r = idx.shape[0]
    d = ys.shape[1]
    w = SC_WINDOW
    idx = jnp.pad(idx.reshape(r // w, w), ((0, 0), (0, LANES - w)))

    @pl.kernel(out_type=jax.ShapeDtypeStruct((r, d), ys.dtype), mesh=_sc_mesh(), scratch_types=[],
               name="sc_gather")
    def run(y_hbm, i_hbm, o_hbm):
        def body(i_vmem, o_vmem):
            pltpu.sync_copy(y_hbm.at[i_vmem.at[0, pl.ds(0, w)]], o_vmem)

        pltpu.emit_pipeline(
            body,
            grid=(r // w,),
            in_specs=[pl.BlockSpec((1, LANES), lambda i: (i, 0))],
            out_specs=[pl.BlockSpec((w, d), lambda i: (i, 0))],
            core_axis_name=("core", "subcore"),
            dimension_semantics=(pltpu.PARALLEL,),
        )(i_hbm, o_hbm)

    return run(ys, idx)


def _ln2_kernel(x1_ref, yg_ref, rt_ref, g_ref, b_ref, o_ref):
    f = jnp.zeros(x1_ref.shape, F32)
    for k in range(TOP_K):
        f = f + _unpack_row(yg_ref[k]) * rt_ref[:, RT_GATE + k:RT_GATE + k + 1]
    o_ref[...] = _layer_norm(DEEPNORM_ALPHA * x1_ref[...] + f, g_ref[...], b_ref[...])


def _combine_ln2(x1, yg, rt, g2, b2):
    n = x1.shape[0]
    tm = LN2_TM
    return pl.pallas_call(
        _ln2_kernel,
        out_shape=jax.ShapeDtypeStruct((n, D_MODEL), F32),
        grid=(n // tm,),
        in_specs=[
            pl.BlockSpec((tm, D_MODEL), lambda i: (i, 0)),
            pl.BlockSpec((TOP_K, tm, HALF), lambda i: (0, i, 0)),
            pl.BlockSpec((tm, ROUTER_COLS), lambda i: (i, 0)),
            pl.BlockSpec((1, D_MODEL), lambda i: (0, 0)),
            pl.BlockSpec((1, D_MODEL), lambda i: (0, 0)),
        ],
        out_specs=pl.BlockSpec((tm, D_MODEL), lambda i: (i, 0)),
        compiler_params=pltpu.CompilerParams(
            dimension_semantics=("parallel",), vmem_limit_bytes=VMEM_LIMIT),
        name="combine_ln2",
    )(x1, yg, rt, g2, b2)


def _rotary_tables(seq):
    inv = ROPE_THETA ** (-jnp.arange(0, ROT_DIM, 2, dtype=F32) / ROT_DIM)
    ang = jnp.arange(seq, dtype=F32)[:, None] * inv[None, :]
    cos, sin = jnp.cos(ang), jnp.sin(ang)
    ones = jnp.ones((seq, QK_DIM - ROT_DIM), F32)
    zeros = jnp.zeros((seq, QK_DIM - ROT_DIM), F32)
    zh = jnp.zeros((seq, ROT_HALF), F32)
    cos64 = jnp.concatenate([cos, cos, ones], axis=1)
    sa64 = jnp.concatenate([-sin, zh, zeros], axis=1)
    sb64 = jnp.concatenate([zh, sin, zeros], axis=1)
    rep = LANES // QK_DIM
    return jnp.tile(cos64, (1, rep)), jnp.tile(sa64, (1, rep)), jnp.tile(sb64, (1, rep))


def _slot_layout(rt, counts, tm):
    n = rt.shape[0]
    p = n * TOP_K + N_EXPERTS * tm
    nb = p // tm
    expert = rt[:, RT_EXPERT:RT_EXPERT + TOP_K].astype(jnp.int32)
    rank = rt[:, RT_RANK:RT_RANK + TOP_K].astype(jnp.int32)
    padded = ((counts + tm - 1) // tm) * tm
    pad_ends = jnp.cumsum(padded)
    pad_starts = pad_ends - padded
    dest_t = (pad_starts[expert] + rank).T
    block_start = jnp.arange(nb, dtype=jnp.int32) * tm
    block_e = jnp.minimum(jnp.sum(pad_ends[None, :] <= block_start[:, None], axis=1),
                          N_EXPERTS - 1).astype(jnp.int32)
    n_valid = jnp.clip(pad_starts[block_e] + counts[block_e] - block_start, 0, tm).astype(jnp.int32)
    n_used = (pad_ends[-1] // tm).astype(jnp.int32).reshape(1)
    return dest_t.astype(jnp.int32), block_e, n_valid, n_used, p


def _trunk(x, prm):
    b, s, d = x.shape
    n = b * s
    x2d = x.reshape(n, d)
    cos_t, sa_t, sb_t = _rotary_tables(s)
    h, vt = _inproj(x2d, prm["w_in"], prm["w_v_t"], cos_t, sa_t, sb_t, b, s)
    a = _attention(h.reshape(b, s, H_COLS), vt, prm["lam"], prm["subln_g"], prm["lam_init"])
    x1, x1p, rt, cnt = _mix(a.reshape(n, d), h, x2d, s, prm["b_gate"], prm["conv_w"], prm["w_attn_br"],
                            prm["w_conv_br"], prm["w_o"], prm["ln1_g"], prm["ln1_b"],
                            prm["w_router"], prm["b_router"])
    counts = cnt[0, :N_EXPERTS].astype(jnp.int32)
    dest_t, block_e, n_valid, n_used, p = _slot_layout(rt, counts, MOE_TM)
    xs = _sc_dispatch(x1p, dest_t, p)
    ys = _moe_experts(block_e, n_valid, n_used, xs, prm["w_exp_gate"], prm["b_exp_gate"],
                      prm["w_exp_up"], prm["b_exp_up"], prm["w_exp_down"], prm["b_exp_down"])
    yg = _sc_gather(ys, dest_t.reshape(-1)).reshape(TOP_K, n, HALF)
    y = _combine_ln2(x1, yg, rt, prm["ln2_g"], prm["ln2_b"])
    return y.reshape(b, s, d)


def kernel(x_prompt, x_sample, w_in, b_branch_gate, lambda_q1, lambda_k1, lambda_q2, lambda_k2, subln_g,
           conv_w, w_attn_br, w_conv_br, w_o, ln1_g, ln1_b, w_router, b_router, w_exp_gate, b_exp_gate,
           w_exp_up, b_exp_up, w_exp_down, b_exp_down, ln2_g, ln2_b):
    l = 0
    lam_init = 0.8 - 0.6 * math.exp(-0.3 * l)
    lam = (jnp.exp(jnp.sum(lambda_q1[l].astype(F32) * lambda_k1[l].astype(F32)))
           - jnp.exp(jnp.sum(lambda_q2[l].astype(F32) * lambda_k2[l].astype(F32))) + lam_init)
    prm = {
        "lam_init": lam_init,
        "lam": lam.reshape(1).astype(F32),
        "w_in": w_in[l].astype(BF16),
        "w_v_t": w_in[l][:, V_STEP * COL_BLOCK:(V_STEP + 1) * COL_BLOCK].T.astype(BF16),
        "b_gate": b_branch_gate[l].reshape(1, 2 * D_MODEL),
        "subln_g": subln_g[l].reshape(1, V_DIM),
        "conv_w": jnp.pad(conv_w[l], ((0, SUBLANES - conv_w.shape[1]), (0, 0))),
        "w_attn_br": w_attn_br[l].astype(BF16),
        "w_conv_br": w_conv_br[l].astype(BF16),
        "w_o": w_o[l].astype(BF16),
        "ln1_g": ln1_g[l].reshape(1, D_MODEL),
        "ln1_b": ln1_b[l].reshape(1, D_MODEL),
        "w_router": jnp.pad(w_router[l], ((0, 0), (0, ROUTER_COLS - N_EXPERTS))).astype(BF16),
        "b_router": jnp.pad(b_router[l], (0, ROUTER_COLS - N_EXPERTS)).reshape(1, ROUTER_COLS),
        "w_exp_gate": w_exp_gate[l],
        "b_exp_gate": b_exp_gate[l].reshape(N_EXPERTS, 1, D_MODEL),
        "w_exp_up": w_exp_up[l],
        "b_exp_up": b_exp_up[l].reshape(N_EXPERTS, 1, D_MODEL),
        "w_exp_down": w_exp_down[l],
        "b_exp_down": b_exp_down[l].reshape(N_EXPERTS, 1, D_MODEL),
        "ln2_g": ln2_g[l].reshape(1, D_MODEL),
        "ln2_b": ln2_b[l].reshape(1, D_MODEL),
    }
    return _trunk(x_prompt, prm), _trunk(x_sample, prm)
```

```python
import functools
import math

import jax
import jax.numpy as jnp
from jax import lax
from jax.experimental import pallas as pl
from jax.experimental.pallas import tpu as pltpu
from jax.experimental.pallas import tpu_sc as plsc

F32 = jnp.float32
BF16 = jnp.bfloat16

D_MODEL = 1024
N_HEADS = 8
QK_DIM = 64
V_DIM = 128
HEAD_COLS = 2 * QK_DIM
ROT_DIM = QK_DIM // 4
ROT_HALF = ROT_DIM // 2
ROPE_THETA = 500000.0
SUBLN_EPS = 1e-5
IN_COLS = 8192
COL_BLOCK = 1024
Q_STEP, K_STEP, V_STEP = 0, 1, 2
H_COLS = IN_COLS - 2 * COL_BLOCK
CB_BLK, CC_BLK, CX_BLK, GA_BLK, GC_BLK = 1, 2, 3, 4, 5
K_HEAD0 = 0
N_EXPERTS = 32
TOP_K = 4
SWIGLU_LIMIT = 7.0
SWIGLU_ALPHA = 1.702
LN_EPS = 1e-5
DEPTH = 1
DEEPNORM_ALPHA = (2 * DEPTH) ** 0.25
LANES = 128
SUBLANES = 8
ROUTER_COLS = LANES
RT_EXPERT, RT_RANK, RT_GATE = 0, 4, 8
SC_WINDOW = 32
ONES_ROWS = 16
Q_SCALE = math.log2(math.e) * QK_DIM ** -0.5

VMEM_LIMIT = 56 * 1024 * 1024

PROJ_TM = 1024
ATTN_TQ = 1024
ATTN_TK = 512
MIX_TM = 256
MOE_TM = 512
LN2_TM = 512


def _rotate_rows(t1, t2, cos, sin):
    return t1 * cos - t2 * sin, t2 * cos + t1 * sin


def _inproj_kernel(x_ref, w_ref, wqt_ref, wvt_ref, cos_ref, sa_ref, sb_ref, cosr_ref, sinr_ref,
                   o_ref, qt_ref, vt_ref):
    j = pl.program_id(1)
    xb = x_ref[...].astype(BF16)
    nt = (((1,), (1,)), ((), ()))

    @pl.when(j == Q_STEP)
    def _():
        acc_t = lax.dot_general(wqt_ref[...], xb, nt, preferred_element_type=F32) * Q_SCALE
        cos, sin = cosr_ref[...], sinr_ref[...]
        for hh in range(N_HEADS):
            parts = []
            for mp in range(2):
                o = hh * HEAD_COLS + mp * QK_DIM
                parts += _rotate_rows(acc_t[o:o + ROT_HALF], acc_t[o + ROT_HALF:o + ROT_DIM], cos, sin)
                parts.append(acc_t[o + ROT_DIM:o + QK_DIM])
            qt_ref[hh] = jnp.concatenate(parts, axis=0).astype(BF16)

    @pl.when(j == V_STEP)
    def _():
        acc_t = lax.dot_general(wvt_ref[...], xb, nt, preferred_element_type=F32)
        n_heads, n_chunks, _, tk = vt_ref.shape
        for hh in range(n_heads):
            for cc in range(n_chunks):
                vt_ref[hh, cc, :V_DIM, :] = acc_t[hh * V_DIM:(hh + 1) * V_DIM,
                                                  cc * tk:(cc + 1) * tk].astype(BF16)
                vt_ref[hh, cc, V_DIM:, :] = jnp.ones((ONES_ROWS, tk), BF16)

    @pl.when(j > V_STEP)
    def _():
        o_ref[...] = jnp.dot(xb, w_ref[...], preferred_element_type=F32).astype(o_ref.dtype)

    @pl.when(j == K_STEP)
    def _():
        acc = jnp.dot(xb, w_ref[...], preferred_element_type=F32)
        cos, sa, sb = cos_ref[...], sa_ref[...], sb_ref[...]
        for c in range(COL_BLOCK // LANES):
            a = acc[:, c * LANES:(c + 1) * LANES]
            r = (a * cos + pltpu.roll(a, LANES - ROT_HALF, 1) * sa + pltpu.roll(a, ROT_HALF, 1) * sb)
            o_ref[:, c * LANES:(c + 1) * LANES] = r.astype(o_ref.dtype)


def _inproj(x2d, w_in_b, wqt_b, wvt_b, tables, batch, seq):
    n = x2d.shape[0]
    tm, tk = PROJ_TM, ATTN_TK
    pos_blocks = seq // tm
    cos_t, sa_t, sb_t, cos_r, sin_r = tables
    tab_spec = pl.BlockSpec((tm, LANES), lambda i, j: (i % pos_blocks, 0))
    rtab_spec = pl.BlockSpec((ROT_HALF, tm), lambda i, j: (0, i % pos_blocks))
    wt_spec = pl.BlockSpec((COL_BLOCK, D_MODEL), lambda i, j: (0, 0))
    return pl.pallas_call(
        _inproj_kernel,
        out_shape=(jax.ShapeDtypeStruct((n, H_COLS), BF16),
                   jax.ShapeDtypeStruct((batch, N_HEADS, HEAD_COLS, seq), BF16),
                   jax.ShapeDtypeStruct((batch, N_HEADS, seq // tk, V_DIM + ONES_ROWS, tk), BF16)),
        grid=(n // tm, IN_COLS // COL_BLOCK),
        in_specs=[
            pl.BlockSpec((tm, D_MODEL), lambda i, j: (i, 0)),
            pl.BlockSpec((D_MODEL, COL_BLOCK), lambda i, j: (0, jnp.where(j <= V_STEP, K_STEP, j))),
            wt_spec, wt_spec, tab_spec, tab_spec, tab_spec, rtab_spec, rtab_spec,
        ],
        out_specs=(
            pl.BlockSpec((tm, COL_BLOCK), lambda i, j: (i, jnp.maximum(j - V_STEP, 0))),
            pl.BlockSpec((None, N_HEADS, HEAD_COLS, tm), lambda i, j: (i // pos_blocks, 0, 0, i % pos_blocks)),
            pl.BlockSpec((None, N_HEADS, tm // tk, V_DIM + ONES_ROWS, tk),
                         lambda i, j: (i // pos_blocks, 0, i % pos_blocks, 0, 0)),
        ),
        compiler_params=pltpu.CompilerParams(
            dimension_semantics=("parallel", "arbitrary"), vmem_limit_bytes=VMEM_LIMIT),
        name="inproj",
    )(x2d, w_in_b, wqt_b, wvt_b, cos_t, sa_t, sb_t, cos_r, sin_r)


def _attn_kernel(lam_ref, q_ref, k_ref, vt_ref, g_ref, o_ref, qt_s, acc_s, sa0, sa1, sb0, sb1, *, out_scale):
    tq = q_ref.shape[1]
    n_chunks, _, tk = vt_ref.shape
    zero_half = jnp.zeros((QK_DIM, tq), BF16)
    qt_s[0, :QK_DIM, :] = q_ref[:QK_DIM, :]
    qt_s[0, QK_DIM:, :] = zero_half
    qt_s[1, :QK_DIM, :] = zero_half
    qt_s[1, QK_DIM:, :] = q_ref[QK_DIM:, :]
    acc_s[...] = jnp.zeros(acc_s.shape, F32)
    sa_s, sb_s = (sa0, sa1), (sb0, sb1)

    def scores(c, s_refs, mp):
        k = k_ref[pl.ds(pl.multiple_of(c * tk, tk), tk), :]
        s = jnp.dot(k, qt_s[mp], preferred_element_type=F32)
        s_refs[mp][...] = s
        return jnp.max(s, axis=0, keepdims=True)

    def update(c, s_refs, mp, m_old, c_max):
        m_new = jnp.maximum(m_old, c_max)
        alpha = jnp.exp2(m_old - m_new)
        p = jnp.exp2(s_refs[mp][...] - m_new).astype(BF16)
        acc_s[mp] = alpha * acc_s[mp] + jnp.dot(vt_ref[c], p, preferred_element_type=F32)
        return m_new

    def step(c, cur, nxt, ms, cur_max, prefetch):
        new_ms, nxt_max = [], []
        for mp in range(2):
            if prefetch:
                nxt_max.append(scores(c + 1, nxt, mp))
            new_ms.append(update(c, cur, mp, ms[mp], cur_max[mp]))
        return tuple(new_ms), tuple(nxt_max)

    def pair(i, carry):
        ms, a_max = carry
        ms, b_max = step(2 * i, sa_s, sb_s, ms, a_max, True)
        return step(2 * i + 1, sb_s, sa_s, ms, b_max, True)

    m0 = jnp.full((1, tq), -jnp.inf, F32)
    a_max = tuple(scores(0, sa_s, mp) for mp in range(2))
    ms, a_max = lax.fori_loop(0, n_chunks // 2 - 1, pair, ((m0, m0), a_max))
    ms, b_max = step(n_chunks - 2, sa_s, sb_s, ms, a_max, True)
    step(n_chunks - 1, sb_s, sa_s, ms, b_max, False)

    lam = lam_ref[0]
    o1 = acc_s[0, :V_DIM, :] / acc_s[0, V_DIM:V_DIM + 1, :]
    o2 = acc_s[1, :V_DIM, :] / acc_s[1, V_DIM:V_DIM + 1, :]
    ot = o1 - lam * o2
    ot = ot * lax.rsqrt(jnp.mean(ot * ot, axis=0, keepdims=True) + SUBLN_EPS)
    o_ref[...] = (ot.T * (g_ref[...] * out_scale)).astype(o_ref.dtype)


def _attention(h3, qt, vt, lam, subln_g, lam_init):
    b, s, _ = h3.shape
    tq, tk = ATTN_TQ, ATTN_TK
    n_chunks = s // tk
    assert n_chunks % 2 == 0 and n_chunks >= 2
    vt_rows = V_DIM + ONES_ROWS
    kern = functools.partial(_attn_kernel, out_scale=1.0 - lam_init)
    return pl.pallas_call(
        kern,
        out_shape=jax.ShapeDtypeStruct((b, s, N_HEADS * V_DIM), BF16),
        grid=(b, N_HEADS, s // tq),
        in_specs=[
            pl.BlockSpec(memory_space=pltpu.SMEM),
            pl.BlockSpec((None, None, HEAD_COLS, tq), lambda bi, hi, qi: (bi, hi, 0, qi)),
            pl.BlockSpec((None, s, HEAD_COLS), lambda bi, hi, qi: (bi, 0, K_HEAD0 + hi)),
            pl.BlockSpec((None, None, n_chunks, vt_rows, tk), lambda bi, hi, qi: (bi, hi, 0, 0, 0)),
            pl.BlockSpec((1, V_DIM), lambda bi, hi, qi: (0, 0)),
        ],
        out_specs=pl.BlockSpec((None, tq, V_DIM), lambda bi, hi, qi: (bi, qi, hi)),
        scratch_shapes=[
            pltpu.VMEM((2, HEAD_COLS, tq), BF16),
            pltpu.VMEM((2, vt_rows, tq), F32),
            pltpu.VMEM((tk, tq), F32), pltpu.VMEM((tk, tq), F32),
            pltpu.VMEM((tk, tq), F32), pltpu.VMEM((tk, tq), F32),
        ],
        compiler_params=pltpu.CompilerParams(
            dimension_semantics=("parallel", "parallel", "arbitrary"),
            vmem_limit_bytes=VMEM_LIMIT),
        name="diff_attention",
    )(lam, qt, h3, vt, subln_g)


HALF = D_MODEL // 2


def _pack_row(x):
    return pltpu.pack_elementwise([x[:, :HALF], x[:, HALF:]], packed_dtype=BF16)


def _unpack_row(w):
    lo = pltpu.unpack_elementwise(w, index=0, packed_dtype=BF16, unpacked_dtype=F32)
    hi = pltpu.unpack_elementwise(w, index=1, packed_dtype=BF16, unpacked_dtype=F32)
    return jnp.concatenate([lo, hi], axis=1)


def _layer_norm(y, g, b):
    mu = jnp.mean(y, axis=-1, keepdims=True)
    d = y - mu
    var = jnp.mean(d * d, axis=-1, keepdims=True)
    return d * lax.rsqrt(var + LN_EPS) * g + b


def _mix_kernel(a_ref, cb_ref, cc_ref, cx_ref, ga_ref, gc_ref, ccp_ref, cxp_ref, ccn_ref, cxn_ref,
                x_ref, bg_ref, cw_ref, wa_ref, wc_ref, wo_ref, g1_ref, b1_ref, wr_ref, br_ref, tri_ref,
                x1_ref, x1p_ref, rt_ref, cnt_ref, *, tiles_per_seq):
    i = pl.program_id(0)

    @pl.when(i == 0)
    def _():
        cnt_ref[...] = jnp.zeros(cnt_ref.shape, F32)

    tm = x_ref.shape[0]
    u = cc_ref[...].astype(F32) * cx_ref[...].astype(F32)
    first = (i % tiles_per_seq) == 0
    last = (i % tiles_per_seq) == tiles_per_seq - 1
    up = ccp_ref[SUBLANES - 1:SUBLANES, :].astype(F32) * cxp_ref[SUBLANES - 1:SUBLANES, :].astype(F32)
    un = ccn_ref[0:1, :].astype(F32) * cxn_ref[0:1, :].astype(F32)
    up = jnp.where(first, 0.0, up)
    un = jnp.where(last, 0.0, un)
    row = lax.broadcasted_iota(jnp.int32, u.shape, 0)
    u_prev = jnp.where(row == 0, up, pltpu.roll(u, 1, 0))
    u_next = jnp.where(row == tm - 1, un, pltpu.roll(u, tm - 1, 0))
    conv = cw_ref[0:1, :] * u_prev + cw_ref[1:2, :] * u + cw_ref[2:3, :] * u_next
    c = (cb_ref[...].astype(F32) * conv).astype(BF16)
    g_a = jax.nn.sigmoid(ga_ref[...].astype(F32) + bg_ref[:, :D_MODEL])
    g_c = jax.nn.sigmoid(gc_ref[...].astype(F32) + bg_ref[:, D_MODEL:])
    merged = (g_a * jnp.dot(a_ref[...], wa_ref[...], preferred_element_type=F32)
              + g_c * jnp.dot(c, wc_ref[...], preferred_element_type=F32))
    m = jnp.dot(merged.astype(BF16), wo_ref[...], preferred_element_type=F32)
    x1 = _layer_norm(DEEPNORM_ALPHA * x_ref[...] + m, g1_ref[...], b1_ref[...])
    x1_ref[...] = x1
    x1p_ref[...] = _pack_row(x1)
    logits =jnp.dot(x1.astype(BF16), wr_ref[...], preferred_element_type=F32) + br_ref[...]

    lane = lax.broadcasted_iota(jnp.int32, logits.shape, 1)
    lane_f = lane.astype(F32)
    work = jnp.where(lane < N_EXPERTS, logits, -jnp.inf)
    hots, vals, idxs = [], [], []
    for _ in range(TOP_K):
        mx = jnp.max(work, axis=-1, keepdims=True)
        ix = jnp.min(jnp.where(work == mx, lane_f, float(ROUTER_COLS)), axis=-1, keepdims=True)
        hot = lane_f == ix
        work = jnp.where(hot, -jnp.inf, work)
        hots.append(hot)
        vals.append(mx)
        idxs.append(ix)
    exps = [jnp.exp(v - vals[0]) for v in vals]
    denom = exps[0] + exps[1] + exps[2] + exps[3]
    picked = jnp.zeros(logits.shape, F32)
    for hot in hots:
        picked = picked + hot.astype(F32)
    before = cnt_ref[...] + jnp.dot(tri_ref[...], picked.astype(BF16), preferred_element_type=F32)
    out = jnp.zeros(logits.shape, F32)
    for j in range(TOP_K):
        rank = jnp.sum(jnp.where(hots[j], before, 0.0), axis=-1, keepdims=True)
        out = jnp.where(lane == RT_EXPERT + j, idxs[j], out)
        out = jnp.where(lane == RT_RANK + j, rank, out)
        out = jnp.where(lane == RT_GATE + j, exps[j] / denom, out)
    rt_ref[...] = out
    cnt_ref[...] = cnt_ref[...] + jnp.sum(picked, axis=0, keepdims=True)


def _mix(a2d, h2d, x2d, seq, bg, cw, wa, wc, wo, g1, b1, wr, br):
    n = x2d.shape[0]
    tm = MIX_TM
    tiles_per_seq = seq // tm
    halo_blocks = tm // SUBLANES
    n_halo = n // SUBLANES

    def col(blk):
        return pl.BlockSpec((tm, COL_BLOCK), lambda i: (i, blk))

    def prev(blk):
        return pl.BlockSpec((SUBLANES, COL_BLOCK), lambda i: (jnp.maximum(i * halo_blocks - 1, 0), blk))

    def nxt(blk):
        return pl.BlockSpec((SUBLANES, COL_BLOCK),
                            lambda i: (jnp.minimum((i + 1) * halo_blocks, n_halo - 1), blk))

    def full(shape):
        return pl.BlockSpec(shape, lambda i: (0,) * len(shape))

    tri = (lax.broadcasted_iota(jnp.int32, (tm, tm), 1)
           < lax.broadcasted_iota(jnp.int32, (tm, tm), 0)).astype(BF16)
    kern = functools.partial(_mix_kernel, tiles_per_seq=tiles_per_seq)
    return pl.pallas_call(
        kern,
        out_shape=(jax.ShapeDtypeStruct((n, D_MODEL), F32),
                   jax.ShapeDtypeStruct((n, HALF), jnp.int32),
                   jax.ShapeDtypeStruct((n, ROUTER_COLS), F32),
                   jax.ShapeDtypeStruct((1, ROUTER_COLS), F32)),
        grid=(n // tm,),
        in_specs=[
            pl.BlockSpec((tm, D_MODEL), lambda i: (i, 0)),
            col(CB_BLK), col(CC_BLK), col(CX_BLK), col(GA_BLK), col(GC_BLK),
            prev(CC_BLK), prev(CX_BLK), nxt(CC_BLK), nxt(CX_BLK),
            pl.BlockSpec((tm, D_MODEL), lambda i: (i, 0)),
            full((1, 2 * D_MODEL)), full((SUBLANES, D_MODEL)),
            full((D_MODEL, D_MODEL)), full((D_MODEL, D_MODEL)), full((D_MODEL, D_MODEL)),
            full((1, D_MODEL)), full((1, D_MODEL)),
            full((D_MODEL, ROUTER_COLS)), full((1, ROUTER_COLS)), full((tm, tm)),
        ],
        out_specs=(pl.BlockSpec((tm, D_MODEL), lambda i: (i, 0)),
                   pl.BlockSpec((tm, HALF), lambda i: (i, 0)),
                   pl.BlockSpec((tm, ROUTER_COLS), lambda i: (i, 0)),
                   pl.BlockSpec((1, ROUTER_COLS), lambda i: (0, 0))),
        compiler_params=pltpu.CompilerParams(
            dimension_semantics=("arbitrary",), vmem_limit_bytes=VMEM_LIMIT),
        name="mix_ln1_router",
    )(a2d, h2d, h2d, h2d, h2d, h2d, h2d, h2d, h2d, h2d, x2d, bg, cw, wa, wc, wo, g1, b1, wr, br, tri)


def _moe_kernel(be_ref, nv_ref, nu_ref, xs_ref, wg_ref, bg_ref, wu_ref, bu_ref, wd_ref, bd_ref, o_ref,
                wg_s, wu_s, wd_s):
    i = pl.program_id(0)
    used = i < nu_ref[0]

    @pl.when(used & ((i == 0) | (be_ref[i] != be_ref[jnp.maximum(i - 1, 0)])))
    def _():
        wg_s[...] = wg_ref[...].astype(BF16)
        wu_s[...] = wu_ref[...].astype(BF16)
        wd_s[...] = wd_ref[...].astype(BF16)

    @pl.when(used)
    def _():
        row = lax.broadcasted_iota(jnp.int32, xs_ref.shape, 0)
        x = _unpack_row(jnp.where(row < nv_ref[i], xs_ref[...], 0)).astype(BF16)
        hg = jnp.minimum(jnp.dot(x, wg_s[...], preferred_element_type=F32) + bg_ref[...], SWIGLU_LIMIT)
        hu = jnp.clip(jnp.dot(x, wu_s[...], preferred_element_type=F32) + bu_ref[...],
                      -SWIGLU_LIMIT, SWIGLU_LIMIT)
        act = hg * jax.nn.sigmoid(SWIGLU_ALPHA * hg) * (hu + 1.0)
        y = jnp.dot(act.astype(BF16), wd_s[...], preferred_element_type=F32) + bd_ref[...]
        o_ref[...] = _pack_row(y)

    @pl.when(jnp.logical_not(used))
    def _():
        o_ref[...] = jnp.zeros(o_ref.shape, o_ref.dtype)


def _moe_experts(block_e, n_valid, n_used, xs, wg, bg, wu, bu, wd, bd):
    p = xs.shape[0]
    tm = MOE_TM
    nb = p // tm

    def row_map(i, be, nv, nu):
        return (jnp.minimum(i, nu[0] - 1), 0)

    def w_map(i, be, nv, nu):
        return (be[jnp.minimum(i, nu[0] - 1)], 0, 0)

    w_spec = pl.BlockSpec((None, D_MODEL, D_MODEL), w_map)
    b_spec = pl.BlockSpec((None, 1, D_MODEL), w_map)
    return pl.pallas_call(
        _moe_kernel,
        out_shape=jax.ShapeDtypeStruct((p, HALF), jnp.int32),
        grid_spec=pltpu.PrefetchScalarGridSpec(
            num_scalar_prefetch=3,
            grid=(nb,),
            in_specs=[
                pl.BlockSpec((tm, HALF), row_map),
                w_spec, b_spec, w_spec, b_spec, w_spec, b_spec,
            ],
            out_specs=pl.BlockSpec((tm, HALF), lambda i, be, nv, nu: (i, 0)),
            scratch_shapes=[pltpu.VMEM((D_MODEL, D_MODEL), BF16)] * 3,
        ),
        compiler_params=pltpu.CompilerParams(
            dimension_semantics=("arbitrary",), vmem_limit_bytes=VMEM_LIMIT),
        name="moe_experts",
    )(block_e, n_valid, n_used, xs, wg, bg, wu, bu, wd, bd)


def _sc_mesh():
    return plsc.VectorSubcoreMesh(core_axis_name="core", subcore_axis_name="subcore")


def _sc_dispatch(x1, dest_t, p):
    n, d = x1.shape
    w = SC_WINDOW
    assert TOP_K * w == LANES
    idx = dest_t.reshape(TOP_K, n // w, w).transpose(1, 0, 2).reshape(n // w, LANES)

    @pl.kernel(out_type=jax.ShapeDtypeStruct((p, d), x1.dtype), mesh=_sc_mesh(), scratch_types=[],
               name="sc_dispatch")
    def run(x_hbm, d_hbm, o_hbm):
        def body(x_vmem, d_vmem):
            for k in range(TOP_K):
                pltpu.sync_copy(x_vmem, o_hbm.at[d_vmem.at[0, pl.ds(k * w, w)]])

        pltpu.emit_pipeline(
            body,
            grid=(n // w,),
            in_specs=[pl.BlockSpec((w, d), lambda i: (i, 0)),
                      pl.BlockSpec((1, LANES), lambda i: (i, 0))],
            out_specs=[],
            core_axis_name=("core", "subcore"),
            dimension_semantics=(pltpu.PARALLEL,),
        )(x_hbm, d_hbm)

    return run(x1, idx)


def _sc_gather(ys, idx):
    r = idx.shape[0]
    d = ys.shape[1]
    w = SC_WINDOW
    idx = jnp.pad(idx.reshape(r // w, w), ((0, 0), (0, LANES - w)))

    @pl.kernel(out_type=jax.ShapeDtypeStruct((r, d), ys.dtype), mesh=_sc_mesh(), scratch_types=[],
               name="sc_gather")
    def run(y_hbm, i_hbm, o_hbm):
        def body(i_vmem, o_vmem):
            pltpu.sync_copy(y_hbm.at[i_vmem.at[0, pl.ds(0, w)]], o_vmem)

        pltpu.emit_pipeline(
            body,
            grid=(r // w,),
            in_specs=[pl.BlockSpec((1, LANES), lambda i: (i, 0))],
            out_specs=[pl.BlockSpec((w, d), lambda i: (i, 0))],
            core_axis_name=("core", "subcore"),
            dimension_semantics=(pltpu.PARALLEL,),
        )(i_hbm, o_hbm)

    return run(ys, idx)


def _ln2_kernel(x1_ref, yg_ref, rt_ref, g_ref, b_ref, o_ref):
    f = jnp.zeros(x1_ref.shape, F32)
    for k in range(TOP_K):
        f = f + _unpack_row(yg_ref[k]) * rt_ref[:, RT_GATE + k:RT_GATE + k + 1]
    o_ref[...] = _layer_norm(DEEPNORM_ALPHA * x1_ref[...] + f, g_ref[...], b_ref[...])


def _combine_ln2(x1, yg, rt, g2, b2):
    n = x1.shape[0]
    tm = LN2_TM
    return pl.pallas_call(
        _ln2_kernel,
        out_shape=jax.ShapeDtypeStruct((n, D_MODEL), F32),
        grid=(n // tm,),
        in_specs=[
            pl.BlockSpec((tm, D_MODEL), lambda i: (i, 0)),
            pl.BlockSpec((TOP_K, tm, HALF), lambda i: (0, i, 0)),
            pl.BlockSpec((tm, ROUTER_COLS), lambda i: (i, 0)),
            pl.BlockSpec((1, D_MODEL), lambda i: (0, 0)),
            pl.BlockSpec((1, D_MODEL), lambda i: (0, 0)),
        ],
        out_specs=pl.BlockSpec((tm, D_MODEL), lambda i: (i, 0)),
        compiler_params=pltpu.CompilerParams(
            dimension_semantics=("parallel",), vmem_limit_bytes=VMEM_LIMIT),
        name="combine_ln2",
    )(x1, yg, rt, g2, b2)


def _rotary_tables(seq):
    inv = ROPE_THETA ** (-jnp.arange(0, ROT_DIM, 2, dtype=F32) / ROT_DIM)
    ang = jnp.arange(seq, dtype=F32)[:, None] * inv[None, :]
    cos, sin = jnp.cos(ang), jnp.sin(ang)
    ones = jnp.ones((seq, QK_DIM - ROT_DIM), F32)
    zeros = jnp.zeros((seq, QK_DIM - ROT_DIM), F32)
    zh = jnp.zeros((seq, ROT_HALF), F32)
    cos64 = jnp.concatenate([cos, cos, ones], axis=1)
    sa64 = jnp.concatenate([-sin, zh, zeros], axis=1)
    sb64 = jnp.concatenate([zh, sin, zeros], axis=1)
    rep = LANES // QK_DIM
    return (jnp.tile(cos64, (1, rep)), jnp.tile(sa64, (1, rep)), jnp.tile(sb64, (1, rep)), cos.T, sin.T)


def _slot_layout(rt, counts, tm):
    n = rt.shape[0]
    p = n * TOP_K + N_EXPERTS * tm
    nb = p // tm
    expert = rt[:, RT_EXPERT:RT_EXPERT + TOP_K].astype(jnp.int32)
    rank = rt[:, RT_RANK:RT_RANK + TOP_K].astype(jnp.int32)
    padded = ((counts + tm - 1) // tm) * tm
    pad_ends = jnp.cumsum(padded)
    pad_starts = pad_ends - padded
    dest_t = (pad_starts[expert] + rank).T
    block_start = jnp.arange(nb, dtype=jnp.int32) * tm
    block_e = jnp.minimum(jnp.sum(pad_ends[None, :] <= block_start[:, None], axis=1),
                          N_EXPERTS - 1).astype(jnp.int32)
    n_valid = jnp.clip(pad_starts[block_e] + counts[block_e] - block_start, 0, tm).astype(jnp.int32)
    n_used = (pad_ends[-1] // tm).astype(jnp.int32).reshape(1)
    return dest_t.astype(jnp.int32), block_e, n_valid, n_used, p


def _trunk(x, prm):
    b, s, d = x.shape
    n = b * s
    x2d = x.reshape(n, d)
    h, qt, vt = _inproj(x2d, prm["w_in"], prm["w_q_t"], prm["w_v_t"], _rotary_tables(s), b, s)
    a = _attention(h.reshape(b, s, H_COLS), qt, vt, prm["lam"], prm["subln_g"], prm["lam_init"])
    x1, x1p, rt, cnt = _mix(a.reshape(n, d), h, x2d, s, prm["b_gate"], prm["conv_w"], prm["w_attn_br"],
                            prm["w_conv_br"], prm["w_o"], prm["ln1_g"], prm["ln1_b"],
                            prm["w_router"], prm["b_router"])
    counts = cnt[0, :N_EXPERTS].astype(jnp.int32)
    dest_t, block_e, n_valid, n_used, p = _slot_layout(rt, counts, MOE_TM)
    xs = _sc_dispatch(x1p, dest_t, p)
    ys = _moe_experts(block_e, n_valid, n_used, xs, prm["w_exp_gate"], prm["b_exp_gate"],
                      prm["w_exp_up"], prm["b_exp_up"], prm["w_exp_down"], prm["b_exp_down"])
    yg = _sc_gather(ys, dest_t.reshape(-1)).reshape(TOP_K, n, HALF)
    y = _combine_ln2(x1, yg, rt, prm["ln2_g"], prm["ln2_b"])
    return y.reshape(b, s, d)


def kernel(x_prompt, x_sample, w_in, b_branch_gate, lambda_q1, lambda_k1, lambda_q2, lambda_k2, subln_g,
           conv_w, w_attn_br, w_conv_br, w_o, ln1_g, ln1_b, w_router, b_router, w_exp_gate, b_exp_gate,
           w_exp_up, b_exp_up, w_exp_down, b_exp_down, ln2_g, ln2_b):
    l = 0
    lam_init = 0.8 - 0.6 * math.exp(-0.3 * l)
    lam = (jnp.exp(jnp.sum(lambda_q1[l].astype(F32) * lambda_k1[l].astype(F32)))
           - jnp.exp(jnp.sum(lambda_q2[l].astype(F32) * lambda_k2[l].astype(F32))) + lam_init)
    prm = {
        "lam_init": lam_init,
        "lam": lam.reshape(1).astype(F32),
        "w_in": w_in[l].astype(BF16),
        "w_q_t": w_in[l][:, Q_STEP * COL_BLOCK:(Q_STEP + 1) * COL_BLOCK].T.astype(BF16),
        "w_v_t": w_in[l][:, V_STEP * COL_BLOCK:(V_STEP + 1) * COL_BLOCK].T.astype(BF16),
        "b_gate": b_branch_gate[l].reshape(1, 2 * D_MODEL),
        "subln_g": subln_g[l].reshape(1, V_DIM),
        "conv_w": jnp.pad(conv_w[l], ((0, SUBLANES - conv_w.shape[1]), (0, 0))),
        "w_attn_br": w_attn_br[l].astype(BF16),
        "w_conv_br": w_conv_br[l].astype(BF16),
        "w_o": w_o[l].astype(BF16),
        "ln1_g": ln1_g[l].reshape(1, D_MODEL),
        "ln1_b": ln1_b[l].reshape(1, D_MODEL),
        "w_router": jnp.pad(w_router[l], ((0, 0), (0, ROUTER_COLS - N_EXPERTS))).astype(BF16),
        "b_router": jnp.pad(b_router[l], (0, ROUTER_COLS - N_EXPERTS)).reshape(1, ROUTER_COLS),
        "w_exp_gate": w_exp_gate[l],
        "b_exp_gate": b_exp_gate[l].reshape(N_EXPERTS, 1, D_MODEL),
        "w_exp_up": w_exp_up[l],
        "b_exp_up": b_exp_up[l].reshape(N_EXPERTS, 1, D_MODEL),
        "w_exp_down": w_exp_down[l],
        "b_exp_down": b_exp_down[l].reshape(N_EXPERTS, 1, D_MODEL),
        "ln2_g": ln2_g[l].reshape(1, D_MODEL),
        "ln2_b": ln2_b[l].reshape(1, D_MODEL),
    }
    return _trunk(x_prompt, prm), _trunk(x_sample, prm)
```

```python
import functools
import math

import jax
import jax.numpy as jnp
from jax import lax
from jax.experimental import pallas as pl
from jax.experimental.pallas import tpu as pltpu
from jax.experimental.pallas import tpu_sc as plsc

F32 = jnp.float32
BF16 = jnp.bfloat16

D_MODEL = 1024
N_HEADS = 8
QK_DIM = 64
V_DIM = 128
HEAD_COLS = 2 * QK_DIM
ROT_DIM = QK_DIM // 4
ROT_HALF = ROT_DIM // 2
ROPE_THETA = 500000.0
SUBLN_EPS = 1e-5
IN_COLS = 8192
COL_BLOCK = 1024
Q_STEP, K_STEP, V_STEP = 0, 1, 2
H_COLS = IN_COLS - 2 * COL_BLOCK
CB_BLK, CC_BLK, CX_BLK, GA_BLK, GC_BLK = 1, 2, 3, 4, 5
K_HEAD0 = 0
N_EXPERTS = 32
TOP_K = 4
SWIGLU_LIMIT = 7.0
SWIGLU_ALPHA = 1.702
LN_EPS = 1e-5
DEPTH = 1
DEEPNORM_ALPHA = (2 * DEPTH) ** 0.25
LANES = 128
SUBLANES = 8
ROUTER_COLS = LANES
RT_EXPERT, RT_RANK, RT_GATE = 0, 4, 8
SC_WINDOW = 32
ONES_ROWS = 16
Q_SCALE = math.log2(math.e) * QK_DIM ** -0.5

VMEM_LIMIT = 56 * 1024 * 1024

PROJ_TM = 1024
ATTN_TQ = 1024
ATTN_TK = 512
MIX_TM = 256
MOE_TM = 512
LN2_TM = 512


def _rotate_rows(t1, t2, cos, sin):
    return t1 * cos - t2 * sin, t2 * cos + t1 * sin


def _inproj_kernel(x_ref, w_ref, wqt_ref, wvt_ref, cos_ref, sa_ref, sb_ref, cosr_ref, sinr_ref,
                   o_ref, qt_ref, vt_ref):
    j = pl.program_id(1)
    xb = x_ref[...].astype(BF16)
    nt = (((1,), (1,)), ((), ()))

    @pl.when(j == Q_STEP)
    def _():
        acc_t = lax.dot_general(wqt_ref[...], xb, nt, preferred_element_type=F32) * Q_SCALE
        cos, sin = cosr_ref[...], sinr_ref[...]
        for hh in range(N_HEADS):
            parts = []
            for mp in range(2):
                o = hh * HEAD_COLS + mp * QK_DIM
                parts += _rotate_rows(acc_t[o:o + ROT_HALF], acc_t[o + ROT_HALF:o + ROT_DIM], cos, sin)
                parts.append(acc_t[o + ROT_DIM:o + QK_DIM])
            qt_ref[hh, 0] = jnp.concatenate(parts, axis=0).astype(BF16)

    @pl.when(j == V_STEP)
    def _():
        acc_t = lax.dot_general(wvt_ref[...], xb, nt, preferred_element_type=F32)
        n_heads, n_chunks, _, tk = vt_ref.shape
        for hh in range(n_heads):
            for cc in range(n_chunks):
                vt_ref[hh, cc, :V_DIM, :] = acc_t[hh * V_DIM:(hh + 1) * V_DIM,
                                                  cc * tk:(cc + 1) * tk].astype(BF16)
                vt_ref[hh, cc, V_DIM:, :] = jnp.ones((ONES_ROWS, tk), BF16)

    @pl.when(j > V_STEP)
    def _():
        o_ref[...] = jnp.dot(xb, w_ref[...], preferred_element_type=F32).astype(o_ref.dtype)

    @pl.when(j == K_STEP)
    def _():
        acc = jnp.dot(xb, w_ref[...], preferred_element_type=F32)
        cos, sa, sb = cos_ref[...], sa_ref[...], sb_ref[...]
        for c in range(COL_BLOCK // LANES):
            a = acc[:, c * LANES:(c + 1) * LANES]
            r = (a * cos + pltpu.roll(a, LANES - ROT_HALF, 1) * sa + pltpu.roll(a, ROT_HALF, 1) * sb)
            o_ref[:, c * LANES:(c + 1) * LANES] = r.astype(o_ref.dtype)


def _inproj(x2d, w_in_b, wqt_b, wvt_b, tables, batch, seq):
    n = x2d.shape[0]
    tm, tk = PROJ_TM, ATTN_TK
    assert tm == ATTN_TQ
    pos_blocks = seq // tm
    cos_t, sa_t, sb_t, cos_r, sin_r = tables
    tab_spec = pl.BlockSpec((tm, LANES), lambda i, j: (i % pos_blocks, 0))
    rtab_spec = pl.BlockSpec((ROT_HALF, tm), lambda i, j: (0, i % pos_blocks))
    wt_spec = pl.BlockSpec((COL_BLOCK, D_MODEL), lambda i, j: (0, 0))
    return pl.pallas_call(
        _inproj_kernel,
        out_shape=(jax.ShapeDtypeStruct((n, H_COLS), BF16),
                   jax.ShapeDtypeStruct((batch, N_HEADS, seq // tm, HEAD_COLS, tm), BF16),
                   jax.ShapeDtypeStruct((batch, N_HEADS, seq // tk, V_DIM + ONES_ROWS, tk), BF16)),
        grid=(n // tm, IN_COLS // COL_BLOCK),
        in_specs=[
            pl.BlockSpec((tm, D_MODEL), lambda i, j: (i, 0)),
            pl.BlockSpec((D_MODEL, COL_BLOCK), lambda i, j: (0, jnp.where(j <= V_STEP, K_STEP, j))),
            wt_spec, wt_spec, tab_spec, tab_spec, tab_spec, rtab_spec, rtab_spec,
        ],
        out_specs=(
            pl.BlockSpec((tm, COL_BLOCK), lambda i, j: (i, jnp.maximum(j - V_STEP, 0))),
            pl.BlockSpec((None, N_HEADS, 1, HEAD_COLS, tm),
                         lambda i, j: (i // pos_blocks, 0, i % pos_blocks, 0, 0)),
            pl.BlockSpec((None, N_HEADS, tm // tk, V_DIM + ONES_ROWS, tk),
                         lambda i, j: (i // pos_blocks, 0, i % pos_blocks, 0, 0)),
        ),
        compiler_params=pltpu.CompilerParams(
            dimension_semantics=("parallel", "arbitrary"), vmem_limit_bytes=VMEM_LIMIT),
        name="inproj",
    )(x2d, w_in_b, wqt_b, wvt_b, cos_t, sa_t, sb_t, cos_r, sin_r)


def _attn_kernel(lam_ref, q_ref, k_ref, vt_ref, g_ref, o_ref, qt_s, acc_s, sa0, sa1, sb0, sb1, *, out_scale):
    n_q, _, tq = q_ref.shape
    n_chunks, _, tk = vt_ref.shape
    sa_s, sb_s = (sa0, sa1), (sb0, sb1)

    def load_q(qi):
        zero_half = jnp.zeros((QK_DIM, tq), BF16)
        qt_s[0, :QK_DIM, :] = q_ref[qi, :QK_DIM, :]
        qt_s[0, QK_DIM:, :] = zero_half
        qt_s[1, :QK_DIM, :] = zero_half
        qt_s[1, QK_DIM:, :] = q_ref[qi, QK_DIM:, :]

    def scores(c, s_refs, mp):
        k = k_ref[pl.ds(pl.multiple_of(c * tk, tk), tk), :]
        s = jnp.dot(k, qt_s[mp], preferred_element_type=F32)
        s_refs[mp][...] = s
        return jnp.max(s, axis=0, keepdims=True)

    def update(c, s_refs, mp, m_old, c_max):
        m_new = jnp.maximum(m_old, c_max)
        alpha = jnp.exp2(m_old - m_new)
        p = jnp.exp2(s_refs[mp][...] - m_new).astype(BF16)
        acc_s[mp] = alpha * acc_s[mp] + jnp.dot(vt_ref[c], p, preferred_element_type=F32)
        return m_new

    def step(c, cur, nxt, ms, cur_max, prefetch):
        new_ms, nxt_max = [], []
        for mp in range(2):
            if prefetch:
                nxt_max.append(scores(c + 1, nxt, mp))
            new_ms.append(update(c, cur, mp, ms[mp], cur_max[mp]))
        return tuple(new_ms), tuple(nxt_max)

    def pair(i, carry):
        ms, a_max = carry
        ms, b_max = step(2 * i, sa_s, sb_s, ms, a_max, True)
        return step(2 * i + 1, sb_s, sa_s, ms, b_max, True)

    def finalize(qi):
        lam = lam_ref[0]
        o1 = acc_s[0, :V_DIM, :] / acc_s[0, V_DIM:V_DIM + 1, :]
        o2 = acc_s[1, :V_DIM, :] / acc_s[1, V_DIM:V_DIM + 1, :]
        ot = o1 - lam * o2
        ot = ot * lax.rsqrt(jnp.mean(ot * ot, axis=0, keepdims=True) + SUBLN_EPS)
        rows = pl.ds(pl.multiple_of(qi * tq, tq), tq)
        o_ref[rows, :] = (ot.T * (g_ref[...] * out_scale)).astype(o_ref.dtype)

    def q_block(qi, a_max, last):
        acc_s[...] = jnp.zeros(acc_s.shape, F32)
        m0 = jnp.full((1, tq), -jnp.inf, F32)
        ms, a_max = lax.fori_loop(0, n_chunks // 2 - 1, pair, ((m0, m0), a_max))
        ms, b_max = step(n_chunks - 2, sa_s, sb_s, ms, a_max, True)
        step(n_chunks - 1, sb_s, sa_s, ms, b_max, False)
        if not last:
            load_q(qi + 1)
            a_max = tuple(scores(0, sa_s, mp) for mp in range(2))
        finalize(qi)
        return a_max

    load_q(0)
    a_max = tuple(scores(0, sa_s, mp) for mp in range(2))
    a_max = lax.fori_loop(0, n_q - 1, lambda qi, am: q_block(qi, am, False), a_max)
    q_block(n_q - 1, a_max, True)


def _attention(h3, qt, vt, lam, subln_g, lam_init):
    b, s, _ = h3.shape
    tq, tk = ATTN_TQ, ATTN_TK
    n_chunks = s // tk
    assert n_chunks % 2 == 0 and n_chunks >= 2
    vt_rows = V_DIM + ONES_ROWS
    kern = functools.partial(_attn_kernel, out_scale=1.0 - lam_init)
    return pl.pallas_call(
        kern,
        out_shape=jax.ShapeDtypeStruct((b, s, N_HEADS * V_DIM), BF16),
        grid=(b, N_HEADS),
        in_specs=[
            pl.BlockSpec(memory_space=pltpu.SMEM),
            pl.BlockSpec((None, None, s // tq, HEAD_COLS, tq), lambda bi, hi: (bi, hi, 0, 0, 0)),
            pl.BlockSpec((None, s, HEAD_COLS), lambda bi, hi: (bi, 0, K_HEAD0 + hi)),
            pl.BlockSpec((None, None, n_chunks, vt_rows, tk), lambda bi, hi: (bi, hi, 0, 0, 0)),
            pl.BlockSpec((1, V_DIM), lambda bi, hi: (0, 0)),
        ],
        out_specs=pl.BlockSpec((None, s, V_DIM), lambda bi, hi: (bi, 0, hi)),
        scratch_shapes=[
            pltpu.VMEM((2, HEAD_COLS, tq), BF16),
            pltpu.VMEM((2, vt_rows, tq), F32),
            pltpu.VMEM((tk, tq), F32), pltpu.VMEM((tk, tq), F32),
            pltpu.VMEM((tk, tq), F32), pltpu.VMEM((tk, tq), F32),
        ],
        compiler_params=pltpu.CompilerParams(
            dimension_semantics=("parallel", "arbitrary"),
            vmem_limit_bytes=VMEM_LIMIT),
        name="diff_attention",
    )(lam, qt, h3, vt, subln_g)


HALF = D_MODEL // 2


def _pack_row(x):
    return pltpu.pack_elementwise([x[:, :HALF], x[:, HALF:]], packed_dtype=BF16)


def _unpack_row(w):
    lo = pltpu.unpack_elementwise(w, index=0, packed_dtype=BF16, unpacked_dtype=F32)
    hi = pltpu.unpack_elementwise(w, index=1, packed_dtype=BF16, unpacked_dtype=F32)
    return jnp.concatenate([lo, hi], axis=1)


def _layer_norm(y, g, b):
    mu = jnp.mean(y, axis=-1, keepdims=True)
    d = y - mu
    var = jnp.mean(d * d, axis=-1, keepdims=True)
    return d * lax.rsqrt(var + LN_EPS) * g + b


def _mix_kernel(a_ref, cb_ref, cc_ref, cx_ref, ga_ref, gc_ref, ccp_ref, cxp_ref, ccn_ref, cxn_ref,
                x_ref, bg_ref, cw_ref, wa_ref, wc_ref, wo_ref, g1_ref, b1_ref, wr_ref, br_ref, tri_ref,
                x1_ref, x1p_ref, rt_ref, cnt_ref, *, tiles_per_seq):
    i = pl.program_id(0)

    @pl.when(i == 0)
    def _():
        cnt_ref[...] = jnp.zeros(cnt_ref.shape, F32)

    tm = x_ref.shape[0]
    u = cc_ref[...].astype(F32) * cx_ref[...].astype(F32)
    first = (i % tiles_per_seq) == 0
    last = (i % tiles_per_seq) == tiles_per_seq - 1
    up = ccp_ref[SUBLANES - 1:SUBLANES, :].astype(F32) * cxp_ref[SUBLANES - 1:SUBLANES, :].astype(F32)
    un = ccn_ref[0:1, :].astype(F32) * cxn_ref[0:1, :].astype(F32)
    up = jnp.where(first, 0.0, up)
    un = jnp.where(last, 0.0, un)
    row = lax.broadcasted_iota(jnp.int32, u.shape, 0)
    u_prev = jnp.where(row == 0, up, pltpu.roll(u, 1, 0))
    u_next = jnp.where(row == tm - 1, un, pltpu.roll(u, tm - 1, 0))
    conv = cw_ref[0:1, :] * u_prev + cw_ref[1:2, :] * u + cw_ref[2:3, :] * u_next
    c = (cb_ref[...].astype(F32) * conv).astype(BF16)
    g_a = jax.nn.sigmoid(ga_ref[...].astype(F32) + bg_ref[:, :D_MODEL])
    g_c = jax.nn.sigmoid(gc_ref[...].astype(F32) + bg_ref[:, D_MODEL:])
    merged = (g_a * jnp.dot(a_ref[...], wa_ref[...], preferred_element_type=F32)
              + g_c * jnp.dot(c, wc_ref[...], preferred_element_type=F32))
    m = jnp.dot(merged.astype(BF16), wo_ref[...], preferred_element_type=F32)
    x1 = _layer_norm(DEEPNORM_ALPHA * x_ref[...] + m, g1_ref[...], b1_ref[...])
    x1_ref[...] = x1
    x1p_ref[...] = _pack_row(x1)
    logits =jnp.dot(x1.astype(BF16), wr_ref[...], preferred_element_type=F32) + br_ref[...]

    lane = lax.broadcasted_iota(jnp.int32, logits.shape, 1)
    lane_f = lane.astype(F32)
    work = jnp.where(lane < N_EXPERTS, logits, -jnp.inf)
    hots, vals, idxs = [], [], []
    for _ in range(TOP_K):
        mx = jnp.max(work, axis=-1, keepdims=True)
        ix = jnp.min(jnp.where(work == mx, lane_f, float(ROUTER_COLS)), axis=-1, keepdims=True)
        hot = lane_f == ix
        work = jnp.where(hot, -jnp.inf, work)
        hots.append(hot)
        vals.append(mx)
        idxs.append(ix)
    exps = [jnp.exp(v - vals[0]) for v in vals]
    denom = exps[0] + exps[1] + exps[2] + exps[3]
    picked = jnp.zeros(logits.shape, F32)
    for hot in hots:
        picked = picked + hot.astype(F32)
    before = cnt_ref[...] + jnp.dot(tri_ref[...], picked.astype(BF16), preferred_element_type=F32)
    out = jnp.zeros(logits.shape, F32)
    for j in range(TOP_K):
        rank = jnp.sum(jnp.where(hots[j], before, 0.0), axis=-1, keepdims=True)
        out = jnp.where(lane == RT_EXPERT + j, idxs[j], out)
        out = jnp.where(lane == RT_RANK + j, rank, out)
        out = jnp.where(lane == RT_GATE + j, exps[j] / denom, out)
    rt_ref[...] = out
    cnt_ref[...] = cnt_ref[...] + jnp.sum(picked, axis=0, keepdims=True)


def _mix(a2d, h2d, x2d, seq, bg, cw, wa, wc, wo, g1, b1, wr, br):
    n = x2d.shape[0]
    tm = MIX_TM
    tiles_per_seq = seq // tm
    halo_blocks = tm // SUBLANES
    n_halo = n // SUBLANES

    def col(blk):
        return pl.BlockSpec((tm, COL_BLOCK), lambda i: (i, blk))

    def prev(blk):
        return pl.BlockSpec((SUBLANES, COL_BLOCK), lambda i: (jnp.maximum(i * halo_blocks - 1, 0), blk))

    def nxt(blk):
        return pl.BlockSpec((SUBLANES, COL_BLOCK),
                            lambda i: (jnp.minimum((i + 1) * halo_blocks, n_halo - 1), blk))

    def full(shape):
        return pl.BlockSpec(shape, lambda i: (0,) * len(shape))

    tri = (lax.broadcasted_iota(jnp.int32, (tm, tm), 1)
           < lax.broadcasted_iota(jnp.int32, (tm, tm), 0)).astype(BF16)
    kern = functools.partial(_mix_kernel, tiles_per_seq=tiles_per_seq)
    return pl.pallas_call(
        kern,
        out_shape=(jax.ShapeDtypeStruct((n, D_MODEL), F32),
                   jax.ShapeDtypeStruct((n, HALF), jnp.int32),
                   jax.ShapeDtypeStruct((n, ROUTER_COLS), F32),
                   jax.ShapeDtypeStruct((1, ROUTER_COLS), F32)),
        grid=(n // tm,),
        in_specs=[
            pl.BlockSpec((tm, D_MODEL), lambda i: (i, 0)),
            col(CB_BLK), col(CC_BLK), col(CX_BLK), col(GA_BLK), col(GC_BLK),
            prev(CC_BLK), prev(CX_BLK), nxt(CC_BLK), nxt(CX_BLK),
            pl.BlockSpec((tm, D_MODEL), lambda i: (i, 0)),
            full((1, 2 * D_MODEL)), full((SUBLANES, D_MODEL)),
            full((D_MODEL, D_MODEL)), full((D_MODEL, D_MODEL)), full((D_MODEL, D_MODEL)),
            full((1, D_MODEL)), full((1, D_MODEL)),
            full((D_MODEL, ROUTER_COLS)), full((1, ROUTER_COLS)), full((tm, tm)),
        ],
        out_specs=(pl.BlockSpec((tm, D_MODEL), lambda i: (i, 0)),
                   pl.BlockSpec((tm, HALF), lambda i: (i, 0)),
                   pl.BlockSpec((tm, ROUTER_COLS), lambda i: (i, 0)),
                   pl.BlockSpec((1, ROUTER_COLS), lambda i: (0, 0))),
        compiler_params=pltpu.CompilerParams(
            dimension_semantics=("arbitrary",), vmem_limit_bytes=VMEM_LIMIT),
        name="mix_ln1_router",
    )(a2d, h2d, h2d, h2d, h2d, h2d, h2d, h2d, h2d, h2d, x2d, bg, cw, wa, wc, wo, g1, b1, wr, br, tri)


def _moe_kernel(be_ref, nv_ref, nu_ref, xs_ref, wg_ref, bg_ref, wu_ref, bu_ref, wd_ref, bd_ref, o_ref,
                wg_s, wu_s, wd_s):
    i = pl.program_id(0)
    used = i < nu_ref[0]

    @pl.when(used & ((i == 0) | (be_ref[i] != be_ref[jnp.maximum(i - 1, 0)])))
    def _():
        wg_s[...] = wg_ref[...].astype(BF16)
        wu_s[...] = wu_ref[...].astype(BF16)
        wd_s[...] = wd_ref[...].astype(BF16)

    @pl.when(used)
    def _():
        row = lax.broadcasted_iota(jnp.int32, xs_ref.shape, 0)
        x = _unpack_row(jnp.where(row < nv_ref[i], xs_ref[...], 0)).astype(BF16)
        hg = jnp.minimum(jnp.dot(x, wg_s[...], preferred_element_type=F32) + bg_ref[...], SWIGLU_LIMIT)
        hu = jnp.clip(jnp.dot(x, wu_s[...], preferred_element_type=F32) + bu_ref[...],
                      -SWIGLU_LIMIT, SWIGLU_LIMIT)
        act = hg * jax.nn.sigmoid(SWIGLU_ALPHA * hg) * (hu + 1.0)
        y = jnp.dot(act.astype(BF16), wd_s[...], preferred_element_type=F32) + bd_ref[...]
        o_ref[...] = _pack_row(y)

    @pl.when(jnp.logical_not(used))
    def _():
        o_ref[...] = jnp.zeros(o_ref.shape, o_ref.dtype)


def _moe_experts(block_e, n_valid, n_used, xs, wg, bg, wu, bu, wd, bd):
    p = xs.shape[0]
    tm = MOE_TM
    nb = p // tm

    def row_map(i, be, nv, nu):
        return (jnp.minimum(i, nu[0] - 1), 0)

    def w_map(i, be, nv, nu):
        return (be[jnp.minimum(i, nu[0] - 1)], 0, 0)

    w_spec = pl.BlockSpec((None, D_MODEL, D_MODEL), w_map)
    b_spec = pl.BlockSpec((None, 1, D_MODEL), w_map)
    return pl.pallas_call(
        _moe_kernel,
        out_shape=jax.ShapeDtypeStruct((p, HALF), jnp.int32),
        grid_spec=pltpu.PrefetchScalarGridSpec(
            num_scalar_prefetch=3,
            grid=(nb,),
            in_specs=[
                pl.BlockSpec((tm, HALF), row_map),
                w_spec, b_spec, w_spec, b_spec, w_spec, b_spec,
            ],
            out_specs=pl.BlockSpec((tm, HALF), lambda i, be, nv, nu: (i, 0)),
            scratch_shapes=[pltpu.VMEM((D_MODEL, D_MODEL), BF16)] * 3,
        ),
        compiler_params=pltpu.CompilerParams(
            dimension_semantics=("arbitrary",), vmem_limit_bytes=VMEM_LIMIT),
        name="moe_experts",
    )(block_e, n_valid, n_used, xs, wg, bg, wu, bu, wd, bd)


def _sc_mesh():
    return plsc.VectorSubcoreMesh(core_axis_name="core", subcore_axis_name="subcore")


def _sc_dispatch(x1, dest_t, p):
    n, d = x1.shape
    w = SC_WINDOW
    assert TOP_K * w == LANES
    idx = dest_t.reshape(TOP_K, n // w, w).transpose(1, 0, 2).reshape(n // w, LANES)

    @pl.kernel(out_type=jax.ShapeDtypeStruct((p, d), x1.dtype), mesh=_sc_mesh(), scratch_types=[],
               name="sc_dispatch")
    def run(x_hbm, d_hbm, o_hbm):
        def body(x_vmem, d_vmem):
            for k in range(TOP_K):
                pltpu.sync_copy(x_vmem, o_hbm.at[d_vmem.at[0, pl.ds(k * w, w)]])

        pltpu.emit_pipeline(
            body,
            grid=(n // w,),
            in_specs=[pl.BlockSpec((w, d), lambda i: (i, 0)),
                      pl.BlockSpec((1, LANES), lambda i: (i, 0))],
            out_specs=[],
            core_axis_name=("core", "subcore"),
            dimension_semantics=(pltpu.PARALLEL,),
        )(x_hbm, d_hbm)

    return run(x1, idx)


def _sc_gather(ys, idx):
    r = idx.shape[0]
    d = ys.shape[1]
    w = SC_WINDOW
    idx = jnp.pad(idx.reshape(r // w, w), ((0, 0), (0, LANES - w)))

    @pl.kernel(out_type=jax.ShapeDtypeStruct((r, d), ys.dtype), mesh=_sc_mesh(), scratch_types=[],
               name="sc_gather")
    def run(y_hbm, i_hbm, o_hbm):
        def body(i_vmem, o_vmem):
            pltpu.sync_copy(y_hbm.at[i_vmem.at[0, pl.ds(0, w)]], o_vmem)

        pltpu.emit_pipeline(
            body,
            grid=(r // w,),
            in_specs=[pl.BlockSpec((1, LANES), lambda i: (i, 0))],
            out_specs=[pl.BlockSpec((w, d), lambda i: (i, 0))],
            core_axis_name=("core", "subcore"),
            dimension_semantics=(pltpu.PARALLEL,),
        )(i_hbm, o_hbm)

    return run(ys, idx)


def _ln2_kernel(x1_ref, yg_ref, rt_ref, g_ref, b_ref, o_ref):
    f = jnp.zeros(x1_ref.shape, F32)
    for k in range(TOP_K):
        f = f + _unpack_row(yg_ref[k]) * rt_ref[:, RT_GATE + k:RT_GATE + k + 1]
    o_ref[...] = _layer_norm(DEEPNORM_ALPHA * x1_ref[...] + f, g_ref[...], b_ref[...])


def _combine_ln2(x1, yg, rt, g2, b2):
    n = x1.shape[0]
    tm = LN2_TM
    return pl.pallas_call(
        _ln2_kernel,
        out_shape=jax.ShapeDtypeStruct((n, D_MODEL), F32),
        grid=(n // tm,),
        in_specs=[
            pl.BlockSpec((tm, D_MODEL), lambda i: (i, 0)),
            pl.BlockSpec((TOP_K, tm, HALF), lambda i: (0, i, 0)),
            pl.BlockSpec((tm, ROUTER_COLS), lambda i: (i, 0)),
            pl.BlockSpec((1, D_MODEL), lambda i: (0, 0)),
            pl.BlockSpec((1, D_MODEL), lambda i: (0, 0)),
        ],
        out_specs=pl.BlockSpec((tm, D_MODEL), lambda i: (i, 0)),
        compiler_params=pltpu.CompilerParams(
            dimension_semantics=("parallel",), vmem_limit_bytes=VMEM_LIMIT),
        name="combine_ln2",
    )(x1, yg, rt, g2, b2)


def _rotary_tables(seq):
    inv = ROPE_THETA ** (-jnp.arange(0, ROT_DIM, 2, dtype=F32) / ROT_DIM)
    ang = jnp.arange(seq, dtype=F32)[:, None] * inv[None, :]
    cos, sin = jnp.cos(ang), jnp.sin(ang)
    ones = jnp.ones((seq, QK_DIM - ROT_DIM), F32)
    zeros = jnp.zeros((seq, QK_DIM - ROT_DIM), F32)
    zh = jnp.zeros((seq, ROT_HALF), F32)
    cos64 = jnp.concatenate([cos, cos, ones], axis=1)
    sa64 = jnp.concatenate([-sin, zh, zeros], axis=1)
    sb64 = jnp.concatenate([zh, sin, zeros], axis=1)
    rep = LANES // QK_DIM
    return (jnp.tile(cos64, (1, rep)), jnp.tile(sa64, (1, rep)), jnp.tile(sb64, (1, rep)), cos.T, sin.T)


def _slot_layout(rt, counts, tm):
    n = rt.shape[0]
    p = n * TOP_K + N_EXPERTS * tm
    nb = p // tm
    expert = rt[:, RT_EXPERT:RT_EXPERT + TOP_K].astype(jnp.int32)
    rank = rt[:, RT_RANK:RT_RANK + TOP_K].astype(jnp.int32)
    padded = ((counts + tm - 1) // tm) * tm
    pad_ends = jnp.cumsum(padded)
    pad_starts = pad_ends - padded
    dest_t = (pad_starts[expert] + rank).T
    block_start = jnp.arange(nb, dtype=jnp.int32) * tm
    block_e = jnp.minimum(jnp.sum(pad_ends[None, :] <= block_start[:, None], axis=1),
                          N_EXPERTS - 1).astype(jnp.int32)
    n_valid = jnp.clip(pad_starts[block_e] + counts[block_e] - block_start, 0, tm).astype(jnp.int32)
    n_used = (pad_ends[-1] // tm).astype(jnp.int32).reshape(1)
    return dest_t.astype(jnp.int32), block_e, n_valid, n_used, p


def _trunk(x, prm):
    b, s, d = x.shape
    n = b * s
    x2d = x.reshape(n, d)
    h, qt, vt = _inproj(x2d, prm["w_in"], prm["w_q_t"], prm["w_v_t"], _rotary_tables(s), b, s)
    a = _attention(h.reshape(b, s, H_COLS), qt, vt, prm["lam"], prm["subln_g"], prm["lam_init"])
    x1, x1p, rt, cnt = _mix(a.reshape(n, d), h, x2d, s, prm["b_gate"], prm["conv_w"], prm["w_attn_br"],
                            prm["w_conv_br"], prm["w_o"], prm["ln1_g"], prm["ln1_b"],
                            prm["w_router"], prm["b_router"])
    counts = cnt[0, :N_EXPERTS].astype(jnp.int32)
    dest_t, block_e, n_valid, n_used, p = _slot_layout(rt, counts, MOE_TM)
    xs = _sc_dispatch(x1p, dest_t, p)
    ys = _moe_experts(block_e, n_valid, n_used, xs, prm["w_exp_gate"], prm["b_exp_gate"],
                      prm["w_exp_up"], prm["b_exp_up"], prm["w_exp_down"], prm["b_exp_down"])
    yg = _sc_gather(ys, dest_t.reshape(-1)).reshape(TOP_K, n, HALF)
    y = _combine_ln2(x1, yg, rt, prm["ln2_g"], prm["ln2_b"])
    return y.reshape(b, s, d)


def kernel(x_prompt, x_sample, w_in, b_branch_gate, lambda_q1, lambda_k1, lambda_q2, lambda_k2, subln_g,
           conv_w, w_attn_br, w_conv_br, w_o, ln1_g, ln1_b, w_router, b_router, w_exp_gate, b_exp_gate,
           w_exp_up, b_exp_up, w_exp_down, b_exp_down, ln2_g, ln2_b):
    l = 0
    lam_init = 0.8 - 0.6 * math.exp(-0.3 * l)
    lam = (jnp.exp(jnp.sum(lambda_q1[l].astype(F32) * lambda_k1[l].astype(F32)))
           - jnp.exp(jnp.sum(lambda_q2[l].astype(F32) * lambda_k2[l].astype(F32))) + lam_init)
    prm = {
        "lam_init": lam_init,
        "lam": lam.reshape(1).astype(F32),
        "w_in": w_in[l].astype(BF16),
        "w_q_t": w_in[l][:, Q_STEP * COL_BLOCK:(Q_STEP + 1) * COL_BLOCK].T.astype(BF16),
        "w_v_t": w_in[l][:, V_STEP * COL_BLOCK:(V_STEP + 1) * COL_BLOCK].T.astype(BF16),
        "b_gate": b_branch_gate[l].reshape(1, 2 * D_MODEL),
        "subln_g": subln_g[l].reshape(1, V_DIM),
        "conv_w": jnp.pad(conv_w[l], ((0, SUBLANES - conv_w.shape[1]), (0, 0))),
        "w_attn_br": w_attn_br[l].astype(BF16),
        "w_conv_br": w_conv_br[l].astype(BF16),
        "w_o": w_o[l].astype(BF16),
        "ln1_g": ln1_g[l].reshape(1, D_MODEL),
        "ln1_b": ln1_b[l].reshape(1, D_MODEL),
        "w_router": jnp.pad(w_router[l], ((0, 0), (0, ROUTER_COLS - N_EXPERTS))).astype(BF16),
        "b_router": jnp.pad(b_router[l], (0, ROUTER_COLS - N_EXPERTS)).reshape(1, ROUTER_COLS),
        "w_exp_gate": w_exp_gate[l],
        "b_exp_gate": b_exp_gate[l].reshape(N_EXPERTS, 1, D_MODEL),
        "w_exp_up": w_exp_up[l],
        "b_exp_up": b_exp_up[l].reshape(N_EXPERTS, 1, D_MODEL),
        "w_exp_down": w_exp_down[l],
        "b_exp_down": b_exp_down[l].reshape(N_EXPERTS, 1, D_MODEL),
        "ln2_g": ln2_g[l].reshape(1, D_MODEL),
        "ln2_b": ln2_b[l].reshape(1, D_MODEL),
    }
    return _trunk(x_prompt, prm), _trunk(x_sample, prm)
```

```python
import functools
import math

import jax
import jax.numpy as jnp
from jax import lax
from jax.experimental import pallas as pl
from jax.experimental.pallas import tpu as pltpu
from jax.experimental.pallas import tpu_sc as plsc

F32 = jnp.float32
BF16 = jnp.bfloat16

D_MODEL = 1024
N_HEADS = 8
QK_DIM = 64
V_DIM = 128
HEAD_COLS = 2 * QK_DIM
ROT_DIM = QK_DIM // 4
ROT_HALF = ROT_DIM // 2
ROPE_THETA = 500000.0
SUBLN_EPS = 1e-5
IN_COLS = 8192
COL_BLOCK = 1024
Q_STEP, K_STEP, V_STEP = 0, 1, 2
H_COLS = IN_COLS - 2 * COL_BLOCK
CB_BLK, CC_BLK, CX_BLK, GA_BLK, GC_BLK = 1, 2, 3, 4, 5
K_HEAD0 = 0
N_EXPERTS = 32
TOP_K = 4
SWIGLU_LIMIT = 7.0
SWIGLU_ALPHA = 1.702
LN_EPS = 1e-5
DEPTH = 1
DEEPNORM_ALPHA = (2 * DEPTH) ** 0.25
LANES = 128
SUBLANES = 8
ROUTER_COLS = LANES
RT_EXPERT, RT_RANK, RT_GATE = 0, 4, 8
RT_ROWS = 16
SC_WINDOW = 32
ONES_ROWS = 16
Q_SCALE = math.log2(math.e) * QK_DIM ** -0.5

VMEM_LIMIT = 56 * 1024 * 1024

PROJ_TM = 1024
ATTN_TQ = 1024
ATTN_TK = 512
MIX_TM = 256
MOE_TM = 512
LN2_TM = 512


def _rotate_rows(t1, t2, cos, sin):
    return t1 * cos - t2 * sin, t2 * cos + t1 * sin


def _inproj_kernel(x_ref, w_ref, wqt_ref, wvt_ref, cos_ref, sa_ref, sb_ref, cosr_ref, sinr_ref,
                   o_ref, qt_ref, vt_ref):
    j = pl.program_id(1)
    xb = x_ref[...].astype(BF16)
    nt = (((1,), (1,)), ((), ()))

    @pl.when(j == Q_STEP)
    def _():
        acc_t = lax.dot_general(wqt_ref[...], xb, nt, preferred_element_type=F32) * Q_SCALE
        cos, sin = cosr_ref[...], sinr_ref[...]
        for hh in range(N_HEADS):
            parts = []
            for mp in range(2):
                o = hh * HEAD_COLS + mp * QK_DIM
                parts += _rotate_rows(acc_t[o:o + ROT_HALF], acc_t[o + ROT_HALF:o + ROT_DIM], cos, sin)
                parts.append(acc_t[o + ROT_DIM:o + QK_DIM])
            qt_ref[hh, 0] = jnp.concatenate(parts, axis=0).astype(BF16)

    @pl.when(j == V_STEP)
    def _():
        acc_t = lax.dot_general(wvt_ref[...], xb, nt, preferred_element_type=F32)
        n_heads, n_chunks, _, tk = vt_ref.shape
        for hh in range(n_heads):
            for cc in range(n_chunks):
                vt_ref[hh, cc, :V_DIM, :] = acc_t[hh * V_DIM:(hh + 1) * V_DIM,
                                                  cc * tk:(cc + 1) * tk].astype(BF16)
                vt_ref[hh, cc, V_DIM:, :] = jnp.ones((ONES_ROWS, tk), BF16)

    @pl.when(j > V_STEP)
    def _():
        o_ref[...] = jnp.dot(xb, w_ref[...], preferred_element_type=F32).astype(o_ref.dtype)

    @pl.when(j == K_STEP)
    def _():
        acc = jnp.dot(xb, w_ref[...], preferred_element_type=F32)
        cos, sa, sb = cos_ref[...], sa_ref[...], sb_ref[...]
        for c in range(COL_BLOCK // LANES):
            a = acc[:, c * LANES:(c + 1) * LANES]
            r = (a * cos + pltpu.roll(a, LANES - ROT_HALF, 1) * sa + pltpu.roll(a, ROT_HALF, 1) * sb)
            o_ref[:, c * LANES:(c + 1) * LANES] = r.astype(o_ref.dtype)


def _inproj(x2d, w_in_b, wqt_b, wvt_b, tables, batch, seq):
    n = x2d.shape[0]
    tm, tk = PROJ_TM, ATTN_TK
    assert tm == ATTN_TQ
    pos_blocks = seq // tm
    cos_t, sa_t, sb_t, cos_r, sin_r = tables
    tab_spec = pl.BlockSpec((tm, LANES), lambda i, j: (i % pos_blocks, 0))
    rtab_spec = pl.BlockSpec((ROT_HALF, tm), lambda i, j: (0, i % pos_blocks))
    wt_spec = pl.BlockSpec((COL_BLOCK, D_MODEL), lambda i, j: (0, 0))
    return pl.pallas_call(
        _inproj_kernel,
        out_shape=(jax.ShapeDtypeStruct((n, H_COLS), BF16),
                   jax.ShapeDtypeStruct((batch, N_HEADS, seq // tm, HEAD_COLS, tm), BF16),
                   jax.ShapeDtypeStruct((batch, N_HEADS, seq // tk, V_DIM + ONES_ROWS, tk), BF16)),
        grid=(n // tm, IN_COLS // COL_BLOCK),
        in_specs=[
            pl.BlockSpec((tm, D_MODEL), lambda i, j: (i, 0)),
            pl.BlockSpec((D_MODEL, COL_BLOCK), lambda i, j: (0, jnp.where(j <= V_STEP, K_STEP, j))),
            wt_spec, wt_spec, tab_spec, tab_spec, tab_spec, rtab_spec, rtab_spec,
        ],
        out_specs=(
            pl.BlockSpec((tm, COL_BLOCK), lambda i, j: (i, jnp.maximum(j - V_STEP, 0))),
            pl.BlockSpec((None, N_HEADS, 1, HEAD_COLS, tm),
                         lambda i, j: (i // pos_blocks, 0, i % pos_blocks, 0, 0)),
            pl.BlockSpec((None, N_HEADS, tm // tk, V_DIM + ONES_ROWS, tk),
                         lambda i, j: (i // pos_blocks, 0, i % pos_blocks, 0, 0)),
        ),
        compiler_params=pltpu.CompilerParams(
            dimension_semantics=("parallel", "arbitrary"), vmem_limit_bytes=VMEM_LIMIT),
        name="inproj",
    )(x2d, w_in_b, wqt_b, wvt_b, cos_t, sa_t, sb_t, cos_r, sin_r)


def _attn_kernel(lam_ref, q_ref, k_ref, vt_ref, g_ref, o_ref, qt_s, acc_s, sa0, sa1, sb0, sb1, *, out_scale):
    n_q, _, tq = q_ref.shape
    n_chunks, _, tk = vt_ref.shape
    sa_s, sb_s = (sa0, sa1), (sb0, sb1)

    def load_q(qi):
        zero_half = jnp.zeros((QK_DIM, tq), BF16)
        qt_s[0, :QK_DIM, :] = q_ref[qi, :QK_DIM, :]
        qt_s[0, QK_DIM:, :] = zero_half
        qt_s[1, :QK_DIM, :] = zero_half
        qt_s[1, QK_DIM:, :] = q_ref[qi, QK_DIM:, :]

    def scores(c, s_refs, mp):
        k = k_ref[pl.ds(pl.multiple_of(c * tk, tk), tk), :]
        s = jnp.dot(k, qt_s[mp], preferred_element_type=F32)
        s_refs[mp][...] = s
        return jnp.max(s, axis=0, keepdims=True)

    def update(c, s_refs, mp, m_old, c_max):
        m_new = jnp.maximum(m_old, c_max)
        alpha = jnp.exp2(m_old - m_new)
        p = jnp.exp2(s_refs[mp][...] - m_new).astype(BF16)
        acc_s[mp] = alpha * acc_s[mp] + jnp.dot(vt_ref[c], p, preferred_element_type=F32)
        return m_new

    def step(c, cur, nxt, ms, cur_max, prefetch):
        new_ms, nxt_max = [], []
        for mp in range(2):
            if prefetch:
                nxt_max.append(scores(c + 1, nxt, mp))
            new_ms.append(update(c, cur, mp, ms[mp], cur_max[mp]))
        return tuple(new_ms), tuple(nxt_max)

    def pair(i, carry):
        ms, a_max = carry
        ms, b_max = step(2 * i, sa_s, sb_s, ms, a_max, True)
        return step(2 * i + 1, sb_s, sa_s, ms, b_max, True)

    def finalize(qi):
        lam = lam_ref[0]
        o1 = acc_s[0, :V_DIM, :] / acc_s[0, V_DIM:V_DIM + 1, :]
        o2 = acc_s[1, :V_DIM, :] / acc_s[1, V_DIM:V_DIM + 1, :]
        ot = o1 - lam * o2
        ot = ot * lax.rsqrt(jnp.mean(ot * ot, axis=0, keepdims=True) + SUBLN_EPS)
        rows = pl.ds(pl.multiple_of(qi * tq, tq), tq)
        o_ref[rows, :] = (ot.T * (g_ref[...] * out_scale)).astype(o_ref.dtype)

    def q_block(qi, a_max, last):
        acc_s[...] = jnp.zeros(acc_s.shape, F32)
        m0 = jnp.full((1, tq), -jnp.inf, F32)
        ms, a_max = lax.fori_loop(0, n_chunks // 2 - 1, pair, ((m0, m0), a_max))
        ms, b_max = step(n_chunks - 2, sa_s, sb_s, ms, a_max, True)
        step(n_chunks - 1, sb_s, sa_s, ms, b_max, False)
        if not last:
            load_q(qi + 1)
            a_max = tuple(scores(0, sa_s, mp) for mp in range(2))
        finalize(qi)
        return a_max

    load_q(0)
    a_max = tuple(scores(0, sa_s, mp) for mp in range(2))
    a_max = lax.fori_loop(0, n_q - 1, lambda qi, am: q_block(qi, am, False), a_max)
    q_block(n_q - 1, a_max, True)


def _attention(h3, qt, vt, lam, subln_g, lam_init):
    b, s, _ = h3.shape
    tq, tk = ATTN_TQ, ATTN_TK
    n_chunks = s // tk
    assert n_chunks % 2 == 0 and n_chunks >= 2
    vt_rows = V_DIM + ONES_ROWS
    kern = functools.partial(_attn_kernel, out_scale=1.0 - lam_init)
    return pl.pallas_call(
        kern,
        out_shape=jax.ShapeDtypeStruct((b, s, N_HEADS * V_DIM), BF16),
        grid=(b, N_HEADS),
        in_specs=[
            pl.BlockSpec(memory_space=pltpu.SMEM),
            pl.BlockSpec((None, None, s // tq, HEAD_COLS, tq), lambda bi, hi: (bi, hi, 0, 0, 0)),
            pl.BlockSpec((None, s, HEAD_COLS), lambda bi, hi: (bi, 0, K_HEAD0 + hi)),
            pl.BlockSpec((None, None, n_chunks, vt_rows, tk), lambda bi, hi: (bi, hi, 0, 0, 0)),
            pl.BlockSpec((1, V_DIM), lambda bi, hi: (0, 0)),
        ],
        out_specs=pl.BlockSpec((None, s, V_DIM), lambda bi, hi: (bi, 0, hi)),
        scratch_shapes=[
            pltpu.VMEM((2, HEAD_COLS, tq), BF16),
            pltpu.VMEM((2, vt_rows, tq), F32),
            pltpu.VMEM((tk, tq), F32), pltpu.VMEM((tk, tq), F32),
            pltpu.VMEM((tk, tq), F32), pltpu.VMEM((tk, tq), F32),
        ],
        compiler_params=pltpu.CompilerParams(
            dimension_semantics=("parallel", "arbitrary"),
            vmem_limit_bytes=VMEM_LIMIT),
        name="diff_attention",
    )(lam, qt, h3, vt, subln_g)


HALF = D_MODEL // 2


def _pack_row(x):
    return pltpu.pack_elementwise([x[:, :HALF], x[:, HALF:]], packed_dtype=BF16)


def _unpack_row(w):
    lo = pltpu.unpack_elementwise(w, index=0, packed_dtype=BF16, unpacked_dtype=F32)
    hi = pltpu.unpack_elementwise(w, index=1, packed_dtype=BF16, unpacked_dtype=F32)
    return jnp.concatenate([lo, hi], axis=1)


def _layer_norm(y, g, b):
    mu = jnp.mean(y, axis=-1, keepdims=True)
    d = y - mu
    var = jnp.mean(d * d, axis=-1, keepdims=True)
    return d * lax.rsqrt(var + LN_EPS) * g + b


def _mix_kernel(a_ref, cb_ref, cc_ref, cx_ref, ga_ref, gc_ref, ccp_ref, cxp_ref, ccn_ref, cxn_ref,
                x_ref, bg_ref, cw_ref, wa_ref, wc_ref, wo_ref, g1_ref, b1_ref, wr_ref, br_ref, tri_ref,
                x1_ref, x1p_ref, rt_ref, rtt_ref, cnt_ref, *, tiles_per_seq):
    i = pl.program_id(0)

    @pl.when(i == 0)
    def _():
        cnt_ref[...] = jnp.zeros(cnt_ref.shape, F32)

    tm = x_ref.shape[0]
    u = cc_ref[...].astype(F32) * cx_ref[...].astype(F32)
    first = (i % tiles_per_seq) == 0
    last = (i % tiles_per_seq) == tiles_per_seq - 1
    up = ccp_ref[SUBLANES - 1:SUBLANES, :].astype(F32) * cxp_ref[SUBLANES - 1:SUBLANES, :].astype(F32)
    un = ccn_ref[0:1, :].astype(F32) * cxn_ref[0:1, :].astype(F32)
    up = jnp.where(first, 0.0, up)
    un = jnp.where(last, 0.0, un)
    row = lax.broadcasted_iota(jnp.int32, u.shape, 0)
    u_prev = jnp.where(row == 0, up, pltpu.roll(u, 1, 0))
    u_next = jnp.where(row == tm - 1, un, pltpu.roll(u, tm - 1, 0))
    conv = cw_ref[0:1, :] * u_prev + cw_ref[1:2, :] * u + cw_ref[2:3, :] * u_next
    c = (cb_ref[...].astype(F32) * conv).astype(BF16)
    g_a = jax.nn.sigmoid(ga_ref[...].astype(F32) + bg_ref[:, :D_MODEL])
    g_c = jax.nn.sigmoid(gc_ref[...].astype(F32) + bg_ref[:, D_MODEL:])
    merged = (g_a * jnp.dot(a_ref[...], wa_ref[...], preferred_element_type=F32)
              + g_c * jnp.dot(c, wc_ref[...], preferred_element_type=F32))
    m = jnp.dot(merged.astype(BF16), wo_ref[...], preferred_element_type=F32)
    x1 = _layer_norm(DEEPNORM_ALPHA * x_ref[...] + m, g1_ref[...], b1_ref[...])
    x1_ref[...] = x1
    x1p_ref[...] = _pack_row(x1)
    logits =jnp.dot(x1.astype(BF16), wr_ref[...], preferred_element_type=F32) + br_ref[...]

    lane = lax.broadcasted_iota(jnp.int32, logits.shape, 1)
    lane_f = lane.astype(F32)
    work = jnp.where(lane < N_EXPERTS, logits, -jnp.inf)
    hots, vals, idxs = [], [], []
    for _ in range(TOP_K):
        mx = jnp.max(work, axis=-1, keepdims=True)
        ix = jnp.min(jnp.where(work == mx, lane_f, float(ROUTER_COLS)), axis=-1, keepdims=True)
        hot = lane_f == ix
        work = jnp.where(hot, -jnp.inf, work)
        hots.append(hot)
        vals.append(mx)
        idxs.append(ix)
    exps = [jnp.exp(v - vals[0]) for v in vals]
    denom = exps[0] + exps[1] + exps[2] + exps[3]
    picked = jnp.zeros(logits.shape, F32)
    for hot in hots:
        picked = picked + hot.astype(F32)
    before = cnt_ref[...] + jnp.dot(tri_ref[...], picked.astype(BF16), preferred_element_type=F32)
    out = jnp.zeros(logits.shape, F32)
    for j in range(TOP_K):
        rank = jnp.sum(jnp.where(hots[j], before, 0.0), axis=-1, keepdims=True)
        out = jnp.where(lane == RT_EXPERT + j, idxs[j], out)
        out = jnp.where(lane == RT_RANK + j, rank, out)
        out = jnp.where(lane == RT_GATE + j, exps[j] / denom, out)
    rt_ref[...] = out
    rtt_ref[...] = out.T[:RT_ROWS, :]
    cnt_ref[...] = cnt_ref[...] + jnp.sum(picked, axis=0, keepdims=True)


def _mix(a2d, h2d, x2d, seq, bg, cw, wa, wc, wo, g1, b1, wr, br):
    n = x2d.shape[0]
    tm = MIX_TM
    tiles_per_seq = seq // tm
    halo_blocks = tm // SUBLANES
    n_halo = n // SUBLANES

    def col(blk):
        return pl.BlockSpec((tm, COL_BLOCK), lambda i: (i, blk))

    def prev(blk):
        return pl.BlockSpec((SUBLANES, COL_BLOCK), lambda i: (jnp.maximum(i * halo_blocks - 1, 0), blk))

    def nxt(blk):
        return pl.BlockSpec((SUBLANES, COL_BLOCK),
                            lambda i: (jnp.minimum((i + 1) * halo_blocks, n_halo - 1), blk))

    def full(shape):
        return pl.BlockSpec(shape, lambda i: (0,) * len(shape))

    tri = (lax.broadcasted_iota(jnp.int32, (tm, tm), 1)
           < lax.broadcasted_iota(jnp.int32, (tm, tm), 0)).astype(BF16)
    kern = functools.partial(_mix_kernel, tiles_per_seq=tiles_per_seq)
    return pl.pallas_call(
        kern,
        out_shape=(jax.ShapeDtypeStruct((n, D_MODEL), F32),
                   jax.ShapeDtypeStruct((n, HALF), jnp.int32),
                   jax.ShapeDtypeStruct((n, ROUTER_COLS), F32),
                   jax.ShapeDtypeStruct((RT_ROWS, n), F32),
                   jax.ShapeDtypeStruct((1, ROUTER_COLS), F32)),
        grid=(n // tm,),
        in_specs=[
            pl.BlockSpec((tm, D_MODEL), lambda i: (i, 0)),
            col(CB_BLK), col(CC_BLK), col(CX_BLK), col(GA_BLK), col(GC_BLK),
            prev(CC_BLK), prev(CX_BLK), nxt(CC_BLK), nxt(CX_BLK),
            pl.BlockSpec((tm, D_MODEL), lambda i: (i, 0)),
            full((1, 2 * D_MODEL)), full((SUBLANES, D_MODEL)),
            full((D_MODEL, D_MODEL)), full((D_MODEL, D_MODEL)), full((D_MODEL, D_MODEL)),
            full((1, D_MODEL)), full((1, D_MODEL)),
            full((D_MODEL, ROUTER_COLS)), full((1, ROUTER_COLS)), full((tm, tm)),
        ],
        out_specs=(pl.BlockSpec((tm, D_MODEL), lambda i: (i, 0)),
                   pl.BlockSpec((tm, HALF), lambda i: (i, 0)),
                   pl.BlockSpec((tm, ROUTER_COLS), lambda i: (i, 0)),
                   pl.BlockSpec((RT_ROWS, tm), lambda i: (0, i)),
                   pl.BlockSpec((1, ROUTER_COLS), lambda i: (0, 0))),
        compiler_params=pltpu.CompilerParams(
            dimension_semantics=("arbitrary",), vmem_limit_bytes=VMEM_LIMIT),
        name="mix_ln1_router",
    )(a2d, h2d, h2d, h2d, h2d, h2d, h2d, h2d, h2d, h2d, x2d, bg, cw, wa, wc, wo, g1, b1, wr, br, tri)


def _moe_kernel(be_ref, nv_ref, nu_ref, xs_ref, wg_ref, bg_ref, wu_ref, bu_ref, wd_ref, bd_ref, o_ref,
                wg_s, wu_s, wd_s):
    i = pl.program_id(0)
    used = i < nu_ref[0]

    @pl.when(used & ((i == 0) | (be_ref[i] != be_ref[jnp.maximum(i - 1, 0)])))
    def _():
        wg_s[...] = wg_ref[...].astype(BF16)
        wu_s[...] = wu_ref[...].astype(BF16)
        wd_s[...] = wd_ref[...].astype(BF16)

    @pl.when(used)
    def _():
        row = lax.broadcasted_iota(jnp.int32, xs_ref.shape, 0)
        x = _unpack_row(jnp.where(row < nv_ref[i], xs_ref[...], 0)).astype(BF16)
        hg = jnp.minimum(jnp.dot(x, wg_s[...], preferred_element_type=F32) + bg_ref[...], SWIGLU_LIMIT)
        hu = jnp.clip(jnp.dot(x, wu_s[...], preferred_element_type=F32) + bu_ref[...],
                      -SWIGLU_LIMIT, SWIGLU_LIMIT)
        act = hg * jax.nn.sigmoid(SWIGLU_ALPHA * hg) * (hu + 1.0)
        y = jnp.dot(act.astype(BF16), wd_s[...], preferred_element_type=F32) + bd_ref[...]
        o_ref[...] = _pack_row(y)

    @pl.when(jnp.logical_not(used))
    def _():
        o_ref[...] = jnp.zeros(o_ref.shape, o_ref.dtype)


def _moe_experts(block_e, n_valid, n_used, xs, wg, bg, wu, bu, wd, bd):
    p = xs.shape[0]
    tm = MOE_TM
    nb = p // tm

    def row_map(i, be, nv, nu):
        return (jnp.minimum(i, nu[0] - 1), 0)

    def w_map(i, be, nv, nu):
        return (be[jnp.minimum(i, nu[0] - 1)], 0, 0)

    w_spec = pl.BlockSpec((None, D_MODEL, D_MODEL), w_map)
    b_spec = pl.BlockSpec((None, 1, D_MODEL), w_map)
    return pl.pallas_call(
        _moe_kernel,
        out_shape=jax.ShapeDtypeStruct((p, HALF), jnp.int32),
        grid_spec=pltpu.PrefetchScalarGridSpec(
            num_scalar_prefetch=3,
            grid=(nb,),
            in_specs=[
                pl.BlockSpec((tm, HALF), row_map),
                w_spec, b_spec, w_spec, b_spec, w_spec, b_spec,
            ],
            out_specs=pl.BlockSpec((tm, HALF), lambda i, be, nv, nu: (i, 0)),
            scratch_shapes=[pltpu.VMEM((D_MODEL, D_MODEL), BF16)] * 3,
        ),
        compiler_params=pltpu.CompilerParams(
            dimension_semantics=("arbitrary",), vmem_limit_bytes=VMEM_LIMIT),
        name="moe_experts",
    )(block_e, n_valid, n_used, xs, wg, bg, wu, bu, wd, bd)


def _sc_mesh():
    return plsc.VectorSubcoreMesh(core_axis_name="core", subcore_axis_name="subcore")


def _sc_index_rows(dest_t):
    k, n = dest_t.shape
    w = SC_WINDOW
    return jnp.pad(dest_t.reshape(k * n // w, w), ((0, 0), (0, LANES - w)))


def _sc_dispatch(x1, slot_idx, p):
    n, d = x1.shape
    w = SC_WINDOW
    windows = n // w

    @pl.kernel(out_type=jax.ShapeDtypeStruct((p, d), x1.dtype), mesh=_sc_mesh(), scratch_types=[],
               name="sc_dispatch")
    def run(x_hbm, d_hbm, o_hbm):
        def body(x_vmem, *d_vmems):
            for d_vmem in d_vmems:
                pltpu.sync_copy(x_vmem, o_hbm.at[d_vmem.at[0, pl.ds(0, w)]])

        pltpu.emit_pipeline(
            body,
            grid=(windows,),
            in_specs=[pl.BlockSpec((w, d), lambda i: (i, 0))]
            + [pl.BlockSpec((1, LANES), functools.partial(lambda k, i: (k * windows + i, 0), k))
               for k in range(TOP_K)],
            out_specs=[],
            core_axis_name=("core", "subcore"),
            dimension_semantics=(pltpu.PARALLEL,),
        )(x_hbm, *([d_hbm] * TOP_K))

    return run(x1, slot_idx)


def _sc_gather(ys, idx):
    w = SC_WINDOW
    r = idx.shape[0] * w
    d = ys.shape[1]

    @pl.kernel(out_type=jax.ShapeDtypeStruct((r, d), ys.dtype), mesh=_sc_mesh(), scratch_types=[],
               name="sc_gather")
    def run(y_hbm, i_hbm, o_hbm):
        def body(i_vmem, o_vmem):
            pltpu.sync_copy(y_hbm.at[i_vmem.at[0, pl.ds(0, w)]], o_vmem)

        pltpu.emit_pipeline(
            body,
            grid=(r // w,),
            in_specs=[pl.BlockSpec((1, LANES), lambda i: (i, 0))],
            out_specs=[pl.BlockSpec((w, d), lambda i: (i, 0))],
            core_axis_name=("core", "subcore"),
            dimension_semantics=(pltpu.PARALLEL,),
        )(i_hbm, o_hbm)

    return run(ys, idx)


def _ln2_kernel(x1_ref, yg_ref, rt_ref, g_ref, b_ref, o_ref):
    f = jnp.zeros(x1_ref.shape, F32)
    for k in range(TOP_K):
        f = f + _unpack_row(yg_ref[k]) * rt_ref[:, RT_GATE + k:RT_GATE + k + 1]
    o_ref[...] = _layer_norm(DEEPNORM_ALPHA * x1_ref[...] + f, g_ref[...], b_ref[...])


def _combine_ln2(x1, yg, rt, g2, b2):
    n = x1.shape[0]
    tm = LN2_TM
    return pl.pallas_call(
        _ln2_kernel,
        out_shape=jax.ShapeDtypeStruct((n, D_MODEL), F32),
        grid=(n // tm,),
        in_specs=[
            pl.BlockSpec((tm, D_MODEL), lambda i: (i, 0)),
            pl.BlockSpec((TOP_K, tm, HALF), lambda i: (0, i, 0)),
            pl.BlockSpec((tm, ROUTER_COLS), lambda i: (i, 0)),
            pl.BlockSpec((1, D_MODEL), lambda i: (0, 0)),
            pl.BlockSpec((1, D_MODEL), lambda i: (0, 0)),
        ],
        out_specs=pl.BlockSpec((tm, D_MODEL), lambda i: (i, 0)),
        compiler_params=pltpu.CompilerParams(
            dimension_semantics=("parallel",), vmem_limit_bytes=VMEM_LIMIT),
        name="combine_ln2",
    )(x1, yg, rt, g2, b2)


def _rotary_tables(seq):
    inv = ROPE_THETA ** (-jnp.arange(0, ROT_DIM, 2, dtype=F32) / ROT_DIM)
    ang = jnp.arange(seq, dtype=F32)[:, None] * inv[None, :]
    cos, sin = jnp.cos(ang), jnp.sin(ang)
    ones = jnp.ones((seq, QK_DIM - ROT_DIM), F32)
    zeros = jnp.zeros((seq, QK_DIM - ROT_DIM), F32)
    zh = jnp.zeros((seq, ROT_HALF), F32)
    cos64 = jnp.concatenate([cos, cos, ones], axis=1)
    sa64 = jnp.concatenate([-sin, zh, zeros], axis=1)
    sb64 = jnp.concatenate([zh, sin, zeros], axis=1)
    rep = LANES // QK_DIM
    return (jnp.tile(cos64, (1, rep)), jnp.tile(sa64, (1, rep)), jnp.tile(sb64, (1, rep)), cos.T, sin.T)


def _slot_layout(rtt, counts, tm):
    n = rtt.shape[1]
    p = n * TOP_K + N_EXPERTS * tm
    nb = p // tm
    expert = rtt[RT_EXPERT:RT_EXPERT + TOP_K].astype(jnp.int32)
    rank = rtt[RT_RANK:RT_RANK + TOP_K].astype(jnp.int32)
    padded = ((counts + tm - 1) // tm) * tm
    pad_ends = jnp.cumsum(padded)
    pad_starts = pad_ends - padded
    dest_t = pad_starts[expert] + rank
    block_start = jnp.arange(nb, dtype=jnp.int32) * tm
    block_e = jnp.minimum(jnp.sum(pad_ends[None, :] <= block_start[:, None], axis=1),
                          N_EXPERTS - 1).astype(jnp.int32)
    n_valid = jnp.clip(pad_starts[block_e] + counts[block_e] - block_start, 0, tm).astype(jnp.int32)
    n_used = (pad_ends[-1] // tm).astype(jnp.int32).reshape(1)
    return dest_t.astype(jnp.int32), block_e, n_valid, n_used, p


def _trunk(x, prm):
    b, s, d = x.shape
    n = b * s
    x2d = x.reshape(n, d)
    h, qt, vt = _inproj(x2d, prm["w_in"], prm["w_q_t"], prm["w_v_t"], _rotary_tables(s), b, s)
    a = _attention(h.reshape(b, s, H_COLS), qt, vt, prm["lam"], prm["subln_g"], prm["lam_init"])
    x1, x1p, rt, rtt, cnt = _mix(a.reshape(n, d), h, x2d, s, prm["b_gate"], prm["conv_w"], prm["w_attn_br"],
                                 prm["w_conv_br"], prm["w_o"], prm["ln1_g"], prm["ln1_b"],
                                 prm["w_router"], prm["b_router"])
    counts = cnt[0, :N_EXPERTS].astype(jnp.int32)
    dest_t, block_e, n_valid, n_used, p = _slot_layout(rtt, counts, MOE_TM)
    slot_idx = _sc_index_rows(dest_t)
    xs = _sc_dispatch(x1p, slot_idx, p)
    ys = _moe_experts(block_e, n_valid, n_used, xs, prm["w_exp_gate"], prm["b_exp_gate"],
                      prm["w_exp_up"], prm["b_exp_up"], prm["w_exp_down"], prm["b_exp_down"])
    yg = _sc_gather(ys, slot_idx).reshape(TOP_K, n, HALF)
    y = _combine_ln2(x1, yg, rt, prm["ln2_g"], prm["ln2_b"])
    return y.reshape(b, s, d)


def kernel(x_prompt, x_sample, w_in, b_branch_gate, lambda_q1, lambda_k1, lambda_q2, lambda_k2, subln_g,
           conv_w, w_attn_br, w_conv_br, w_o, ln1_g, ln1_b, w_router, b_router, w_exp_gate, b_exp_gate,
           w_exp_up, b_exp_up, w_exp_down, b_exp_down, ln2_g, ln2_b):
    l = 0
    lam_init = 0.8 - 0.6 * math.exp(-0.3 * l)
    lam = (jnp.exp(jnp.sum(lambda_q1[l].astype(F32) * lambda_k1[l].astype(F32)))
           - jnp.exp(jnp.sum(lambda_q2[l].astype(F32) * lambda_k2[l].astype(F32))) + lam_init)
    prm = {
        "lam_init": lam_init,
        "lam": lam.reshape(1).astype(F32),
        "w_in": w_in[l].astype(BF16),
        "w_q_t": w_in[l][:, Q_STEP * COL_BLOCK:(Q_STEP + 1) * COL_BLOCK].T.astype(BF16),
        "w_v_t": w_in[l][:, V_STEP * COL_BLOCK:(V_STEP + 1) * COL_BLOCK].T.astype(BF16),
        "b_gate": b_branch_gate[l].reshape(1, 2 * D_MODEL),
        "subln_g": subln_g[l].reshape(1, V_DIM),
        "conv_w": jnp.pad(conv_w[l], ((0, SUBLANES - conv_w.shape[1]), (0, 0))),
        "w_attn_br": w_attn_br[l].astype(BF16),
        "w_conv_br": w_conv_br[l].astype(BF16),
        "w_o": w_o[l].astype(BF16),
        "ln1_g": ln1_g[l].reshape(1, D_MODEL),
        "ln1_b": ln1_b[l].reshape(1, D_MODEL),
        "w_router": jnp.pad(w_router[l], ((0, 0), (0, ROUTER_COLS - N_EXPERTS))).astype(BF16),
        "b_router": jnp.pad(b_router[l], (0, ROUTER_COLS - N_EXPERTS)).reshape(1, ROUTER_COLS),
        "w_exp_gate": w_exp_gate[l],
        "b_exp_gate": b_exp_gate[l].reshape(N_EXPERTS, 1, D_MODEL),
        "w_exp_up": w_exp_up[l],
        "b_exp_up": b_exp_up[l].reshape(N_EXPERTS, 1, D_MODEL),
        "w_exp_down": w_exp_down[l],
        "b_exp_down": b_exp_down[l].reshape(N_EXPERTS, 1, D_MODEL),
        "ln2_g": ln2_g[l].reshape(1, D_MODEL),
        "ln2_b": ln2_b[l].reshape(1, D_MODEL),
    }
    return _trunk(x_prompt, prm), _trunk(x_sample, prm)
```

```python
import functools
import math

import jax
import jax.numpy as jnp
from jax import lax
from jax.experimental import pallas as pl
from jax.experimental.pallas import tpu as pltpu
from jax.experimental.pallas import tpu_sc as plsc

F32 = jnp.float32
BF16 = jnp.bfloat16

D_MODEL = 1024
N_HEADS = 8
QK_DIM = 64
V_DIM = 128
HEAD_COLS = 2 * QK_DIM
ROT_DIM = QK_DIM // 4
ROT_HALF = ROT_DIM // 2
ROPE_THETA = 500000.0
SUBLN_EPS = 1e-5
IN_COLS = 8192
COL_BLOCK = 1024
Q_STEP, K_STEP, V_STEP = 0, 1, 2
H_COLS = IN_COLS - 2 * COL_BLOCK
CB_BLK, CC_BLK, CX_BLK, GA_BLK, GC_BLK = 1, 2, 3, 4, 5
K_HEAD0 = 0
N_EXPERTS = 32
TOP_K = 4
SWIGLU_LIMIT = 7.0
SWIGLU_ALPHA = 1.702
LN_EPS = 1e-5
DEPTH = 1
DEEPNORM_ALPHA = (2 * DEPTH) ** 0.25
LANES = 128
SUBLANES = 8
ROUTER_COLS = LANES
RT_EXPERT, RT_RANK, RT_GATE = 0, 4, 8
RT_ROWS = 16
SC_WINDOW = 32
ONES_ROWS = 16
Q_SCALE = math.log2(math.e) * QK_DIM ** -0.5

VMEM_LIMIT = 56 * 1024 * 1024

PROJ_TM = 1024
ATTN_TQ = 1024
ATTN_TK = 512
MIX_TM = 256
MOE_TM = 512
LN2_TM = 512


def _rotate_rows(t1, t2, cos, sin):
    return t1 * cos - t2 * sin, t2 * cos + t1 * sin


def _inproj_kernel(x_ref, w_ref, wqt_ref, wvt_ref, cos_ref, sa_ref, sb_ref, cosr_ref, sinr_ref,
                   o_ref, qt_ref, vt_ref):
    j = pl.program_id(1)
    xb = x_ref[...].astype(BF16)
    nt = (((1,), (1,)), ((), ()))

    @pl.when(j == Q_STEP)
    def _():
        acc_t = lax.dot_general(wqt_ref[...], xb, nt, preferred_element_type=F32) * Q_SCALE
        cos, sin = cosr_ref[...], sinr_ref[...]
        for hh in range(N_HEADS):
            parts = []
            for mp in range(2):
                o = hh * HEAD_COLS + mp * QK_DIM
                parts += _rotate_rows(acc_t[o:o + ROT_HALF], acc_t[o + ROT_HALF:o + ROT_DIM], cos, sin)
                parts.append(acc_t[o + ROT_DIM:o + QK_DIM])
            qt_ref[hh, 0] = jnp.concatenate(parts, axis=0).astype(BF16)

    @pl.when(j == V_STEP)
    def _():
        acc_t = lax.dot_general(wvt_ref[...], xb, nt, preferred_element_type=F32)
        n_heads, n_chunks, _, tk = vt_ref.shape
        for hh in range(n_heads):
            for cc in range(n_chunks):
                vt_ref[hh, cc, :V_DIM, :] = acc_t[hh * V_DIM:(hh + 1) * V_DIM,
                                                  cc * tk:(cc + 1) * tk].astype(BF16)
                vt_ref[hh, cc, V_DIM:, :] = jnp.ones((ONES_ROWS, tk), BF16)

    @pl.when(j > V_STEP)
    def _():
        o_ref[...] = jnp.dot(xb, w_ref[...], preferred_element_type=F32).astype(o_ref.dtype)

    @pl.when(j == K_STEP)
    def _():
        acc = jnp.dot(xb, w_ref[...], preferred_element_type=F32)
        cos, sa, sb = cos_ref[...], sa_ref[...], sb_ref[...]
        for c in range(COL_BLOCK // LANES):
            a = acc[:, c * LANES:(c + 1) * LANES]
            r = (a * cos + pltpu.roll(a, LANES - ROT_HALF, 1) * sa + pltpu.roll(a, ROT_HALF, 1) * sb)
            o_ref[:, c * LANES:(c + 1) * LANES] = r.astype(o_ref.dtype)


def _inproj(x2d, w_in_b, wqt_b, wvt_b, tables, batch, seq):
    n = x2d.shape[0]
    tm, tk = PROJ_TM, ATTN_TK
    assert tm == ATTN_TQ
    pos_blocks = seq // tm
    cos_t, sa_t, sb_t, cos_r, sin_r = tables
    tab_spec = pl.BlockSpec((tm, LANES), lambda i, j: (i % pos_blocks, 0))
    rtab_spec = pl.BlockSpec((ROT_HALF, tm), lambda i, j: (0, i % pos_blocks))
    wt_spec = pl.BlockSpec((COL_BLOCK, D_MODEL), lambda i, j: (0, 0))
    return pl.pallas_call(
        _inproj_kernel,
        out_shape=(jax.ShapeDtypeStruct((n, H_COLS), BF16),
                   jax.ShapeDtypeStruct((batch, N_HEADS, seq // tm, HEAD_COLS, tm), BF16),
                   jax.ShapeDtypeStruct((batch, N_HEADS, seq // tk, V_DIM + ONES_ROWS, tk), BF16)),
        grid=(n // tm, IN_COLS // COL_BLOCK),
        in_specs=[
            pl.BlockSpec((tm, D_MODEL), lambda i, j: (i, 0)),
            pl.BlockSpec((D_MODEL, COL_BLOCK), lambda i, j: (0, jnp.where(j <= V_STEP, K_STEP, j))),
            wt_spec, wt_spec, tab_spec, tab_spec, tab_spec, rtab_spec, rtab_spec,
        ],
        out_specs=(
            pl.BlockSpec((tm, COL_BLOCK), lambda i, j: (i, jnp.maximum(j - V_STEP, 0))),
            pl.BlockSpec((None, N_HEADS, 1, HEAD_COLS, tm),
                         lambda i, j: (i // pos_blocks, 0, i % pos_blocks, 0, 0)),
            pl.BlockSpec((None, N_HEADS, tm // tk, V_DIM + ONES_ROWS, tk),
                         lambda i, j: (i // pos_blocks, 0, i % pos_blocks, 0, 0)),
        ),
        compiler_params=pltpu.CompilerParams(
            dimension_semantics=("parallel", "arbitrary"), vmem_limit_bytes=VMEM_LIMIT),
        name="inproj",
    )(x2d, w_in_b, wqt_b, wvt_b, cos_t, sa_t, sb_t, cos_r, sin_r)


def _attn_kernel(lam_ref, q_ref, k_ref, vt_ref, g_ref, o_ref, qt_s, acc_s, sa0, sa1, sb0, sb1, *, out_scale):
    n_q, _, tq = q_ref.shape
    n_chunks, _, tk = vt_ref.shape
    sa_s, sb_s = (sa0, sa1), (sb0, sb1)

    def load_q(qi):
        zero_half = jnp.zeros((QK_DIM, tq), BF16)
        qt_s[0, :QK_DIM, :] = q_ref[qi, :QK_DIM, :]
        qt_s[0, QK_DIM:, :] = zero_half
        qt_s[1, :QK_DIM, :] = zero_half
        qt_s[1, QK_DIM:, :] = q_ref[qi, QK_DIM:, :]

    def scores(c, s_refs, mp):
        k = k_ref[pl.ds(pl.multiple_of(c * tk, tk), tk), :]
        s = jnp.dot(k, qt_s[mp], preferred_element_type=F32)
        s_refs[mp][...] = s
        return jnp.max(s, axis=0, keepdims=True)

    def update(c, s_refs, mp, m_old, c_max):
        m_new = jnp.maximum(m_old, c_max)
        alpha = jnp.exp2(m_old - m_new)
        p = jnp.exp2(s_refs[mp][...] - m_new).astype(BF16)
        acc_s[mp] = alpha * acc_s[mp] + jnp.dot(vt_ref[c], p, preferred_element_type=F32)
        return m_new

    def step(c, cur, nxt, ms, cur_max, prefetch):
        new_ms, nxt_max = [], []
        for mp in range(2):
            if prefetch:
                nxt_max.append(scores(c + 1, nxt, mp))
            new_ms.append(update(c, cur, mp, ms[mp], cur_max[mp]))
        return tuple(new_ms), tuple(nxt_max)

    def pair(i, carry):
        ms, a_max = carry
        ms, b_max = step(2 * i, sa_s, sb_s, ms, a_max, True)
        return step(2 * i + 1, sb_s, sa_s, ms, b_max, True)

    def finalize(qi):
        lam = lam_ref[0]
        o1 = acc_s[0, :V_DIM, :] / acc_s[0, V_DIM:V_DIM + 1, :]
        o2 = acc_s[1, :V_DIM, :] / acc_s[1, V_DIM:V_DIM + 1, :]
        ot = o1 - lam * o2
        ot = ot * lax.rsqrt(jnp.mean(ot * ot, axis=0, keepdims=True) + SUBLN_EPS)
        rows = pl.ds(pl.multiple_of(qi * tq, tq), tq)
        o_ref[rows, :] = (ot.T * (g_ref[...] * out_scale)).astype(o_ref.dtype)

    def q_block(qi, a_max, last):
        acc_s[...] = jnp.zeros(acc_s.shape, F32)
        m0 = jnp.full((1, tq), -jnp.inf, F32)
        ms, a_max = lax.fori_loop(0, n_chunks // 2 - 1, pair, ((m0, m0), a_max))
        ms, b_max = step(n_chunks - 2, sa_s, sb_s, ms, a_max, True)
        step(n_chunks - 1, sb_s, sa_s, ms, b_max, False)
        if not last:
            load_q(qi + 1)
            a_max = tuple(scores(0, sa_s, mp) for mp in range(2))
        finalize(qi)
        return a_max

    load_q(0)
    a_max = tuple(scores(0, sa_s, mp) for mp in range(2))
    a_max = lax.fori_loop(0, n_q - 1, lambda qi, am: q_block(qi, am, False), a_max)
    q_block(n_q - 1, a_max, True)


def _attention(h3, qt, vt, lam, subln_g, lam_init):
    b, s, _ = h3.shape
    tq, tk = ATTN_TQ, ATTN_TK
    n_chunks = s // tk
    assert n_chunks % 2 == 0 and n_chunks >= 2
    vt_rows = V_DIM + ONES_ROWS
    kern = functools.partial(_attn_kernel, out_scale=1.0 - lam_init)
    return pl.pallas_call(
        kern,
        out_shape=jax.ShapeDtypeStruct((b, s, N_HEADS * V_DIM), BF16),
        grid=(b, N_HEADS),
        in_specs=[
            pl.BlockSpec(memory_space=pltpu.SMEM),
            pl.BlockSpec((None, None, s // tq, HEAD_COLS, tq), lambda bi, hi: (bi, hi, 0, 0, 0)),
            pl.BlockSpec((None, s, HEAD_COLS), lambda bi, hi: (bi, 0, K_HEAD0 + hi)),
            pl.BlockSpec((None, None, n_chunks, vt_rows, tk), lambda bi, hi: (bi, hi, 0, 0, 0)),
            pl.BlockSpec((1, V_DIM), lambda bi, hi: (0, 0)),
        ],
        out_specs=pl.BlockSpec((None, s, V_DIM), lambda bi, hi: (bi, 0, hi)),
        scratch_shapes=[
            pltpu.VMEM((2, HEAD_COLS, tq), BF16),
            pltpu.VMEM((2, vt_rows, tq), F32),
            pltpu.VMEM((tk, tq), F32), pltpu.VMEM((tk, tq), F32),
            pltpu.VMEM((tk, tq), F32), pltpu.VMEM((tk, tq), F32),
        ],
        compiler_params=pltpu.CompilerParams(
            dimension_semantics=("parallel", "arbitrary"),
            vmem_limit_bytes=VMEM_LIMIT),
        name="diff_attention",
    )(lam, qt, h3, vt, subln_g)


HALF = D_MODEL // 2


def _pack_row(x):
    return pltpu.pack_elementwise([x[:, :HALF], x[:, HALF:]], packed_dtype=BF16)


def _unpack_row(w):
    lo = pltpu.unpack_elementwise(w, index=0, packed_dtype=BF16, unpacked_dtype=F32)
    hi = pltpu.unpack_elementwise(w, index=1, packed_dtype=BF16, unpacked_dtype=F32)
    return jnp.concatenate([lo, hi], axis=1)


def _layer_norm(y, g, b):
    mu = jnp.mean(y, axis=-1, keepdims=True)
    d = y - mu
    var = jnp.mean(d * d, axis=-1, keepdims=True)
    return d * lax.rsqrt(var + LN_EPS) * g + b


def _mix_kernel(a_ref, cb_ref, cc_ref, cx_ref, ga_ref, gc_ref, ccp_ref, cxp_ref, ccn_ref, cxn_ref,
                x_ref, bg_ref, cw_ref, wa_ref, wc_ref, wo_ref, g1_ref, b1_ref, wr_ref, br_ref, tri_ref,
                x1_ref, x1p_ref, rt_ref, rtt_ref, cnt_ref, *, tiles_per_seq):
    i = pl.program_id(0)

    @pl.when(i == 0)
    def _():
        cnt_ref[...] = jnp.zeros(cnt_ref.shape, F32)

    tm = x_ref.shape[0]
    u = cc_ref[...].astype(F32) * cx_ref[...].astype(F32)
    first = (i % tiles_per_seq) == 0
    last = (i % tiles_per_seq) == tiles_per_seq - 1
    up = ccp_ref[SUBLANES - 1:SUBLANES, :].astype(F32) * cxp_ref[SUBLANES - 1:SUBLANES, :].astype(F32)
    un = ccn_ref[0:1, :].astype(F32) * cxn_ref[0:1, :].astype(F32)
    up = jnp.where(first, 0.0, up)
    un = jnp.where(last, 0.0, un)
    row = lax.broadcasted_iota(jnp.int32, u.shape, 0)
    u_prev = jnp.where(row == 0, up, pltpu.roll(u, 1, 0))
    u_next = jnp.where(row == tm - 1, un, pltpu.roll(u, tm - 1, 0))
    conv = cw_ref[0:1, :] * u_prev + cw_ref[1:2, :] * u + cw_ref[2:3, :] * u_next
    c = (cb_ref[...].astype(F32) * conv).astype(BF16)
    g_a = jax.nn.sigmoid(ga_ref[...].astype(F32) + bg_ref[:, :D_MODEL])
    g_c = jax.nn.sigmoid(gc_ref[...].astype(F32) + bg_ref[:, D_MODEL:])
    merged = (g_a * jnp.dot(a_ref[...], wa_ref[...], preferred_element_type=F32)
              + g_c * jnp.dot(c, wc_ref[...], preferred_element_type=F32))
    m = jnp.dot(merged.astype(BF16), wo_ref[...], preferred_element_type=F32)
    x1 = _layer_norm(DEEPNORM_ALPHA * x_ref[...] + m, g1_ref[...], b1_ref[...])
    x1_ref[...] = x1
    x1p_ref[...] = _pack_row(x1)
    logits =jnp.dot(x1.astype(BF16), wr_ref[...], preferred_element_type=F32) + br_ref[...]

    lane = lax.broadcasted_iota(jnp.int32, logits.shape, 1)
    lane_f = lane.astype(F32)
    work = jnp.where(lane < N_EXPERTS, logits, -jnp.inf)
    hots, vals, idxs = [], [], []
    for _ in range(TOP_K):
        mx = jnp.max(work, axis=-1, keepdims=True)
        ix = jnp.min(jnp.where(work == mx, lane_f, float(ROUTER_COLS)), axis=-1, keepdims=True)
        hot = lane_f == ix
        work = jnp.where(hot, -jnp.inf, work)
        hots.append(hot)
        vals.append(mx)
        idxs.append(ix)
    exps = [jnp.exp(v - vals[0]) for v in vals]
    denom = exps[0] + exps[1] + exps[2] + exps[3]
    picked = jnp.zeros(logits.shape, F32)
    for hot in hots:
        picked = picked + hot.astype(F32)
    before = cnt_ref[...] + jnp.dot(tri_ref[...], picked.astype(BF16), preferred_element_type=F32)
    out = jnp.zeros(logits.shape, F32)
    for j in range(TOP_K):
        rank = jnp.sum(jnp.where(hots[j], before, 0.0), axis=-1, keepdims=True)
        out = jnp.where(lane == RT_EXPERT + j, idxs[j], out)
        out = jnp.where(lane == RT_RANK + j, rank, out)
        out = jnp.where(lane == RT_GATE + j, exps[j] / denom, out)
    rt_ref[...] = out
    rtt_ref[...] = out.T[:RT_ROWS, :]
    cnt_ref[...] = cnt_ref[...] + jnp.sum(picked, axis=0, keepdims=True)


def _mix(a2d, h2d, x2d, seq, bg, cw, wa, wc, wo, g1, b1, wr, br):
    n = x2d.shape[0]
    tm = MIX_TM
    tiles_per_seq = seq // tm
    halo_blocks = tm // SUBLANES
    n_halo = n // SUBLANES

    def col(blk):
        return pl.BlockSpec((tm, COL_BLOCK), lambda i: (i, blk))

    def prev(blk):
        return pl.BlockSpec((SUBLANES, COL_BLOCK), lambda i: (jnp.maximum(i * halo_blocks - 1, 0), blk))

    def nxt(blk):
        return pl.BlockSpec((SUBLANES, COL_BLOCK),
                            lambda i: (jnp.minimum((i + 1) * halo_blocks, n_halo - 1), blk))

    def full(shape):
        return pl.BlockSpec(shape, lambda i: (0,) * len(shape))

    tri = (lax.broadcasted_iota(jnp.int32, (tm, tm), 1)
           < lax.broadcasted_iota(jnp.int32, (tm, tm), 0)).astype(BF16)
    kern = functools.partial(_mix_kernel, tiles_per_seq=tiles_per_seq)
    return pl.pallas_call(
        kern,
        out_shape=(jax.ShapeDtypeStruct((n, D_MODEL), F32),
                   jax.ShapeDtypeStruct((n, HALF), jnp.int32),
                   jax.ShapeDtypeStruct((n, ROUTER_COLS), F32),
                   jax.ShapeDtypeStruct((RT_ROWS, n), F32),
                   jax.ShapeDtypeStruct((1, ROUTER_COLS), F32)),
        grid=(n // tm,),
        in_specs=[
            pl.BlockSpec((tm, D_MODEL), lambda i: (i, 0)),
            col(CB_BLK), col(CC_BLK), col(CX_BLK), col(GA_BLK), col(GC_BLK),
            prev(CC_BLK), prev(CX_BLK), nxt(CC_BLK), nxt(CX_BLK),
            pl.BlockSpec((tm, D_MODEL), lambda i: (i, 0)),
            full((1, 2 * D_MODEL)), full((SUBLANES, D_MODEL)),
            full((D_MODEL, D_MODEL)), full((D_MODEL, D_MODEL)), full((D_MODEL, D_MODEL)),
            full((1, D_MODEL)), full((1, D_MODEL)),
            full((D_MODEL, ROUTER_COLS)), full((1, ROUTER_COLS)), full((tm, tm)),
        ],
        out_specs=(pl.BlockSpec((tm, D_MODEL), lambda i: (i, 0)),
                   pl.BlockSpec((tm, HALF), lambda i: (i, 0)),
                   pl.BlockSpec((tm, ROUTER_COLS), lambda i: (i, 0)),
                   pl.BlockSpec((RT_ROWS, tm), lambda i: (0, i)),
                   pl.BlockSpec((1, ROUTER_COLS), lambda i: (0, 0))),
        compiler_params=pltpu.CompilerParams(
            dimension_semantics=("arbitrary",), vmem_limit_bytes=VMEM_LIMIT),
        name="mix_ln1_router",
    )(a2d, h2d, h2d, h2d, h2d, h2d, h2d, h2d, h2d, h2d, x2d, bg, cw, wa, wc, wo, g1, b1, wr, br, tri)


def _moe_kernel(be_ref, nv_ref, nu_ref, xs_ref, wg_ref, bg_ref, wu_ref, bu_ref, wd_ref, bd_ref, o_ref,
                wg_s, wu_s, wd_s):
    i = pl.program_id(0)
    used = i < nu_ref[0]

    @pl.when(used & ((i == 0) | (be_ref[i] != be_ref[jnp.maximum(i - 1, 0)])))
    def _():
        wg_s[...] = wg_ref[...].astype(BF16)
        wu_s[...] = wu_ref[...].astype(BF16)
        wd_s[...] = wd_ref[...].astype(BF16)

    @pl.when(used)
    def _():
        row = lax.broadcasted_iota(jnp.int32, xs_ref.shape, 0)
        x = _unpack_row(jnp.where(row < nv_ref[i], xs_ref[...], 0)).astype(BF16)
        hg = jnp.minimum(jnp.dot(x, wg_s[...], preferred_element_type=F32) + bg_ref[...], SWIGLU_LIMIT)
        hu = jnp.clip(jnp.dot(x, wu_s[...], preferred_element_type=F32) + bu_ref[...],
                      -SWIGLU_LIMIT, SWIGLU_LIMIT)
        act = hg * jax.nn.sigmoid(SWIGLU_ALPHA * hg) * (hu + 1.0)
        y = jnp.dot(act.astype(BF16), wd_s[...], preferred_element_type=F32) + bd_ref[...]
        o_ref[...] = _pack_row(y)

    @pl.when(jnp.logical_not(used))
    def _():
        o_ref[...] = jnp.zeros(o_ref.shape, o_ref.dtype)


def _moe_experts(block_e, n_valid, n_used, xs, wg, bg, wu, bu, wd, bd):
    p = xs.shape[0]
    tm = MOE_TM
    nb = p // tm

    def row_map(i, be, nv, nu):
        return (jnp.minimum(i, nu[0] - 1), 0)

    def w_map(i, be, nv, nu):
        return (be[jnp.minimum(i, nu[0] - 1)], 0, 0)

    w_spec = pl.BlockSpec((None, D_MODEL, D_MODEL), w_map)
    b_spec = pl.BlockSpec((None, 1, D_MODEL), w_map)
    return pl.pallas_call(
        _moe_kernel,
        out_shape=jax.ShapeDtypeStruct((p, HALF), jnp.int32),
        grid_spec=pltpu.PrefetchScalarGridSpec(
            num_scalar_prefetch=3,
            grid=(nb,),
            in_specs=[
                pl.BlockSpec((tm, HALF), row_map),
                w_spec, b_spec, w_spec, b_spec, w_spec, b_spec,
            ],
            out_specs=pl.BlockSpec((tm, HALF), lambda i, be, nv, nu: (i, 0)),
            scratch_shapes=[pltpu.VMEM((D_MODEL, D_MODEL), BF16)] * 3,
        ),
        compiler_params=pltpu.CompilerParams(
            dimension_semantics=("arbitrary",), vmem_limit_bytes=VMEM_LIMIT),
        name="moe_experts",
    )(block_e, n_valid, n_used, xs, wg, bg, wu, bu, wd, bd)


def _sc_mesh():
    return plsc.VectorSubcoreMesh(core_axis_name="core", subcore_axis_name="subcore")


def _sc_index_rows(dest_t):
    k, n = dest_t.shape
    w = SC_WINDOW
    return jnp.pad(dest_t.reshape(k * n // w, w), ((0, 0), (0, LANES - w)))


def _sc_dispatch(x1, slot_idx, p):
    n, d = x1.shape
    w = SC_WINDOW
    windows = n // w

    @pl.kernel(out_type=jax.ShapeDtypeStruct((p, d), x1.dtype), mesh=_sc_mesh(), scratch_types=[],
               name="sc_dispatch")
    def run(x_hbm, d_hbm, o_hbm):
        def body(x_vmem, *d_vmems):
            for d_vmem in d_vmems:
                pltpu.sync_copy(x_vmem, o_hbm.at[d_vmem.at[0, pl.ds(0, w)]])

        pltpu.emit_pipeline(
            body,
            grid=(windows,),
            in_specs=[pl.BlockSpec((w, d), lambda i: (i, 0))]
            + [pl.BlockSpec((1, LANES), functools.partial(lambda k, i: (k * windows + i, 0), k))
               for k in range(TOP_K)],
            out_specs=[],
            core_axis_name=("core", "subcore"),
            dimension_semantics=(pltpu.PARALLEL,),
        )(x_hbm, *([d_hbm] * TOP_K))

    return run(x1, slot_idx)


def _sc_gather(ys, idx):
    w = SC_WINDOW
    r = idx.shape[0] * w
    d = ys.shape[1]

    @pl.kernel(out_type=jax.ShapeDtypeStruct((r, d), ys.dtype), mesh=_sc_mesh(), scratch_types=[],
               name="sc_gather")
    def run(y_hbm, i_hbm, o_hbm):
        def body(i_vmem, o_vmem):
            pltpu.sync_copy(y_hbm.at[i_vmem.at[0, pl.ds(0, w)]], o_vmem)

        pltpu.emit_pipeline(
            body,
            grid=(r // w,),
            in_specs=[pl.BlockSpec((1, LANES), lambda i: (i, 0))],
            out_specs=[pl.BlockSpec((w, d), lambda i: (i, 0))],
            core_axis_name=("core", "subcore"),
            dimension_semantics=(pltpu.PARALLEL,),
        )(i_hbm, o_hbm)

    return run(ys, idx)


def _ln2_kernel(x1_ref, yg_ref, rt_ref, g_ref, b_ref, o_ref):
    f = jnp.zeros(x1_ref.shape, F32)
    for k in range(TOP_K):
        f = f + _unpack_row(yg_ref[k]) * rt_ref[:, RT_GATE + k:RT_GATE + k + 1]
    o_ref[...] = _layer_norm(DEEPNORM_ALPHA * x1_ref[...] + f, g_ref[...], b_ref[...])


def _combine_ln2(x1, yg, rt, g2, b2):
    n = x1.shape[0]
    tm = LN2_TM
    return pl.pallas_call(
        _ln2_kernel,
        out_shape=jax.ShapeDtypeStruct((n, D_MODEL), F32),
        grid=(n // tm,),
        in_specs=[
            pl.BlockSpec((tm, D_MODEL), lambda i: (i, 0)),
            pl.BlockSpec((TOP_K, tm, HALF), lambda i: (0, i, 0)),
            pl.BlockSpec((tm, ROUTER_COLS), lambda i: (i, 0)),
            pl.BlockSpec((1, D_MODEL), lambda i: (0, 0)),
            pl.BlockSpec((1, D_MODEL), lambda i: (0, 0)),
        ],
        out_specs=pl.BlockSpec((tm, D_MODEL), lambda i: (i, 0)),
        compiler_params=pltpu.CompilerParams(
            dimension_semantics=("parallel",), vmem_limit_bytes=VMEM_LIMIT),
        name="combine_ln2",
    )(x1, yg, rt, g2, b2)


def _rotary_tables(seq):
    inv = ROPE_THETA ** (-jnp.arange(0, ROT_DIM, 2, dtype=F32) / ROT_DIM)
    ang = jnp.arange(seq, dtype=F32)[:, None] * inv[None, :]
    cos, sin = jnp.cos(ang), jnp.sin(ang)
    ones = jnp.ones((seq, QK_DIM - ROT_DIM), F32)
    zeros = jnp.zeros((seq, QK_DIM - ROT_DIM), F32)
    zh = jnp.zeros((seq, ROT_HALF), F32)
    cos64 = jnp.concatenate([cos, cos, ones], axis=1)
    sa64 = jnp.concatenate([-sin, zh, zeros], axis=1)
    sb64 = jnp.concatenate([zh, sin, zeros], axis=1)
    rep = LANES // QK_DIM
    return (jnp.tile(cos64, (1, rep)), jnp.tile(sa64, (1, rep)), jnp.tile(sb64, (1, rep)), cos.T, sin.T)


def _slot_layout(rtt, counts, tm):
    n = rtt.shape[1]
    p = n * TOP_K + N_EXPERTS * tm
    nb = p // tm
    expert = rtt[RT_EXPERT:RT_EXPERT + TOP_K].astype(jnp.int32)
    rank = rtt[RT_RANK:RT_RANK + TOP_K].astype(jnp.int32)
    padded = ((counts + tm - 1) // tm) * tm
    pad_ends = jnp.cumsum(padded)
    pad_starts = pad_ends - padded
    ids = jnp.arange(N_EXPERTS, dtype=jnp.int32)[:, None, None]
    dest_t = jnp.sum(jnp.where(expert[None] == ids, pad_starts[:, None, None], 0), axis=0) + rank
    block_start = jnp.arange(nb, dtype=jnp.int32) * tm
    block_e = jnp.minimum(jnp.sum(pad_ends[None, :] <= block_start[:, None], axis=1),
                          N_EXPERTS - 1).astype(jnp.int32)
    n_valid = jnp.clip(pad_starts[block_e] + counts[block_e] - block_start, 0, tm).astype(jnp.int32)
    n_used = (pad_ends[-1] // tm).astype(jnp.int32).reshape(1)
    return dest_t.astype(jnp.int32), block_e, n_valid, n_used, p


def _trunk(x, prm):
    b, s, d = x.shape
    n = b * s
    x2d = x.reshape(n, d)
    h, qt, vt = _inproj(x2d, prm["w_in"], prm["w_q_t"], prm["w_v_t"], _rotary_tables(s), b, s)
    a = _attention(h.reshape(b, s, H_COLS), qt, vt, prm["lam"], prm["subln_g"], prm["lam_init"])
    x1, x1p, rt, rtt, cnt = _mix(a.reshape(n, d), h, x2d, s, prm["b_gate"], prm["conv_w"], prm["w_attn_br"],
                                 prm["w_conv_br"], prm["w_o"], prm["ln1_g"], prm["ln1_b"],
                                 prm["w_router"], prm["b_router"])
    counts = cnt[0, :N_EXPERTS].astype(jnp.int32)
    dest_t, block_e, n_valid, n_used, p = _slot_layout(rtt, counts, MOE_TM)
    slot_idx = _sc_index_rows(dest_t)
    xs = _sc_dispatch(x1p, slot_idx, p)
    ys = _moe_experts(block_e, n_valid, n_used, xs, prm["w_exp_gate"], prm["b_exp_gate"],
                      prm["w_exp_up"], prm["b_exp_up"], prm["w_exp_down"], prm["b_exp_down"])
    yg = _sc_gather(ys, slot_idx).reshape(TOP_K, n, HALF)
    y = _combine_ln2(x1, yg, rt, prm["ln2_g"], prm["ln2_b"])
    return y.reshape(b, s, d)


def kernel(x_prompt, x_sample, w_in, b_branch_gate, lambda_q1, lambda_k1, lambda_q2, lambda_k2, subln_g,
           conv_w, w_attn_br, w_conv_br, w_o, ln1_g, ln1_b, w_router, b_router, w_exp_gate, b_exp_gate,
           w_exp_up, b_exp_up, w_exp_down, b_exp_down, ln2_g, ln2_b):
    l = 0
    lam_init = 0.8 - 0.6 * math.exp(-0.3 * l)
    lam = (jnp.exp(jnp.sum(lambda_q1[l].astype(F32) * lambda_k1[l].astype(F32)))
           - jnp.exp(jnp.sum(lambda_q2[l].astype(F32) * lambda_k2[l].astype(F32))) + lam_init)
    prm = {
        "lam_init": lam_init,
        "lam": lam.reshape(1).astype(F32),
        "w_in": w_in[l].astype(BF16),
        "w_q_t": w_in[l][:, Q_STEP * COL_BLOCK:(Q_STEP + 1) * COL_BLOCK].T.astype(BF16),
        "w_v_t": w_in[l][:, V_STEP * COL_BLOCK:(V_STEP + 1) * COL_BLOCK].T.astype(BF16),
        "b_gate": b_branch_gate[l].reshape(1, 2 * D_MODEL),
        "subln_g": subln_g[l].reshape(1, V_DIM),
        "conv_w": jnp.pad(conv_w[l], ((0, SUBLANES - conv_w.shape[1]), (0, 0))),
        "w_attn_br": w_attn_br[l].astype(BF16),
        "w_conv_br": w_conv_br[l].astype(BF16),
        "w_o": w_o[l].astype(BF16),
        "ln1_g": ln1_g[l].reshape(1, D_MODEL),
        "ln1_b": ln1_b[l].reshape(1, D_MODEL),
        "w_router": jnp.pad(w_router[l], ((0, 0), (0, ROUTER_COLS - N_EXPERTS))).astype(BF16),
        "b_router": jnp.pad(b_router[l], (0, ROUTER_COLS - N_EXPERTS)).reshape(1, ROUTER_COLS),
        "w_exp_gate": w_exp_gate[l],
        "b_exp_gate": b_exp_gate[l].reshape(N_EXPERTS, 1, D_MODEL),
        "w_exp_up": w_exp_up[l],
        "b_exp_up": b_exp_up[l].reshape(N_EXPERTS, 1, D_MODEL),
        "w_exp_down": w_exp_down[l],
        "b_exp_down": b_exp_down[l].reshape(N_EXPERTS, 1, D_MODEL),
        "ln2_g": ln2_g[l].reshape(1, D_MODEL),
        "ln2_b": ln2_b[l].reshape(1, D_MODEL),
    }
    return _trunk(x_prompt, prm), _trunk(x_sample, prm)
```

```python
import functools
import math

import jax
import jax.numpy as jnp
from jax import lax
from jax.experimental import pallas as pl
from jax.experimental.pallas import tpu as pltpu
from jax.experimental.pallas import tpu_sc as plsc

F32 = jnp.float32
BF16 = jnp.bfloat16

D_MODEL = 1024
N_HEADS = 8
QK_DIM = 64
V_DIM = 128
HEAD_COLS = 2 * QK_DIM
ROT_DIM = QK_DIM // 4
ROT_HALF = ROT_DIM // 2
ROPE_THETA = 500000.0
SUBLN_EPS = 1e-5
IN_COLS = 8192
COL_BLOCK = 1024
Q_STEP, K_STEP, V_STEP = 0, 1, 2
H_COLS = IN_COLS - 2 * COL_BLOCK
CB_BLK, CC_BLK, CX_BLK, GA_BLK, GC_BLK = 1, 2, 3, 4, 5
K_HEAD0 = 0
N_EXPERTS = 32
TOP_K = 4
SWIGLU_LIMIT = 7.0
SWIGLU_ALPHA = 1.702
LN_EPS = 1e-5
DEPTH = 1
DEEPNORM_ALPHA = (2 * DEPTH) ** 0.25
LANES = 128
SUBLANES = 8
ROUTER_COLS = LANES
RT_EXPERT, RT_RANK, RT_GATE = 0, 4, 8
RT_ROWS = 16
SC_WINDOW = 32
ONES_ROWS = 16
Q_SCALE = math.log2(math.e) * QK_DIM ** -0.5

VMEM_LIMIT = 56 * 1024 * 1024

PROJ_TM = 1024
ATTN_TQ = 1024
ATTN_TK = 512
MIX_TM = 512
MIX_SUB = 256
MOE_TM = 512
LN2_TM = 1024


def _rotate_rows(t1, t2, cos, sin):
    return t1 * cos - t2 * sin, t2 * cos + t1 * sin


def _inproj_kernel(x_ref, w_ref, wqt_ref, wvt_ref, cos_ref, sa_ref, sb_ref, cosr_ref, sinr_ref,
                   o_ref, qt_ref, vt_ref):
    j = pl.program_id(1)
    xb = x_ref[...].astype(BF16)
    nt = (((1,), (1,)), ((), ()))

    @pl.when(j == Q_STEP)
    def _():
        acc_t = lax.dot_general(wqt_ref[...], xb, nt, preferred_element_type=F32) * Q_SCALE
        cos, sin = cosr_ref[...], sinr_ref[...]
        for hh in range(N_HEADS):
            parts = []
            for mp in range(2):
                o = hh * HEAD_COLS + mp * QK_DIM
                parts += _rotate_rows(acc_t[o:o + ROT_HALF], acc_t[o + ROT_HALF:o + ROT_DIM], cos, sin)
                parts.append(acc_t[o + ROT_DIM:o + QK_DIM])
            qt_ref[hh, 0] = jnp.concatenate(parts, axis=0).astype(BF16)

    @pl.when(j == V_STEP)
    def _():
        acc_t = lax.dot_general(wvt_ref[...], xb, nt, preferred_element_type=F32)
        n_heads, n_chunks, _, tk = vt_ref.shape
        for hh in range(n_heads):
            for cc in range(n_chunks):
                vt_ref[hh, cc, :V_DIM, :] = acc_t[hh * V_DIM:(hh + 1) * V_DIM,
                                                  cc * tk:(cc + 1) * tk].astype(BF16)
                vt_ref[hh, cc, V_DIM:, :] = jnp.ones((ONES_ROWS, tk), BF16)

    @pl.when(j > V_STEP)
    def _():
        o_ref[...] = jnp.dot(xb, w_ref[...], preferred_element_type=F32).astype(o_ref.dtype)

    @pl.when(j == K_STEP)
    def _():
        acc = jnp.dot(xb, w_ref[...], preferred_element_type=F32)
        cos, sa, sb = cos_ref[...], sa_ref[...], sb_ref[...]
        for c in range(COL_BLOCK // LANES):
            a = acc[:, c * LANES:(c + 1) * LANES]
            r = (a * cos + pltpu.roll(a, LANES - ROT_HALF, 1) * sa + pltpu.roll(a, ROT_HALF, 1) * sb)
            o_ref[:, c * LANES:(c + 1) * LANES] = r.astype(o_ref.dtype)


def _inproj(x2d, w_in_b, wqt_b, wvt_b, tables, batch, seq):
    n = x2d.shape[0]
    tm, tk = PROJ_TM, ATTN_TK
    assert tm == ATTN_TQ
    pos_blocks = seq // tm
    cos_t, sa_t, sb_t, cos_r, sin_r = tables
    tab_spec = pl.BlockSpec((tm, LANES), lambda i, j: (i % pos_blocks, 0))
    rtab_spec = pl.BlockSpec((ROT_HALF, tm), lambda i, j: (0, i % pos_blocks))
    wt_spec = pl.BlockSpec((COL_BLOCK, D_MODEL), lambda i, j: (0, 0))
    return pl.pallas_call(
        _inproj_kernel,
        out_shape=(jax.ShapeDtypeStruct((n, H_COLS), BF16),
                   jax.ShapeDtypeStruct((batch, N_HEADS, seq // tm, HEAD_COLS, tm), BF16),
                   jax.ShapeDtypeStruct((batch, N_HEADS, seq // tk, V_DIM + ONES_ROWS, tk), BF16)),
        grid=(n // tm, IN_COLS // COL_BLOCK),
        in_specs=[
            pl.BlockSpec((tm, D_MODEL), lambda i, j: (i, 0)),
            pl.BlockSpec((D_MODEL, COL_BLOCK), lambda i, j: (0, jnp.where(j <= V_STEP, K_STEP, j))),
            wt_spec, wt_spec, tab_spec, tab_spec, tab_spec, rtab_spec, rtab_spec,
        ],
        out_specs=(
            pl.BlockSpec((tm, COL_BLOCK), lambda i, j: (i, jnp.maximum(j - V_STEP, 0))),
            pl.BlockSpec((None, N_HEADS, 1, HEAD_COLS, tm),
                         lambda i, j: (i // pos_blocks, 0, i % pos_blocks, 0, 0)),
            pl.BlockSpec((None, N_HEADS, tm // tk, V_DIM + ONES_ROWS, tk),
                         lambda i, j: (i // pos_blocks, 0, i % pos_blocks, 0, 0)),
        ),
        compiler_params=pltpu.CompilerParams(
            dimension_semantics=("parallel", "arbitrary"), vmem_limit_bytes=VMEM_LIMIT),
        name="inproj",
    )(x2d, w_in_b, wqt_b, wvt_b, cos_t, sa_t, sb_t, cos_r, sin_r)


def _attn_kernel(lam_ref, q_ref, k_ref, vt_ref, g_ref, o_ref, qt_s, acc_s, sa0, sa1, sb0, sb1, *, out_scale):
    n_q, _, tq = q_ref.shape
    n_chunks, _, tk = vt_ref.shape
    sa_s, sb_s = (sa0, sa1), (sb0, sb1)

    def load_q(qi):
        zero_half = jnp.zeros((QK_DIM, tq), BF16)
        qt_s[0, :QK_DIM, :] = q_ref[qi, :QK_DIM, :]
        qt_s[0, QK_DIM:, :] = zero_half
        qt_s[1, :QK_DIM, :] = zero_half
        qt_s[1, QK_DIM:, :] = q_ref[qi, QK_DIM:, :]

    def scores(c, s_refs, mp):
        k = k_ref[pl.ds(pl.multiple_of(c * tk, tk), tk), :]
        s = jnp.dot(k, qt_s[mp], preferred_element_type=F32)
        s_refs[mp][...] = s
        return jnp.max(s, axis=0, keepdims=True)

    def update(c, s_refs, mp, m_old, c_max):
        m_new = jnp.maximum(m_old, c_max)
        alpha = jnp.exp2(m_old - m_new)
        p = jnp.exp2(s_refs[mp][...] - m_new).astype(BF16)
        acc_s[mp] = alpha * acc_s[mp] + jnp.dot(vt_ref[c], p, preferred_element_type=F32)
        return m_new

    def step(c, cur, nxt, ms, cur_max, prefetch):
        new_ms, nxt_max = [], []
        for mp in range(2):
            if prefetch:
                nxt_max.append(scores(c + 1, nxt, mp))
            new_ms.append(update(c, cur, mp, ms[mp], cur_max[mp]))
        return tuple(new_ms), tuple(nxt_max)

    def pair(i, carry):
        ms, a_max = carry
        ms, b_max = step(2 * i, sa_s, sb_s, ms, a_max, True)
        return step(2 * i + 1, sb_s, sa_s, ms, b_max, True)

    def finalize(qi):
        lam = lam_ref[0]
        o1 = acc_s[0, :V_DIM, :] / acc_s[0, V_DIM:V_DIM + 1, :]
        o2 = acc_s[1, :V_DIM, :] / acc_s[1, V_DIM:V_DIM + 1, :]
        ot = o1 - lam * o2
        ot = ot * lax.rsqrt(jnp.mean(ot * ot, axis=0, keepdims=True) + SUBLN_EPS)
        rows = pl.ds(pl.multiple_of(qi * tq, tq), tq)
        o_ref[rows, :] = (ot.T * (g_ref[...] * out_scale)).astype(o_ref.dtype)

    def q_block(qi, a_max, last):
        acc_s[...] = jnp.zeros(acc_s.shape, F32)
        m0 = jnp.full((1, tq), -jnp.inf, F32)
        ms, a_max = lax.fori_loop(0, n_chunks // 2 - 1, pair, ((m0, m0), a_max))
        ms, b_max = step(n_chunks - 2, sa_s, sb_s, ms, a_max, True)
        step(n_chunks - 1, sb_s, sa_s, ms, b_max, False)
        if not last:
            load_q(qi + 1)
            a_max = tuple(scores(0, sa_s, mp) for mp in range(2))
        finalize(qi)
        return a_max

    load_q(0)
    a_max = tuple(scores(0, sa_s, mp) for mp in range(2))
    a_max = lax.fori_loop(0, n_q - 1, lambda qi, am: q_block(qi, am, False), a_max)
    q_block(n_q - 1, a_max, True)


def _attention(h3, qt, vt, lam, subln_g, lam_init):
    b, s, _ = h3.shape
    tq, tk = ATTN_TQ, ATTN_TK
    n_chunks = s // tk
    assert n_chunks % 2 == 0 and n_chunks >= 2
    vt_rows = V_DIM + ONES_ROWS
    kern = functools.partial(_attn_kernel, out_scale=1.0 - lam_init)
    return pl.pallas_call(
        kern,
        out_shape=jax.ShapeDtypeStruct((b, s, N_HEADS * V_DIM), BF16),
        grid=(b, N_HEADS),
        in_specs=[
            pl.BlockSpec(memory_space=pltpu.SMEM),
            pl.BlockSpec((None, None, s // tq, HEAD_COLS, tq), lambda bi, hi: (bi, hi, 0, 0, 0)),
            pl.BlockSpec((None, s, HEAD_COLS), lambda bi, hi: (bi, 0, K_HEAD0 + hi)),
            pl.BlockSpec((None, None, n_chunks, vt_rows, tk), lambda bi, hi: (bi, hi, 0, 0, 0)),
            pl.BlockSpec((1, V_DIM), lambda bi, hi: (0, 0)),
        ],
        out_specs=pl.BlockSpec((None, s, V_DIM), lambda bi, hi: (bi, 0, hi)),
        scratch_shapes=[
            pltpu.VMEM((2, HEAD_COLS, tq), BF16),
            pltpu.VMEM((2, vt_rows, tq), F32),
            pltpu.VMEM((tk, tq), F32), pltpu.VMEM((tk, tq), F32),
            pltpu.VMEM((tk, tq), F32), pltpu.VMEM((tk, tq), F32),
        ],
        compiler_params=pltpu.CompilerParams(
            dimension_semantics=("parallel", "arbitrary"),
            vmem_limit_bytes=VMEM_LIMIT),
        name="diff_attention",
    )(lam, qt, h3, vt, subln_g)


HALF = D_MODEL // 2


def _pack_row(x):
    return pltpu.pack_elementwise([x[:, :HALF], x[:, HALF:]], packed_dtype=BF16)


def _unpack_row(w):
    lo = pltpu.unpack_elementwise(w, index=0, packed_dtype=BF16, unpacked_dtype=F32)
    hi = pltpu.unpack_elementwise(w, index=1, packed_dtype=BF16, unpacked_dtype=F32)
    return jnp.concatenate([lo, hi], axis=1)


def _layer_norm(y, g, b):
    mu = jnp.mean(y, axis=-1, keepdims=True)
    d = y - mu
    var = jnp.mean(d * d, axis=-1, keepdims=True)
    return d * lax.rsqrt(var + LN_EPS) * g + b


def _mix_kernel(a_ref, cb_ref, cc_ref, cx_ref, ga_ref, gc_ref, ccp_ref, cxp_ref, ccn_ref, cxn_ref,
                x_ref, bg_ref, cw_ref, wa_ref, wc_ref, wo_ref, g1_ref, b1_ref, wr_ref, br_ref, tri_ref,
                x1_ref, x1p_ref, rt_ref, rtt_ref, cnt_ref, *, tiles_per_seq):
    i = pl.program_id(0)

    @pl.when(i == 0)
    def _():
        cnt_ref[...] = jnp.zeros(cnt_ref.shape, F32)

    tm = x_ref.shape[0]
    u = cc_ref[...].astype(F32) * cx_ref[...].astype(F32)
    first = (i % tiles_per_seq) == 0
    last = (i % tiles_per_seq) == tiles_per_seq - 1
    up = ccp_ref[SUBLANES - 1:SUBLANES, :].astype(F32) * cxp_ref[SUBLANES - 1:SUBLANES, :].astype(F32)
    un = ccn_ref[0:1, :].astype(F32) * cxn_ref[0:1, :].astype(F32)
    up = jnp.where(first, 0.0, up)
    un = jnp.where(last, 0.0, un)
    row = lax.broadcasted_iota(jnp.int32, u.shape, 0)
    u_prev = jnp.where(row == 0, up, pltpu.roll(u, 1, 0))
    u_next = jnp.where(row == tm - 1, un, pltpu.roll(u, tm - 1, 0))
    conv = cw_ref[0:1, :] * u_prev + cw_ref[1:2, :] * u + cw_ref[2:3, :] * u_next
    c = (cb_ref[...].astype(F32) * conv).astype(BF16)

    sub = tri_ref.shape[0]
    counts = cnt_ref[...]
    for s in range(tm // sub):
        rows = slice(s * sub, (s + 1) * sub)
        g_a = jax.nn.sigmoid(ga_ref[rows, :].astype(F32) + bg_ref[:, :D_MODEL])
        g_c = jax.nn.sigmoid(gc_ref[rows, :].astype(F32) + bg_ref[:, D_MODEL:])
        merged = (g_a * jnp.dot(a_ref[rows, :], wa_ref[...], preferred_element_type=F32)
                  + g_c * jnp.dot(c[rows, :], wc_ref[...], preferred_element_type=F32))
        m = jnp.dot(merged.astype(BF16), wo_ref[...], preferred_element_type=F32)
        x1 = _layer_norm(DEEPNORM_ALPHA * x_ref[rows, :] + m, g1_ref[...], b1_ref[...])
        x1_ref[rows, :] = x1
        x1p_ref[rows, :] = _pack_row(x1)
        logits = jnp.dot(x1.astype(BF16), wr_ref[...], preferred_element_type=F32) + br_ref[...]

        lane = lax.broadcasted_iota(jnp.int32, logits.shape, 1)
        lane_f = lane.astype(F32)
        work = jnp.where(lane < N_EXPERTS, logits, -jnp.inf)
        hots, vals, idxs = [], [], []
        for _ in range(TOP_K):
            mx = jnp.max(work, axis=-1, keepdims=True)
            ix = jnp.min(jnp.where(work == mx, lane_f, float(ROUTER_COLS)), axis=-1, keepdims=True)
            hot = lane_f == ix
            work = jnp.where(hot, -jnp.inf, work)
            hots.append(hot)
            vals.append(mx)
            idxs.append(ix)
        exps = [jnp.exp(v - vals[0]) for v in vals]
        denom = exps[0] + exps[1] + exps[2] + exps[3]
        picked = jnp.zeros(logits.shape, F32)
        for hot in hots:
            picked = picked + hot.astype(F32)
        before = counts + jnp.dot(tri_ref[...], picked.astype(BF16), preferred_element_type=F32)
        out = jnp.zeros(logits.shape, F32)
        for j in range(TOP_K):
            rank = jnp.sum(jnp.where(hots[j], before, 0.0), axis=-1, keepdims=True)
            out = jnp.where(lane == RT_EXPERT + j, idxs[j], out)
            out = jnp.where(lane == RT_RANK + j, rank, out)
            out = jnp.where(lane == RT_GATE + j, exps[j] / denom, out)
        rt_ref[rows, :] = out
        rtt_ref[:, rows] = out.T[:RT_ROWS, :]
        counts = counts + jnp.sum(picked, axis=0, keepdims=True)
    cnt_ref[...] = counts


def _mix(a2d, h2d, x2d, seq, bg, cw, wa, wc, wo, g1, b1, wr, br):
    n = x2d.shape[0]
    tm = MIX_TM
    tiles_per_seq = seq // tm
    halo_blocks = tm // SUBLANES
    n_halo = n // SUBLANES

    def col(blk):
        return pl.BlockSpec((tm, COL_BLOCK), lambda i: (i, blk))

    def prev(blk):
        return pl.BlockSpec((SUBLANES, COL_BLOCK), lambda i: (jnp.maximum(i * halo_blocks - 1, 0), blk))

    def nxt(blk):
        return pl.BlockSpec((SUBLANES, COL_BLOCK),
                            lambda i: (jnp.minimum((i + 1) * halo_blocks, n_halo - 1), blk))

    def full(shape):
        return pl.BlockSpec(shape, lambda i: (0,) * len(shape))

    sub = MIX_SUB
    tri = (lax.broadcasted_iota(jnp.int32, (sub, sub), 1)
           < lax.broadcasted_iota(jnp.int32, (sub, sub), 0)).astype(BF16)
    kern = functools.partial(_mix_kernel, tiles_per_seq=tiles_per_seq)
    return pl.pallas_call(
        kern,
        out_shape=(jax.ShapeDtypeStruct((n, D_MODEL), F32),
                   jax.ShapeDtypeStruct((n, HALF), jnp.int32),
                   jax.ShapeDtypeStruct((n, ROUTER_COLS), F32),
                   jax.ShapeDtypeStruct((RT_ROWS, n), F32),
                   jax.ShapeDtypeStruct((1, ROUTER_COLS), F32)),
        grid=(n // tm,),
        in_specs=[
            pl.BlockSpec((tm, D_MODEL), lambda i: (i, 0)),
            col(CB_BLK), col(CC_BLK), col(CX_BLK), col(GA_BLK), col(GC_BLK),
            prev(CC_BLK), prev(CX_BLK), nxt(CC_BLK), nxt(CX_BLK),
            pl.BlockSpec((tm, D_MODEL), lambda i: (i, 0)),
            full((1, 2 * D_MODEL)), full((SUBLANES, D_MODEL)),
            full((D_MODEL, D_MODEL)), full((D_MODEL, D_MODEL)), full((D_MODEL, D_MODEL)),
            full((1, D_MODEL)), full((1, D_MODEL)),
            full((D_MODEL, ROUTER_COLS)), full((1, ROUTER_COLS)), full((sub, sub)),
        ],
        out_specs=(pl.BlockSpec((tm, D_MODEL), lambda i: (i, 0)),
                   pl.BlockSpec((tm, HALF), lambda i: (i, 0)),
                   pl.BlockSpec((tm, ROUTER_COLS), lambda i: (i, 0)),
                   pl.BlockSpec((RT_ROWS, tm), lambda i: (0, i)),
                   pl.BlockSpec((1, ROUTER_COLS), lambda i: (0, 0))),
        compiler_params=pltpu.CompilerParams(
            dimension_semantics=("arbitrary",), vmem_limit_bytes=VMEM_LIMIT),
        name="mix_ln1_router",
    )(a2d, h2d, h2d, h2d, h2d, h2d, h2d, h2d, h2d, h2d, x2d, bg, cw, wa, wc, wo, g1, b1, wr, br, tri)


def _moe_kernel(be_ref, nv_ref, nu_ref, xs_ref, wg_ref, bg_ref, wu_ref, bu_ref, wd_ref, bd_ref, o_ref,
                wg_s, wu_s, wd_s):
    i = pl.program_id(0)
    used = i < nu_ref[0]

    @pl.when(used & ((i == 0) | (be_ref[i] != be_ref[jnp.maximum(i - 1, 0)])))
    def _():
        wg_s[...] = wg_ref[...].astype(BF16)
        wu_s[...] = wu_ref[...].astype(BF16)
        wd_s[...] = wd_ref[...].astype(BF16)

    @pl.when(used)
    def _():
        row = lax.broadcasted_iota(jnp.int32, xs_ref.shape, 0)
        x = _unpack_row(jnp.where(row < nv_ref[i], xs_ref[...], 0)).astype(BF16)
        hg = jnp.minimum(jnp.dot(x, wg_s[...], preferred_element_type=F32) + bg_ref[...], SWIGLU_LIMIT)
        hu = jnp.clip(jnp.dot(x, wu_s[...], preferred_element_type=F32) + bu_ref[...],
                      -SWIGLU_LIMIT, SWIGLU_LIMIT)
        act = hg * jax.nn.sigmoid(SWIGLU_ALPHA * hg) * (hu + 1.0)
        y = jnp.dot(act.astype(BF16), wd_s[...], preferred_element_type=F32) + bd_ref[...]
        o_ref[...] = _pack_row(y)

    @pl.when(jnp.logical_not(used))
    def _():
        o_ref[...] = jnp.zeros(o_ref.shape, o_ref.dtype)


def _moe_experts(block_e, n_valid, n_used, xs, wg, bg, wu, bu, wd, bd):
    p = xs.shape[0]
    tm = MOE_TM
    nb = p // tm

    def row_map(i, be, nv, nu):
        return (jnp.minimum(i, nu[0] - 1), 0)

    def w_map(i, be, nv, nu):
        return (be[jnp.minimum(i, nu[0] - 1)], 0, 0)

    w_spec = pl.BlockSpec((None, D_MODEL, D_MODEL), w_map)
    b_spec = pl.BlockSpec((None, 1, D_MODEL), w_map)
    return pl.pallas_call(
        _moe_kernel,
        out_shape=jax.ShapeDtypeStruct((p, HALF), jnp.int32),
        grid_spec=pltpu.PrefetchScalarGridSpec(
            num_scalar_prefetch=3,
            grid=(nb,),
            in_specs=[
                pl.BlockSpec((tm, HALF), row_map),
                w_spec, b_spec, w_spec, b_spec, w_spec, b_spec,
            ],
            out_specs=pl.BlockSpec((tm, HALF), lambda i, be, nv, nu: (i, 0)),
            scratch_shapes=[pltpu.VMEM((D_MODEL, D_MODEL), BF16)] * 3,
        ),
        compiler_params=pltpu.CompilerParams(
            dimension_semantics=("arbitrary",), vmem_limit_bytes=VMEM_LIMIT),
        name="moe_experts",
    )(block_e, n_valid, n_used, xs, wg, bg, wu, bu, wd, bd)


def _sc_mesh():
    return plsc.VectorSubcoreMesh(core_axis_name="core", subcore_axis_name="subcore")


def _sc_index_rows(dest_t):
    k, n = dest_t.shape
    w = SC_WINDOW
    return jnp.pad(dest_t.reshape(k * n // w, w), ((0, 0), (0, LANES - w)))


def _sc_dispatch(x1, slot_idx, p):
    n, d = x1.shape
    w = SC_WINDOW
    windows = n // w

    @pl.kernel(out_type=jax.ShapeDtypeStruct((p, d), x1.dtype), mesh=_sc_mesh(), scratch_types=[],
               name="sc_dispatch")
    def run(x_hbm, d_hbm, o_hbm):
        def body(x_vmem, *d_vmems):
            for d_vmem in d_vmems:
                pltpu.sync_copy(x_vmem, o_hbm.at[d_vmem.at[0, pl.ds(0, w)]])

        pltpu.emit_pipeline(
            body,
            grid=(windows,),
            in_specs=[pl.BlockSpec((w, d), lambda i: (i, 0))]
            + [pl.BlockSpec((1, LANES), functools.partial(lambda k, i: (k * windows + i, 0), k))
               for k in range(TOP_K)],
            out_specs=[],
            core_axis_name=("core", "subcore"),
            dimension_semantics=(pltpu.PARALLEL,),
        )(x_hbm, *([d_hbm] * TOP_K))

    return run(x1, slot_idx)


def _sc_gather(ys, idx):
    w = SC_WINDOW
    r = idx.shape[0] * w
    d = ys.shape[1]

    @pl.kernel(out_type=jax.ShapeDtypeStruct((r, d), ys.dtype), mesh=_sc_mesh(), scratch_types=[],
               name="sc_gather")
    def run(y_hbm, i_hbm, o_hbm):
        def body(i_vmem, o_vmem):
            pltpu.sync_copy(y_hbm.at[i_vmem.at[0, pl.ds(0, w)]], o_vmem)

        pltpu.emit_pipeline(
            body,
            grid=(r // w,),
            in_specs=[pl.BlockSpec((1, LANES), lambda i: (i, 0))],
            out_specs=[pl.BlockSpec((w, d), lambda i: (i, 0))],
            core_axis_name=("core", "subcore"),
            dimension_semantics=(pltpu.PARALLEL,),
        )(i_hbm, o_hbm)

    return run(ys, idx)


def _ln2_kernel(x1_ref, yg_ref, rt_ref, g_ref, b_ref, o_ref):
    f = jnp.zeros(x1_ref.shape, F32)
    for k in range(TOP_K):
        f = f + _unpack_row(yg_ref[k]) * rt_ref[:, RT_GATE + k:RT_GATE + k + 1]
    o_ref[...] = _layer_norm(DEEPNORM_ALPHA * x1_ref[...] + f, g_ref[...], b_ref[...])


def _combine_ln2(x1, yg, rt, g2, b2):
    n = x1.shape[0]
    tm = LN2_TM
    return pl.pallas_call(
        _ln2_kernel,
        out_shape=jax.ShapeDtypeStruct((n, D_MODEL), F32),
        grid=(n // tm,),
        in_specs=[
            pl.BlockSpec((tm, D_MODEL), lambda i: (i, 0)),
            pl.BlockSpec((TOP_K, tm, HALF), lambda i: (0, i, 0)),
            pl.BlockSpec((tm, ROUTER_COLS), lambda i: (i, 0)),
            pl.BlockSpec((1, D_MODEL), lambda i: (0, 0)),
            pl.BlockSpec((1, D_MODEL), lambda i: (0, 0)),
        ],
        out_specs=pl.BlockSpec((tm, D_MODEL), lambda i: (i, 0)),
        compiler_params=pltpu.CompilerParams(
            dimension_semantics=("parallel",), vmem_limit_bytes=VMEM_LIMIT),
        name="combine_ln2",
    )(x1, yg, rt, g2, b2)


def _rotary_tables(seq):
    inv = ROPE_THETA ** (-jnp.arange(0, ROT_DIM, 2, dtype=F32) / ROT_DIM)
    ang = jnp.arange(seq, dtype=F32)[:, None] * inv[None, :]
    cos, sin = jnp.cos(ang), jnp.sin(ang)
    ones = jnp.ones((seq, QK_DIM - ROT_DIM), F32)
    zeros = jnp.zeros((seq, QK_DIM - ROT_DIM), F32)
    zh = jnp.zeros((seq, ROT_HALF), F32)
    cos64 = jnp.concatenate([cos, cos, ones], axis=1)
    sa64 = jnp.concatenate([-sin, zh, zeros], axis=1)
    sb64 = jnp.concatenate([zh, sin, zeros], axis=1)
    rep = LANES // QK_DIM
    return (jnp.tile(cos64, (1, rep)), jnp.tile(sa64, (1, rep)), jnp.tile(sb64, (1, rep)), cos.T, sin.T)


def _slot_layout(rtt, counts, tm):
    n = rtt.shape[1]
    p = n * TOP_K + N_EXPERTS * tm
    nb = p // tm
    expert = rtt[RT_EXPERT:RT_EXPERT + TOP_K].astype(jnp.int32)
    rank = rtt[RT_RANK:RT_RANK + TOP_K].astype(jnp.int32)
    padded = ((counts + tm - 1) // tm) * tm
    pad_ends = jnp.cumsum(padded)
    pad_starts = pad_ends - padded
    ids = jnp.arange(N_EXPERTS, dtype=jnp.int32)[:, None, None]
    dest_t = jnp.sum(jnp.where(expert[None] == ids, pad_starts[:, None, None], 0), axis=0) + rank
    block_start = jnp.arange(nb, dtype=jnp.int32) * tm
    block_e = jnp.minimum(jnp.sum(pad_ends[None, :] <= block_start[:, None], axis=1),
                          N_EXPERTS - 1).astype(jnp.int32)
    n_valid = jnp.clip(pad_starts[block_e] + counts[block_e] - block_start, 0, tm).astype(jnp.int32)
    n_used = (pad_ends[-1] // tm).astype(jnp.int32).reshape(1)
    return dest_t.astype(jnp.int32), block_e, n_valid, n_used, p


def _trunk(x, prm):
    b, s, d = x.shape
    n = b * s
    x2d = x.reshape(n, d)
    h, qt, vt = _inproj(x2d, prm["w_in"], prm["w_q_t"], prm["w_v_t"], _rotary_tables(s), b, s)
    a = _attention(h.reshape(b, s, H_COLS), qt, vt, prm["lam"], prm["subln_g"], prm["lam_init"])
    x1, x1p, rt, rtt, cnt = _mix(a.reshape(n, d), h, x2d, s, prm["b_gate"], prm["conv_w"], prm["w_attn_br"],
                                 prm["w_conv_br"], prm["w_o"], prm["ln1_g"], prm["ln1_b"],
                                 prm["w_router"], prm["b_router"])
    counts = cnt[0, :N_EXPERTS].astype(jnp.int32)
    dest_t, block_e, n_valid, n_used, p = _slot_layout(rtt, counts, MOE_TM)
    slot_idx = _sc_index_rows(dest_t)
    xs = _sc_dispatch(x1p, slot_idx, p)
    ys = _moe_experts(block_e, n_valid, n_used, xs, prm["w_exp_gate"], prm["b_exp_gate"],
                      prm["w_exp_up"], prm["b_exp_up"], prm["w_exp_down"], prm["b_exp_down"])
    yg = _sc_gather(ys, slot_idx).reshape(TOP_K, n, HALF)
    y = _combine_ln2(x1, yg, rt, prm["ln2_g"], prm["ln2_b"])
    return y.reshape(b, s, d)


def kernel(x_prompt, x_sample, w_in, b_branch_gate, lambda_q1, lambda_k1, lambda_q2, lambda_k2, subln_g,
           conv_w, w_attn_br, w_conv_br, w_o, ln1_g, ln1_b, w_router, b_router, w_exp_gate, b_exp_gate,
           w_exp_up, b_exp_up, w_exp_down, b_exp_down, ln2_g, ln2_b):
    l = 0
    lam_init = 0.8 - 0.6 * math.exp(-0.3 * l)
    lam = (jnp.exp(jnp.sum(lambda_q1[l].astype(F32) * lambda_k1[l].astype(F32)))
           - jnp.exp(jnp.sum(lambda_q2[l].astype(F32) * lambda_k2[l].astype(F32))) + lam_init)
    prm = {
        "lam_init": lam_init,
        "lam": lam.reshape(1).astype(F32),
        "w_in": w_in[l].astype(BF16),
        "w_q_t": w_in[l][:, Q_STEP * COL_BLOCK:(Q_STEP + 1) * COL_BLOCK].T.astype(BF16),
        "w_v_t": w_in[l][:, V_STEP * COL_BLOCK:(V_STEP + 1) * COL_BLOCK].T.astype(BF16),
        "b_gate": b_branch_gate[l].reshape(1, 2 * D_MODEL),
        "subln_g": subln_g[l].reshape(1, V_DIM),
        "conv_w": jnp.pad(conv_w[l], ((0, SUBLANES - conv_w.shape[1]), (0, 0))),
        "w_attn_br": w_attn_br[l].astype(BF16),
        "w_conv_br": w_conv_br[l].astype(BF16),
        "w_o": w_o[l].astype(BF16),
        "ln1_g": ln1_g[l].reshape(1, D_MODEL),
        "ln1_b": ln1_b[l].reshape(1, D_MODEL),
        "w_router": jnp.pad(w_router[l], ((0, 0), (0, ROUTER_COLS - N_EXPERTS))).astype(BF16),
        "b_router": jnp.pad(b_router[l], (0, ROUTER_COLS - N_EXPERTS)).reshape(1, ROUTER_COLS),
        "w_exp_gate": w_exp_gate[l],
        "b_exp_gate": b_exp_gate[l].reshape(N_EXPERTS, 1, D_MODEL),
        "w_exp_up": w_exp_up[l],
        "b_exp_up": b_exp_up[l].reshape(N_EXPERTS, 1, D_MODEL),
        "w_exp_down": w_exp_down[l],
        "b_exp_down": b_exp_down[l].reshape(N_EXPERTS, 1, D_MODEL),
        "ln2_g": ln2_g[l].reshape(1, D_MODEL),
        "ln2_b": ln2_b[l].reshape(1, D_MODEL),
    }
    return _trunk(x_prompt, prm), _trunk(x_sample, prm)
```

```python
import functools
import math

import jax
import jax.numpy as jnp
from jax import lax
from jax.experimental import pallas as pl
from jax.experimental.pallas import tpu as pltpu
from jax.experimental.pallas import tpu_sc as plsc

F32 = jnp.float32
BF16 = jnp.bfloat16

D_MODEL = 1024
N_HEADS = 8
QK_DIM = 64
V_DIM = 128
HEAD_COLS = 2 * QK_DIM
ROT_DIM = QK_DIM // 4
ROT_HALF = ROT_DIM // 2
ROPE_THETA = 500000.0
SUBLN_EPS = 1e-5
IN_COLS = 8192
COL_BLOCK = 1024
Q_STEP, K_STEP, V_STEP = 0, 1, 2
GATE_STEP = 6
H_COLS = IN_COLS - 2 * COL_BLOCK
CB_BLK, CC_BLK, CX_BLK, GA_BLK, GC_BLK = 1, 2, 3, 4, 5
K_HEAD0 = 0
N_EXPERTS = 32
TOP_K = 4
SWIGLU_LIMIT = 7.0
SWIGLU_ALPHA = 1.702
LN_EPS = 1e-5
DEPTH = 1
DEEPNORM_ALPHA = (2 * DEPTH) ** 0.25
LANES = 128
SUBLANES = 8
ROUTER_COLS = LANES
RT_EXPERT, RT_RANK, RT_GATE = 0, 4, 8
RT_ROWS = 16
SC_WINDOW = 32
ONES_ROWS = 16
Q_SCALE = math.log2(math.e) * QK_DIM ** -0.5

VMEM_LIMIT = 56 * 1024 * 1024

PROJ_TM = 1024
ATTN_TQ = 1024
ATTN_TK = 512
MIX_TM = 512
MIX_SUB = 256
MOE_TM = 512
LN2_TM = 1024


def _rotate_rows(t1, t2, cos, sin):
    return t1 * cos - t2 * sin, t2 * cos + t1 * sin


def _inproj_kernel(x_ref, w_ref, wqt_ref, wvt_ref, bg_ref, cos_ref, sa_ref, sb_ref, cosr_ref, sinr_ref,
                   o_ref, qt_ref, vt_ref, xb_s):
    j = pl.program_id(1)

    @pl.when(j == 0)
    def _():
        xb_s[...] = x_ref[...].astype(BF16)

    xb = xb_s[...]
    nt = (((1,), (1,)), ((), ()))

    @pl.when(j == Q_STEP)
    def _():
        acc_t = lax.dot_general(wqt_ref[...], xb, nt, preferred_element_type=F32) * Q_SCALE
        cos, sin = cosr_ref[...], sinr_ref[...]
        for hh in range(N_HEADS):
            parts = []
            for mp in range(2):
                o = hh * HEAD_COLS + mp * QK_DIM
                parts += _rotate_rows(acc_t[o:o + ROT_HALF], acc_t[o + ROT_HALF:o + ROT_DIM], cos, sin)
                parts.append(acc_t[o + ROT_DIM:o + QK_DIM])
            qt_ref[hh, 0] = jnp.concatenate(parts, axis=0).astype(BF16)

    @pl.when(j == V_STEP)
    def _():
        acc_t = lax.dot_general(wvt_ref[...], xb, nt, preferred_element_type=F32)
        n_heads, n_chunks, _, tk = vt_ref.shape
        for hh in range(n_heads):
            for cc in range(n_chunks):
                vt_ref[hh, cc, :V_DIM, :] = acc_t[hh * V_DIM:(hh + 1) * V_DIM,
                                                  cc * tk:(cc + 1) * tk].astype(BF16)
                vt_ref[hh, cc, V_DIM:, :] = jnp.ones((ONES_ROWS, tk), BF16)

    @pl.when((j > V_STEP) & (j < GATE_STEP))
    def _():
        o_ref[...] = jnp.dot(xb, w_ref[...], preferred_element_type=F32).astype(o_ref.dtype)

    @pl.when(j >= GATE_STEP)
    def _():
        gates = jnp.dot(xb, w_ref[...], preferred_element_type=F32) + bg_ref[...]
        o_ref[...] = jax.nn.sigmoid(gates).astype(o_ref.dtype)

    @pl.when(j == K_STEP)
    def _():
        acc = jnp.dot(xb, w_ref[...], preferred_element_type=F32)
        cos, sa, sb = cos_ref[...], sa_ref[...], sb_ref[...]
        for c in range(COL_BLOCK // LANES):
            a = acc[:, c * LANES:(c + 1) * LANES]
            r = (a * cos + pltpu.roll(a, LANES - ROT_HALF, 1) * sa + pltpu.roll(a, ROT_HALF, 1) * sb)
            o_ref[:, c * LANES:(c + 1) * LANES] = r.astype(o_ref.dtype)


def _inproj(x2d, w_in_b, wqt_b, wvt_b, b_gate, tables, batch, seq):
    n = x2d.shape[0]
    tm, tk = PROJ_TM, ATTN_TK
    assert tm == ATTN_TQ
    pos_blocks = seq // tm
    cos_t, sa_t, sb_t, cos_r, sin_r = tables
    tab_spec = pl.BlockSpec((tm, LANES), lambda i, j: (i % pos_blocks, 0))
    rtab_spec = pl.BlockSpec((ROT_HALF, tm), lambda i, j: (0, i % pos_blocks))
    wt_spec = pl.BlockSpec((COL_BLOCK, D_MODEL), lambda i, j: (0, 0))
    return pl.pallas_call(
        _inproj_kernel,
        out_shape=(jax.ShapeDtypeStruct((n, H_COLS), BF16),
                   jax.ShapeDtypeStruct((batch, N_HEADS, seq // tm, HEAD_COLS, tm), BF16),
                   jax.ShapeDtypeStruct((batch, N_HEADS, seq // tk, V_DIM + ONES_ROWS, tk), BF16)),
        grid=(n // tm, IN_COLS // COL_BLOCK),
        in_specs=[
            pl.BlockSpec((tm, D_MODEL), lambda i, j: (i, 0)),
            pl.BlockSpec((D_MODEL, COL_BLOCK), lambda i, j: (0, jnp.where(j <= V_STEP, K_STEP, j))),
            wt_spec, wt_spec,
            pl.BlockSpec((1, COL_BLOCK), lambda i, j: (0, jnp.maximum(j - GATE_STEP, 0))),
            tab_spec, tab_spec, tab_spec, rtab_spec, rtab_spec,
        ],
        out_specs=(
            pl.BlockSpec((tm, COL_BLOCK), lambda i, j: (i, jnp.maximum(j - V_STEP, 0))),
            pl.BlockSpec((None, N_HEADS, 1, HEAD_COLS, tm),
                         lambda i, j: (i // pos_blocks, 0, i % pos_blocks, 0, 0)),
            pl.BlockSpec((None, N_HEADS, tm // tk, V_DIM + ONES_ROWS, tk),
                         lambda i, j: (i // pos_blocks, 0, i % pos_blocks, 0, 0)),
        ),
        scratch_shapes=[pltpu.VMEM((tm, D_MODEL), BF16)],
        compiler_params=pltpu.CompilerParams(
            dimension_semantics=("parallel", "arbitrary"), vmem_limit_bytes=VMEM_LIMIT),
        name="inproj",
    )(x2d, w_in_b, wqt_b, wvt_b, b_gate, cos_t, sa_t, sb_t, cos_r, sin_r)


def _attn_kernel(lam_ref, q_ref, k_ref, vt_ref, g_ref, o_ref, qt_s, acc_s, sa0, sa1, sb0, sb1, *, out_scale):
    n_q, _, tq = q_ref.shape
    n_chunks, _, tk = vt_ref.shape
    sa_s, sb_s = (sa0, sa1), (sb0, sb1)

    def load_q(qi):
        zero_half = jnp.zeros((QK_DIM, tq), BF16)
        qt_s[0, :QK_DIM, :] = q_ref[qi, :QK_DIM, :]
        qt_s[0, QK_DIM:, :] = zero_half
        qt_s[1, :QK_DIM, :] = zero_half
        qt_s[1, QK_DIM:, :] = q_ref[qi, QK_DIM:, :]

    def scores(c, s_refs, mp):
        k = k_ref[pl.ds(pl.multiple_of(c * tk, tk), tk), :]
        s = jnp.dot(k, qt_s[mp], preferred_element_type=F32)
        s_refs[mp][...] = s
        return jnp.max(s, axis=0, keepdims=True)

    def update(c, s_refs, mp, m_old, c_max):
        m_new = jnp.maximum(m_old, c_max)
        alpha = jnp.exp2(m_old - m_new)
        p = jnp.exp2(s_refs[mp][...] - m_new).astype(BF16)
        acc_s[mp] = alpha * acc_s[mp] + jnp.dot(vt_ref[c], p, preferred_element_type=F32)
        return m_new

    def step(c, cur, nxt, ms, cur_max, prefetch):
        new_ms, nxt_max = [], []
        for mp in range(2):
            if prefetch:
                nxt_max.append(scores(c + 1, nxt, mp))
            new_ms.append(update(c, cur, mp, ms[mp], cur_max[mp]))
        return tuple(new_ms), tuple(nxt_max)

    def pair(i, carry):
        ms, a_max = carry
        ms, b_max = step(2 * i, sa_s, sb_s, ms, a_max, True)
        return step(2 * i + 1, sb_s, sa_s, ms, b_max, True)

    def finalize(qi):
        lam = lam_ref[0]
        o1 = acc_s[0, :V_DIM, :] / acc_s[0, V_DIM:V_DIM + 1, :]
        o2 = acc_s[1, :V_DIM, :] / acc_s[1, V_DIM:V_DIM + 1, :]
        ot = o1 - lam * o2
        ot = ot * lax.rsqrt(jnp.mean(ot * ot, axis=0, keepdims=True) + SUBLN_EPS)
        rows = pl.ds(pl.multiple_of(qi * tq, tq), tq)
        o_ref[rows, :] = (ot.T * (g_ref[...] * out_scale)).astype(o_ref.dtype)

    def q_block(qi, a_max, last):
        acc_s[...] = jnp.zeros(acc_s.shape, F32)
        m0 = jnp.full((1, tq), -jnp.inf, F32)
        ms, a_max = lax.fori_loop(0, n_chunks // 2 - 1, pair, ((m0, m0), a_max))
        ms, b_max = step(n_chunks - 2, sa_s, sb_s, ms, a_max, True)
        step(n_chunks - 1, sb_s, sa_s, ms, b_max, False)
        if not last:
            load_q(qi + 1)
            a_max = tuple(scores(0, sa_s, mp) for mp in range(2))
        finalize(qi)
        return a_max

    load_q(0)
    a_max = tuple(scores(0, sa_s, mp) for mp in range(2))
    a_max = lax.fori_loop(0, n_q - 1, lambda qi, am: q_block(qi, am, False), a_max)
    q_block(n_q - 1, a_max, True)


def _attention(h3, qt, vt, lam, subln_g, lam_init):
    b, s, _ = h3.shape
    tq, tk = ATTN_TQ, ATTN_TK
    n_chunks = s // tk
    assert n_chunks % 2 == 0 and n_chunks >= 2
    vt_rows = V_DIM + ONES_ROWS
    kern = functools.partial(_attn_kernel, out_scale=1.0 - lam_init)
    return pl.pallas_call(
        kern,
        out_shape=jax.ShapeDtypeStruct((b, s, N_HEADS * V_DIM), BF16),
        grid=(b, N_HEADS),
        in_specs=[
            pl.BlockSpec(memory_space=pltpu.SMEM),
            pl.BlockSpec((None, None, s // tq, HEAD_COLS, tq), lambda bi, hi: (bi, hi, 0, 0, 0)),
            pl.BlockSpec((None, s, HEAD_COLS), lambda bi, hi: (bi, 0, K_HEAD0 + hi)),
            pl.BlockSpec((None, None, n_chunks, vt_rows, tk), lambda bi, hi: (bi, hi, 0, 0, 0)),
            pl.BlockSpec((1, V_DIM), lambda bi, hi: (0, 0)),
        ],
        out_specs=pl.BlockSpec((None, s, V_DIM), lambda bi, hi: (bi, 0, hi)),
        scratch_shapes=[
            pltpu.VMEM((2, HEAD_COLS, tq), BF16),
            pltpu.VMEM((2, vt_rows, tq), F32),
            pltpu.VMEM((tk, tq), F32), pltpu.VMEM((tk, tq), F32),
            pltpu.VMEM((tk, tq), F32), pltpu.VMEM((tk, tq), F32),
        ],
        compiler_params=pltpu.CompilerParams(
            dimension_semantics=("parallel", "arbitrary"),
            vmem_limit_bytes=VMEM_LIMIT),
        name="diff_attention",
    )(lam, qt, h3, vt, subln_g)


HALF = D_MODEL // 2


def _pack_row(x):
    return pltpu.pack_elementwise([x[:, :HALF], x[:, HALF:]], packed_dtype=BF16)


def _unpack_row(w):
    lo = pltpu.unpack_elementwise(w, index=0, packed_dtype=BF16, unpacked_dtype=F32)
    hi = pltpu.unpack_elementwise(w, index=1, packed_dtype=BF16, unpacked_dtype=F32)
    return jnp.concatenate([lo, hi], axis=1)


def _layer_norm(y, g, b):
    mu = jnp.mean(y, axis=-1, keepdims=True)
    d = y - mu
    var = jnp.mean(d * d, axis=-1, keepdims=True)
    return d * lax.rsqrt(var + LN_EPS) * g + b


def _mix_kernel(a_ref, cb_ref, cc_ref, cx_ref, ga_ref, gc_ref, ccp_ref, cxp_ref, ccn_ref, cxn_ref,
                x_ref, cw_ref, wa_ref, wc_ref, wo_ref, g1_ref, b1_ref, wr_ref, br_ref, tri_ref,
                x1_ref, x1p_ref, rt_ref, rtt_ref, cnt_ref, *, tiles_per_seq):
    i = pl.program_id(0)

    @pl.when(i == 0)
    def _():
        cnt_ref[...] = jnp.zeros(cnt_ref.shape, F32)

    tm = x_ref.shape[0]
    u = cc_ref[...].astype(F32) * cx_ref[...].astype(F32)
    first = (i % tiles_per_seq) == 0
    last = (i % tiles_per_seq) == tiles_per_seq - 1
    up = ccp_ref[SUBLANES - 1:SUBLANES, :].astype(F32) * cxp_ref[SUBLANES - 1:SUBLANES, :].astype(F32)
    un = ccn_ref[0:1, :].astype(F32) * cxn_ref[0:1, :].astype(F32)
    up = jnp.where(first, 0.0, up)
    un = jnp.where(last, 0.0, un)
    row = lax.broadcasted_iota(jnp.int32, u.shape, 0)
    u_prev = jnp.where(row == 0, up, pltpu.roll(u, 1, 0))
    u_next = jnp.where(row == tm - 1, un, pltpu.roll(u, tm - 1, 0))
    conv = cw_ref[0:1, :] * u_prev + cw_ref[1:2, :] * u + cw_ref[2:3, :] * u_next
    c = (cb_ref[...].astype(F32) * conv).astype(BF16)

    sub = tri_ref.shape[0]
    counts = cnt_ref[...]
    for s in range(tm // sub):
        rows = slice(s * sub, (s + 1) * sub)
        g_a = ga_ref[rows, :].astype(F32)
        g_c = gc_ref[rows, :].astype(F32)
        merged = (g_a * jnp.dot(a_ref[rows, :], wa_ref[...], preferred_element_type=F32)
                  + g_c * jnp.dot(c[rows, :], wc_ref[...], preferred_element_type=F32))
        m = jnp.dot(merged.astype(BF16), wo_ref[...], preferred_element_type=F32)
        x1 = _layer_norm(DEEPNORM_ALPHA * x_ref[rows, :] + m, g1_ref[...], b1_ref[...])
        x1_ref[rows, :] = x1
        x1p_ref[rows, :] = _pack_row(x1)
        logits = jnp.dot(x1.astype(BF16), wr_ref[...], preferred_element_type=F32) + br_ref[...]

        lane = lax.broadcasted_iota(jnp.int32, logits.shape, 1)
        lane_f = lane.astype(F32)
        work = jnp.where(lane < N_EXPERTS, logits, -jnp.inf)
        hots, vals, idxs = [], [], []
        for _ in range(TOP_K):
            mx = jnp.max(work, axis=-1, keepdims=True)
            ix = jnp.min(jnp.where(work == mx, lane_f, float(ROUTER_COLS)), axis=-1, keepdims=True)
            hot = lane_f == ix
            work = jnp.where(hot, -jnp.inf, work)
            hots.append(hot)
            vals.append(mx)
            idxs.append(ix)
        exps = [jnp.exp(v - vals[0]) for v in vals]
        denom = exps[0] + exps[1] + exps[2] + exps[3]
        picked = jnp.zeros(logits.shape, F32)
        for hot in hots:
            picked = picked + hot.astype(F32)
        before = counts + jnp.dot(tri_ref[...], picked.astype(BF16), preferred_element_type=F32)
        out = jnp.zeros(logits.shape, F32)
        for j in range(TOP_K):
            rank = jnp.sum(jnp.where(hots[j], before, 0.0), axis=-1, keepdims=True)
            out = jnp.where(lane == RT_EXPERT + j, idxs[j], out)
            out = jnp.where(lane == RT_RANK + j, rank, out)
            out = jnp.where(lane == RT_GATE + j, exps[j] / denom, out)
        rt_ref[rows, :] = out
        rtt_ref[:, rows] = out.T[:RT_ROWS, :]
        counts = counts + jnp.sum(picked, axis=0, keepdims=True)
    cnt_ref[...] = counts


def _mix(a2d, h2d, x2d, seq, cw, wa, wc, wo, g1, b1, wr, br):
    n = x2d.shape[0]
    tm = MIX_TM
    tiles_per_seq = seq // tm
    halo_blocks = tm // SUBLANES
    n_halo = n // SUBLANES

    def col(blk):
        return pl.BlockSpec((tm, COL_BLOCK), lambda i: (i, blk))

    def prev(blk):
        return pl.BlockSpec((SUBLANES, COL_BLOCK), lambda i: (jnp.maximum(i * halo_blocks - 1, 0), blk))

    def nxt(blk):
        return pl.BlockSpec((SUBLANES, COL_BLOCK),
                            lambda i: (jnp.minimum((i + 1) * halo_blocks, n_halo - 1), blk))

    def full(shape):
        return pl.BlockSpec(shape, lambda i: (0,) * len(shape))

    sub = MIX_SUB
    tri = (lax.broadcasted_iota(jnp.int32, (sub, sub), 1)
           < lax.broadcasted_iota(jnp.int32, (sub, sub), 0)).astype(BF16)
    kern = functools.partial(_mix_kernel, tiles_per_seq=tiles_per_seq)
    return pl.pallas_call(
        kern,
        out_shape=(jax.ShapeDtypeStruct((n, D_MODEL), F32),
                   jax.ShapeDtypeStruct((n, HALF), jnp.int32),
                   jax.ShapeDtypeStruct((n, ROUTER_COLS), F32),
                   jax.ShapeDtypeStruct((RT_ROWS, n), F32),
                   jax.ShapeDtypeStruct((1, ROUTER_COLS), F32)),
        grid=(n // tm,),
        in_specs=[
            pl.BlockSpec((tm, D_MODEL), lambda i: (i, 0)),
            col(CB_BLK), col(CC_BLK), col(CX_BLK), col(GA_BLK), col(GC_BLK),
            prev(CC_BLK), prev(CX_BLK), nxt(CC_BLK), nxt(CX_BLK),
            pl.BlockSpec((tm, D_MODEL), lambda i: (i, 0)),
            full((SUBLANES, D_MODEL)),
            full((D_MODEL, D_MODEL)), full((D_MODEL, D_MODEL)), full((D_MODEL, D_MODEL)),
            full((1, D_MODEL)), full((1, D_MODEL)),
            full((D_MODEL, ROUTER_COLS)), full((1, ROUTER_COLS)), full((sub, sub)),
        ],
        out_specs=(pl.BlockSpec((tm, D_MODEL), lambda i: (i, 0)),
                   pl.BlockSpec((tm, HALF), lambda i: (i, 0)),
                   pl.BlockSpec((tm, ROUTER_COLS), lambda i: (i, 0)),
                   pl.BlockSpec((RT_ROWS, tm), lambda i: (0, i)),
                   pl.BlockSpec((1, ROUTER_COLS), lambda i: (0, 0))),
        compiler_params=pltpu.CompilerParams(
            dimension_semantics=("arbitrary",), vmem_limit_bytes=VMEM_LIMIT),
        name="mix_ln1_router",
    )(a2d, h2d, h2d, h2d, h2d, h2d, h2d, h2d, h2d, h2d, x2d, cw, wa, wc, wo, g1, b1, wr, br, tri)


def _moe_kernel(be_ref, nv_ref, nu_ref, xs_ref, wg_ref, bg_ref, wu_ref, bu_ref, wd_ref, bd_ref, o_ref,
                wg_s, wu_s, wd_s):
    i = pl.program_id(0)
    used = i < nu_ref[0]

    @pl.when(used & ((i == 0) | (be_ref[i] != be_ref[jnp.maximum(i - 1, 0)])))
    def _():
        wg_s[...] = wg_ref[...].astype(BF16)
        wu_s[...] = wu_ref[...].astype(BF16)
        wd_s[...] = wd_ref[...].astype(BF16)

    @pl.when(used)
    def _():
        row = lax.broadcasted_iota(jnp.int32, xs_ref.shape, 0)
        x = _unpack_row(jnp.where(row < nv_ref[i], xs_ref[...], 0)).astype(BF16)
        hg = jnp.minimum(jnp.dot(x, wg_s[...], preferred_element_type=F32) + bg_ref[...], SWIGLU_LIMIT)
        hu = jnp.clip(jnp.dot(x, wu_s[...], preferred_element_type=F32) + bu_ref[...],
                      -SWIGLU_LIMIT, SWIGLU_LIMIT)
        act = hg * jax.nn.sigmoid(SWIGLU_ALPHA * hg) * (hu + 1.0)
        y = jnp.dot(act.astype(BF16), wd_s[...], preferred_element_type=F32) + bd_ref[...]
        o_ref[...] = _pack_row(y)

    @pl.when(jnp.logical_not(used))
    def _():
        o_ref[...] = jnp.zeros(o_ref.shape, o_ref.dtype)


def _moe_experts(block_e, n_valid, n_used, xs, wg, bg, wu, bu, wd, bd):
    p = xs.shape[0]
    tm = MOE_TM
    nb = p // tm

    def row_map(i, be, nv, nu):
        return (jnp.minimum(i, nu[0] - 1), 0)

    def w_map(i, be, nv, nu):
        return (be[jnp.minimum(i, nu[0] - 1)], 0, 0)

    w_spec = pl.BlockSpec((None, D_MODEL, D_MODEL), w_map)
    b_spec = pl.BlockSpec((None, 1, D_MODEL), w_map)
    return pl.pallas_call(
        _moe_kernel,
        out_shape=jax.ShapeDtypeStruct((p, HALF), jnp.int32),
        grid_spec=pltpu.PrefetchScalarGridSpec(
            num_scalar_prefetch=3,
            grid=(nb,),
            in_specs=[
                pl.BlockSpec((tm, HALF), row_map),
                w_spec, b_spec, w_spec, b_spec, w_spec, b_spec,
            ],
            out_specs=pl.BlockSpec((tm, HALF), lambda i, be, nv, nu: (i, 0)),
            scratch_shapes=[pltpu.VMEM((D_MODEL, D_MODEL), BF16)] * 3,
        ),
        compiler_params=pltpu.CompilerParams(
            dimension_semantics=("arbitrary",), vmem_limit_bytes=VMEM_LIMIT),
        name="moe_experts",
    )(block_e, n_valid, n_used, xs, wg, bg, wu, bu, wd, bd)


def _sc_mesh():
    return plsc.VectorSubcoreMesh(core_axis_name="core", subcore_axis_name="subcore")


def _sc_index_rows(dest_t):
    k, n = dest_t.shape
    w = SC_WINDOW
    return jnp.pad(dest_t.reshape(k * n // w, w), ((0, 0), (0, LANES - w)))


def _sc_dispatch(x1, slot_idx, p):
    n, d = x1.shape
    w = SC_WINDOW
    windows = n // w

    @pl.kernel(out_type=jax.ShapeDtypeStruct((p, d), x1.dtype), mesh=_sc_mesh(), scratch_types=[],
               name="sc_dispatch")
    def run(x_hbm, d_hbm, o_hbm):
        def body(x_vmem, *d_vmems):
            for d_vmem in d_vmems:
                pltpu.sync_copy(x_vmem, o_hbm.at[d_vmem.at[0, pl.ds(0, w)]])

        pltpu.emit_pipeline(
            body,
            grid=(windows,),
            in_specs=[pl.BlockSpec((w, d), lambda i: (i, 0))]
            + [pl.BlockSpec((1, LANES), functools.partial(lambda k, i: (k * windows + i, 0), k))
               for k in range(TOP_K)],
            out_specs=[],
            core_axis_name=("core", "subcore"),
            dimension_semantics=(pltpu.PARALLEL,),
        )(x_hbm, *([d_hbm] * TOP_K))

    return run(x1, slot_idx)


def _sc_gather(ys, idx):
    w = SC_WINDOW
    r = idx.shape[0] * w
    d = ys.shape[1]

    @pl.kernel(out_type=jax.ShapeDtypeStruct((r, d), ys.dtype), mesh=_sc_mesh(), scratch_types=[],
               name="sc_gather")
    def run(y_hbm, i_hbm, o_hbm):
        def body(i_vmem, o_vmem):
            pltpu.sync_copy(y_hbm.at[i_vmem.at[0, pl.ds(0, w)]], o_vmem)

        pltpu.emit_pipeline(
            body,
            grid=(r // w,),
            in_specs=[pl.BlockSpec((1, LANES), lambda i: (i, 0))],
            out_specs=[pl.BlockSpec((w, d), lambda i: (i, 0))],
            core_axis_name=("core", "subcore"),
            dimension_semantics=(pltpu.PARALLEL,),
        )(i_hbm, o_hbm)

    return run(ys, idx)


def _ln2_kernel(x1_ref, yg_ref, rt_ref, g_ref, b_ref, o_ref):
    f = jnp.zeros(x1_ref.shape, F32)
    for k in range(TOP_K):
        f = f + _unpack_row(yg_ref[k]) * rt_ref[:, RT_GATE + k:RT_GATE + k + 1]
    o_ref[...] = _layer_norm(DEEPNORM_ALPHA * x1_ref[...] + f, g_ref[...], b_ref[...])


def _combine_ln2(x1, yg, rt, g2, b2):
    n = x1.shape[0]
    tm = LN2_TM
    return pl.pallas_call(
        _ln2_kernel,
        out_shape=jax.ShapeDtypeStruct((n, D_MODEL), F32),
        grid=(n // tm,),
        in_specs=[
            pl.BlockSpec((tm, D_MODEL), lambda i: (i, 0)),
            pl.BlockSpec((TOP_K, tm, HALF), lambda i: (0, i, 0)),
            pl.BlockSpec((tm, ROUTER_COLS), lambda i: (i, 0)),
            pl.BlockSpec((1, D_MODEL), lambda i: (0, 0)),
            pl.BlockSpec((1, D_MODEL), lambda i: (0, 0)),
        ],
        out_specs=pl.BlockSpec((tm, D_MODEL), lambda i: (i, 0)),
        compiler_params=pltpu.CompilerParams(
            dimension_semantics=("parallel",), vmem_limit_bytes=VMEM_LIMIT),
        name="combine_ln2",
    )(x1, yg, rt, g2, b2)


def _rotary_tables(seq):
    inv = ROPE_THETA ** (-jnp.arange(0, ROT_DIM, 2, dtype=F32) / ROT_DIM)
    ang = jnp.arange(seq, dtype=F32)[:, None] * inv[None, :]
    cos, sin = jnp.cos(ang), jnp.sin(ang)
    ones = jnp.ones((seq, QK_DIM - ROT_DIM), F32)
    zeros = jnp.zeros((seq, QK_DIM - ROT_DIM), F32)
    zh = jnp.zeros((seq, ROT_HALF), F32)
    cos64 = jnp.concatenate([cos, cos, ones], axis=1)
    sa64 = jnp.concatenate([-sin, zh, zeros], axis=1)
    sb64 = jnp.concatenate([zh, sin, zeros], axis=1)
    rep = LANES // QK_DIM
    return (jnp.tile(cos64, (1, rep)), jnp.tile(sa64, (1, rep)), jnp.tile(sb64, (1, rep)), cos.T, sin.T)


def _slot_layout(rtt, counts, tm):
    n = rtt.shape[1]
    p = n * TOP_K + N_EXPERTS * tm
    nb = p // tm
    expert = rtt[RT_EXPERT:RT_EXPERT + TOP_K].astype(jnp.int32)
    rank = rtt[RT_RANK:RT_RANK + TOP_K].astype(jnp.int32)
    padded = ((counts + tm - 1) // tm) * tm
    pad_ends = jnp.cumsum(padded)
    pad_starts = pad_ends - padded
    ids = jnp.arange(N_EXPERTS, dtype=jnp.int32)[:, None, None]
    dest_t = jnp.sum(jnp.where(expert[None] == ids, pad_starts[:, None, None], 0), axis=0) + rank
    block_start = jnp.arange(nb, dtype=jnp.int32) * tm
    block_e = jnp.minimum(jnp.sum(pad_ends[None, :] <= block_start[:, None], axis=1),
                          N_EXPERTS - 1).astype(jnp.int32)
    n_valid = jnp.clip(pad_starts[block_e] + counts[block_e] - block_start, 0, tm).astype(jnp.int32)
    n_used = (pad_ends[-1] // tm).astype(jnp.int32).reshape(1)
    return dest_t.astype(jnp.int32), block_e, n_valid, n_used, p


def _trunk(x, prm):
    b, s, d = x.shape
    n = b * s
    x2d = x.reshape(n, d)
    h, qt, vt = _inproj(x2d, prm["w_in"], prm["w_q_t"], prm["w_v_t"], prm["b_gate"], _rotary_tables(s), b, s)
    a = _attention(h.reshape(b, s, H_COLS), qt, vt, prm["lam"], prm["subln_g"], prm["lam_init"])
    x1, x1p, rt, rtt, cnt = _mix(a.reshape(n, d), h, x2d, s, prm["conv_w"], prm["w_attn_br"],
                                 prm["w_conv_br"], prm["w_o"], prm["ln1_g"], prm["ln1_b"],
                                 prm["w_router"], prm["b_router"])
    counts = cnt[0, :N_EXPERTS].astype(jnp.int32)
    dest_t, block_e, n_valid, n_used, p = _slot_layout(rtt, counts, MOE_TM)
    slot_idx = _sc_index_rows(dest_t)
    xs = _sc_dispatch(x1p, slot_idx, p)
    ys = _moe_experts(block_e, n_valid, n_used, xs, prm["w_exp_gate"], prm["b_exp_gate"],
                      prm["w_exp_up"], prm["b_exp_up"], prm["w_exp_down"], prm["b_exp_down"])
    yg = _sc_gather(ys, slot_idx).reshape(TOP_K, n, HALF)
    y = _combine_ln2(x1, yg, rt, prm["ln2_g"], prm["ln2_b"])
    return y.reshape(b, s, d)


def kernel(x_prompt, x_sample, w_in, b_branch_gate, lambda_q1, lambda_k1, lambda_q2, lambda_k2, subln_g,
           conv_w, w_attn_br, w_conv_br, w_o, ln1_g, ln1_b, w_router, b_router, w_exp_gate, b_exp_gate,
           w_exp_up, b_exp_up, w_exp_down, b_exp_down, ln2_g, ln2_b):
    l = 0
    lam_init = 0.8 - 0.6 * math.exp(-0.3 * l)
    lam = (jnp.exp(jnp.sum(lambda_q1[l].astype(F32) * lambda_k1[l].astype(F32)))
           - jnp.exp(jnp.sum(lambda_q2[l].astype(F32) * lambda_k2[l].astype(F32))) + lam_init)
    prm = {
        "lam_init": lam_init,
        "lam": lam.reshape(1).astype(F32),
        "w_in": w_in[l].astype(BF16),
        "w_q_t": w_in[l][:, Q_STEP * COL_BLOCK:(Q_STEP + 1) * COL_BLOCK].T.astype(BF16),
        "w_v_t": w_in[l][:, V_STEP * COL_BLOCK:(V_STEP + 1) * COL_BLOCK].T.astype(BF16),
        "b_gate": b_branch_gate[l].reshape(1, 2 * D_MODEL),
        "subln_g": subln_g[l].reshape(1, V_DIM),
        "conv_w": jnp.pad(conv_w[l], ((0, SUBLANES - conv_w.shape[1]), (0, 0))),
        "w_attn_br": w_attn_br[l].astype(BF16),
        "w_conv_br": w_conv_br[l].astype(BF16),
        "w_o": w_o[l].astype(BF16),
        "ln1_g": ln1_g[l].reshape(1, D_MODEL),
        "ln1_b": ln1_b[l].reshape(1, D_MODEL),
        "w_router": jnp.pad(w_router[l], ((0, 0), (0, ROUTER_COLS - N_EXPERTS))).astype(BF16),
        "b_router": jnp.pad(b_router[l], (0, ROUTER_COLS - N_EXPERTS)).reshape(1, ROUTER_COLS),
        "w_exp_gate": w_exp_gate[l],
        "b_exp_gate": b_exp_gate[l].reshape(N_EXPERTS, 1, D_MODEL),
        "w_exp_up": w_exp_up[l],
        "b_exp_up": b_exp_up[l].reshape(N_EXPERTS, 1, D_MODEL),
        "w_exp_down": w_exp_down[l],
        "b_exp_down": b_exp_down[l].reshape(N_EXPERTS, 1, D_MODEL),
        "ln2_g": ln2_g[l].reshape(1, D_MODEL),
        "ln2_b": ln2_b[l].reshape(1, D_MODEL),
    }
    return _trunk(x_prompt, prm), _trunk(x_sample, prm)
```

```python
import functools
import math

import jax
import jax.numpy as jnp
from jax import lax
from jax.experimental import pallas as pl
from jax.experimental.pallas import tpu as pltpu
from jax.experimental.pallas import tpu_sc as plsc

F32 = jnp.float32
BF16 = jnp.bfloat16

D_MODEL = 1024
N_HEADS = 8
QK_DIM = 64
V_DIM = 128
HEAD_COLS = 2 * QK_DIM
ROT_DIM = QK_DIM // 4
ROT_HALF = ROT_DIM // 2
ROPE_THETA = 500000.0
SUBLN_EPS = 1e-5
IN_COLS = 8192
COL_BLOCK = 1024
Q_STEP, K_STEP, V_STEP = 0, 1, 2
GATE_STEP = 6
H_COLS = IN_COLS - 2 * COL_BLOCK
CB_BLK, CC_BLK, CX_BLK, GA_BLK, GC_BLK = 1, 2, 3, 4, 5
K_HEAD0 = 0
N_EXPERTS = 32
TOP_K = 4
SWIGLU_LIMIT = 7.0
SWIGLU_ALPHA = 1.702
LN_EPS = 1e-5
DEPTH = 1
DEEPNORM_ALPHA = (2 * DEPTH) ** 0.25
LANES = 128
SUBLANES = 8
ROUTER_COLS = LANES
RT_EXPERT, RT_RANK, RT_GATE = 0, 4, 8
RT_ROWS = 16
SC_WINDOW = 32
ONES_ROWS = 16
Q_SCALE = math.log2(math.e) * QK_DIM ** -0.5

VMEM_LIMIT = 56 * 1024 * 1024

PROJ_TM = 1024
ATTN_TQ = 1024
ATTN_TK = 512
MIX_TM = 512
MIX_SUB = 256
MOE_TM = 512
LN2_TM = 1024


def _rotate_rows(t1, t2, cos, sin):
    return t1 * cos - t2 * sin, t2 * cos + t1 * sin


def _inproj_kernel(x_ref, w_ref, wqt_ref, wvt_ref, bg_ref, cos_ref, sa_ref, sb_ref, cosr_ref, sinr_ref,
                   o_ref, qt_ref, vt_ref, xb_s):
    j = pl.program_id(1)

    @pl.when(j == 0)
    def _():
        xb_s[...] = x_ref[...].astype(BF16)

    xb = xb_s[...]
    nt = (((1,), (1,)), ((), ()))

    @pl.when(j == Q_STEP)
    def _():
        acc_t = lax.dot_general(wqt_ref[...], xb, nt, preferred_element_type=F32) * Q_SCALE
        cos, sin = cosr_ref[...], sinr_ref[...]
        for hh in range(N_HEADS):
            parts = []
            for mp in range(2):
                o = hh * HEAD_COLS + mp * QK_DIM
                parts += _rotate_rows(acc_t[o:o + ROT_HALF], acc_t[o + ROT_HALF:o + ROT_DIM], cos, sin)
                parts.append(acc_t[o + ROT_DIM:o + QK_DIM])
            qt_ref[hh, 0] = jnp.concatenate(parts, axis=0).astype(BF16)

    @pl.when(j == V_STEP)
    def _():
        acc_t = lax.dot_general(wvt_ref[...], xb, nt, preferred_element_type=F32)
        n_heads, n_chunks, _, tk = vt_ref.shape
        for hh in range(n_heads):
            for cc in range(n_chunks):
                vt_ref[hh, cc, :V_DIM, :] = acc_t[hh * V_DIM:(hh + 1) * V_DIM,
                                                  cc * tk:(cc + 1) * tk].astype(BF16)
                vt_ref[hh, cc, V_DIM:, :] = jnp.ones((ONES_ROWS, tk), BF16)

    @pl.when((j > V_STEP) & (j < GATE_STEP))
    def _():
        o_ref[...] = jnp.dot(xb, w_ref[...], preferred_element_type=F32).astype(o_ref.dtype)

    @pl.when(j >= GATE_STEP)
    def _():
        gates = jnp.dot(xb, w_ref[...], preferred_element_type=F32) + bg_ref[...]
        o_ref[...] = jax.nn.sigmoid(gates).astype(o_ref.dtype)

    @pl.when(j == K_STEP)
    def _():
        acc = jnp.dot(xb, w_ref[...], preferred_element_type=F32)
        cos, sa, sb = cos_ref[...], sa_ref[...], sb_ref[...]
        for c in range(COL_BLOCK // LANES):
            a = acc[:, c * LANES:(c + 1) * LANES]
            r = (a * cos + pltpu.roll(a, LANES - ROT_HALF, 1) * sa + pltpu.roll(a, ROT_HALF, 1) * sb)
            o_ref[:, c * LANES:(c + 1) * LANES] = r.astype(o_ref.dtype)


def _inproj(x2d, w_in_b, wqt_b, wvt_b, b_gate, tables, batch, seq):
    n = x2d.shape[0]
    tm, tk = PROJ_TM, ATTN_TK
    assert tm == ATTN_TQ
    pos_blocks = seq // tm
    cos_t, sa_t, sb_t, cos_r, sin_r = tables
    tab_spec = pl.BlockSpec((tm, LANES), lambda i, j: (i % pos_blocks, 0))
    rtab_spec = pl.BlockSpec((ROT_HALF, tm), lambda i, j: (0, i % pos_blocks))
    wt_spec = pl.BlockSpec((COL_BLOCK, D_MODEL), lambda i, j: (0, 0))
    return pl.pallas_call(
        _inproj_kernel,
        out_shape=(jax.ShapeDtypeStruct((n, H_COLS), BF16),
                   jax.ShapeDtypeStruct((batch, N_HEADS, seq // tm, HEAD_COLS, tm), BF16),
                   jax.ShapeDtypeStruct((batch, N_HEADS, seq // tk, V_DIM + ONES_ROWS, tk), BF16)),
        grid=(n // tm, IN_COLS // COL_BLOCK),
        in_specs=[
            pl.BlockSpec((tm, D_MODEL), lambda i, j: (i, 0)),
            pl.BlockSpec((D_MODEL, COL_BLOCK), lambda i, j: (0, jnp.where(j <= V_STEP, K_STEP, j))),
            wt_spec, wt_spec,
            pl.BlockSpec((1, COL_BLOCK), lambda i, j: (0, jnp.maximum(j - GATE_STEP, 0))),
            tab_spec, tab_spec, tab_spec, rtab_spec, rtab_spec,
        ],
        out_specs=(
            pl.BlockSpec((tm, COL_BLOCK), lambda i, j: (i, jnp.maximum(j - V_STEP, 0))),
            pl.BlockSpec((None, N_HEADS, 1, HEAD_COLS, tm),
                         lambda i, j: (i // pos_blocks, 0, i % pos_blocks, 0, 0)),
            pl.BlockSpec((None, N_HEADS, tm // tk, V_DIM + ONES_ROWS, tk),
                         lambda i, j: (i // pos_blocks, 0, i % pos_blocks, 0, 0)),
        ),
        scratch_shapes=[pltpu.VMEM((tm, D_MODEL), BF16)],
        compiler_params=pltpu.CompilerParams(
            dimension_semantics=("parallel", "arbitrary"), vmem_limit_bytes=VMEM_LIMIT),
        name="inproj",
    )(x2d, w_in_b, wqt_b, wvt_b, b_gate, cos_t, sa_t, sb_t, cos_r, sin_r)


def _attn_kernel(lam_ref, q_ref, k_ref, vt_ref, g_ref, o_ref, qt_s, acc_s, sa0, sa1, sb0, sb1, *, out_scale):
    n_q, _, tq = q_ref.shape
    n_chunks, _, tk = vt_ref.shape
    sa_s, sb_s = (sa0, sa1), (sb0, sb1)

    def load_q(qi):
        zero_half = jnp.zeros((QK_DIM, tq), BF16)
        qt_s[0, :QK_DIM, :] = q_ref[qi, :QK_DIM, :]
        qt_s[0, QK_DIM:, :] = zero_half
        qt_s[1, :QK_DIM, :] = zero_half
        qt_s[1, QK_DIM:, :] = q_ref[qi, QK_DIM:, :]

    def scores(c, s_refs, mp):
        k = k_ref[pl.ds(pl.multiple_of(c * tk, tk), tk), :]
        s = jnp.dot(k, qt_s[mp], preferred_element_type=F32)
        s_refs[mp][...] = s
        return jnp.max(s, axis=0, keepdims=True)

    def update(c, s_refs, mp, m_old, c_max):
        m_new = jnp.maximum(m_old, c_max)
        alpha = jnp.exp2(m_old - m_new)
        p = jnp.exp2(s_refs[mp][...] - m_new).astype(BF16)
        acc_s[mp] = alpha * acc_s[mp] + jnp.dot(vt_ref[c], p, preferred_element_type=F32)
        return m_new

    def step(c, cur, nxt, ms, cur_max, prefetch):
        new_ms, nxt_max = [], []
        for mp in range(2):
            if prefetch:
                nxt_max.append(scores(c + 1, nxt, mp))
            new_ms.append(update(c, cur, mp, ms[mp], cur_max[mp]))
        return tuple(new_ms), tuple(nxt_max)

    def pair(i, carry):
        ms, a_max = carry
        ms, b_max = step(2 * i, sa_s, sb_s, ms, a_max, True)
        return step(2 * i + 1, sb_s, sa_s, ms, b_max, True)

    def finalize(qi):
        lam = lam_ref[0]
        o1 = acc_s[0, :V_DIM, :] / acc_s[0, V_DIM:V_DIM + 1, :]
        o2 = acc_s[1, :V_DIM, :] / acc_s[1, V_DIM:V_DIM + 1, :]
        ot = o1 - lam * o2
        ot = ot * lax.rsqrt(jnp.mean(ot * ot, axis=0, keepdims=True) + SUBLN_EPS)
        rows = pl.ds(pl.multiple_of(qi * tq, tq), tq)
        o_ref[rows, :] = (ot.T * (g_ref[...] * out_scale)).astype(o_ref.dtype)

    def q_block(qi, a_max, last):
        acc_s[...] = jnp.zeros(acc_s.shape, F32)
        m0 = jnp.full((1, tq), -jnp.inf, F32)
        ms, a_max = lax.fori_loop(0, n_chunks // 2 - 1, pair, ((m0, m0), a_max))
        ms, b_max = step(n_chunks - 2, sa_s, sb_s, ms, a_max, True)
        step(n_chunks - 1, sb_s, sa_s, ms, b_max, False)
        if not last:
            load_q(qi + 1)
            a_max = tuple(scores(0, sa_s, mp) for mp in range(2))
        finalize(qi)
        return a_max

    load_q(0)
    a_max = tuple(scores(0, sa_s, mp) for mp in range(2))
    a_max = lax.fori_loop(0, n_q - 1, lambda qi, am: q_block(qi, am, False), a_max)
    q_block(n_q - 1, a_max, True)


def _attention(h3, qt, vt, lam, subln_g, lam_init):
    b, s, _ = h3.shape
    tq, tk = ATTN_TQ, ATTN_TK
    n_chunks = s // tk
    assert n_chunks % 2 == 0 and n_chunks >= 2
    vt_rows = V_DIM + ONES_ROWS
    kern = functools.partial(_attn_kernel, out_scale=1.0 - lam_init)
    return pl.pallas_call(
        kern,
        out_shape=jax.ShapeDtypeStruct((b, s, N_HEADS * V_DIM), BF16),
        grid=(b, N_HEADS),
        in_specs=[
            pl.BlockSpec(memory_space=pltpu.SMEM),
            pl.BlockSpec((None, None, s // tq, HEAD_COLS, tq), lambda bi, hi: (bi, hi, 0, 0, 0)),
            pl.BlockSpec((None, s, HEAD_COLS), lambda bi, hi: (bi, 0, K_HEAD0 + hi)),
            pl.BlockSpec((None, None, n_chunks, vt_rows, tk), lambda bi, hi: (bi, hi, 0, 0, 0)),
            pl.BlockSpec((1, V_DIM), lambda bi, hi: (0, 0)),
        ],
        out_specs=pl.BlockSpec((None, s, V_DIM), lambda bi, hi: (bi, 0, hi)),
        scratch_shapes=[
            pltpu.VMEM((2, HEAD_COLS, tq), BF16),
            pltpu.VMEM((2, vt_rows, tq), F32),
            pltpu.VMEM((tk, tq), F32), pltpu.VMEM((tk, tq), F32),
            pltpu.VMEM((tk, tq), F32), pltpu.VMEM((tk, tq), F32),
        ],
        compiler_params=pltpu.CompilerParams(
            dimension_semantics=("parallel", "arbitrary"),
            vmem_limit_bytes=VMEM_LIMIT),
        name="diff_attention",
    )(lam, qt, h3, vt, subln_g)


HALF = D_MODEL // 2


def _pack_row(x):
    return pltpu.pack_elementwise([x[:, :HALF], x[:, HALF:]], packed_dtype=BF16)


def _unpack_row(w):
    lo = pltpu.unpack_elementwise(w, index=0, packed_dtype=BF16, unpacked_dtype=F32)
    hi = pltpu.unpack_elementwise(w, index=1, packed_dtype=BF16, unpacked_dtype=F32)
    return jnp.concatenate([lo, hi], axis=1)


def _layer_norm(y, g, b):
    mu = jnp.mean(y, axis=-1, keepdims=True)
    d = y - mu
    var = jnp.mean(d * d, axis=-1, keepdims=True)
    return d * lax.rsqrt(var + LN_EPS) * g + b


def _mix_kernel(a_ref, cb_ref, cc_ref, cx_ref, ga_ref, gc_ref, ccp_ref, cxp_ref, ccn_ref, cxn_ref,
                x_ref, cw_ref, wa_ref, wc_ref, wo_ref, g1_ref, b1_ref, wr_ref, br_ref, tri_ref,
                x1_ref, x1p_ref, rt_ref, rtt_ref, cnt_ref, *, tiles_per_seq):
    i = pl.program_id(0)

    @pl.when(i == 0)
    def _():
        cnt_ref[...] = jnp.zeros(cnt_ref.shape, F32)

    tm = x_ref.shape[0]
    u = cc_ref[...].astype(F32) * cx_ref[...].astype(F32)
    first = (i % tiles_per_seq) == 0
    last = (i % tiles_per_seq) == tiles_per_seq - 1
    up = ccp_ref[SUBLANES - 1:SUBLANES, :].astype(F32) * cxp_ref[SUBLANES - 1:SUBLANES, :].astype(F32)
    un = ccn_ref[0:1, :].astype(F32) * cxn_ref[0:1, :].astype(F32)
    up = jnp.where(first, 0.0, up)
    un = jnp.where(last, 0.0, un)
    row = lax.broadcasted_iota(jnp.int32, u.shape, 0)
    u_prev = jnp.where(row == 0, up, pltpu.roll(u, 1, 0))
    u_next = jnp.where(row == tm - 1, un, pltpu.roll(u, tm - 1, 0))
    conv = cw_ref[0:1, :] * u_prev + cw_ref[1:2, :] * u + cw_ref[2:3, :] * u_next
    c = (cb_ref[...].astype(F32) * conv).astype(BF16)

    sub = tri_ref.shape[0]
    counts = cnt_ref[...]
    for s in range(tm // sub):
        rows = slice(s * sub, (s + 1) * sub)
        g_a = ga_ref[rows, :].astype(F32)
        g_c = gc_ref[rows, :].astype(F32)
        merged = (g_a * jnp.dot(a_ref[rows, :], wa_ref[...], preferred_element_type=F32)
                  + g_c * jnp.dot(c[rows, :], wc_ref[...], preferred_element_type=F32))
        m = jnp.dot(merged.astype(BF16), wo_ref[...], preferred_element_type=F32)
        x1 = _layer_norm(DEEPNORM_ALPHA * x_ref[rows, :] + m, g1_ref[...], b1_ref[...])
        x1_ref[rows, :] = x1
        x1p_ref[rows, :] = _pack_row(x1)
        logits = jnp.dot(x1.astype(BF16), wr_ref[...], preferred_element_type=F32) + br_ref[...]

        lane = lax.broadcasted_iota(jnp.int32, logits.shape, 1)
        lane_f = lane.astype(F32)
        work = jnp.where(lane < N_EXPERTS, logits, -jnp.inf)
        hots, vals, idxs = [], [], []
        for _ in range(TOP_K):
            mx = jnp.max(work, axis=-1, keepdims=True)
            ix = jnp.min(jnp.where(work == mx, lane_f, float(ROUTER_COLS)), axis=-1, keepdims=True)
            hot = lane_f == ix
            work = jnp.where(hot, -jnp.inf, work)
            hots.append(hot)
            vals.append(mx)
            idxs.append(ix)
        exps = [jnp.exp(v - vals[0]) for v in vals]
        denom = exps[0] + exps[1] + exps[2] + exps[3]
        picked = jnp.zeros(logits.shape, F32)
        for hot in hots:
            picked = picked + hot.astype(F32)
        before = counts + jnp.dot(tri_ref[...], picked.astype(BF16), preferred_element_type=F32)
        out = jnp.zeros(logits.shape, F32)
        for j in range(TOP_K):
            rank = jnp.sum(jnp.where(hots[j], before, 0.0), axis=-1, keepdims=True)
            out = jnp.where(lane == RT_EXPERT + j, idxs[j], out)
            out = jnp.where(lane == RT_RANK + j, rank, out)
            out = jnp.where(lane == RT_GATE + j, exps[j] / denom, out)
        rt_ref[rows, :] = out
        rtt_ref[:, rows] = out.T[:RT_ROWS, :]
        counts = counts + jnp.sum(picked, axis=0, keepdims=True)
    cnt_ref[...] = counts


def _mix(a2d, h2d, x2d, seq, cw, wa, wc, wo, g1, b1, wr, br):
    n = x2d.shape[0]
    tm = MIX_TM
    tiles_per_seq = seq // tm
    halo_blocks = tm // SUBLANES
    n_halo = n // SUBLANES

    def col(blk):
        return pl.BlockSpec((tm, COL_BLOCK), lambda i: (i, blk))

    def prev(blk):
        return pl.BlockSpec((SUBLANES, COL_BLOCK), lambda i: (jnp.maximum(i * halo_blocks - 1, 0), blk))

    def nxt(blk):
        return pl.BlockSpec((SUBLANES, COL_BLOCK),
                            lambda i: (jnp.minimum((i + 1) * halo_blocks, n_halo - 1), blk))

    def full(shape):
        return pl.BlockSpec(shape, lambda i: (0,) * len(shape))

    sub = MIX_SUB
    tri = (lax.broadcasted_iota(jnp.int32, (sub, sub), 1)
           < lax.broadcasted_iota(jnp.int32, (sub, sub), 0)).astype(BF16)
    kern = functools.partial(_mix_kernel, tiles_per_seq=tiles_per_seq)
    return pl.pallas_call(
        kern,
        out_shape=(jax.ShapeDtypeStruct((n, D_MODEL), F32),
                   jax.ShapeDtypeStruct((n, HALF), jnp.int32),
                   jax.ShapeDtypeStruct((n, ROUTER_COLS), F32),
                   jax.ShapeDtypeStruct((RT_ROWS, n), F32),
                   jax.ShapeDtypeStruct((1, ROUTER_COLS), F32)),
        grid=(n // tm,),
        in_specs=[
            pl.BlockSpec((tm, D_MODEL), lambda i: (i, 0)),
            col(CB_BLK), col(CC_BLK), col(CX_BLK), col(GA_BLK), col(GC_BLK),
            prev(CC_BLK), prev(CX_BLK), nxt(CC_BLK), nxt(CX_BLK),
            pl.BlockSpec((tm, D_MODEL), lambda i: (i, 0)),
            full((SUBLANES, D_MODEL)),
            full((D_MODEL, D_MODEL)), full((D_MODEL, D_MODEL)), full((D_MODEL, D_MODEL)),
            full((1, D_MODEL)), full((1, D_MODEL)),
            full((D_MODEL, ROUTER_COLS)), full((1, ROUTER_COLS)), full((sub, sub)),
        ],
        out_specs=(pl.BlockSpec((tm, D_MODEL), lambda i: (i, 0)),
                   pl.BlockSpec((tm, HALF), lambda i: (i, 0)),
                   pl.BlockSpec((tm, ROUTER_COLS), lambda i: (i, 0)),
                   pl.BlockSpec((RT_ROWS, tm), lambda i: (0, i)),
                   pl.BlockSpec((1, ROUTER_COLS), lambda i: (0, 0))),
        compiler_params=pltpu.CompilerParams(
            dimension_semantics=("arbitrary",), vmem_limit_bytes=VMEM_LIMIT),
        name="mix_ln1_router",
    )(a2d, h2d, h2d, h2d, h2d, h2d, h2d, h2d, h2d, h2d, x2d, cw, wa, wc, wo, g1, b1, wr, br, tri)


def _moe_kernel(be_ref, nv_ref, nu_ref, xs_ref, wg_ref, bg_ref, wu_ref, bu_ref, wd_ref, bd_ref, o_ref,
                wg_s, wu_s, wd_s):
    i = pl.program_id(0)
    used = i < nu_ref[0]

    @pl.when(used & ((i == 0) | (be_ref[i] != be_ref[jnp.maximum(i - 1, 0)])))
    def _():
        wg_s[...] = wg_ref[...].astype(BF16)
        wu_s[...] = wu_ref[...].astype(BF16)
        wd_s[...] = wd_ref[...].astype(BF16)

    tm = xs_ref.shape[0]
    n_valid = nv_ref[i]

    def experts_rows(rows):
        row = lax.broadcasted_iota(jnp.int32, (rows, xs_ref.shape[1]), 0)
        x = _unpack_row(jnp.where(row < n_valid, xs_ref[:rows, :], 0)).astype(BF16)
        hg = jnp.minimum(jnp.dot(x, wg_s[...], preferred_element_type=F32) + bg_ref[...], SWIGLU_LIMIT)
        hu = jnp.clip(jnp.dot(x, wu_s[...], preferred_element_type=F32) + bu_ref[...],
                      -SWIGLU_LIMIT, SWIGLU_LIMIT)
        act = hg * jax.nn.sigmoid(SWIGLU_ALPHA * hg) * (hu + 1.0)
        y = jnp.dot(act.astype(BF16), wd_s[...], preferred_element_type=F32) + bd_ref[...]
        o_ref[:rows, :] = _pack_row(y)

    @pl.when(used & (n_valid > tm // 2))
    def _():
        experts_rows(tm)

    @pl.when(used & (n_valid <= tm // 2))
    def _():
        experts_rows(tm // 2)
        o_ref[tm // 2:, :] = jnp.zeros((tm - tm // 2, o_ref.shape[1]), o_ref.dtype)

    @pl.when(jnp.logical_not(used))
    def _():
        o_ref[...] = jnp.zeros(o_ref.shape, o_ref.dtype)


def _moe_experts(block_e, n_valid, n_used, xs, wg, bg, wu, bu, wd, bd):
    p = xs.shape[0]
    tm = MOE_TM
    nb = p // tm

    def row_map(i, be, nv, nu):
        return (jnp.minimum(i, nu[0] - 1), 0)

    def w_map(i, be, nv, nu):
        return (be[jnp.minimum(i, nu[0] - 1)], 0, 0)

    w_spec = pl.BlockSpec((None, D_MODEL, D_MODEL), w_map)
    b_spec = pl.BlockSpec((None, 1, D_MODEL), w_map)
    return pl.pallas_call(
        _moe_kernel,
        out_shape=jax.ShapeDtypeStruct((p, HALF), jnp.int32),
        grid_spec=pltpu.PrefetchScalarGridSpec(
            num_scalar_prefetch=3,
            grid=(nb,),
            in_specs=[
                pl.BlockSpec((tm, HALF), row_map),
                w_spec, b_spec, w_spec, b_spec, w_spec, b_spec,
            ],
            out_specs=pl.BlockSpec((tm, HALF), lambda i, be, nv, nu: (i, 0)),
            scratch_shapes=[pltpu.VMEM((D_MODEL, D_MODEL), BF16)] * 3,
        ),
        compiler_params=pltpu.CompilerParams(
            dimension_semantics=("arbitrary",), vmem_limit_bytes=VMEM_LIMIT),
        name="moe_experts",
    )(block_e, n_valid, n_used, xs, wg, bg, wu, bu, wd, bd)


def _sc_mesh():
    return plsc.VectorSubcoreMesh(core_axis_name="core", subcore_axis_name="subcore")


def _sc_index_rows(dest_t):
    k, n = dest_t.shape
    w = SC_WINDOW
    return jnp.pad(dest_t.reshape(k * n // w, w), ((0, 0), (0, LANES - w)))


def _sc_dispatch(x1, slot_idx, p):
    n, d = x1.shape
    w = SC_WINDOW
    windows = n // w

    @pl.kernel(out_type=jax.ShapeDtypeStruct((p, d), x1.dtype), mesh=_sc_mesh(), scratch_types=[],
               name="sc_dispatch")
    def run(x_hbm, d_hbm, o_hbm):
        def body(x_vmem, *d_vmems):
            for d_vmem in d_vmems:
                pltpu.sync_copy(x_vmem, o_hbm.at[d_vmem.at[0, pl.ds(0, w)]])

        pltpu.emit_pipeline(
            body,
            grid=(windows,),
            in_specs=[pl.BlockSpec((w, d), lambda i: (i, 0))]
            + [pl.BlockSpec((1, LANES), functools.partial(lambda k, i: (k * windows + i, 0), k))
               for k in range(TOP_K)],
            out_specs=[],
            core_axis_name=("core", "subcore"),
            dimension_semantics=(pltpu.PARALLEL,),
        )(x_hbm, *([d_hbm] * TOP_K))

    return run(x1, slot_idx)


def _sc_gather(ys, idx):
    w = SC_WINDOW
    r = idx.shape[0] * w
    d = ys.shape[1]

    @pl.kernel(out_type=jax.ShapeDtypeStruct((r, d), ys.dtype), mesh=_sc_mesh(), scratch_types=[],
               name="sc_gather")
    def run(y_hbm, i_hbm, o_hbm):
        def body(i_vmem, o_vmem):
            pltpu.sync_copy(y_hbm.at[i_vmem.at[0, pl.ds(0, w)]], o_vmem)

        pltpu.emit_pipeline(
            body,
            grid=(r // w,),
            in_specs=[pl.BlockSpec((1, LANES), lambda i: (i, 0))],
            out_specs=[pl.BlockSpec((w, d), lambda i: (i, 0))],
            core_axis_name=("core", "subcore"),
            dimension_semantics=(pltpu.PARALLEL,),
        )(i_hbm, o_hbm)

    return run(ys, idx)


def _ln2_kernel(x1_ref, yg_ref, rt_ref, g_ref, b_ref, o_ref):
    f = jnp.zeros(x1_ref.shape, F32)
    for k in range(TOP_K):
        f = f + _unpack_row(yg_ref[k]) * rt_ref[:, RT_GATE + k:RT_GATE + k + 1]
    o_ref[...] = _layer_norm(DEEPNORM_ALPHA * x1_ref[...] + f, g_ref[...], b_ref[...])


def _combine_ln2(x1, yg, rt, g2, b2):
    n = x1.shape[0]
    tm = LN2_TM
    return pl.pallas_call(
        _ln2_kernel,
        out_shape=jax.ShapeDtypeStruct((n, D_MODEL), F32),
        grid=(n // tm,),
        in_specs=[
            pl.BlockSpec((tm, D_MODEL), lambda i: (i, 0)),
            pl.BlockSpec((TOP_K, tm, HALF), lambda i: (0, i, 0)),
            pl.BlockSpec((tm, ROUTER_COLS), lambda i: (i, 0)),
            pl.BlockSpec((1, D_MODEL), lambda i: (0, 0)),
            pl.BlockSpec((1, D_MODEL), lambda i: (0, 0)),
        ],
        out_specs=pl.BlockSpec((tm, D_MODEL), lambda i: (i, 0)),
        compiler_params=pltpu.CompilerParams(
            dimension_semantics=("parallel",), vmem_limit_bytes=VMEM_LIMIT),
        name="combine_ln2",
    )(x1, yg, rt, g2, b2)


def _rotary_tables(seq):
    inv = ROPE_THETA ** (-jnp.arange(0, ROT_DIM, 2, dtype=F32) / ROT_DIM)
    ang = jnp.arange(seq, dtype=F32)[:, None] * inv[None, :]
    cos, sin = jnp.cos(ang), jnp.sin(ang)
    ones = jnp.ones((seq, QK_DIM - ROT_DIM), F32)
    zeros = jnp.zeros((seq, QK_DIM - ROT_DIM), F32)
    zh = jnp.zeros((seq, ROT_HALF), F32)
    cos64 = jnp.concatenate([cos, cos, ones], axis=1)
    sa64 = jnp.concatenate([-sin, zh, zeros], axis=1)
    sb64 = jnp.concatenate([zh, sin, zeros], axis=1)
    rep = LANES // QK_DIM
    return (jnp.tile(cos64, (1, rep)), jnp.tile(sa64, (1, rep)), jnp.tile(sb64, (1, rep)), cos.T, sin.T)


def _slot_layout(rtt, counts, tm):
    n = rtt.shape[1]
    p = n * TOP_K + N_EXPERTS * tm
    nb = p // tm
    expert = rtt[RT_EXPERT:RT_EXPERT + TOP_K].astype(jnp.int32)
    rank = rtt[RT_RANK:RT_RANK + TOP_K].astype(jnp.int32)
    padded = ((counts + tm - 1) // tm) * tm
    pad_ends = jnp.cumsum(padded)
    pad_starts = pad_ends - padded
    ids = jnp.arange(N_EXPERTS, dtype=jnp.int32)[:, None, None]
    dest_t = jnp.sum(jnp.where(expert[None] == ids, pad_starts[:, None, None], 0), axis=0) + rank
    block_start = jnp.arange(nb, dtype=jnp.int32) * tm
    block_e = jnp.minimum(jnp.sum(pad_ends[None, :] <= block_start[:, None], axis=1),
                          N_EXPERTS - 1).astype(jnp.int32)
    own = block_e[:, None] == jnp.arange(N_EXPERTS, dtype=jnp.int32)[None, :]
    valid_end = jnp.sum(jnp.where(own, (pad_starts + counts)[None, :], 0), axis=1)
    n_valid = jnp.clip(valid_end - block_start, 0, tm).astype(jnp.int32)
    n_used = (pad_ends[-1] // tm).astype(jnp.int32).reshape(1)
    return dest_t.astype(jnp.int32), block_e, n_valid, n_used, p


def _trunk(x, prm):
    b, s, d = x.shape
    n = b * s
    x2d = x.reshape(n, d)
    h, qt, vt = _inproj(x2d, prm["w_in"], prm["w_q_t"], prm["w_v_t"], prm["b_gate"], _rotary_tables(s), b, s)
    a = _attention(h.reshape(b, s, H_COLS), qt, vt, prm["lam"], prm["subln_g"], prm["lam_init"])
    x1, x1p, rt, rtt, cnt = _mix(a.reshape(n, d), h, x2d, s, prm["conv_w"], prm["w_attn_br"],
                                 prm["w_conv_br"], prm["w_o"], prm["ln1_g"], prm["ln1_b"],
                                 prm["w_router"], prm["b_router"])
    counts = cnt[0, :N_EXPERTS].astype(jnp.int32)
    dest_t, block_e, n_valid, n_used, p = _slot_layout(rtt, counts, MOE_TM)
    slot_idx = _sc_index_rows(dest_t)
    xs = _sc_dispatch(x1p, slot_idx, p)
    ys = _moe_experts(block_e, n_valid, n_used, xs, prm["w_exp_gate"], prm["b_exp_gate"],
                      prm["w_exp_up"], prm["b_exp_up"], prm["w_exp_down"], prm["b_exp_down"])
    yg = _sc_gather(ys, slot_idx).reshape(TOP_K, n, HALF)
    y = _combine_ln2(x1, yg, rt, prm["ln2_g"], prm["ln2_b"])
    return y.reshape(b, s, d)


def kernel(x_prompt, x_sample, w_in, b_branch_gate, lambda_q1, lambda_k1, lambda_q2, lambda_k2, subln_g,
           conv_w, w_attn_br, w_conv_br, w_o, ln1_g, ln1_b, w_router, b_router, w_exp_gate, b_exp_gate,
           w_exp_up, b_exp_up, w_exp_down, b_exp_down, ln2_g, ln2_b):
    l = 0
    lam_init = 0.8 - 0.6 * math.exp(-0.3 * l)
    lam = (jnp.exp(jnp.sum(lambda_q1[l].astype(F32) * lambda_k1[l].astype(F32)))
           - jnp.exp(jnp.sum(lambda_q2[l].astype(F32) * lambda_k2[l].astype(F32))) + lam_init)
    prm = {
        "lam_init": lam_init,
        "lam": lam.reshape(1).astype(F32),
        "w_in": w_in[l].astype(BF16),
        "w_q_t": w_in[l][:, Q_STEP * COL_BLOCK:(Q_STEP + 1) * COL_BLOCK].T.astype(BF16),
        "w_v_t": w_in[l][:, V_STEP * COL_BLOCK:(V_STEP + 1) * COL_BLOCK].T.astype(BF16),
        "b_gate": b_branch_gate[l].reshape(1, 2 * D_MODEL),
        "subln_g": subln_g[l].reshape(1, V_DIM),
        "conv_w": jnp.pad(conv_w[l], ((0, SUBLANES - conv_w.shape[1]), (0, 0))),
        "w_attn_br": w_attn_br[l].astype(BF16),
        "w_conv_br": w_conv_br[l].astype(BF16),
        "w_o": w_o[l].astype(BF16),
        "ln1_g": ln1_g[l].reshape(1, D_MODEL),
        "ln1_b": ln1_b[l].reshape(1, D_MODEL),
        "w_router": jnp.pad(w_router[l], ((0, 0), (0, ROUTER_COLS - N_EXPERTS))).astype(BF16),
        "b_router": jnp.pad(b_router[l], (0, ROUTER_COLS - N_EXPERTS)).reshape(1, ROUTER_COLS),
        "w_exp_gate": w_exp_gate[l],
        "b_exp_gate": b_exp_gate[l].reshape(N_EXPERTS, 1, D_MODEL),
        "w_exp_up": w_exp_up[l],
        "b_exp_up": b_exp_up[l].reshape(N_EXPERTS, 1, D_MODEL),
        "w_exp_down": w_exp_down[l],
        "b_exp_down": b_exp_down[l].reshape(N_EXPERTS, 1, D_MODEL),
        "ln2_g": ln2_g[l].reshape(1, D_MODEL),
        "ln2_b": ln2_b[l].reshape(1, D_MODEL),
    }
    return _trunk(x_prompt, prm), _trunk(x_sample, prm)
```

```python
import functools
import math

import jax
import jax.numpy as jnp
from jax import lax
from jax.experimental import pallas as pl
from jax.experimental.pallas import tpu as pltpu
from jax.experimental.pallas import tpu_sc as plsc

F32 = jnp.float32
BF16 = jnp.bfloat16

D_MODEL = 1024
N_HEADS = 8
QK_DIM = 64
V_DIM = 128
HEAD_COLS = 2 * QK_DIM
ROT_DIM = QK_DIM // 4
ROT_HALF = ROT_DIM // 2
ROPE_THETA = 500000.0
SUBLN_EPS = 1e-5
IN_COLS = 8192
COL_BLOCK = 1024
Q_STEP, K_STEP, V_STEP = 0, 1, 2
GATE_STEP = 6
H_COLS = IN_COLS - 2 * COL_BLOCK
CB_BLK, CC_BLK, CX_BLK, GA_BLK, GC_BLK = 1, 2, 3, 4, 5
K_HEAD0 = 0
N_EXPERTS = 32
TOP_K = 4
SWIGLU_LIMIT = 7.0
SWIGLU_ALPHA = 1.702
LN_EPS = 1e-5
DEPTH = 1
DEEPNORM_ALPHA = (2 * DEPTH) ** 0.25
LANES = 128
SUBLANES = 8
ROUTER_COLS = LANES
RT_EXPERT, RT_RANK, RT_GATE = 0, 4, 8
RT_ROWS = 16
SC_WINDOW = 32
ONES_ROWS = 16
Q_SCALE = math.log2(math.e) * QK_DIM ** -0.5

VMEM_LIMIT = 56 * 1024 * 1024

PROJ_TM = 1024
ATTN_TQ = 1024
ATTN_TK = 512
MIX_TM = 512
MIX_SUB = 256
MOE_TM = 512
LN2_TM = 1024


def _rotate_rows(t1, t2, cos, sin):
    return t1 * cos - t2 * sin, t2 * cos + t1 * sin


def _inproj_kernel(x_ref, w_ref, wqt_ref, wvt_ref, bg_ref, cos_ref, sa_ref, sb_ref, cosr_ref, sinr_ref,
                   o_ref, qt_ref, vt_ref, xb_s):
    j = pl.program_id(1)

    @pl.when(j == 0)
    def _():
        xb_s[...] = x_ref[...].astype(BF16)

    xb = xb_s[...]
    nt = (((1,), (1,)), ((), ()))

    @pl.when(j == Q_STEP)
    def _():
        acc_t = lax.dot_general(wqt_ref[...], xb, nt, preferred_element_type=F32) * Q_SCALE
        cos, sin = cosr_ref[...], sinr_ref[...]
        for hh in range(N_HEADS):
            parts = []
            for mp in range(2):
                o = hh * HEAD_COLS + mp * QK_DIM
                parts += _rotate_rows(acc_t[o:o + ROT_HALF], acc_t[o + ROT_HALF:o + ROT_DIM], cos, sin)
                parts.append(acc_t[o + ROT_DIM:o + QK_DIM])
            qt_ref[hh, 0] = jnp.concatenate(parts, axis=0).astype(BF16)

    @pl.when(j == V_STEP)
    def _():
        acc_t = lax.dot_general(wvt_ref[...], xb, nt, preferred_element_type=F32)
        n_heads, n_chunks, _, tk = vt_ref.shape
        for hh in range(n_heads):
            for cc in range(n_chunks):
                vt_ref[hh, cc, :V_DIM, :] = acc_t[hh * V_DIM:(hh + 1) * V_DIM,
                                                  cc * tk:(cc + 1) * tk].astype(BF16)
                vt_ref[hh, cc, V_DIM:, :] = jnp.ones((ONES_ROWS, tk), BF16)

    @pl.when((j > V_STEP) & (j < GATE_STEP))
    def _():
        o_ref[...] = jnp.dot(xb, w_ref[...], preferred_element_type=F32).astype(o_ref.dtype)

    @pl.when(j >= GATE_STEP)
    def _():
        gates = jnp.dot(xb, w_ref[...], preferred_element_type=F32) + bg_ref[...]
        o_ref[...] = jax.nn.sigmoid(gates).astype(o_ref.dtype)

    @pl.when(j == K_STEP)
    def _():
        acc = jnp.dot(xb, w_ref[...], preferred_element_type=F32)
        cos, sa, sb = cos_ref[...], sa_ref[...], sb_ref[...]
        for c in range(COL_BLOCK // LANES):
            a = acc[:, c * LANES:(c + 1) * LANES]
            r = (a * cos + pltpu.roll(a, LANES - ROT_HALF, 1) * sa + pltpu.roll(a, ROT_HALF, 1) * sb)
            o_ref[:, c * LANES:(c + 1) * LANES] = r.astype(o_ref.dtype)


def _inproj(x2d, w_in_b, wqt_b, wvt_b, b_gate, tables, batch, seq):
    n = x2d.shape[0]
    tm, tk = PROJ_TM, ATTN_TK
    assert tm == ATTN_TQ
    pos_blocks = seq // tm
    cos_t, sa_t, sb_t, cos_r, sin_r = tables
    tab_spec = pl.BlockSpec((tm, LANES), lambda i, j: (i % pos_blocks, 0))
    rtab_spec = pl.BlockSpec((ROT_HALF, tm), lambda i, j: (0, i % pos_blocks))
    wt_spec = pl.BlockSpec((COL_BLOCK, D_MODEL), lambda i, j: (0, 0))
    return pl.pallas_call(
        _inproj_kernel,
        out_shape=(jax.ShapeDtypeStruct((n, H_COLS), BF16),
                   jax.ShapeDtypeStruct((batch, N_HEADS, seq // tm, HEAD_COLS, tm), BF16),
                   jax.ShapeDtypeStruct((batch, N_HEADS, seq // tk, V_DIM + ONES_ROWS, tk), BF16)),
        grid=(n // tm, IN_COLS // COL_BLOCK),
        in_specs=[
            pl.BlockSpec((tm, D_MODEL), lambda i, j: (i, 0)),
            pl.BlockSpec((D_MODEL, COL_BLOCK), lambda i, j: (0, jnp.where(j <= V_STEP, K_STEP, j))),
            wt_spec, wt_spec,
            pl.BlockSpec((1, COL_BLOCK), lambda i, j: (0, jnp.maximum(j - GATE_STEP, 0))),
            tab_spec, tab_spec, tab_spec, rtab_spec, rtab_spec,
        ],
        out_specs=(
            pl.BlockSpec((tm, COL_BLOCK), lambda i, j: (i, jnp.maximum(j - V_STEP, 0))),
            pl.BlockSpec((None, N_HEADS, 1, HEAD_COLS, tm),
                         lambda i, j: (i // pos_blocks, 0, i % pos_blocks, 0, 0)),
            pl.BlockSpec((None, N_HEADS, tm // tk, V_DIM + ONES_ROWS, tk),
                         lambda i, j: (i // pos_blocks, 0, i % pos_blocks, 0, 0)),
        ),
        scratch_shapes=[pltpu.VMEM((tm, D_MODEL), BF16)],
        compiler_params=pltpu.CompilerParams(
            dimension_semantics=("parallel", "arbitrary"), vmem_limit_bytes=VMEM_LIMIT),
        name="inproj",
    )(x2d, w_in_b, wqt_b, wvt_b, b_gate, cos_t, sa_t, sb_t, cos_r, sin_r)


def _attn_kernel(lam_ref, q_ref, k_ref, vt_ref, g_ref, o_ref, qt_s, acc_s, sa0, sa1, sb0, sb1, *, out_scale):
    n_q, _, tq = q_ref.shape
    n_chunks, _, tk = vt_ref.shape
    sa_s, sb_s = (sa0, sa1), (sb0, sb1)

    def load_q(qi):
        zero_half = jnp.zeros((QK_DIM, tq), BF16)
        qt_s[0, :QK_DIM, :] = q_ref[qi, :QK_DIM, :]
        qt_s[0, QK_DIM:, :] = zero_half
        qt_s[1, :QK_DIM, :] = zero_half
        qt_s[1, QK_DIM:, :] = q_ref[qi, QK_DIM:, :]

    def scores(c, s_refs, mp):
        k = k_ref[pl.ds(pl.multiple_of(c * tk, tk), tk), :]
        s = jnp.dot(k, qt_s[mp], preferred_element_type=F32)
        s_refs[mp][...] = s
        return jnp.max(s, axis=0, keepdims=True)

    def update(c, s_refs, mp, m_old, c_max):
        m_new = jnp.maximum(m_old, c_max)
        alpha = jnp.exp2(m_old - m_new)
        p = jnp.exp2(s_refs[mp][...] - m_new).astype(BF16)
        acc_s[mp] = alpha * acc_s[mp] + jnp.dot(vt_ref[c], p, preferred_element_type=F32)
        return m_new

    def step(c, cur, nxt, ms, cur_max, prefetch):
        new_ms, nxt_max = [], []
        for mp in range(2):
            if prefetch:
                nxt_max.append(scores(c + 1, nxt, mp))
            new_ms.append(update(c, cur, mp, ms[mp], cur_max[mp]))
        return tuple(new_ms), tuple(nxt_max)

    def pair(i, carry):
        ms, a_max = carry
        ms, b_max = step(2 * i, sa_s, sb_s, ms, a_max, True)
        return step(2 * i + 1, sb_s, sa_s, ms, b_max, True)

    def finalize(qi):
        lam = lam_ref[0]
        o1 = acc_s[0, :V_DIM, :] / acc_s[0, V_DIM:V_DIM + 1, :]
        o2 = acc_s[1, :V_DIM, :] / acc_s[1, V_DIM:V_DIM + 1, :]
        ot = o1 - lam * o2
        ot = ot * lax.rsqrt(jnp.mean(ot * ot, axis=0, keepdims=True) + SUBLN_EPS)
        rows = pl.ds(pl.multiple_of(qi * tq, tq), tq)
        o_ref[rows, :] = (ot.T * (g_ref[...] * out_scale)).astype(o_ref.dtype)

    def q_block(qi, a_max, last):
        acc_s[...] = jnp.zeros(acc_s.shape, F32)
        m0 = jnp.full((1, tq), -jnp.inf, F32)
        ms, a_max = lax.fori_loop(0, n_chunks // 2 - 1, pair, ((m0, m0), a_max))
        ms, b_max = step(n_chunks - 2, sa_s, sb_s, ms, a_max, True)
        step(n_chunks - 1, sb_s, sa_s, ms, b_max, False)
        if not last:
            load_q(qi + 1)
            a_max = tuple(scores(0, sa_s, mp) for mp in range(2))
        finalize(qi)
        return a_max

    load_q(0)
    a_max = tuple(scores(0, sa_s, mp) for mp in range(2))
    a_max = lax.fori_loop(0, n_q - 1, lambda qi, am: q_block(qi, am, False), a_max)
    q_block(n_q - 1, a_max, True)


def _attention(h3, qt, vt, lam, subln_g, lam_init):
    b, s, _ = h3.shape
    tq, tk = ATTN_TQ, ATTN_TK
    n_chunks = s // tk
    assert n_chunks % 2 == 0 and n_chunks >= 2
    vt_rows = V_DIM + ONES_ROWS
    kern = functools.partial(_attn_kernel, out_scale=1.0 - lam_init)
    return pl.pallas_call(
        kern,
        out_shape=jax.ShapeDtypeStruct((b, s, N_HEADS * V_DIM), BF16),
        grid=(b, N_HEADS),
        in_specs=[
            pl.BlockSpec(memory_space=pltpu.SMEM),
            pl.BlockSpec((None, None, s // tq, HEAD_COLS, tq), lambda bi, hi: (bi, hi, 0, 0, 0)),
            pl.BlockSpec((None, s, HEAD_COLS), lambda bi, hi: (bi, 0, K_HEAD0 + hi)),
            pl.BlockSpec((None, None, n_chunks, vt_rows, tk), lambda bi, hi: (bi, hi, 0, 0, 0)),
            pl.BlockSpec((1, V_DIM), lambda bi, hi: (0, 0)),
        ],
        out_specs=pl.BlockSpec((None, s, V_DIM), lambda bi, hi: (bi, 0, hi)),
        scratch_shapes=[
            pltpu.VMEM((2, HEAD_COLS, tq), BF16),
            pltpu.VMEM((2, vt_rows, tq), F32),
            pltpu.VMEM((tk, tq), F32), pltpu.VMEM((tk, tq), F32),
            pltpu.VMEM((tk, tq), F32), pltpu.VMEM((tk, tq), F32),
        ],
        compiler_params=pltpu.CompilerParams(
            dimension_semantics=("parallel", "arbitrary"),
            vmem_limit_bytes=VMEM_LIMIT),
        name="diff_attention",
    )(lam, qt, h3, vt, subln_g)


HALF = D_MODEL // 2


def _pack_row(x):
    return pltpu.pack_elementwise([x[:, :HALF], x[:, HALF:]], packed_dtype=BF16)


def _unpack_row(w):
    lo = pltpu.unpack_elementwise(w, index=0, packed_dtype=BF16, unpacked_dtype=F32)
    hi = pltpu.unpack_elementwise(w, index=1, packed_dtype=BF16, unpacked_dtype=F32)
    return jnp.concatenate([lo, hi], axis=1)


def _layer_norm(y, g, b):
    mu = jnp.mean(y, axis=-1, keepdims=True)
    d = y - mu
    var = jnp.mean(d * d, axis=-1, keepdims=True)
    return d * lax.rsqrt(var + LN_EPS) * g + b


def _mix_kernel(a_ref, cb_ref, cc_ref, cx_ref, ga_ref, gc_ref, ccp_ref, cxp_ref, ccn_ref, cxn_ref,
                x_ref, cw_ref, wa_ref, wc_ref, wo_ref, g1_ref, b1_ref, wr_ref, br_ref, tri_ref,
                x1_ref, x1p_ref, rt_ref, rtt_ref, cnt_ref, *, tiles_per_seq):
    i = pl.program_id(0)

    @pl.when(i == 0)
    def _():
        cnt_ref[...] = jnp.zeros(cnt_ref.shape, F32)

    tm = x_ref.shape[0]
    u = cc_ref[...].astype(F32) * cx_ref[...].astype(F32)
    first = (i % tiles_per_seq) == 0
    last = (i % tiles_per_seq) == tiles_per_seq - 1
    up = ccp_ref[SUBLANES - 1:SUBLANES, :].astype(F32) * cxp_ref[SUBLANES - 1:SUBLANES, :].astype(F32)
    un = ccn_ref[0:1, :].astype(F32) * cxn_ref[0:1, :].astype(F32)
    up = jnp.where(first, 0.0, up)
    un = jnp.where(last, 0.0, un)
    row = lax.broadcasted_iota(jnp.int32, u.shape, 0)
    u_prev = jnp.where(row == 0, up, pltpu.roll(u, 1, 0))
    u_next = jnp.where(row == tm - 1, un, pltpu.roll(u, tm - 1, 0))
    conv = cw_ref[0:1, :] * u_prev + cw_ref[1:2, :] * u + cw_ref[2:3, :] * u_next
    c = (cb_ref[...].astype(F32) * conv).astype(BF16)

    sub = tri_ref.shape[0]
    counts = cnt_ref[...]
    for s in range(tm // sub):
        rows = slice(s * sub, (s + 1) * sub)
        g_a = ga_ref[rows, :].astype(F32)
        g_c = gc_ref[rows, :].astype(F32)
        merged = (g_a * jnp.dot(a_ref[rows, :], wa_ref[...], preferred_element_type=F32)
                  + g_c * jnp.dot(c[rows, :], wc_ref[...], preferred_element_type=F32))
        m = jnp.dot(merged.astype(BF16), wo_ref[...], preferred_element_type=F32)
        x1 = _layer_norm(DEEPNORM_ALPHA * x_ref[rows, :] + m, g1_ref[...], b1_ref[...])
        x1_ref[rows, :] = x1
        x1p_ref[rows, :] = _pack_row(x1)
        logits = jnp.dot(x1.astype(BF16), wr_ref[...], preferred_element_type=F32) + br_ref[...]

        lane = lax.broadcasted_iota(jnp.int32, logits.shape, 1)
        lane_f = lane.astype(F32)
        work = jnp.where(lane < N_EXPERTS, logits, -jnp.inf)
        hots, vals, idxs = [], [], []
        for _ in range(TOP_K):
            mx = jnp.max(work, axis=-1, keepdims=True)
            ix = jnp.min(jnp.where(work == mx, lane_f, float(ROUTER_COLS)), axis=-1, keepdims=True)
            hot = lane_f == ix
            work = jnp.where(hot, -jnp.inf, work)
            hots.append(hot)
            vals.append(mx)
            idxs.append(ix)
        exps = [jnp.exp(v - vals[0]) for v in vals]
        denom = exps[0] + exps[1] + exps[2] + exps[3]
        picked = jnp.zeros(logits.shape, F32)
        for hot in hots:
            picked = picked + hot.astype(F32)
        before = counts + jnp.dot(tri_ref[...], picked.astype(BF16), preferred_element_type=F32)
        out = jnp.zeros(logits.shape, F32)
        for j in range(TOP_K):
            rank = jnp.sum(jnp.where(hots[j], before, 0.0), axis=-1, keepdims=True)
            out = jnp.where(lane == RT_EXPERT + j, idxs[j], out)
            out = jnp.where(lane == RT_RANK + j, rank, out)
            out = jnp.where(lane == RT_GATE + j, exps[j] / denom, out)
        rt_ref[rows, :] = out
        rtt_ref[:, rows] = out.T[:RT_ROWS, :]
        counts = counts + jnp.sum(picked, axis=0, keepdims=True)
    cnt_ref[...] = counts


def _mix(a2d, h2d, x2d, seq, cw, wa, wc, wo, g1, b1, wr, br):
    n = x2d.shape[0]
    tm = MIX_TM
    tiles_per_seq = seq // tm
    halo_blocks = tm // SUBLANES
    n_halo = n // SUBLANES

    def col(blk):
        return pl.BlockSpec((tm, COL_BLOCK), lambda i: (i, blk))

    def prev(blk):
        return pl.BlockSpec((SUBLANES, COL_BLOCK), lambda i: (jnp.maximum(i * halo_blocks - 1, 0), blk))

    def nxt(blk):
        return pl.BlockSpec((SUBLANES, COL_BLOCK),
                            lambda i: (jnp.minimum((i + 1) * halo_blocks, n_halo - 1), blk))

    def full(shape):
        return pl.BlockSpec(shape, lambda i: (0,) * len(shape))

    sub = MIX_SUB
    tri = (lax.broadcasted_iota(jnp.int32, (sub, sub), 1)
           < lax.broadcasted_iota(jnp.int32, (sub, sub), 0)).astype(BF16)
    kern = functools.partial(_mix_kernel, tiles_per_seq=tiles_per_seq)
    return pl.pallas_call(
        kern,
        out_shape=(jax.ShapeDtypeStruct((n, D_MODEL), F32),
                   jax.ShapeDtypeStruct((n, HALF), jnp.int32),
                   jax.ShapeDtypeStruct((n, ROUTER_COLS), F32),
                   jax.ShapeDtypeStruct((RT_ROWS, n), F32),
                   jax.ShapeDtypeStruct((1, ROUTER_COLS), F32)),
        grid=(n // tm,),
        in_specs=[
            pl.BlockSpec((tm, D_MODEL), lambda i: (i, 0)),
            col(CB_BLK), col(CC_BLK), col(CX_BLK), col(GA_BLK), col(GC_BLK),
            prev(CC_BLK), prev(CX_BLK), nxt(CC_BLK), nxt(CX_BLK),
            pl.BlockSpec((tm, D_MODEL), lambda i: (i, 0)),
            full((SUBLANES, D_MODEL)),
            full((D_MODEL, D_MODEL)), full((D_MODEL, D_MODEL)), full((D_MODEL, D_MODEL)),
            full((1, D_MODEL)), full((1, D_MODEL)),
            full((D_MODEL, ROUTER_COLS)), full((1, ROUTER_COLS)), full((sub, sub)),
        ],
        out_specs=(pl.BlockSpec((tm, D_MODEL), lambda i: (i, 0)),
                   pl.BlockSpec((tm, HALF), lambda i: (i, 0)),
                   pl.BlockSpec((tm, ROUTER_COLS), lambda i: (i, 0)),
                   pl.BlockSpec((RT_ROWS, tm), lambda i: (0, i)),
                   pl.BlockSpec((1, ROUTER_COLS), lambda i: (0, 0))),
        compiler_params=pltpu.CompilerParams(
            dimension_semantics=("arbitrary",), vmem_limit_bytes=VMEM_LIMIT),
        name="mix_ln1_router",
    )(a2d, h2d, h2d, h2d, h2d, h2d, h2d, h2d, h2d, h2d, x2d, cw, wa, wc, wo, g1, b1, wr, br, tri)


def _moe_kernel(be_ref, nv_ref, nu_ref, xs_ref, wg_ref, bg_ref, wu_ref, bu_ref, wd_ref, bd_ref, o_ref,
                wg_s, wu_s, wd_s):
    i = pl.program_id(0)
    used = i < nu_ref[0]

    @pl.when(used & ((i == 0) | (be_ref[i] != be_ref[jnp.maximum(i - 1, 0)])))
    def _():
        wg_s[...] = wg_ref[...].astype(BF16)
        wu_s[...] = wu_ref[...].astype(BF16)
        wd_s[...] = wd_ref[...].astype(BF16)

    @pl.when(used)
    def _():
        row = lax.broadcasted_iota(jnp.int32, xs_ref.shape, 0)
        x = _unpack_row(jnp.where(row < nv_ref[i], xs_ref[...], 0)).astype(BF16)
        hg = jnp.minimum(jnp.dot(x, wg_s[...], preferred_element_type=F32) + bg_ref[...], SWIGLU_LIMIT)
        hu = jnp.clip(jnp.dot(x, wu_s[...], preferred_element_type=F32) + bu_ref[...],
                      -SWIGLU_LIMIT, SWIGLU_LIMIT)
        act = hg * jax.nn.sigmoid(SWIGLU_ALPHA * hg) * (hu + 1.0)
        y = jnp.dot(act.astype(BF16), wd_s[...], preferred_element_type=F32) + bd_ref[...]
        o_ref[...] = _pack_row(y)

    @pl.when(jnp.logical_not(used))
    def _():
        o_ref[...] = jnp.zeros(o_ref.shape, o_ref.dtype)


def _moe_experts(block_e, n_valid, n_used, xs, wg, bg, wu, bu, wd, bd):
    p = xs.shape[0]
    tm = MOE_TM
    nb = p // tm

    def row_map(i, be, nv, nu):
        return (jnp.minimum(i, nu[0] - 1), 0)

    def w_map(i, be, nv, nu):
        return (be[jnp.minimum(i, nu[0] - 1)], 0, 0)

    w_spec = pl.BlockSpec((None, D_MODEL, D_MODEL), w_map)
    b_spec = pl.BlockSpec((None, 1, D_MODEL), w_map)
    return pl.pallas_call(
        _moe_kernel,
        out_shape=jax.ShapeDtypeStruct((p, HALF), jnp.int32),
        grid_spec=pltpu.PrefetchScalarGridSpec(
            num_scalar_prefetch=3,
            grid=(nb,),
            in_specs=[
                pl.BlockSpec((tm, HALF), row_map),
                w_spec, b_spec, w_spec, b_spec, w_spec, b_spec,
            ],
            out_specs=pl.BlockSpec((tm, HALF), lambda i, be, nv, nu: (i, 0)),
            scratch_shapes=[pltpu.VMEM((D_MODEL, D_MODEL), BF16)] * 3,
        ),
        compiler_params=pltpu.CompilerParams(
            dimension_semantics=("arbitrary",), vmem_limit_bytes=VMEM_LIMIT),
        name="moe_experts",
    )(block_e, n_valid, n_used, xs, wg, bg, wu, bu, wd, bd)


def _sc_mesh():
    return plsc.VectorSubcoreMesh(core_axis_name="core", subcore_axis_name="subcore")


def _sc_index_rows(dest_t):
    k, n = dest_t.shape
    w = SC_WINDOW
    return jnp.pad(dest_t.reshape(k * n // w, w), ((0, 0), (0, LANES - w)))


def _sc_dispatch(x1, slot_idx, p):
    n, d = x1.shape
    w = SC_WINDOW
    windows = n // w

    @pl.kernel(out_type=jax.ShapeDtypeStruct((p, d), x1.dtype), mesh=_sc_mesh(), scratch_types=[],
               name="sc_dispatch")
    def run(x_hbm, d_hbm, o_hbm):
        def body(x_vmem, *d_vmems):
            for d_vmem in d_vmems:
                pltpu.sync_copy(x_vmem, o_hbm.at[d_vmem.at[0, pl.ds(0, w)]])

        pltpu.emit_pipeline(
            body,
            grid=(windows,),
            in_specs=[pl.BlockSpec((w, d), lambda i: (i, 0))]
            + [pl.BlockSpec((1, LANES), functools.partial(lambda k, i: (k * windows + i, 0), k))
               for k in range(TOP_K)],
            out_specs=[],
            core_axis_name=("core", "subcore"),
            dimension_semantics=(pltpu.PARALLEL,),
        )(x_hbm, *([d_hbm] * TOP_K))

    return run(x1, slot_idx)


def _sc_gather(ys, idx):
    w = SC_WINDOW
    r = idx.shape[0] * w
    d = ys.shape[1]

    @pl.kernel(out_type=jax.ShapeDtypeStruct((r, d), ys.dtype), mesh=_sc_mesh(), scratch_types=[],
               name="sc_gather")
    def run(y_hbm, i_hbm, o_hbm):
        def body(i_vmem, o_vmem):
            pltpu.sync_copy(y_hbm.at[i_vmem.at[0, pl.ds(0, w)]], o_vmem)

        pltpu.emit_pipeline(
            body,
            grid=(r // w,),
            in_specs=[pl.BlockSpec((1, LANES), lambda i: (i, 0))],
            out_specs=[pl.BlockSpec((w, d), lambda i: (i, 0))],
            core_axis_name=("core", "subcore"),
            dimension_semantics=(pltpu.PARALLEL,),
        )(i_hbm, o_hbm)

    return run(ys, idx)


def _ln2_kernel(x1_ref, yg_ref, rt_ref, g_ref, b_ref, o_ref):
    f = jnp.zeros(x1_ref.shape, F32)
    for k in range(TOP_K):
        f = f + _unpack_row(yg_ref[k]) * rt_ref[:, RT_GATE + k:RT_GATE + k + 1]
    o_ref[...] = _layer_norm(DEEPNORM_ALPHA * x1_ref[...] + f, g_ref[...], b_ref[...])


def _combine_ln2(x1, yg, rt, g2, b2):
    n = x1.shape[0]
    tm = LN2_TM
    return pl.pallas_call(
        _ln2_kernel,
        out_shape=jax.ShapeDtypeStruct((n, D_MODEL), F32),
        grid=(n // tm,),
        in_specs=[
            pl.BlockSpec((tm, D_MODEL), lambda i: (i, 0)),
            pl.BlockSpec((TOP_K, tm, HALF), lambda i: (0, i, 0)),
            pl.BlockSpec((tm, ROUTER_COLS), lambda i: (i, 0)),
            pl.BlockSpec((1, D_MODEL), lambda i: (0, 0)),
            pl.BlockSpec((1, D_MODEL), lambda i: (0, 0)),
        ],
        out_specs=pl.BlockSpec((tm, D_MODEL), lambda i: (i, 0)),
        compiler_params=pltpu.CompilerParams(
            dimension_semantics=("parallel",), vmem_limit_bytes=VMEM_LIMIT),
        name="combine_ln2",
    )(x1, yg, rt, g2, b2)


def _rotary_tables(seq):
    inv = ROPE_THETA ** (-jnp.arange(0, ROT_DIM, 2, dtype=F32) / ROT_DIM)
    ang = jnp.arange(seq, dtype=F32)[:, None] * inv[None, :]
    cos, sin = jnp.cos(ang), jnp.sin(ang)
    ones = jnp.ones((seq, QK_DIM - ROT_DIM), F32)
    zeros = jnp.zeros((seq, QK_DIM - ROT_DIM), F32)
    zh = jnp.zeros((seq, ROT_HALF), F32)
    cos64 = jnp.concatenate([cos, cos, ones], axis=1)
    sa64 = jnp.concatenate([-sin, zh, zeros], axis=1)
    sb64 = jnp.concatenate([zh, sin, zeros], axis=1)
    rep = LANES // QK_DIM
    return (jnp.tile(cos64, (1, rep)), jnp.tile(sa64, (1, rep)), jnp.tile(sb64, (1, rep)), cos.T, sin.T)


def _slot_layout(rtt, counts, tm):
    n = rtt.shape[1]
    p = n * TOP_K + N_EXPERTS * tm
    nb = p // tm
    expert = rtt[RT_EXPERT:RT_EXPERT + TOP_K].astype(jnp.int32)
    rank = rtt[RT_RANK:RT_RANK + TOP_K].astype(jnp.int32)
    padded = ((counts + tm - 1) // tm) * tm
    pad_ends = jnp.cumsum(padded)
    pad_starts = pad_ends - padded
    ids = jnp.arange(N_EXPERTS, dtype=jnp.int32)[:, None, None]
    dest_t = jnp.sum(jnp.where(expert[None] == ids, pad_starts[:, None, None], 0), axis=0) + rank
    block_start = jnp.arange(nb, dtype=jnp.int32) * tm
    block_e = jnp.minimum(jnp.sum(pad_ends[None, :] <= block_start[:, None], axis=1),
                          N_EXPERTS - 1).astype(jnp.int32)
    own = block_e[:, None] == jnp.arange(N_EXPERTS, dtype=jnp.int32)[None, :]
    valid_end = jnp.sum(jnp.where(own, (pad_starts + counts)[None, :], 0), axis=1)
    n_valid = jnp.clip(valid_end - block_start, 0, tm).astype(jnp.int32)
    n_used = (pad_ends[-1] // tm).astype(jnp.int32).reshape(1)
    return dest_t.astype(jnp.int32), block_e, n_valid, n_used, p


def _trunk(x, prm):
    b, s, d = x.shape
    n = b * s
    x2d = x.reshape(n, d)
    h, qt, vt = _inproj(x2d, prm["w_in"], prm["w_q_t"], prm["w_v_t"], prm["b_gate"], _rotary_tables(s), b, s)
    a = _attention(h.reshape(b, s, H_COLS), qt, vt, prm["lam"], prm["subln_g"], prm["lam_init"])
    x1, x1p, rt, rtt, cnt = _mix(a.reshape(n, d), h, x2d, s, prm["conv_w"], prm["w_attn_br"],
                                 prm["w_conv_br"], prm["w_o"], prm["ln1_g"], prm["ln1_b"],
                                 prm["w_router"], prm["b_router"])
    counts = cnt[0, :N_EXPERTS].astype(jnp.int32)
    dest_t, block_e, n_valid, n_used, p = _slot_layout(rtt, counts, MOE_TM)
    slot_idx = _sc_index_rows(dest_t)
    xs = _sc_dispatch(x1p, slot_idx, p)
    ys = _moe_experts(block_e, n_valid, n_used, xs, prm["w_exp_gate"], prm["b_exp_gate"],
                      prm["w_exp_up"], prm["b_exp_up"], prm["w_exp_down"], prm["b_exp_down"])
    yg = _sc_gather(ys, slot_idx).reshape(TOP_K, n, HALF)
    y = _combine_ln2(x1, yg, rt, prm["ln2_g"], prm["ln2_b"])
    return y.reshape(b, s, d)


def kernel(x_prompt, x_sample, w_in, b_branch_gate, lambda_q1, lambda_k1, lambda_q2, lambda_k2, subln_g,
           conv_w, w_attn_br, w_conv_br, w_o, ln1_g, ln1_b, w_router, b_router, w_exp_gate, b_exp_gate,
           w_exp_up, b_exp_up, w_exp_down, b_exp_down, ln2_g, ln2_b):
    l = 0
    lam_init = 0.8 - 0.6 * math.exp(-0.3 * l)
    lam = (jnp.exp(jnp.sum(lambda_q1[l].astype(F32) * lambda_k1[l].astype(F32)))
           - jnp.exp(jnp.sum(lambda_q2[l].astype(F32) * lambda_k2[l].astype(F32))) + lam_init)
    prm = {
        "lam_init": lam_init,
        "lam": lam.reshape(1).astype(F32),
        "w_in": w_in[l].astype(BF16),
        "w_q_t": w_in[l][:, Q_STEP * COL_BLOCK:(Q_STEP + 1) * COL_BLOCK].T.astype(BF16),
        "w_v_t": w_in[l][:, V_STEP * COL_BLOCK:(V_STEP + 1) * COL_BLOCK].T.astype(BF16),
        "b_gate": b_branch_gate[l].reshape(1, 2 * D_MODEL),
        "subln_g": subln_g[l].reshape(1, V_DIM),
        "conv_w": jnp.pad(conv_w[l], ((0, SUBLANES - conv_w.shape[1]), (0, 0))),
        "w_attn_br": w_attn_br[l].astype(BF16),
        "w_conv_br": w_conv_br[l].astype(BF16),
        "w_o": w_o[l].astype(BF16),
        "ln1_g": ln1_g[l].reshape(1, D_MODEL),
        "ln1_b": ln1_b[l].reshape(1, D_MODEL),
        "w_router": jnp.pad(w_router[l], ((0, 0), (0, ROUTER_COLS - N_EXPERTS))).astype(BF16),
        "b_router": jnp.pad(b_router[l], (0, ROUTER_COLS - N_EXPERTS)).reshape(1, ROUTER_COLS),
        "w_exp_gate": w_exp_gate[l],
        "b_exp_gate": b_exp_gate[l].reshape(N_EXPERTS, 1, D_MODEL),
        "w_exp_up": w_exp_up[l],
        "b_exp_up": b_exp_up[l].reshape(N_EXPERTS, 1, D_MODEL),
        "w_exp_down": w_exp_down[l],
        "b_exp_down": b_exp_down[l].reshape(N_EXPERTS, 1, D_MODEL),
        "ln2_g": ln2_g[l].reshape(1, D_MODEL),
        "ln2_b": ln2_b[l].reshape(1, D_MODEL),
    }
    return _trunk(x_prompt, prm), _trunk(x_sample, prm)
```

```python
import functools
import math

import jax
import jax.numpy as jnp
from jax import lax
from jax.experimental import pallas as pl
from jax.experimental.pallas import tpu as pltpu
from jax.experimental.pallas import tpu_sc as plsc

F32 = jnp.float32
BF16 = jnp.bfloat16

D_MODEL = 1024
N_HEADS = 8
QK_DIM = 64
V_DIM = 128
HEAD_COLS = 2 * QK_DIM
ROT_DIM = QK_DIM // 4
ROT_HALF = ROT_DIM // 2
ROPE_THETA = 500000.0
SUBLN_EPS = 1e-5
IN_COLS = 8192
COL_BLOCK = 1024
Q_STEP, K_STEP, V_STEP = 0, 1, 2
GATE_STEP = 6
H_COLS = IN_COLS - 2 * COL_BLOCK
CB_BLK, CC_BLK, CX_BLK, GA_BLK, GC_BLK = 1, 2, 3, 4, 5
K_HEAD0 = 0
N_EXPERTS = 32
TOP_K = 4
SWIGLU_LIMIT = 7.0
SWIGLU_ALPHA = 1.702
LN_EPS = 1e-5
DEPTH = 1
DEEPNORM_ALPHA = (2 * DEPTH) ** 0.25
LANES = 128
SUBLANES = 8
ROUTER_COLS = LANES
RT_EXPERT, RT_RANK, RT_GATE = 0, 4, 8
RT_ROWS = 16
SC_WINDOW = 32
ONES_ROWS = 16
Q_SCALE = math.log2(math.e) * QK_DIM ** -0.5

VMEM_LIMIT = 56 * 1024 * 1024

PROJ_TM = 1024
ATTN_TQ = 1024
ATTN_TK = 512
MIX_TM = 512
MIX_SUB = 256
MOE_TM = 512
LN2_TM = 1024


def _rotate_rows(t1, t2, cos, sin):
    return t1 * cos - t2 * sin, t2 * cos + t1 * sin


def _inproj_kernel(x_ref, w_ref, wqt_ref, wvt_ref, bg_ref, cos_ref, sa_ref, sb_ref, cosr_ref, sinr_ref,
                   o_ref, qt_ref, vt_ref, xb_s):
    j = pl.program_id(1)

    @pl.when(j == 0)
    def _():
        xb_s[...] = x_ref[...].astype(BF16)

    xb = xb_s[...]
    nt = (((1,), (1,)), ((), ()))

    @pl.when(j == Q_STEP)
    def _():
        acc_t = lax.dot_general(wqt_ref[...], xb, nt, preferred_element_type=F32) * Q_SCALE
        cos, sin = cosr_ref[...], sinr_ref[...]
        for hh in range(N_HEADS):
            parts = []
            for mp in range(2):
                o = hh * HEAD_COLS + mp * QK_DIM
                parts += _rotate_rows(acc_t[o:o + ROT_HALF], acc_t[o + ROT_HALF:o + ROT_DIM], cos, sin)
                parts.append(acc_t[o + ROT_DIM:o + QK_DIM])
            qt_ref[hh, 0] = jnp.concatenate(parts, axis=0).astype(BF16)

    @pl.when(j == V_STEP)
    def _():
        acc_t = lax.dot_general(wvt_ref[...], xb, nt, preferred_element_type=F32)
        n_heads, n_chunks, _, tk = vt_ref.shape
        for hh in range(n_heads):
            for cc in range(n_chunks):
                vt_ref[hh, cc, :V_DIM, :] = acc_t[hh * V_DIM:(hh + 1) * V_DIM,
                                                  cc * tk:(cc + 1) * tk].astype(BF16)
                vt_ref[hh, cc, V_DIM:, :] = jnp.ones((ONES_ROWS, tk), BF16)

    @pl.when((j > V_STEP) & (j < GATE_STEP))
    def _():
        o_ref[...] = jnp.dot(xb, w_ref[...], preferred_element_type=F32).astype(o_ref.dtype)

    @pl.when(j >= GATE_STEP)
    def _():
        gates = jnp.dot(xb, w_ref[...], preferred_element_type=F32) + bg_ref[...]
        o_ref[...] = jax.nn.sigmoid(gates).astype(o_ref.dtype)

    @pl.when(j == K_STEP)
    def _():
        acc = jnp.dot(xb, w_ref[...], preferred_element_type=F32)
        cos, sa, sb = cos_ref[...], sa_ref[...], sb_ref[...]
        for c in range(COL_BLOCK // LANES):
            a = acc[:, c * LANES:(c + 1) * LANES]
            r = (a * cos + pltpu.roll(a, LANES - ROT_HALF, 1) * sa + pltpu.roll(a, ROT_HALF, 1) * sb)
            o_ref[:, c * LANES:(c + 1) * LANES] = r.astype(o_ref.dtype)


def _inproj(x2d, w_in_b, wqt_b, wvt_b, b_gate, tables, batch, seq):
    n = x2d.shape[0]
    tm, tk = PROJ_TM, ATTN_TK
    assert tm == ATTN_TQ
    pos_blocks = seq // tm
    cos_t, sa_t, sb_t, cos_r, sin_r = tables
    tab_spec = pl.BlockSpec((tm, LANES), lambda i, j: (i % pos_blocks, 0))
    rtab_spec = pl.BlockSpec((ROT_HALF, tm), lambda i, j: (0, i % pos_blocks))
    wt_spec = pl.BlockSpec((COL_BLOCK, D_MODEL), lambda i, j: (0, 0))
    return pl.pallas_call(
        _inproj_kernel,
        out_shape=(jax.ShapeDtypeStruct((n, H_COLS), BF16),
                   jax.ShapeDtypeStruct((batch, N_HEADS, seq // tm, HEAD_COLS, tm), BF16),
                   jax.ShapeDtypeStruct((batch, N_HEADS, seq // tk, V_DIM + ONES_ROWS, tk), BF16)),
        grid=(n // tm, IN_COLS // COL_BLOCK),
        in_specs=[
            pl.BlockSpec((tm, D_MODEL), lambda i, j: (i, 0)),
            pl.BlockSpec((D_MODEL, COL_BLOCK), lambda i, j: (0, jnp.where(j <= V_STEP, K_STEP, j))),
            wt_spec, wt_spec,
            pl.BlockSpec((1, COL_BLOCK), lambda i, j: (0, jnp.maximum(j - GATE_STEP, 0))),
            tab_spec, tab_spec, tab_spec, rtab_spec, rtab_spec,
        ],
        out_specs=(
            pl.BlockSpec((tm, COL_BLOCK), lambda i, j: (i, jnp.maximum(j - V_STEP, 0))),
            pl.BlockSpec((None, N_HEADS, 1, HEAD_COLS, tm),
                         lambda i, j: (i // pos_blocks, 0, i % pos_blocks, 0, 0)),
            pl.BlockSpec((None, N_HEADS, tm // tk, V_DIM + ONES_ROWS, tk),
                         lambda i, j: (i // pos_blocks, 0, i % pos_blocks, 0, 0)),
        ),
        scratch_shapes=[pltpu.VMEM((tm, D_MODEL), BF16)],
        compiler_params=pltpu.CompilerParams(
            dimension_semantics=("parallel", "arbitrary"), vmem_limit_bytes=VMEM_LIMIT),
        name="inproj",
    )(x2d, w_in_b, wqt_b, wvt_b, b_gate, cos_t, sa_t, sb_t, cos_r, sin_r)


def _attn_kernel(lam_ref, q_ref, k_ref, vt_ref, g_ref, o_ref, qt_s, acc_s, sa0, sa1, sb0, sb1, *, out_scale):
    n_q, _, tq = q_ref.shape
    n_chunks, _, tk = vt_ref.shape
    sa_s, sb_s = (sa0, sa1), (sb0, sb1)

    def load_q(qi):
        zero_half = jnp.zeros((QK_DIM, tq), BF16)
        qt_s[0, :QK_DIM, :] = q_ref[qi, :QK_DIM, :]
        qt_s[0, QK_DIM:, :] = zero_half
        qt_s[1, :QK_DIM, :] = zero_half
        qt_s[1, QK_DIM:, :] = q_ref[qi, QK_DIM:, :]

    def scores(c, s_refs, mp):
        k = k_ref[pl.ds(pl.multiple_of(c * tk, tk), tk), :]
        s = jnp.dot(k, qt_s[mp], preferred_element_type=F32)
        s_refs[mp][...] = s
        return jnp.max(s, axis=0, keepdims=True)

    def update(c, s_refs, mp, m_old, c_max):
        m_new = jnp.maximum(m_old, c_max)
        alpha = jnp.exp2(m_old - m_new)
        p = jnp.exp2(s_refs[mp][...] - m_new).astype(BF16)
        acc_s[mp] = alpha * acc_s[mp] + jnp.dot(vt_ref[c], p, preferred_element_type=F32)
        return m_new

    def step(c, cur, nxt, ms, cur_max, prefetch):
        new_ms, nxt_max = [], []
        for mp in range(2):
            if prefetch:
                nxt_max.append(scores(c + 1, nxt, mp))
            new_ms.append(update(c, cur, mp, ms[mp], cur_max[mp]))
        return tuple(new_ms), tuple(nxt_max)

    def pair(i, carry):
        ms, a_max = carry
        ms, b_max = step(2 * i, sa_s, sb_s, ms, a_max, True)
        return step(2 * i + 1, sb_s, sa_s, ms, b_max, True)

    def finalize(qi):
        lam = lam_ref[0]
        o1 = acc_s[0, :V_DIM, :] / acc_s[0, V_DIM:V_DIM + 1, :]
        o2 = acc_s[1, :V_DIM, :] / acc_s[1, V_DIM:V_DIM + 1, :]
        ot = o1 - lam * o2
        ot = ot * lax.rsqrt(jnp.mean(ot * ot, axis=0, keepdims=True) + SUBLN_EPS)
        rows = pl.ds(pl.multiple_of(qi * tq, tq), tq)
        o_ref[rows, :] = (ot.T * (g_ref[...] * out_scale)).astype(o_ref.dtype)

    def q_block(qi, a_max, last):
        acc_s[...] = jnp.zeros(acc_s.shape, F32)
        m0 = jnp.full((1, tq), -jnp.inf, F32)
        ms, a_max = lax.fori_loop(0, n_chunks // 2 - 1, pair, ((m0, m0), a_max))
        ms, b_max = step(n_chunks - 2, sa_s, sb_s, ms, a_max, True)
        step(n_chunks - 1, sb_s, sa_s, ms, b_max, False)
        if not last:
            load_q(qi + 1)
            a_max = tuple(scores(0, sa_s, mp) for mp in range(2))
        finalize(qi)
        return a_max

    load_q(0)
    a_max = tuple(scores(0, sa_s, mp) for mp in range(2))
    a_max = lax.fori_loop(0, n_q - 1, lambda qi, am: q_block(qi, am, False), a_max)
    q_block(n_q - 1, a_max, True)


def _attention(h3, qt, vt, lam, subln_g, lam_init):
    b, s, _ = h3.shape
    tq, tk = ATTN_TQ, ATTN_TK
    n_chunks = s // tk
    assert n_chunks % 2 == 0 and n_chunks >= 2
    vt_rows = V_DIM + ONES_ROWS
    kern = functools.partial(_attn_kernel, out_scale=1.0 - lam_init)
    return pl.pallas_call(
        kern,
        out_shape=jax.ShapeDtypeStruct((b, s, N_HEADS * V_DIM), BF16),
        grid=(b, N_HEADS),
        in_specs=[
            pl.BlockSpec(memory_space=pltpu.SMEM),
            pl.BlockSpec((None, None, s // tq, HEAD_COLS, tq), lambda bi, hi: (bi, hi, 0, 0, 0)),
            pl.BlockSpec((None, s, HEAD_COLS), lambda bi, hi: (bi, 0, K_HEAD0 + hi)),
            pl.BlockSpec((None, None, n_chunks, vt_rows, tk), lambda bi, hi: (bi, hi, 0, 0, 0)),
            pl.BlockSpec((1, V_DIM), lambda bi, hi: (0, 0)),
        ],
        out_specs=pl.BlockSpec((None, s, V_DIM), lambda bi, hi: (bi, 0, hi)),
        scratch_shapes=[
            pltpu.VMEM((2, HEAD_COLS, tq), BF16),
            pltpu.VMEM((2, vt_rows, tq), F32),
            pltpu.VMEM((tk, tq), F32), pltpu.VMEM((tk, tq), F32),
            pltpu.VMEM((tk, tq), F32), pltpu.VMEM((tk, tq), F32),
        ],
        compiler_params=pltpu.CompilerParams(
            dimension_semantics=("parallel", "arbitrary"),
            vmem_limit_bytes=VMEM_LIMIT),
        name="diff_attention",
    )(lam, qt, h3, vt, subln_g)


HALF = D_MODEL // 2


def _pack_row(x):
    return pltpu.pack_elementwise([x[:, :HALF], x[:, HALF:]], packed_dtype=BF16)


def _unpack_row(w):
    lo = pltpu.unpack_elementwise(w, index=0, packed_dtype=BF16, unpacked_dtype=F32)
    hi = pltpu.unpack_elementwise(w, index=1, packed_dtype=BF16, unpacked_dtype=F32)
    return jnp.concatenate([lo, hi], axis=1)


def _layer_norm(y, g, b):
    mu = jnp.mean(y, axis=-1, keepdims=True)
    d = y - mu
    var = jnp.mean(d * d, axis=-1, keepdims=True)
    return d * lax.rsqrt(var + LN_EPS) * g + b


def _mix_kernel(a_ref, cb_ref, cc_ref, cx_ref, ga_ref, gc_ref, ccp_ref, cxp_ref, ccn_ref, cxn_ref,
                x_ref, cw_ref, wa_ref, wc_ref, wo_ref, g1_ref, b1_ref, wrt_ref, brb_ref, triu_ref,
                x1_ref, x1p_ref, rt_ref, rtt_ref, cnt_ref, *, tiles_per_seq):
    i = pl.program_id(0)

    @pl.when(i == 0)
    def _():
        cnt_ref[...] = jnp.zeros(cnt_ref.shape, F32)

    tm = x_ref.shape[0]
    u = cc_ref[...].astype(F32) * cx_ref[...].astype(F32)
    first = (i % tiles_per_seq) == 0
    last = (i % tiles_per_seq) == tiles_per_seq - 1
    up = ccp_ref[SUBLANES - 1:SUBLANES, :].astype(F32) * cxp_ref[SUBLANES - 1:SUBLANES, :].astype(F32)
    un = ccn_ref[0:1, :].astype(F32) * cxn_ref[0:1, :].astype(F32)
    up = jnp.where(first, 0.0, up)
    un = jnp.where(last, 0.0, un)
    row = lax.broadcasted_iota(jnp.int32, u.shape, 0)
    u_prev = jnp.where(row == 0, up, pltpu.roll(u, 1, 0))
    u_next = jnp.where(row == tm - 1, un, pltpu.roll(u, tm - 1, 0))
    conv = cw_ref[0:1, :] * u_prev + cw_ref[1:2, :] * u + cw_ref[2:3, :] * u_next
    c = (cb_ref[...].astype(F32) * conv).astype(BF16)

    sub = triu_ref.shape[0]
    counts = cnt_ref[:, 0:1]
    for s in range(tm // sub):
        rows = slice(s * sub, (s + 1) * sub)
        g_a = ga_ref[rows, :].astype(F32)
        g_c = gc_ref[rows, :].astype(F32)
        merged = (g_a * jnp.dot(a_ref[rows, :], wa_ref[...], preferred_element_type=F32)
                  + g_c * jnp.dot(c[rows, :], wc_ref[...], preferred_element_type=F32))
        m = jnp.dot(merged.astype(BF16), wo_ref[...], preferred_element_type=F32)
        x1 = _layer_norm(DEEPNORM_ALPHA * x_ref[rows, :] + m, g1_ref[...], b1_ref[...])
        x1_ref[rows, :] = x1
        x1p_ref[rows, :] = _pack_row(x1)
        nt = (((1,), (1,)), ((), ()))
        logits_t = lax.dot_general(wrt_ref[...], x1.astype(BF16), nt, preferred_element_type=F32)
        work = logits_t[:N_EXPERTS, :] + brb_ref[...]

        row_f = lax.broadcasted_iota(jnp.int32, work.shape, 0).astype(F32)
        hots, vals, idxs = [], [], []
        for _ in range(TOP_K):
            mx = jnp.max(work, axis=0, keepdims=True)
            ix = jnp.min(jnp.where(work == mx, row_f, float(N_EXPERTS)), axis=0, keepdims=True)
            hot = row_f == ix
            work = jnp.where(hot, -jnp.inf, work)
            hots.append(hot)
            vals.append(mx)
            idxs.append(ix)
        exps = [jnp.exp(v - vals[0]) for v in vals]
        denom = exps[0] + exps[1] + exps[2] + exps[3]
        picked = jnp.zeros(work.shape, F32)
        for hot in hots:
            picked = picked + hot.astype(F32)
        before = counts + jnp.dot(picked.astype(BF16), triu_ref[...], preferred_element_type=F32)
        ranks = [jnp.sum(jnp.where(hot, before, 0.0), axis=0, keepdims=True) for hot in hots]
        fields = idxs + ranks + [e / denom for e in exps]
        rec_row = lax.broadcasted_iota(jnp.int32, (RT_ROWS, sub), 0)
        rec = jnp.zeros((RT_ROWS, sub), F32)
        for r, val in enumerate(fields):
            rec = jnp.where(rec_row == r, val, rec)
        rtt_ref[:, rows] = rec
        rt_ref[rows, :] = jnp.concatenate([rec, jnp.zeros((ROUTER_COLS - RT_ROWS, sub), F32)], axis=0).T
        counts = counts + jnp.sum(picked, axis=1, keepdims=True)
    cnt_ref[...] = jnp.broadcast_to(counts, cnt_ref.shape)


def _mix(a2d, h2d, x2d, seq, cw, wa, wc, wo, g1, b1, wr_t, br_b):
    n = x2d.shape[0]
    tm = MIX_TM
    tiles_per_seq = seq // tm
    halo_blocks = tm // SUBLANES
    n_halo = n // SUBLANES

    def col(blk):
        return pl.BlockSpec((tm, COL_BLOCK), lambda i: (i, blk))

    def prev(blk):
        return pl.BlockSpec((SUBLANES, COL_BLOCK), lambda i: (jnp.maximum(i * halo_blocks - 1, 0), blk))

    def nxt(blk):
        return pl.BlockSpec((SUBLANES, COL_BLOCK),
                            lambda i: (jnp.minimum((i + 1) * halo_blocks, n_halo - 1), blk))

    def full(shape):
        return pl.BlockSpec(shape, lambda i: (0,) * len(shape))

    sub = MIX_SUB
    tri_u = (lax.broadcasted_iota(jnp.int32, (sub, sub), 0)
             < lax.broadcasted_iota(jnp.int32, (sub, sub), 1)).astype(BF16)
    kern = functools.partial(_mix_kernel, tiles_per_seq=tiles_per_seq)
    return pl.pallas_call(
        kern,
        out_shape=(jax.ShapeDtypeStruct((n, D_MODEL), F32),
                   jax.ShapeDtypeStruct((n, HALF), jnp.int32),
                   jax.ShapeDtypeStruct((n, ROUTER_COLS), F32),
                   jax.ShapeDtypeStruct((RT_ROWS, n), F32),
                   jax.ShapeDtypeStruct((N_EXPERTS, LANES), F32)),
        grid=(n // tm,),
        in_specs=[
            pl.BlockSpec((tm, D_MODEL), lambda i: (i, 0)),
            col(CB_BLK), col(CC_BLK), col(CX_BLK), col(GA_BLK), col(GC_BLK),
            prev(CC_BLK), prev(CX_BLK), nxt(CC_BLK), nxt(CX_BLK),
            pl.BlockSpec((tm, D_MODEL), lambda i: (i, 0)),
            full((SUBLANES, D_MODEL)),
            full((D_MODEL, D_MODEL)), full((D_MODEL, D_MODEL)), full((D_MODEL, D_MODEL)),
            full((1, D_MODEL)), full((1, D_MODEL)),
            full((ROUTER_COLS, D_MODEL)), full((N_EXPERTS, sub)), full((sub, sub)),
        ],
        out_specs=(pl.BlockSpec((tm, D_MODEL), lambda i: (i, 0)),
                   pl.BlockSpec((tm, HALF), lambda i: (i, 0)),
                   pl.BlockSpec((tm, ROUTER_COLS), lambda i: (i, 0)),
                   pl.BlockSpec((RT_ROWS, tm), lambda i: (0, i)),
                   pl.BlockSpec((N_EXPERTS, LANES), lambda i: (0, 0))),
        compiler_params=pltpu.CompilerParams(
            dimension_semantics=("arbitrary",), vmem_limit_bytes=VMEM_LIMIT),
        name="mix_ln1_router",
    )(a2d, h2d, h2d, h2d, h2d, h2d, h2d, h2d, h2d, h2d, x2d, cw, wa, wc, wo, g1, b1, wr_t, br_b, tri_u)


def _moe_kernel(be_ref, nv_ref, nu_ref, xs_ref, wg_ref, bg_ref, wu_ref, bu_ref, wd_ref, bd_ref, o_ref,
                wg_s, wu_s, wd_s):
    i = pl.program_id(0)
    used = i < nu_ref[0]

    @pl.when(used & ((i == 0) | (be_ref[i] != be_ref[jnp.maximum(i - 1, 0)])))
    def _():
        wg_s[...] = wg_ref[...].astype(BF16)
        wu_s[...] = wu_ref[...].astype(BF16)
        wd_s[...] = wd_ref[...].astype(BF16)

    @pl.when(used)
    def _():
        row = lax.broadcasted_iota(jnp.int32, xs_ref.shape, 0)
        x = _unpack_row(jnp.where(row < nv_ref[i], xs_ref[...], 0)).astype(BF16)
        hg = jnp.minimum(jnp.dot(x, wg_s[...], preferred_element_type=F32) + bg_ref[...], SWIGLU_LIMIT)
        hu = jnp.clip(jnp.dot(x, wu_s[...], preferred_element_type=F32) + bu_ref[...],
                      -SWIGLU_LIMIT, SWIGLU_LIMIT)
        act = hg * jax.nn.sigmoid(SWIGLU_ALPHA * hg) * (hu + 1.0)
        y = jnp.dot(act.astype(BF16), wd_s[...], preferred_element_type=F32) + bd_ref[...]
        o_ref[...] = _pack_row(y)

    @pl.when(jnp.logical_not(used))
    def _():
        o_ref[...] = jnp.zeros(o_ref.shape, o_ref.dtype)


def _moe_experts(block_e, n_valid, n_used, xs, wg, bg, wu, bu, wd, bd):
    p = xs.shape[0]
    tm = MOE_TM
    nb = p // tm

    def row_map(i, be, nv, nu):
        return (jnp.minimum(i, nu[0] - 1), 0)

    def w_map(i, be, nv, nu):
        return (be[jnp.minimum(i, nu[0] - 1)], 0, 0)

    w_spec = pl.BlockSpec((None, D_MODEL, D_MODEL), w_map)
    b_spec = pl.BlockSpec((None, 1, D_MODEL), w_map)
    return pl.pallas_call(
        _moe_kernel,
        out_shape=jax.ShapeDtypeStruct((p, HALF), jnp.int32),
        grid_spec=pltpu.PrefetchScalarGridSpec(
            num_scalar_prefetch=3,
            grid=(nb,),
            in_specs=[
                pl.BlockSpec((tm, HALF), row_map),
                w_spec, b_spec, w_spec, b_spec, w_spec, b_spec,
            ],
            out_specs=pl.BlockSpec((tm, HALF), lambda i, be, nv, nu: (i, 0)),
            scratch_shapes=[pltpu.VMEM((D_MODEL, D_MODEL), BF16)] * 3,
        ),
        compiler_params=pltpu.CompilerParams(
            dimension_semantics=("arbitrary",), vmem_limit_bytes=VMEM_LIMIT),
        name="moe_experts",
    )(block_e, n_valid, n_used, xs, wg, bg, wu, bu, wd, bd)


def _sc_mesh():
    return plsc.VectorSubcoreMesh(core_axis_name="core", subcore_axis_name="subcore")


def _sc_index_rows(dest_t):
    k, n = dest_t.shape
    w = SC_WINDOW
    return jnp.pad(dest_t.reshape(k * n // w, w), ((0, 0), (0, LANES - w)))


def _sc_dispatch(x1, slot_idx, p):
    n, d = x1.shape
    w = SC_WINDOW
    windows = n // w

    @pl.kernel(out_type=jax.ShapeDtypeStruct((p, d), x1.dtype), mesh=_sc_mesh(), scratch_types=[],
               name="sc_dispatch")
    def run(x_hbm, d_hbm, o_hbm):
        def body(x_vmem, *d_vmems):
            for d_vmem in d_vmems:
                pltpu.sync_copy(x_vmem, o_hbm.at[d_vmem.at[0, pl.ds(0, w)]])

        pltpu.emit_pipeline(
            body,
            grid=(windows,),
            in_specs=[pl.BlockSpec((w, d), lambda i: (i, 0))]
            + [pl.BlockSpec((1, LANES), functools.partial(lambda k, i: (k * windows + i, 0), k))
               for k in range(TOP_K)],
            out_specs=[],
            core_axis_name=("core", "subcore"),
            dimension_semantics=(pltpu.PARALLEL,),
        )(x_hbm, *([d_hbm] * TOP_K))

    return run(x1, slot_idx)


def _sc_gather(ys, idx):
    w = SC_WINDOW
    r = idx.shape[0] * w
    d = ys.shape[1]

    @pl.kernel(out_type=jax.ShapeDtypeStruct((r, d), ys.dtype), mesh=_sc_mesh(), scratch_types=[],
               name="sc_gather")
    def run(y_hbm, i_hbm, o_hbm):
        def body(i_vmem, o_vmem):
            pltpu.sync_copy(y_hbm.at[i_vmem.at[0, pl.ds(0, w)]], o_vmem)

        pltpu.emit_pipeline(
            body,
            grid=(r // w,),
            in_specs=[pl.BlockSpec((1, LANES), lambda i: (i, 0))],
            out_specs=[pl.BlockSpec((w, d), lambda i: (i, 0))],
            core_axis_name=("core", "subcore"),
            dimension_semantics=(pltpu.PARALLEL,),
        )(i_hbm, o_hbm)

    return run(ys, idx)


def _ln2_kernel(x1_ref, yg_ref, rt_ref, g_ref, b_ref, o_ref):
    f = jnp.zeros(x1_ref.shape, F32)
    for k in range(TOP_K):
        f = f + _unpack_row(yg_ref[k]) * rt_ref[:, RT_GATE + k:RT_GATE + k + 1]
    o_ref[...] = _layer_norm(DEEPNORM_ALPHA * x1_ref[...] + f, g_ref[...], b_ref[...])


def _combine_ln2(x1, yg, rt, g2, b2):
    n = x1.shape[0]
    tm = LN2_TM
    return pl.pallas_call(
        _ln2_kernel,
        out_shape=jax.ShapeDtypeStruct((n, D_MODEL), F32),
        grid=(n // tm,),
        in_specs=[
            pl.BlockSpec((tm, D_MODEL), lambda i: (i, 0)),
            pl.BlockSpec((TOP_K, tm, HALF), lambda i: (0, i, 0)),
            pl.BlockSpec((tm, ROUTER_COLS), lambda i: (i, 0)),
            pl.BlockSpec((1, D_MODEL), lambda i: (0, 0)),
            pl.BlockSpec((1, D_MODEL), lambda i: (0, 0)),
        ],
        out_specs=pl.BlockSpec((tm, D_MODEL), lambda i: (i, 0)),
        compiler_params=pltpu.CompilerParams(
            dimension_semantics=("parallel",), vmem_limit_bytes=VMEM_LIMIT),
        name="combine_ln2",
    )(x1, yg, rt, g2, b2)


def _rotary_tables(seq):
    inv = ROPE_THETA ** (-jnp.arange(0, ROT_DIM, 2, dtype=F32) / ROT_DIM)
    ang = jnp.arange(seq, dtype=F32)[:, None] * inv[None, :]
    cos, sin = jnp.cos(ang), jnp.sin(ang)
    ones = jnp.ones((seq, QK_DIM - ROT_DIM), F32)
    zeros = jnp.zeros((seq, QK_DIM - ROT_DIM), F32)
    zh = jnp.zeros((seq, ROT_HALF), F32)
    cos64 = jnp.concatenate([cos, cos, ones], axis=1)
    sa64 = jnp.concatenate([-sin, zh, zeros], axis=1)
    sb64 = jnp.concatenate([zh, sin, zeros], axis=1)
    rep = LANES // QK_DIM
    return (jnp.tile(cos64, (1, rep)), jnp.tile(sa64, (1, rep)), jnp.tile(sb64, (1, rep)), cos.T, sin.T)


def _slot_layout(rtt, counts, tm):
    n = rtt.shape[1]
    p = n * TOP_K + N_EXPERTS * tm
    nb = p // tm
    expert = rtt[RT_EXPERT:RT_EXPERT + TOP_K].astype(jnp.int32)
    rank = rtt[RT_RANK:RT_RANK + TOP_K].astype(jnp.int32)
    padded = ((counts + tm - 1) // tm) * tm
    pad_ends = jnp.cumsum(padded)
    pad_starts = pad_ends - padded
    ids = jnp.arange(N_EXPERTS, dtype=jnp.int32)[:, None, None]
    dest_t = jnp.sum(jnp.where(expert[None] == ids, pad_starts[:, None, None], 0), axis=0) + rank
    block_start = jnp.arange(nb, dtype=jnp.int32) * tm
    block_e = jnp.minimum(jnp.sum(pad_ends[None, :] <= block_start[:, None], axis=1),
                          N_EXPERTS - 1).astype(jnp.int32)
    own = block_e[:, None] == jnp.arange(N_EXPERTS, dtype=jnp.int32)[None, :]
    valid_end = jnp.sum(jnp.where(own, (pad_starts + counts)[None, :], 0), axis=1)
    n_valid = jnp.clip(valid_end - block_start, 0, tm).astype(jnp.int32)
    n_used = (pad_ends[-1] // tm).astype(jnp.int32).reshape(1)
    return dest_t.astype(jnp.int32), block_e, n_valid, n_used, p


def _trunk(x, prm):
    b, s, d = x.shape
    n = b * s
    x2d = x.reshape(n, d)
    h, qt, vt = _inproj(x2d, prm["w_in"], prm["w_q_t"], prm["w_v_t"], prm["b_gate"], _rotary_tables(s), b, s)
    a = _attention(h.reshape(b, s, H_COLS), qt, vt, prm["lam"], prm["subln_g"], prm["lam_init"])
    x1, x1p, rt, rtt, cnt = _mix(a.reshape(n, d), h, x2d, s, prm["conv_w"], prm["w_attn_br"],
                                 prm["w_conv_br"], prm["w_o"], prm["ln1_g"], prm["ln1_b"],
                                 prm["w_router_t"], prm["b_router_b"])
    counts = cnt[:, 0].astype(jnp.int32)
    dest_t, block_e, n_valid, n_used, p = _slot_layout(rtt, counts, MOE_TM)
    slot_idx = _sc_index_rows(dest_t)
    xs = _sc_dispatch(x1p, slot_idx, p)
    ys = _moe_experts(block_e, n_valid, n_used, xs, prm["w_exp_gate"], prm["b_exp_gate"],
                      prm["w_exp_up"], prm["b_exp_up"], prm["w_exp_down"], prm["b_exp_down"])
    yg = _sc_gather(ys, slot_idx).reshape(TOP_K, n, HALF)
    y = _combine_ln2(x1, yg, rt, prm["ln2_g"], prm["ln2_b"])
    return y.reshape(b, s, d)


def kernel(x_prompt, x_sample, w_in, b_branch_gate, lambda_q1, lambda_k1, lambda_q2, lambda_k2, subln_g,
           conv_w, w_attn_br, w_conv_br, w_o, ln1_g, ln1_b, w_router, b_router, w_exp_gate, b_exp_gate,
           w_exp_up, b_exp_up, w_exp_down, b_exp_down, ln2_g, ln2_b):
    l = 0
    lam_init = 0.8 - 0.6 * math.exp(-0.3 * l)
    lam = (jnp.exp(jnp.sum(lambda_q1[l].astype(F32) * lambda_k1[l].astype(F32)))
           - jnp.exp(jnp.sum(lambda_q2[l].astype(F32) * lambda_k2[l].astype(F32))) + lam_init)
    prm = {
        "lam_init": lam_init,
        "lam": lam.reshape(1).astype(F32),
        "w_in": w_in[l].astype(BF16),
        "w_q_t": w_in[l][:, Q_STEP * COL_BLOCK:(Q_STEP + 1) * COL_BLOCK].T.astype(BF16),
        "w_v_t": w_in[l][:, V_STEP * COL_BLOCK:(V_STEP + 1) * COL_BLOCK].T.astype(BF16),
        "b_gate": b_branch_gate[l].reshape(1, 2 * D_MODEL),
        "subln_g": subln_g[l].reshape(1, V_DIM),
        "conv_w": jnp.pad(conv_w[l], ((0, SUBLANES - conv_w.shape[1]), (0, 0))),
        "w_attn_br": w_attn_br[l].astype(BF16),
        "w_conv_br": w_conv_br[l].astype(BF16),
        "w_o": w_o[l].astype(BF16),
        "ln1_g": ln1_g[l].reshape(1, D_MODEL),
        "ln1_b": ln1_b[l].reshape(1, D_MODEL),
        "w_router_t": jnp.pad(w_router[l].T, ((0, ROUTER_COLS - N_EXPERTS), (0, 0))).astype(BF16),
        "b_router_b": jnp.broadcast_to(b_router[l].astype(F32)[:, None], (N_EXPERTS, MIX_SUB)),
        "w_exp_gate": w_exp_gate[l],
        "b_exp_gate": b_exp_gate[l].reshape(N_EXPERTS, 1, D_MODEL),
        "w_exp_up": w_exp_up[l],
        "b_exp_up": b_exp_up[l].reshape(N_EXPERTS, 1, D_MODEL),
        "w_exp_down": w_exp_down[l],
        "b_exp_down": b_exp_down[l].reshape(N_EXPERTS, 1, D_MODEL),
        "ln2_g": ln2_g[l].reshape(1, D_MODEL),
        "ln2_b": ln2_b[l].reshape(1, D_MODEL),
    }
    return _trunk(x_prompt, prm), _trunk(x_sample, prm)
```

```python
import functools
import math

import jax
import jax.numpy as jnp
from jax import lax
from jax.experimental import pallas as pl
from jax.experimental.pallas import tpu as pltpu
from jax.experimental.pallas import tpu_sc as plsc

F32 = jnp.float32
BF16 = jnp.bfloat16

D_MODEL = 1024
N_HEADS = 8
QK_DIM = 64
V_DIM = 128
HEAD_COLS = 2 * QK_DIM
ROT_DIM = QK_DIM // 4
ROT_HALF = ROT_DIM // 2
ROPE_THETA = 500000.0
SUBLN_EPS = 1e-5
IN_COLS = 8192
COL_BLOCK = 1024
Q_STEP, K_STEP, V_STEP = 0, 1, 2
GATE_STEP = 6
H_COLS = IN_COLS - 2 * COL_BLOCK
CB_BLK, CC_BLK, CX_BLK, GA_BLK, GC_BLK = 1, 2, 3, 4, 5
K_HEAD0 = 0
N_EXPERTS = 32
TOP_K = 4
SWIGLU_LIMIT = 7.0
SWIGLU_ALPHA = 1.702
LN_EPS = 1e-5
DEPTH = 1
DEEPNORM_ALPHA = (2 * DEPTH) ** 0.25
LANES = 128
SUBLANES = 8
ROUTER_COLS = LANES
RT_EXPERT, RT_RANK, RT_GATE = 0, 4, 8
RT_ROWS = 16
SC_WINDOW = 32
ONES_ROWS = 16
Q_SCALE = math.log2(math.e) * QK_DIM ** -0.5

VMEM_LIMIT = 56 * 1024 * 1024

PROJ_TM = 1024
ATTN_TQ = 1024
ATTN_TK = 512
MIX_TM = 512
MIX_SUB = 256
MOE_TM = 512
LN2_TM = 1024


def _rotate_rows(t1, t2, cos, sin):
    return t1 * cos - t2 * sin, t2 * cos + t1 * sin


def _inproj_kernel(x_ref, w_ref, wqt_ref, wvt_ref, bg_ref, cos_ref, sa_ref, sb_ref, cosr_ref, sinr_ref,
                   o_ref, qt_ref, vt_ref, xb_s):
    j = pl.program_id(1)

    @pl.when(j == 0)
    def _():
        xb_s[...] = x_ref[...].astype(BF16)

    xb = xb_s[...]
    nt = (((1,), (1,)), ((), ()))

    @pl.when(j == Q_STEP)
    def _():
        acc_t = lax.dot_general(wqt_ref[...], xb, nt, preferred_element_type=F32) * Q_SCALE
        cos, sin = cosr_ref[...], sinr_ref[...]
        for hh in range(N_HEADS):
            parts = []
            for mp in range(2):
                o = hh * HEAD_COLS + mp * QK_DIM
                parts += _rotate_rows(acc_t[o:o + ROT_HALF], acc_t[o + ROT_HALF:o + ROT_DIM], cos, sin)
                parts.append(acc_t[o + ROT_DIM:o + QK_DIM])
            qt_ref[hh, 0] = jnp.concatenate(parts, axis=0).astype(BF16)

    @pl.when(j == V_STEP)
    def _():
        acc_t = lax.dot_general(wvt_ref[...], xb, nt, preferred_element_type=F32)
        n_heads, n_chunks, _, tk = vt_ref.shape
        for hh in range(n_heads):
            for cc in range(n_chunks):
                vt_ref[hh, cc, :V_DIM, :] = acc_t[hh * V_DIM:(hh + 1) * V_DIM,
                                                  cc * tk:(cc + 1) * tk].astype(BF16)
                vt_ref[hh, cc, V_DIM:, :] = jnp.ones((ONES_ROWS, tk), BF16)

    @pl.when((j > V_STEP) & (j < GATE_STEP))
    def _():
        o_ref[...] = jnp.dot(xb, w_ref[...], preferred_element_type=F32).astype(o_ref.dtype)

    @pl.when(j >= GATE_STEP)
    def _():
        gates = jnp.dot(xb, w_ref[...], preferred_element_type=F32) + bg_ref[...]
        o_ref[...] = jax.nn.sigmoid(gates).astype(o_ref.dtype)

    @pl.when(j == K_STEP)
    def _():
        acc = jnp.dot(xb, w_ref[...], preferred_element_type=F32)
        cos, sa, sb = cos_ref[...], sa_ref[...], sb_ref[...]
        for c in range(COL_BLOCK // LANES):
            a = acc[:, c * LANES:(c + 1) * LANES]
            r = (a * cos + pltpu.roll(a, LANES - ROT_HALF, 1) * sa + pltpu.roll(a, ROT_HALF, 1) * sb)
            o_ref[:, c * LANES:(c + 1) * LANES] = r.astype(o_ref.dtype)


def _inproj(x2d, w_in_b, wqt_b, wvt_b, b_gate, tables, batch, seq):
    n = x2d.shape[0]
    tm, tk = PROJ_TM, ATTN_TK
    assert tm == ATTN_TQ
    pos_blocks = seq // tm
    cos_t, sa_t, sb_t, cos_r, sin_r = tables
    tab_spec = pl.BlockSpec((tm, LANES), lambda i, j: (i % pos_blocks, 0))
    rtab_spec = pl.BlockSpec((ROT_HALF, tm), lambda i, j: (0, i % pos_blocks))
    wt_spec = pl.BlockSpec((COL_BLOCK, D_MODEL), lambda i, j: (0, 0))
    return pl.pallas_call(
        _inproj_kernel,
        out_shape=(jax.ShapeDtypeStruct((n, H_COLS), BF16),
                   jax.ShapeDtypeStruct((batch, N_HEADS, seq // tm, HEAD_COLS, tm), BF16),
                   jax.ShapeDtypeStruct((batch, N_HEADS, seq // tk, V_DIM + ONES_ROWS, tk), BF16)),
        grid=(n // tm, IN_COLS // COL_BLOCK),
        in_specs=[
            pl.BlockSpec((tm, D_MODEL), lambda i, j: (i, 0)),
            pl.BlockSpec((D_MODEL, COL_BLOCK), lambda i, j: (0, jnp.where(j <= V_STEP, K_STEP, j))),
            wt_spec, wt_spec,
            pl.BlockSpec((1, COL_BLOCK), lambda i, j: (0, jnp.maximum(j - GATE_STEP, 0))),
            tab_spec, tab_spec, tab_spec, rtab_spec, rtab_spec,
        ],
        out_specs=(
            pl.BlockSpec((tm, COL_BLOCK), lambda i, j: (i, jnp.maximum(j - V_STEP, 0))),
            pl.BlockSpec((None, N_HEADS, 1, HEAD_COLS, tm),
                         lambda i, j: (i // pos_blocks, 0, i % pos_blocks, 0, 0)),
            pl.BlockSpec((None, N_HEADS, tm // tk, V_DIM + ONES_ROWS, tk),
                         lambda i, j: (i // pos_blocks, 0, i % pos_blocks, 0, 0)),
        ),
        scratch_shapes=[pltpu.VMEM((tm, D_MODEL), BF16)],
        compiler_params=pltpu.CompilerParams(
            dimension_semantics=("parallel", "arbitrary"), vmem_limit_bytes=VMEM_LIMIT),
        name="inproj",
    )(x2d, w_in_b, wqt_b, wvt_b, b_gate, cos_t, sa_t, sb_t, cos_r, sin_r)


def _attn_kernel(lam_ref, q_ref, k_ref, vt_ref, g_ref, o_ref, qt_s, acc_s, sa0, sa1, sb0, sb1, *, out_scale):
    n_q, _, tq = q_ref.shape
    n_chunks, _, tk = vt_ref.shape
    sa_s, sb_s = (sa0, sa1), (sb0, sb1)

    def load_q(qi):
        zero_half = jnp.zeros((QK_DIM, tq), BF16)
        qt_s[0, :QK_DIM, :] = q_ref[qi, :QK_DIM, :]
        qt_s[0, QK_DIM:, :] = zero_half
        qt_s[1, :QK_DIM, :] = zero_half
        qt_s[1, QK_DIM:, :] = q_ref[qi, QK_DIM:, :]

    def scores(c, s_refs, mp):
        k = k_ref[pl.ds(pl.multiple_of(c * tk, tk), tk), :]
        s = jnp.dot(k, qt_s[mp], preferred_element_type=F32)
        s_refs[mp][...] = s
        return jnp.max(s, axis=0, keepdims=True)

    def update(c, s_refs, mp, m_old, c_max):
        m_new = jnp.maximum(m_old, c_max)
        alpha = jnp.exp2(m_old - m_new)
        p = jnp.exp2(s_refs[mp][...] - m_new).astype(BF16)
        acc_s[mp] = alpha * acc_s[mp] + jnp.dot(vt_ref[c], p, preferred_element_type=F32)
        return m_new

    def step(c, cur, nxt, ms, cur_max, prefetch):
        new_ms, nxt_max = [], []
        for mp in range(2):
            if prefetch:
                nxt_max.append(scores(c + 1, nxt, mp))
            new_ms.append(update(c, cur, mp, ms[mp], cur_max[mp]))
        return tuple(new_ms), tuple(nxt_max)

    def pair(i, carry):
        ms, a_max = carry
        ms, b_max = step(2 * i, sa_s, sb_s, ms, a_max, True)
        return step(2 * i + 1, sb_s, sa_s, ms, b_max, True)

    def finalize(qi):
        lam = lam_ref[0]
        o1 = acc_s[0, :V_DIM, :] / acc_s[0, V_DIM:V_DIM + 1, :]
        o2 = acc_s[1, :V_DIM, :] / acc_s[1, V_DIM:V_DIM + 1, :]
        ot = o1 - lam * o2
        ot = ot * lax.rsqrt(jnp.mean(ot * ot, axis=0, keepdims=True) + SUBLN_EPS)
        rows = pl.ds(pl.multiple_of(qi * tq, tq), tq)
        o_ref[rows, :] = (ot.T * (g_ref[...] * out_scale)).astype(o_ref.dtype)

    def q_block(qi, a_max, last):
        acc_s[...] = jnp.zeros(acc_s.shape, F32)
        m0 = jnp.full((1, tq), -jnp.inf, F32)
        ms, a_max = lax.fori_loop(0, n_chunks // 2 - 1, pair, ((m0, m0), a_max))
        ms, b_max = step(n_chunks - 2, sa_s, sb_s, ms, a_max, True)
        step(n_chunks - 1, sb_s, sa_s, ms, b_max, False)
        if not last:
            load_q(qi + 1)
            a_max = tuple(scores(0, sa_s, mp) for mp in range(2))
        finalize(qi)
        return a_max

    load_q(0)
    a_max = tuple(scores(0, sa_s, mp) for mp in range(2))
    a_max = lax.fori_loop(0, n_q - 1, lambda qi, am: q_block(qi, am, False), a_max)
    q_block(n_q - 1, a_max, True)


def _attention(h3, qt, vt, lam, subln_g, lam_init):
    b, s, _ = h3.shape
    tq, tk = ATTN_TQ, ATTN_TK
    n_chunks = s // tk
    assert n_chunks % 2 == 0 and n_chunks >= 2
    vt_rows = V_DIM + ONES_ROWS
    kern = functools.partial(_attn_kernel, out_scale=1.0 - lam_init)
    return pl.pallas_call(
        kern,
        out_shape=jax.ShapeDtypeStruct((b, s, N_HEADS * V_DIM), BF16),
        grid=(b, N_HEADS),
        in_specs=[
            pl.BlockSpec(memory_space=pltpu.SMEM),
            pl.BlockSpec((None, None, s // tq, HEAD_COLS, tq), lambda bi, hi: (bi, hi, 0, 0, 0)),
            pl.BlockSpec((None, s, HEAD_COLS), lambda bi, hi: (bi, 0, K_HEAD0 + hi)),
            pl.BlockSpec((None, None, n_chunks, vt_rows, tk), lambda bi, hi: (bi, hi, 0, 0, 0)),
            pl.BlockSpec((1, V_DIM), lambda bi, hi: (0, 0)),
        ],
        out_specs=pl.BlockSpec((None, s, V_DIM), lambda bi, hi: (bi, 0, hi)),
        scratch_shapes=[
            pltpu.VMEM((2, HEAD_COLS, tq), BF16),
            pltpu.VMEM((2, vt_rows, tq), F32),
            pltpu.VMEM((tk, tq), F32), pltpu.VMEM((tk, tq), F32),
            pltpu.VMEM((tk, tq), F32), pltpu.VMEM((tk, tq), F32),
        ],
        compiler_params=pltpu.CompilerParams(
            dimension_semantics=("parallel", "arbitrary"),
            vmem_limit_bytes=VMEM_LIMIT),
        name="diff_attention",
    )(lam, qt, h3, vt, subln_g)


HALF = D_MODEL // 2


def _pack_row(x):
    return pltpu.pack_elementwise([x[:, :HALF], x[:, HALF:]], packed_dtype=BF16)


def _unpack_row(w):
    lo = pltpu.unpack_elementwise(w, index=0, packed_dtype=BF16, unpacked_dtype=F32)
    hi = pltpu.unpack_elementwise(w, index=1, packed_dtype=BF16, unpacked_dtype=F32)
    return jnp.concatenate([lo, hi], axis=1)


def _layer_norm(y, g, b):
    mu = jnp.mean(y, axis=-1, keepdims=True)
    d = y - mu
    var = jnp.mean(d * d, axis=-1, keepdims=True)
    return d * lax.rsqrt(var + LN_EPS) * g + b


def _mix_kernel(a_ref, cb_ref, cc_ref, cx_ref, ga_ref, gc_ref, ccp_ref, cxp_ref, ccn_ref, cxn_ref,
                x_ref, cw_ref, wa_ref, wc_ref, wo_ref, g1_ref, b1_ref, wrt_ref, brb_ref, triu_ref,
                x1_ref, x1p_ref, rt_ref, rtt_ref, cnt_ref, *, tiles_per_seq):
    i = pl.program_id(0)

    @pl.when(i == 0)
    def _():
        cnt_ref[...] = jnp.zeros(cnt_ref.shape, F32)

    tm = x_ref.shape[0]
    u = cc_ref[...].astype(F32) * cx_ref[...].astype(F32)
    first = (i % tiles_per_seq) == 0
    last = (i % tiles_per_seq) == tiles_per_seq - 1
    up = ccp_ref[SUBLANES - 1:SUBLANES, :].astype(F32) * cxp_ref[SUBLANES - 1:SUBLANES, :].astype(F32)
    un = ccn_ref[0:1, :].astype(F32) * cxn_ref[0:1, :].astype(F32)
    up = jnp.where(first, 0.0, up)
    un = jnp.where(last, 0.0, un)
    row = lax.broadcasted_iota(jnp.int32, u.shape, 0)
    u_prev = jnp.where(row == 0, up, pltpu.roll(u, 1, 0))
    u_next = jnp.where(row == tm - 1, un, pltpu.roll(u, tm - 1, 0))
    conv = cw_ref[0:1, :] * u_prev + cw_ref[1:2, :] * u + cw_ref[2:3, :] * u_next
    c = (cb_ref[...].astype(F32) * conv).astype(BF16)

    sub = triu_ref.shape[0]
    counts = cnt_ref[:, 0:1]
    tiles = [slice(s * sub, (s + 1) * sub) for s in range(tm // sub)]
    branches = [(jnp.dot(a_ref[rows, :], wa_ref[...], preferred_element_type=F32),
                 jnp.dot(c[rows, :], wc_ref[...], preferred_element_type=F32)) for rows in tiles]
    mixed = []
    for rows, (attn_b, conv_b) in zip(tiles, branches):
        merged = ga_ref[rows, :].astype(F32) * attn_b + gc_ref[rows, :].astype(F32) * conv_b
        mixed.append(jnp.dot(merged.astype(BF16), wo_ref[...], preferred_element_type=F32))
    normed = []
    for rows, m in zip(tiles, mixed):
        x1 = _layer_norm(DEEPNORM_ALPHA * x_ref[rows, :] + m, g1_ref[...], b1_ref[...])
        x1_ref[rows, :] = x1
        x1p_ref[rows, :] = _pack_row(x1)
        normed.append(x1)
    for rows, x1 in zip(tiles, normed):
        nt = (((1,), (1,)), ((), ()))
        logits_t = lax.dot_general(wrt_ref[...], x1.astype(BF16), nt, preferred_element_type=F32)
        work = logits_t[:N_EXPERTS, :] + brb_ref[...]

        row_f = lax.broadcasted_iota(jnp.int32, work.shape, 0).astype(F32)
        hots, vals, idxs = [], [], []
        for _ in range(TOP_K):
            mx = jnp.max(work, axis=0, keepdims=True)
            ix = jnp.min(jnp.where(work == mx, row_f, float(N_EXPERTS)), axis=0, keepdims=True)
            hot = row_f == ix
            work = jnp.where(hot, -jnp.inf, work)
            hots.append(hot)
            vals.append(mx)
            idxs.append(ix)
        exps = [jnp.exp(v - vals[0]) for v in vals]
        denom = exps[0] + exps[1] + exps[2] + exps[3]
        picked = jnp.zeros(work.shape, F32)
        for hot in hots:
            picked = picked + hot.astype(F32)
        before = counts + jnp.dot(picked.astype(BF16), triu_ref[...], preferred_element_type=F32)
        ranks = [jnp.sum(jnp.where(hot, before, 0.0), axis=0, keepdims=True) for hot in hots]
        fields = idxs + ranks + [e / denom for e in exps]
        rec_row = lax.broadcasted_iota(jnp.int32, (RT_ROWS, sub), 0)
        rec = jnp.zeros((RT_ROWS, sub), F32)
        for r, val in enumerate(fields):
            rec = jnp.where(rec_row == r, val, rec)
        rtt_ref[:, rows] = rec
        rt_ref[rows, :] = jnp.concatenate([rec, jnp.zeros((ROUTER_COLS - RT_ROWS, sub), F32)], axis=0).T
        counts = counts + jnp.sum(picked, axis=1, keepdims=True)
    cnt_ref[...] = jnp.broadcast_to(counts, cnt_ref.shape)


def _mix(a2d, h2d, x2d, seq, cw, wa, wc, wo, g1, b1, wr_t, br_b):
    n = x2d.shape[0]
    tm = MIX_TM
    tiles_per_seq = seq // tm
    halo_blocks = tm // SUBLANES
    n_halo = n // SUBLANES

    def col(blk):
        return pl.BlockSpec((tm, COL_BLOCK), lambda i: (i, blk))

    def prev(blk):
        return pl.BlockSpec((SUBLANES, COL_BLOCK), lambda i: (jnp.maximum(i * halo_blocks - 1, 0), blk))

    def nxt(blk):
        return pl.BlockSpec((SUBLANES, COL_BLOCK),
                            lambda i: (jnp.minimum((i + 1) * halo_blocks, n_halo - 1), blk))

    def full(shape):
        return pl.BlockSpec(shape, lambda i: (0,) * len(shape))

    sub = MIX_SUB
    tri_u = (lax.broadcasted_iota(jnp.int32, (sub, sub), 0)
             < lax.broadcasted_iota(jnp.int32, (sub, sub), 1)).astype(BF16)
    kern = functools.partial(_mix_kernel, tiles_per_seq=tiles_per_seq)
    return pl.pallas_call(
        kern,
        out_shape=(jax.ShapeDtypeStruct((n, D_MODEL), F32),
                   jax.ShapeDtypeStruct((n, HALF), jnp.int32),
                   jax.ShapeDtypeStruct((n, ROUTER_COLS), F32),
                   jax.ShapeDtypeStruct((RT_ROWS, n), F32),
                   jax.ShapeDtypeStruct((N_EXPERTS, LANES), F32)),
        grid=(n // tm,),
        in_specs=[
            pl.BlockSpec((tm, D_MODEL), lambda i: (i, 0)),
            col(CB_BLK), col(CC_BLK), col(CX_BLK), col(GA_BLK), col(GC_BLK),
            prev(CC_BLK), prev(CX_BLK), nxt(CC_BLK), nxt(CX_BLK),
            pl.BlockSpec((tm, D_MODEL), lambda i: (i, 0)),
            full((SUBLANES, D_MODEL)),
            full((D_MODEL, D_MODEL)), full((D_MODEL, D_MODEL)), full((D_MODEL, D_MODEL)),
            full((1, D_MODEL)), full((1, D_MODEL)),
            full((ROUTER_COLS, D_MODEL)), full((N_EXPERTS, sub)), full((sub, sub)),
        ],
        out_specs=(pl.BlockSpec((tm, D_MODEL), lambda i: (i, 0)),
                   pl.BlockSpec((tm, HALF), lambda i: (i, 0)),
                   pl.BlockSpec((tm, ROUTER_COLS), lambda i: (i, 0)),
                   pl.BlockSpec((RT_ROWS, tm), lambda i: (0, i)),
                   pl.BlockSpec((N_EXPERTS, LANES), lambda i: (0, 0))),
        compiler_params=pltpu.CompilerParams(
            dimension_semantics=("arbitrary",), vmem_limit_bytes=VMEM_LIMIT),
        name="mix_ln1_router",
    )(a2d, h2d, h2d, h2d, h2d, h2d, h2d, h2d, h2d, h2d, x2d, cw, wa, wc, wo, g1, b1, wr_t, br_b, tri_u)


def _moe_kernel(be_ref, nv_ref, nu_ref, xs_ref, wg_ref, bg_ref, wu_ref, bu_ref, wd_ref, bd_ref, o_ref,
                wg_s, wu_s, wd_s):
    i = pl.program_id(0)
    used = i < nu_ref[0]

    @pl.when(used & ((i == 0) | (be_ref[i] != be_ref[jnp.maximum(i - 1, 0)])))
    def _():
        wg_s[...] = wg_ref[...].astype(BF16)
        wu_s[...] = wu_ref[...].astype(BF16)
        wd_s[...] = wd_ref[...].astype(BF16)

    @pl.when(used)
    def _():
        row = lax.broadcasted_iota(jnp.int32, xs_ref.shape, 0)
        x = _unpack_row(jnp.where(row < nv_ref[i], xs_ref[...], 0)).astype(BF16)
        hg = jnp.minimum(jnp.dot(x, wg_s[...], preferred_element_type=F32) + bg_ref[...], SWIGLU_LIMIT)
        hu = jnp.clip(jnp.dot(x, wu_s[...], preferred_element_type=F32) + bu_ref[...],
                      -SWIGLU_LIMIT, SWIGLU_LIMIT)
        act = hg * jax.nn.sigmoid(SWIGLU_ALPHA * hg) * (hu + 1.0)
        y = jnp.dot(act.astype(BF16), wd_s[...], preferred_element_type=F32) + bd_ref[...]
        o_ref[...] = _pack_row(y)

    @pl.when(jnp.logical_not(used))
    def _():
        o_ref[...] = jnp.zeros(o_ref.shape, o_ref.dtype)


def _moe_experts(block_e, n_valid, n_used, xs, wg, bg, wu, bu, wd, bd):
    p = xs.shape[0]
    tm = MOE_TM
    nb = p // tm

    def row_map(i, be, nv, nu):
        return (jnp.minimum(i, nu[0] - 1), 0)

    def w_map(i, be, nv, nu):
        return (be[jnp.minimum(i, nu[0] - 1)], 0, 0)

    w_spec = pl.BlockSpec((None, D_MODEL, D_MODEL), w_map)
    b_spec = pl.BlockSpec((None, 1, D_MODEL), w_map)
    return pl.pallas_call(
        _moe_kernel,
        out_shape=jax.ShapeDtypeStruct((p, HALF), jnp.int32),
        grid_spec=pltpu.PrefetchScalarGridSpec(
            num_scalar_prefetch=3,
            grid=(nb,),
            in_specs=[
                pl.BlockSpec((tm, HALF), row_map),
                w_spec, b_spec, w_spec, b_spec, w_spec, b_spec,
            ],
            out_specs=pl.BlockSpec((tm, HALF), lambda i, be, nv, nu: (i, 0)),
            scratch_shapes=[pltpu.VMEM((D_MODEL, D_MODEL), BF16)] * 3,
        ),
        compiler_params=pltpu.CompilerParams(
            dimension_semantics=("arbitrary",), vmem_limit_bytes=VMEM_LIMIT),
        name="moe_experts",
    )(block_e, n_valid, n_used, xs, wg, bg, wu, bu, wd, bd)


def _sc_mesh():
    return plsc.VectorSubcoreMesh(core_axis_name="core", subcore_axis_name="subcore")


def _sc_index_rows(dest_t):
    k, n = dest_t.shape
    w = SC_WINDOW
    return jnp.pad(dest_t.reshape(k * n // w, w), ((0, 0), (0, LANES - w)))


def _sc_dispatch(x1, slot_idx, p):
    n, d = x1.shape
    w = SC_WINDOW
    windows = n // w

    @pl.kernel(out_type=jax.ShapeDtypeStruct((p, d), x1.dtype), mesh=_sc_mesh(), scratch_types=[],
               name="sc_dispatch")
    def run(x_hbm, d_hbm, o_hbm):
        def body(x_vmem, *d_vmems):
            for d_vmem in d_vmems:
                pltpu.sync_copy(x_vmem, o_hbm.at[d_vmem.at[0, pl.ds(0, w)]])

        pltpu.emit_pipeline(
            body,
            grid=(windows,),
            in_specs=[pl.BlockSpec((w, d), lambda i: (i, 0))]
            + [pl.BlockSpec((1, LANES), functools.partial(lambda k, i: (k * windows + i, 0), k))
               for k in range(TOP_K)],
            out_specs=[],
            core_axis_name=("core", "subcore"),
            dimension_semantics=(pltpu.PARALLEL,),
        )(x_hbm, *([d_hbm] * TOP_K))

    return run(x1, slot_idx)


def _sc_gather(ys, idx):
    w = SC_WINDOW
    r = idx.shape[0] * w
    d = ys.shape[1]

    @pl.kernel(out_type=jax.ShapeDtypeStruct((r, d), ys.dtype), mesh=_sc_mesh(), scratch_types=[],
               name="sc_gather")
    def run(y_hbm, i_hbm, o_hbm):
        def body(i_vmem, o_vmem):
            pltpu.sync_copy(y_hbm.at[i_vmem.at[0, pl.ds(0, w)]], o_vmem)

        pltpu.emit_pipeline(
            body,
            grid=(r // w,),
            in_specs=[pl.BlockSpec((1, LANES), lambda i: (i, 0))],
            out_specs=[pl.BlockSpec((w, d), lambda i: (i, 0))],
            core_axis_name=("core", "subcore"),
            dimension_semantics=(pltpu.PARALLEL,),
        )(i_hbm, o_hbm)

    return run(ys, idx)


def _ln2_kernel(x1_ref, yg_ref, rt_ref, g_ref, b_ref, o_ref):
    f = jnp.zeros(x1_ref.shape, F32)
    for k in range(TOP_K):
        f = f + _unpack_row(yg_ref[k]) * rt_ref[:, RT_GATE + k:RT_GATE + k + 1]
    o_ref[...] = _layer_norm(DEEPNORM_ALPHA * x1_ref[...] + f, g_ref[...], b_ref[...])


def _combine_ln2(x1, yg, rt, g2, b2):
    n = x1.shape[0]
    tm = LN2_TM
    return pl.pallas_call(
        _ln2_kernel,
        out_shape=jax.ShapeDtypeStruct((n, D_MODEL), F32),
        grid=(n // tm,),
        in_specs=[
            pl.BlockSpec((tm, D_MODEL), lambda i: (i, 0)),
            pl.BlockSpec((TOP_K, tm, HALF), lambda i: (0, i, 0)),
            pl.BlockSpec((tm, ROUTER_COLS), lambda i: (i, 0)),
            pl.BlockSpec((1, D_MODEL), lambda i: (0, 0)),
            pl.BlockSpec((1, D_MODEL), lambda i: (0, 0)),
        ],
        out_specs=pl.BlockSpec((tm, D_MODEL), lambda i: (i, 0)),
        compiler_params=pltpu.CompilerParams(
            dimension_semantics=("parallel",), vmem_limit_bytes=VMEM_LIMIT),
        name="combine_ln2",
    )(x1, yg, rt, g2, b2)


def _rotary_tables(seq):
    inv = ROPE_THETA ** (-jnp.arange(0, ROT_DIM, 2, dtype=F32) / ROT_DIM)
    ang = jnp.arange(seq, dtype=F32)[:, None] * inv[None, :]
    cos, sin = jnp.cos(ang), jnp.sin(ang)
    ones = jnp.ones((seq, QK_DIM - ROT_DIM), F32)
    zeros = jnp.zeros((seq, QK_DIM - ROT_DIM), F32)
    zh = jnp.zeros((seq, ROT_HALF), F32)
    cos64 = jnp.concatenate([cos, cos, ones], axis=1)
    sa64 = jnp.concatenate([-sin, zh, zeros], axis=1)
    sb64 = jnp.concatenate([zh, sin, zeros], axis=1)
    rep = LANES // QK_DIM
    return (jnp.tile(cos64, (1, rep)), jnp.tile(sa64, (1, rep)), jnp.tile(sb64, (1, rep)), cos.T, sin.T)


def _slot_layout(rtt, counts, tm):
    n = rtt.shape[1]
    p = n * TOP_K + N_EXPERTS * tm
    nb = p // tm
    expert = rtt[RT_EXPERT:RT_EXPERT + TOP_K].astype(jnp.int32)
    rank = rtt[RT_RANK:RT_RANK + TOP_K].astype(jnp.int32)
    padded = ((counts + tm - 1) // tm) * tm
    pad_ends = jnp.cumsum(padded)
    pad_starts = pad_ends - padded
    ids = jnp.arange(N_EXPERTS, dtype=jnp.int32)[:, None, None]
    dest_t = jnp.sum(jnp.where(expert[None] == ids, pad_starts[:, None, None], 0), axis=0) + rank
    block_start = jnp.arange(nb, dtype=jnp.int32) * tm
    block_e = jnp.minimum(jnp.sum(pad_ends[None, :] <= block_start[:, None], axis=1),
                          N_EXPERTS - 1).astype(jnp.int32)
    own = block_e[:, None] == jnp.arange(N_EXPERTS, dtype=jnp.int32)[None, :]
    valid_end = jnp.sum(jnp.where(own, (pad_starts + counts)[None, :], 0), axis=1)
    n_valid = jnp.clip(valid_end - block_start, 0, tm).astype(jnp.int32)
    n_used = (pad_ends[-1] // tm).astype(jnp.int32).reshape(1)
    return dest_t.astype(jnp.int32), block_e, n_valid, n_used, p


def _trunk(x, prm):
    b, s, d = x.shape
    n = b * s
    x2d = x.reshape(n, d)
    h, qt, vt = _inproj(x2d, prm["w_in"], prm["w_q_t"], prm["w_v_t"], prm["b_gate"], _rotary_tables(s), b, s)
    a = _attention(h.reshape(b, s, H_COLS), qt, vt, prm["lam"], prm["subln_g"], prm["lam_init"])
    x1, x1p, rt, rtt, cnt = _mix(a.reshape(n, d), h, x2d, s, prm["conv_w"], prm["w_attn_br"],
                                 prm["w_conv_br"], prm["w_o"], prm["ln1_g"], prm["ln1_b"],
                                 prm["w_router_t"], prm["b_router_b"])
    counts = cnt[:, 0].astype(jnp.int32)
    dest_t, block_e, n_valid, n_used, p = _slot_layout(rtt, counts, MOE_TM)
    slot_idx = _sc_index_rows(dest_t)
    xs = _sc_dispatch(x1p, slot_idx, p)
    ys = _moe_experts(block_e, n_valid, n_used, xs, prm["w_exp_gate"], prm["b_exp_gate"],
                      prm["w_exp_up"], prm["b_exp_up"], prm["w_exp_down"], prm["b_exp_down"])
    yg = _sc_gather(ys, slot_idx).reshape(TOP_K, n, HALF)
    y = _combine_ln2(x1, yg, rt, prm["ln2_g"], prm["ln2_b"])
    return y.reshape(b, s, d)


def kernel(x_prompt, x_sample, w_in, b_branch_gate, lambda_q1, lambda_k1, lambda_q2, lambda_k2, subln_g,
           conv_w, w_attn_br, w_conv_br, w_o, ln1_g, ln1_b, w_router, b_router, w_exp_gate, b_exp_gate,
           w_exp_up, b_exp_up, w_exp_down, b_exp_down, ln2_g, ln2_b):
    l = 0
    lam_init = 0.8 - 0.6 * math.exp(-0.3 * l)
    lam = (jnp.exp(jnp.sum(lambda_q1[l].astype(F32) * lambda_k1[l].astype(F32)))
           - jnp.exp(jnp.sum(lambda_q2[l].astype(F32) * lambda_k2[l].astype(F32))) + lam_init)
    prm = {
        "lam_init": lam_init,
        "lam": lam.reshape(1).astype(F32),
        "w_in": w_in[l].astype(BF16),
        "w_q_t": w_in[l][:, Q_STEP * COL_BLOCK:(Q_STEP + 1) * COL_BLOCK].T.astype(BF16),
        "w_v_t": w_in[l][:, V_STEP * COL_BLOCK:(V_STEP + 1) * COL_BLOCK].T.astype(BF16),
        "b_gate": b_branch_gate[l].reshape(1, 2 * D_MODEL),
        "subln_g": subln_g[l].reshape(1, V_DIM),
        "conv_w": jnp.pad(conv_w[l], ((0, SUBLANES - conv_w.shape[1]), (0, 0))),
        "w_attn_br": w_attn_br[l].astype(BF16),
        "w_conv_br": w_conv_br[l].astype(BF16),
        "w_o": w_o[l].astype(BF16),
        "ln1_g": ln1_g[l].reshape(1, D_MODEL),
        "ln1_b": ln1_b[l].reshape(1, D_MODEL),
        "w_router_t": jnp.pad(w_router[l].T, ((0, ROUTER_COLS - N_EXPERTS), (0, 0))).astype(BF16),
        "b_router_b": jnp.broadcast_to(b_router[l].astype(F32)[:, None], (N_EXPERTS, MIX_SUB)),
        "w_exp_gate": w_exp_gate[l],
        "b_exp_gate": b_exp_gate[l].reshape(N_EXPERTS, 1, D_MODEL),
        "w_exp_up": w_exp_up[l],
        "b_exp_up": b_exp_up[l].reshape(N_EXPERTS, 1, D_MODEL),
        "w_exp_down": w_exp_down[l],
        "b_exp_down": b_exp_down[l].reshape(N_EXPERTS, 1, D_MODEL),
        "ln2_g": ln2_g[l].reshape(1, D_MODEL),
        "ln2_b": ln2_b[l].reshape(1, D_MODEL),
    }
    return _trunk(x_prompt, prm), _trunk(x_sample, prm)
```

```python
import functools
import math

import jax
import jax.numpy as jnp
from jax import lax
from jax.experimental import pallas as pl
from jax.experimental.pallas import tpu as pltpu
from jax.experimental.pallas import tpu_sc as plsc

F32 = jnp.float32
BF16 = jnp.bfloat16

D_MODEL = 1024
N_HEADS = 8
QK_DIM = 64
V_DIM = 128
HEAD_COLS = 2 * QK_DIM
ROT_DIM = QK_DIM // 4
ROT_HALF = ROT_DIM // 2
ROPE_THETA = 500000.0
SUBLN_EPS = 1e-5
IN_COLS = 8192
COL_BLOCK = 1024
Q_STEP, K_STEP, V_STEP = 0, 1, 2
GATE_STEP = 6
H_COLS = IN_COLS - 2 * COL_BLOCK
CB_BLK, CC_BLK, CX_BLK, GA_BLK, GC_BLK = 1, 2, 3, 4, 5
K_HEAD0 = 0
N_EXPERTS = 32
TOP_K = 4
SWIGLU_LIMIT = 7.0
SWIGLU_ALPHA = 1.702
LN_EPS = 1e-5
DEPTH = 1
DEEPNORM_ALPHA = (2 * DEPTH) ** 0.25
LANES = 128
SUBLANES = 8
ROUTER_COLS = LANES
RT_EXPERT, RT_RANK, RT_GATE = 0, 4, 8
RT_ROWS = 16
SC_WINDOW = 32
ONES_ROWS = 16
Q_SCALE = math.log2(math.e) * QK_DIM ** -0.5

VMEM_LIMIT = 56 * 1024 * 1024

PROJ_TM = 1024
ATTN_TQ = 1024
ATTN_TK = 512
MIX_TM = 512
MIX_SUB = 256
MOE_TM = 512
LN2_TM = 1024


def _rotate_rows(t1, t2, cos, sin):
    return t1 * cos - t2 * sin, t2 * cos + t1 * sin


def _inproj_kernel(x_ref, w_ref, wqt_ref, wvt_ref, bg_ref, cos_ref, sa_ref, sb_ref, cosr_ref, sinr_ref,
                   o_ref, qt_ref, vt_ref, xb_s):
    j = pl.program_id(1)

    @pl.when(j == 0)
    def _():
        xb_s[...] = x_ref[...].astype(BF16)

    xb = xb_s[...]
    nt = (((1,), (1,)), ((), ()))
    tm = x_ref.shape[0]
    halves = (slice(0, tm // 2), slice(tm // 2, tm))

    @pl.when(j == Q_STEP)
    def _():
        acc_t = lax.dot_general(wqt_ref[...], xb, nt, preferred_element_type=F32) * Q_SCALE
        cos, sin = cosr_ref[...], sinr_ref[...]
        for hh in range(N_HEADS):
            parts = []
            for mp in range(2):
                o = hh * HEAD_COLS + mp * QK_DIM
                parts += _rotate_rows(acc_t[o:o + ROT_HALF], acc_t[o + ROT_HALF:o + ROT_DIM], cos, sin)
                parts.append(acc_t[o + ROT_DIM:o + QK_DIM])
            qt_ref[hh, 0] = jnp.concatenate(parts, axis=0).astype(BF16)

    @pl.when(j == V_STEP)
    def _():
        acc_t = lax.dot_general(wvt_ref[...], xb, nt, preferred_element_type=F32)
        n_heads, n_chunks, _, tk = vt_ref.shape
        for hh in range(n_heads):
            for cc in range(n_chunks):
                vt_ref[hh, cc, :V_DIM, :] = acc_t[hh * V_DIM:(hh + 1) * V_DIM,
                                                  cc * tk:(cc + 1) * tk].astype(BF16)
                vt_ref[hh, cc, V_DIM:, :] = jnp.ones((ONES_ROWS, tk), BF16)

    @pl.when((j > V_STEP) & (j < GATE_STEP))
    def _():
        o_ref[...] = jnp.dot(xb, w_ref[...], preferred_element_type=F32).astype(o_ref.dtype)

    @pl.when(j >= GATE_STEP)
    def _():
        accs = [jnp.dot(xb_s[r, :], w_ref[...], preferred_element_type=F32) for r in halves]
        for r, acc in zip(halves, accs):
            o_ref[r, :] = jax.nn.sigmoid(acc + bg_ref[...]).astype(o_ref.dtype)

    @pl.when(j == K_STEP)
    def _():
        accs = [jnp.dot(xb_s[r, :], w_ref[...], preferred_element_type=F32) for r in halves]
        for r, acc in zip(halves, accs):
            cos, sa, sb = cos_ref[r, :], sa_ref[r, :], sb_ref[r, :]
            for c in range(COL_BLOCK // LANES):
                a = acc[:, c * LANES:(c + 1) * LANES]
                rot = (a * cos + pltpu.roll(a, LANES - ROT_HALF, 1) * sa + pltpu.roll(a, ROT_HALF, 1) * sb)
                o_ref[r, c * LANES:(c + 1) * LANES] = rot.astype(o_ref.dtype)


def _inproj(x2d, w_in_b, wqt_b, wvt_b, b_gate, tables, batch, seq):
    n = x2d.shape[0]
    tm, tk = PROJ_TM, ATTN_TK
    assert tm == ATTN_TQ
    pos_blocks = seq // tm
    cos_t, sa_t, sb_t, cos_r, sin_r = tables
    tab_spec = pl.BlockSpec((tm, LANES), lambda i, j: (i % pos_blocks, 0))
    rtab_spec = pl.BlockSpec((ROT_HALF, tm), lambda i, j: (0, i % pos_blocks))
    wt_spec = pl.BlockSpec((COL_BLOCK, D_MODEL), lambda i, j: (0, 0))
    return pl.pallas_call(
        _inproj_kernel,
        out_shape=(jax.ShapeDtypeStruct((n, H_COLS), BF16),
                   jax.ShapeDtypeStruct((batch, N_HEADS, seq // tm, HEAD_COLS, tm), BF16),
                   jax.ShapeDtypeStruct((batch, N_HEADS, seq // tk, V_DIM + ONES_ROWS, tk), BF16)),
        grid=(n // tm, IN_COLS // COL_BLOCK),
        in_specs=[
            pl.BlockSpec((tm, D_MODEL), lambda i, j: (i, 0)),
            pl.BlockSpec((D_MODEL, COL_BLOCK), lambda i, j: (0, jnp.where(j <= V_STEP, K_STEP, j))),
            wt_spec, wt_spec,
            pl.BlockSpec((1, COL_BLOCK), lambda i, j: (0, jnp.maximum(j - GATE_STEP, 0))),
            tab_spec, tab_spec, tab_spec, rtab_spec, rtab_spec,
        ],
        out_specs=(
            pl.BlockSpec((tm, COL_BLOCK), lambda i, j: (i, jnp.maximum(j - V_STEP, 0))),
            pl.BlockSpec((None, N_HEADS, 1, HEAD_COLS, tm),
                         lambda i, j: (i // pos_blocks, 0, i % pos_blocks, 0, 0)),
            pl.BlockSpec((None, N_HEADS, tm // tk, V_DIM + ONES_ROWS, tk),
                         lambda i, j: (i // pos_blocks, 0, i % pos_blocks, 0, 0)),
        ),
        scratch_shapes=[pltpu.VMEM((tm, D_MODEL), BF16)],
        compiler_params=pltpu.CompilerParams(
            dimension_semantics=("parallel", "arbitrary"), vmem_limit_bytes=VMEM_LIMIT),
        name="inproj",
    )(x2d, w_in_b, wqt_b, wvt_b, b_gate, cos_t, sa_t, sb_t, cos_r, sin_r)


def _attn_kernel(lam_ref, q_ref, k_ref, vt_ref, g_ref, o_ref, qt_s, acc_s, sa0, sa1, sb0, sb1, *, out_scale):
    n_q, _, tq = q_ref.shape
    n_chunks, _, tk = vt_ref.shape
    sa_s, sb_s = (sa0, sa1), (sb0, sb1)

    def load_q(qi):
        zero_half = jnp.zeros((QK_DIM, tq), BF16)
        qt_s[0, :QK_DIM, :] = q_ref[qi, :QK_DIM, :]
        qt_s[0, QK_DIM:, :] = zero_half
        qt_s[1, :QK_DIM, :] = zero_half
        qt_s[1, QK_DIM:, :] = q_ref[qi, QK_DIM:, :]

    def scores(c, s_refs, mp):
        k = k_ref[pl.ds(pl.multiple_of(c * tk, tk), tk), :]
        s = jnp.dot(k, qt_s[mp], preferred_element_type=F32)
        s_refs[mp][...] = s
        return jnp.max(s, axis=0, keepdims=True)

    def update(c, s_refs, mp, m_old, c_max):
        m_new = jnp.maximum(m_old, c_max)
        alpha = jnp.exp2(m_old - m_new)
        p = jnp.exp2(s_refs[mp][...] - m_new).astype(BF16)
        acc_s[mp] = alpha * acc_s[mp] + jnp.dot(vt_ref[c], p, preferred_element_type=F32)
        return m_new

    def step(c, cur, nxt, ms, cur_max, prefetch):
        new_ms, nxt_max = [], []
        for mp in range(2):
            if prefetch:
                nxt_max.append(scores(c + 1, nxt, mp))
            new_ms.append(update(c, cur, mp, ms[mp], cur_max[mp]))
        return tuple(new_ms), tuple(nxt_max)

    def pair(i, carry):
        ms, a_max = carry
        ms, b_max = step(2 * i, sa_s, sb_s, ms, a_max, True)
        return step(2 * i + 1, sb_s, sa_s, ms, b_max, True)

    def finalize(qi):
        lam = lam_ref[0]
        o1 = acc_s[0, :V_DIM, :] / acc_s[0, V_DIM:V_DIM + 1, :]
        o2 = acc_s[1, :V_DIM, :] / acc_s[1, V_DIM:V_DIM + 1, :]
        ot = o1 - lam * o2
        ot = ot * lax.rsqrt(jnp.mean(ot * ot, axis=0, keepdims=True) + SUBLN_EPS)
        rows = pl.ds(pl.multiple_of(qi * tq, tq), tq)
        o_ref[rows, :] = (ot.T * (g_ref[...] * out_scale)).astype(o_ref.dtype)

    def q_block(qi, a_max, last):
        acc_s[...] = jnp.zeros(acc_s.shape, F32)
        m0 = jnp.full((1, tq), -jnp.inf, F32)
        ms, a_max = lax.fori_loop(0, n_chunks // 2 - 1, pair, ((m0, m0), a_max))
        ms, b_max = step(n_chunks - 2, sa_s, sb_s, ms, a_max, True)
        step(n_chunks - 1, sb_s, sa_s, ms, b_max, False)
        if not last:
            load_q(qi + 1)
            a_max = tuple(scores(0, sa_s, mp) for mp in range(2))
        finalize(qi)
        return a_max

    load_q(0)
    a_max = tuple(scores(0, sa_s, mp) for mp in range(2))
    a_max = lax.fori_loop(0, n_q - 1, lambda qi, am: q_block(qi, am, False), a_max)
    q_block(n_q - 1, a_max, True)


def _attention(h3, qt, vt, lam, subln_g, lam_init):
    b, s, _ = h3.shape
    tq, tk = ATTN_TQ, ATTN_TK
    n_chunks = s // tk
    assert n_chunks % 2 == 0 and n_chunks >= 2
    vt_rows = V_DIM + ONES_ROWS
    kern = functools.partial(_attn_kernel, out_scale=1.0 - lam_init)
    return pl.pallas_call(
        kern,
        out_shape=jax.ShapeDtypeStruct((b, s, N_HEADS * V_DIM), BF16),
        grid=(b, N_HEADS),
        in_specs=[
            pl.BlockSpec(memory_space=pltpu.SMEM),
            pl.BlockSpec((None, None, s // tq, HEAD_COLS, tq), lambda bi, hi: (bi, hi, 0, 0, 0)),
            pl.BlockSpec((None, s, HEAD_COLS), lambda bi, hi: (bi, 0, K_HEAD0 + hi)),
            pl.BlockSpec((None, None, n_chunks, vt_rows, tk), lambda bi, hi: (bi, hi, 0, 0, 0)),
            pl.BlockSpec((1, V_DIM), lambda bi, hi: (0, 0)),
        ],
        out_specs=pl.BlockSpec((None, s, V_DIM), lambda bi, hi: (bi, 0, hi)),
        scratch_shapes=[
            pltpu.VMEM((2, HEAD_COLS, tq), BF16),
            pltpu.VMEM((2, vt_rows, tq), F32),
            pltpu.VMEM((tk, tq), F32), pltpu.VMEM((tk, tq), F32),
            pltpu.VMEM((tk, tq), F32), pltpu.VMEM((tk, tq), F32),
        ],
        compiler_params=pltpu.CompilerParams(
            dimension_semantics=("parallel", "arbitrary"),
            vmem_limit_bytes=VMEM_LIMIT),
        name="diff_attention",
    )(lam, qt, h3, vt, subln_g)


HALF = D_MODEL // 2


def _pack_row(x):
    return pltpu.pack_elementwise([x[:, :HALF], x[:, HALF:]], packed_dtype=BF16)


def _unpack_row(w):
    lo = pltpu.unpack_elementwise(w, index=0, packed_dtype=BF16, unpacked_dtype=F32)
    hi = pltpu.unpack_elementwise(w, index=1, packed_dtype=BF16, unpacked_dtype=F32)
    return jnp.concatenate([lo, hi], axis=1)


def _layer_norm(y, g, b):
    mu = jnp.mean(y, axis=-1, keepdims=True)
    d = y - mu
    var = jnp.mean(d * d, axis=-1, keepdims=True)
    return d * lax.rsqrt(var + LN_EPS) * g + b


def _mix_kernel(a_ref, cb_ref, cc_ref, cx_ref, ga_ref, gc_ref, ccp_ref, cxp_ref, ccn_ref, cxn_ref,
                x_ref, cw_ref, wa_ref, wc_ref, wo_ref, g1_ref, b1_ref, wrt_ref, brb_ref, triu_ref,
                x1_ref, x1p_ref, rt_ref, rtt_ref, cnt_ref, *, tiles_per_seq):
    i = pl.program_id(0)

    @pl.when(i == 0)
    def _():
        cnt_ref[...] = jnp.zeros(cnt_ref.shape, F32)

    tm = x_ref.shape[0]
    u = cc_ref[...].astype(F32) * cx_ref[...].astype(F32)
    first = (i % tiles_per_seq) == 0
    last = (i % tiles_per_seq) == tiles_per_seq - 1
    up = ccp_ref[SUBLANES - 1:SUBLANES, :].astype(F32) * cxp_ref[SUBLANES - 1:SUBLANES, :].astype(F32)
    un = ccn_ref[0:1, :].astype(F32) * cxn_ref[0:1, :].astype(F32)
    up = jnp.where(first, 0.0, up)
    un = jnp.where(last, 0.0, un)
    row = lax.broadcasted_iota(jnp.int32, u.shape, 0)
    u_prev = jnp.where(row == 0, up, pltpu.roll(u, 1, 0))
    u_next = jnp.where(row == tm - 1, un, pltpu.roll(u, tm - 1, 0))
    conv = cw_ref[0:1, :] * u_prev + cw_ref[1:2, :] * u + cw_ref[2:3, :] * u_next
    c = (cb_ref[...].astype(F32) * conv).astype(BF16)

    sub = triu_ref.shape[0]
    counts = cnt_ref[:, 0:1]
    tiles = [slice(s * sub, (s + 1) * sub) for s in range(tm // sub)]
    branches = [(jnp.dot(a_ref[rows, :], wa_ref[...], preferred_element_type=F32),
                 jnp.dot(c[rows, :], wc_ref[...], preferred_element_type=F32)) for rows in tiles]
    mixed = []
    for rows, (attn_b, conv_b) in zip(tiles, branches):
        merged = ga_ref[rows, :].astype(F32) * attn_b + gc_ref[rows, :].astype(F32) * conv_b
        mixed.append(jnp.dot(merged.astype(BF16), wo_ref[...], preferred_element_type=F32))
    normed = []
    for rows, m in zip(tiles, mixed):
        x1 = _layer_norm(DEEPNORM_ALPHA * x_ref[rows, :] + m, g1_ref[...], b1_ref[...])
        x1_ref[rows, :] = x1
        x1p_ref[rows, :] = _pack_row(x1)
        normed.append(x1)
    for rows, x1 in zip(tiles, normed):
        nt = (((1,), (1,)), ((), ()))
        logits_t = lax.dot_general(wrt_ref[...], x1.astype(BF16), nt, preferred_element_type=F32)
        work = logits_t[:N_EXPERTS, :] + brb_ref[...]

        row_f = lax.broadcasted_iota(jnp.int32, work.shape, 0).astype(F32)
        hots, vals, idxs = [], [], []
        for _ in range(TOP_K):
            mx = jnp.max(work, axis=0, keepdims=True)
            ix = jnp.min(jnp.where(work == mx, row_f, float(N_EXPERTS)), axis=0, keepdims=True)
            hot = row_f == ix
            work = jnp.where(hot, -jnp.inf, work)
            hots.append(hot)
            vals.append(mx)
            idxs.append(ix)
        exps = [jnp.exp(v - vals[0]) for v in vals]
        denom = exps[0] + exps[1] + exps[2] + exps[3]
        picked = jnp.zeros(work.shape, F32)
        for hot in hots:
            picked = picked + hot.astype(F32)
        before = counts + jnp.dot(picked.astype(BF16), triu_ref[...], preferred_element_type=F32)
        ranks = [jnp.sum(jnp.where(hot, before, 0.0), axis=0, keepdims=True) for hot in hots]
        fields = idxs + ranks + [e / denom for e in exps]
        rec_row = lax.broadcasted_iota(jnp.int32, (RT_ROWS, sub), 0)
        rec = jnp.zeros((RT_ROWS, sub), F32)
        for r, val in enumerate(fields):
            rec = jnp.where(rec_row == r, val, rec)
        rtt_ref[:, rows] = rec
        rt_ref[rows, :] = jnp.concatenate([rec, jnp.zeros((ROUTER_COLS - RT_ROWS, sub), F32)], axis=0).T
        counts = counts + jnp.sum(picked, axis=1, keepdims=True)
    cnt_ref[...] = jnp.broadcast_to(counts, cnt_ref.shape)


def _mix(a2d, h2d, x2d, seq, cw, wa, wc, wo, g1, b1, wr_t, br_b):
    n = x2d.shape[0]
    tm = MIX_TM
    tiles_per_seq = seq // tm
    halo_blocks = tm // SUBLANES
    n_halo = n // SUBLANES

    def col(blk):
        return pl.BlockSpec((tm, COL_BLOCK), lambda i: (i, blk))

    def prev(blk):
        return pl.BlockSpec((SUBLANES, COL_BLOCK), lambda i: (jnp.maximum(i * halo_blocks - 1, 0), blk))

    def nxt(blk):
        return pl.BlockSpec((SUBLANES, COL_BLOCK),
                            lambda i: (jnp.minimum((i + 1) * halo_blocks, n_halo - 1), blk))

    def full(shape):
        return pl.BlockSpec(shape, lambda i: (0,) * len(shape))

    sub = MIX_SUB
    tri_u = (lax.broadcasted_iota(jnp.int32, (sub, sub), 0)
             < lax.broadcasted_iota(jnp.int32, (sub, sub), 1)).astype(BF16)
    kern = functools.partial(_mix_kernel, tiles_per_seq=tiles_per_seq)
    return pl.pallas_call(
        kern,
        out_shape=(jax.ShapeDtypeStruct((n, D_MODEL), F32),
                   jax.ShapeDtypeStruct((n, HALF), jnp.int32),
                   jax.ShapeDtypeStruct((n, ROUTER_COLS), F32),
                   jax.ShapeDtypeStruct((RT_ROWS, n), F32),
                   jax.ShapeDtypeStruct((N_EXPERTS, LANES), F32)),
        grid=(n // tm,),
        in_specs=[
            pl.BlockSpec((tm, D_MODEL), lambda i: (i, 0)),
            col(CB_BLK), col(CC_BLK), col(CX_BLK), col(GA_BLK), col(GC_BLK),
            prev(CC_BLK), prev(CX_BLK), nxt(CC_BLK), nxt(CX_BLK),
            pl.BlockSpec((tm, D_MODEL), lambda i: (i, 0)),
            full((SUBLANES, D_MODEL)),
            full((D_MODEL, D_MODEL)), full((D_MODEL, D_MODEL)), full((D_MODEL, D_MODEL)),
            full((1, D_MODEL)), full((1, D_MODEL)),
            full((ROUTER_COLS, D_MODEL)), full((N_EXPERTS, sub)), full((sub, sub)),
        ],
        out_specs=(pl.BlockSpec((tm, D_MODEL), lambda i: (i, 0)),
                   pl.BlockSpec((tm, HALF), lambda i: (i, 0)),
                   pl.BlockSpec((tm, ROUTER_COLS), lambda i: (i, 0)),
                   pl.BlockSpec((RT_ROWS, tm), lambda i: (0, i)),
                   pl.BlockSpec((N_EXPERTS, LANES), lambda i: (0, 0))),
        compiler_params=pltpu.CompilerParams(
            dimension_semantics=("arbitrary",), vmem_limit_bytes=VMEM_LIMIT),
        name="mix_ln1_router",
    )(a2d, h2d, h2d, h2d, h2d, h2d, h2d, h2d, h2d, h2d, x2d, cw, wa, wc, wo, g1, b1, wr_t, br_b, tri_u)


def _moe_kernel(be_ref, nv_ref, nu_ref, xs_ref, wg_ref, bg_ref, wu_ref, bu_ref, wd_ref, bd_ref, o_ref,
                wg_s, wu_s, wd_s):
    i = pl.program_id(0)
    used = i < nu_ref[0]

    @pl.when(used & ((i == 0) | (be_ref[i] != be_ref[jnp.maximum(i - 1, 0)])))
    def _():
        wg_s[...] = wg_ref[...].astype(BF16)
        wu_s[...] = wu_ref[...].astype(BF16)
        wd_s[...] = wd_ref[...].astype(BF16)

    @pl.when(used)
    def _():
        row = lax.broadcasted_iota(jnp.int32, xs_ref.shape, 0)
        x = _unpack_row(jnp.where(row < nv_ref[i], xs_ref[...], 0)).astype(BF16)
        hg = jnp.minimum(jnp.dot(x, wg_s[...], preferred_element_type=F32) + bg_ref[...], SWIGLU_LIMIT)
        hu = jnp.clip(jnp.dot(x, wu_s[...], preferred_element_type=F32) + bu_ref[...],
                      -SWIGLU_LIMIT, SWIGLU_LIMIT)
        act = hg * jax.nn.sigmoid(SWIGLU_ALPHA * hg) * (hu + 1.0)
        y = jnp.dot(act.astype(BF16), wd_s[...], preferred_element_type=F32) + bd_ref[...]
        o_ref[...] = _pack_row(y)

    @pl.when(jnp.logical_not(used))
    def _():
        o_ref[...] = jnp.zeros(o_ref.shape, o_ref.dtype)


def _moe_experts(block_e, n_valid, n_used, xs, wg, bg, wu, bu, wd, bd):
    p = xs.shape[0]
    tm = MOE_TM
    nb = p // tm

    def row_map(i, be, nv, nu):
        return (jnp.minimum(i, nu[0] - 1), 0)

    def w_map(i, be, nv, nu):
        return (be[jnp.minimum(i, nu[0] - 1)], 0, 0)

    w_spec = pl.BlockSpec((None, D_MODEL, D_MODEL), w_map)
    b_spec = pl.BlockSpec((None, 1, D_MODEL), w_map)
    return pl.pallas_call(
        _moe_kernel,
        out_shape=jax.ShapeDtypeStruct((p, HALF), jnp.int32),
        grid_spec=pltpu.PrefetchScalarGridSpec(
            num_scalar_prefetch=3,
            grid=(nb,),
            in_specs=[
                pl.BlockSpec((tm, HALF), row_map),
                w_spec, b_spec, w_spec, b_spec, w_spec, b_spec,
            ],
            out_specs=pl.BlockSpec((tm, HALF), lambda i, be, nv, nu: (i, 0)),
            scratch_shapes=[pltpu.VMEM((D_MODEL, D_MODEL), BF16)] * 3,
        ),
        compiler_params=pltpu.CompilerParams(
            dimension_semantics=("arbitrary",), vmem_limit_bytes=VMEM_LIMIT),
        name="moe_experts",
    )(block_e, n_valid, n_used, xs, wg, bg, wu, bu, wd, bd)


def _sc_mesh():
    return plsc.VectorSubcoreMesh(core_axis_name="core", subcore_axis_name="subcore")


def _sc_index_rows(dest_t):
    k, n = dest_t.shape
    w = SC_WINDOW
    return jnp.pad(dest_t.reshape(k * n // w, w), ((0, 0), (0, LANES - w)))


def _sc_dispatch(x1, slot_idx, p):
    n, d = x1.shape
    w = SC_WINDOW
    windows = n // w

    @pl.kernel(out_type=jax.ShapeDtypeStruct((p, d), x1.dtype), mesh=_sc_mesh(), scratch_types=[],
               name="sc_dispatch")
    def run(x_hbm, d_hbm, o_hbm):
        def body(x_vmem, *d_vmems):
            for d_vmem in d_vmems:
                pltpu.sync_copy(x_vmem, o_hbm.at[d_vmem.at[0, pl.ds(0, w)]])

        pltpu.emit_pipeline(
            body,
            grid=(windows,),
            in_specs=[pl.BlockSpec((w, d), lambda i: (i, 0))]
            + [pl.BlockSpec((1, LANES), functools.partial(lambda k, i: (k * windows + i, 0), k))
               for k in range(TOP_K)],
            out_specs=[],
            core_axis_name=("core", "subcore"),
            dimension_semantics=(pltpu.PARALLEL,),
        )(x_hbm, *([d_hbm] * TOP_K))

    return run(x1, slot_idx)


def _sc_gather(ys, idx):
    w = SC_WINDOW
    r = idx.shape[0] * w
    d = ys.shape[1]

    @pl.kernel(out_type=jax.ShapeDtypeStruct((r, d), ys.dtype), mesh=_sc_mesh(), scratch_types=[],
               name="sc_gather")
    def run(y_hbm, i_hbm, o_hbm):
        def body(i_vmem, o_vmem):
            pltpu.sync_copy(y_hbm.at[i_vmem.at[0, pl.ds(0, w)]], o_vmem)

        pltpu.emit_pipeline(
            body,
            grid=(r // w,),
            in_specs=[pl.BlockSpec((1, LANES), lambda i: (i, 0))],
            out_specs=[pl.BlockSpec((w, d), lambda i: (i, 0))],
            core_axis_name=("core", "subcore"),
            dimension_semantics=(pltpu.PARALLEL,),
        )(i_hbm, o_hbm)

    return run(ys, idx)


def _ln2_kernel(x1_ref, yg_ref, rt_ref, g_ref, b_ref, o_ref):
    f = jnp.zeros(x1_ref.shape, F32)
    for k in range(TOP_K):
        f = f + _unpack_row(yg_ref[k]) * rt_ref[:, RT_GATE + k:RT_GATE + k + 1]
    o_ref[...] = _layer_norm(DEEPNORM_ALPHA * x1_ref[...] + f, g_ref[...], b_ref[...])


def _combine_ln2(x1, yg, rt, g2, b2):
    n = x1.shape[0]
    tm = LN2_TM
    return pl.pallas_call(
        _ln2_kernel,
        out_shape=jax.ShapeDtypeStruct((n, D_MODEL), F32),
        grid=(n // tm,),
        in_specs=[
            pl.BlockSpec((tm, D_MODEL), lambda i: (i, 0)),
            pl.BlockSpec((TOP_K, tm, HALF), lambda i: (0, i, 0)),
            pl.BlockSpec((tm, ROUTER_COLS), lambda i: (i, 0)),
            pl.BlockSpec((1, D_MODEL), lambda i: (0, 0)),
            pl.BlockSpec((1, D_MODEL), lambda i: (0, 0)),
        ],
        out_specs=pl.BlockSpec((tm, D_MODEL), lambda i: (i, 0)),
        compiler_params=pltpu.CompilerParams(
            dimension_semantics=("parallel",), vmem_limit_bytes=VMEM_LIMIT),
        name="combine_ln2",
    )(x1, yg, rt, g2, b2)


def _rotary_tables(seq):
    inv = ROPE_THETA ** (-jnp.arange(0, ROT_DIM, 2, dtype=F32) / ROT_DIM)
    ang = jnp.arange(seq, dtype=F32)[:, None] * inv[None, :]
    cos, sin = jnp.cos(ang), jnp.sin(ang)
    ones = jnp.ones((seq, QK_DIM - ROT_DIM), F32)
    zeros = jnp.zeros((seq, QK_DIM - ROT_DIM), F32)
    zh = jnp.zeros((seq, ROT_HALF), F32)
    cos64 = jnp.concatenate([cos, cos, ones], axis=1)
    sa64 = jnp.concatenate([-sin, zh, zeros], axis=1)
    sb64 = jnp.concatenate([zh, sin, zeros], axis=1)
    rep = LANES // QK_DIM
    return (jnp.tile(cos64, (1, rep)), jnp.tile(sa64, (1, rep)), jnp.tile(sb64, (1, rep)), cos.T, sin.T)


def _slot_layout(rtt, counts, tm):
    n = rtt.shape[1]
    p = n * TOP_K + N_EXPERTS * tm
    nb = p // tm
    expert = rtt[RT_EXPERT:RT_EXPERT + TOP_K].astype(jnp.int32)
    rank = rtt[RT_RANK:RT_RANK + TOP_K].astype(jnp.int32)
    padded = ((counts + tm - 1) // tm) * tm
    pad_ends = jnp.cumsum(padded)
    pad_starts = pad_ends - padded
    ids = jnp.arange(N_EXPERTS, dtype=jnp.int32)[:, None, None]
    dest_t = jnp.sum(jnp.where(expert[None] == ids, pad_starts[:, None, None], 0), axis=0) + rank
    block_start = jnp.arange(nb, dtype=jnp.int32) * tm
    block_e = jnp.minimum(jnp.sum(pad_ends[None, :] <= block_start[:, None], axis=1),
                          N_EXPERTS - 1).astype(jnp.int32)
    own = block_e[:, None] == jnp.arange(N_EXPERTS, dtype=jnp.int32)[None, :]
    valid_end = jnp.sum(jnp.where(own, (pad_starts + counts)[None, :], 0), axis=1)
    n_valid = jnp.clip(valid_end - block_start, 0, tm).astype(jnp.int32)
    n_used = (pad_ends[-1] // tm).astype(jnp.int32).reshape(1)
    return dest_t.astype(jnp.int32), block_e, n_valid, n_used, p


def _trunk(x, prm):
    b, s, d = x.shape
    n = b * s
    x2d = x.reshape(n, d)
    h, qt, vt = _inproj(x2d, prm["w_in"], prm["w_q_t"], prm["w_v_t"], prm["b_gate"], _rotary_tables(s), b, s)
    a = _attention(h.reshape(b, s, H_COLS), qt, vt, prm["lam"], prm["subln_g"], prm["lam_init"])
    x1, x1p, rt, rtt, cnt = _mix(a.reshape(n, d), h, x2d, s, prm["conv_w"], prm["w_attn_br"],
                                 prm["w_conv_br"], prm["w_o"], prm["ln1_g"], prm["ln1_b"],
                                 prm["w_router_t"], prm["b_router_b"])
    counts = cnt[:, 0].astype(jnp.int32)
    dest_t, block_e, n_valid, n_used, p = _slot_layout(rtt, counts, MOE_TM)
    slot_idx = _sc_index_rows(dest_t)
    xs = _sc_dispatch(x1p, slot_idx, p)
    ys = _moe_experts(block_e, n_valid, n_used, xs, prm["w_exp_gate"], prm["b_exp_gate"],
                      prm["w_exp_up"], prm["b_exp_up"], prm["w_exp_down"], prm["b_exp_down"])
    yg = _sc_gather(ys, slot_idx).reshape(TOP_K, n, HALF)
    y = _combine_ln2(x1, yg, rt, prm["ln2_g"], prm["ln2_b"])
    return y.reshape(b, s, d)


def kernel(x_prompt, x_sample, w_in, b_branch_gate, lambda_q1, lambda_k1, lambda_q2, lambda_k2, subln_g,
           conv_w, w_attn_br, w_conv_br, w_o, ln1_g, ln1_b, w_router, b_router, w_exp_gate, b_exp_gate,
           w_exp_up, b_exp_up, w_exp_down, b_exp_down, ln2_g, ln2_b):
    l = 0
    lam_init = 0.8 - 0.6 * math.exp(-0.3 * l)
    lam = (jnp.exp(jnp.sum(lambda_q1[l].astype(F32) * lambda_k1[l].astype(F32)))
           - jnp.exp(jnp.sum(lambda_q2[l].astype(F32) * lambda_k2[l].astype(F32))) + lam_init)
    prm = {
        "lam_init": lam_init,
        "lam": lam.reshape(1).astype(F32),
        "w_in": w_in[l].astype(BF16),
        "w_q_t": w_in[l][:, Q_STEP * COL_BLOCK:(Q_STEP + 1) * COL_BLOCK].T.astype(BF16),
        "w_v_t": w_in[l][:, V_STEP * COL_BLOCK:(V_STEP + 1) * COL_BLOCK].T.astype(BF16),
        "b_gate": b_branch_gate[l].reshape(1, 2 * D_MODEL),
        "subln_g": subln_g[l].reshape(1, V_DIM),
        "conv_w": jnp.pad(conv_w[l], ((0, SUBLANES - conv_w.shape[1]), (0, 0))),
        "w_attn_br": w_attn_br[l].astype(BF16),
        "w_conv_br": w_conv_br[l].astype(BF16),
        "w_o": w_o[l].astype(BF16),
        "ln1_g": ln1_g[l].reshape(1, D_MODEL),
        "ln1_b": ln1_b[l].reshape(1, D_MODEL),
        "w_router_t": jnp.pad(w_router[l].T, ((0, ROUTER_COLS - N_EXPERTS), (0, 0))).astype(BF16),
        "b_router_b": jnp.broadcast_to(b_router[l].astype(F32)[:, None], (N_EXPERTS, MIX_SUB)),
        "w_exp_gate": w_exp_gate[l],
        "b_exp_gate": b_exp_gate[l].reshape(N_EXPERTS, 1, D_MODEL),
        "w_exp_up": w_exp_up[l],
        "b_exp_up": b_exp_up[l].reshape(N_EXPERTS, 1, D_MODEL),
        "w_exp_down": w_exp_down[l],
        "b_exp_down": b_exp_down[l].reshape(N_EXPERTS, 1, D_MODEL),
        "ln2_g": ln2_g[l].reshape(1, D_MODEL),
        "ln2_b": ln2_b[l].reshape(1, D_MODEL),
    }
    return _trunk(x_prompt, prm), _trunk(x_sample, prm)
```

```python
import functools
import math

import jax
import jax.numpy as jnp
from jax import lax
from jax.experimental import pallas as pl
from jax.experimental.pallas import tpu as pltpu
from jax.experimental.pallas import tpu_sc as plsc

F32 = jnp.float32
BF16 = jnp.bfloat16

D_MODEL = 1024
N_HEADS = 8
QK_DIM = 64
V_DIM = 128
HEAD_COLS = 2 * QK_DIM
ROT_DIM = QK_DIM // 4
ROT_HALF = ROT_DIM // 2
ROPE_THETA = 500000.0
SUBLN_EPS = 1e-5
IN_COLS = 8192
COL_BLOCK = 1024
Q_STEP, K_STEP, V_STEP = 0, 1, 2
GATE_STEP = 6
H_COLS = IN_COLS - 2 * COL_BLOCK
CB_BLK, CC_BLK, CX_BLK, GA_BLK, GC_BLK = 1, 2, 3, 4, 5
K_HEAD0 = 0
N_EXPERTS = 32
TOP_K = 4
SWIGLU_LIMIT = 7.0
SWIGLU_ALPHA = 1.702
LN_EPS = 1e-5
DEPTH = 1
DEEPNORM_ALPHA = (2 * DEPTH) ** 0.25
LANES = 128
SUBLANES = 8
ROUTER_COLS = LANES
RT_EXPERT, RT_RANK, RT_GATE = 0, 4, 8
RT_ROWS = 16
SC_WINDOW = 32
ONES_ROWS = 16
Q_SCALE = math.log2(math.e) * QK_DIM ** -0.5

VMEM_LIMIT = 56 * 1024 * 1024

PROJ_TM = 1024
ATTN_TQ = 1024
ATTN_TK = 512
MIX_TM = 512
MIX_SUB = 256
MOE_TM = 512
LN2_TM = 1024


def _rotate_rows(t1, t2, cos, sin):
    return t1 * cos - t2 * sin, t2 * cos + t1 * sin


def _inproj_kernel(x_ref, w_ref, wqt_ref, wvt_ref, bg_ref, cos_ref, sa_ref, sb_ref, cosr_ref, sinr_ref,
                   o_ref, qt_ref, vt_ref, xb_s):
    j = pl.program_id(1)

    @pl.when(j == 0)
    def _():
        xb_s[...] = x_ref[...].astype(BF16)

    xb = xb_s[...]
    nt = (((1,), (1,)), ((), ()))
    tm = x_ref.shape[0]
    halves = (slice(0, tm // 2), slice(tm // 2, tm))

    @pl.when(j == Q_STEP)
    def _():
        acc_t = lax.dot_general(wqt_ref[...], xb, nt, preferred_element_type=F32) * Q_SCALE
        cos, sin = cosr_ref[...], sinr_ref[...]
        for hh in range(N_HEADS):
            parts = []
            for mp in range(2):
                o = hh * HEAD_COLS + mp * QK_DIM
                parts += _rotate_rows(acc_t[o:o + ROT_HALF], acc_t[o + ROT_HALF:o + ROT_DIM], cos, sin)
                parts.append(acc_t[o + ROT_DIM:o + QK_DIM])
            qt_ref[hh, 0] = jnp.concatenate(parts, axis=0).astype(BF16)

    @pl.when(j == V_STEP)
    def _():
        acc_t = lax.dot_general(wvt_ref[...], xb, nt, preferred_element_type=F32)
        n_heads, n_chunks, _, tk = vt_ref.shape
        for hh in range(n_heads):
            for cc in range(n_chunks):
                vt_ref[hh, cc, :V_DIM, :] = acc_t[hh * V_DIM:(hh + 1) * V_DIM,
                                                  cc * tk:(cc + 1) * tk].astype(BF16)
                vt_ref[hh, cc, V_DIM:, :] = jnp.ones((ONES_ROWS, tk), BF16)

    @pl.when((j > V_STEP) & (j < GATE_STEP))
    def _():
        o_ref[...] = jnp.dot(xb, w_ref[...], preferred_element_type=F32).astype(o_ref.dtype)

    @pl.when(j >= GATE_STEP)
    def _():
        accs = [jnp.dot(xb_s[r, :], w_ref[...], preferred_element_type=F32) for r in halves]
        for r, acc in zip(halves, accs):
            o_ref[r, :] = jax.nn.sigmoid(acc + bg_ref[...]).astype(o_ref.dtype)

    @pl.when(j == K_STEP)
    def _():
        accs = [jnp.dot(xb_s[r, :], w_ref[...], preferred_element_type=F32) for r in halves]
        for r, acc in zip(halves, accs):
            cos, sa, sb = cos_ref[r, :], sa_ref[r, :], sb_ref[r, :]
            for c in range(COL_BLOCK // LANES):
                a = acc[:, c * LANES:(c + 1) * LANES]
                rot = (a * cos + pltpu.roll(a, LANES - ROT_HALF, 1) * sa + pltpu.roll(a, ROT_HALF, 1) * sb)
                o_ref[r, c * LANES:(c + 1) * LANES] = rot.astype(o_ref.dtype)


def _inproj(x2d, w_in_b, wqt_b, wvt_b, b_gate, tables, batch, seq):
    n = x2d.shape[0]
    tm, tk = PROJ_TM, ATTN_TK
    assert tm == ATTN_TQ
    pos_blocks = seq // tm
    cos_t, sa_t, sb_t, cos_r, sin_r = tables
    tab_spec = pl.BlockSpec((tm, LANES), lambda i, j: (i % pos_blocks, 0))
    rtab_spec = pl.BlockSpec((ROT_HALF, tm), lambda i, j: (0, i % pos_blocks))
    wt_spec = pl.BlockSpec((COL_BLOCK, D_MODEL), lambda i, j: (0, 0))
    return pl.pallas_call(
        _inproj_kernel,
        out_shape=(jax.ShapeDtypeStruct((n, H_COLS), BF16),
                   jax.ShapeDtypeStruct((batch, N_HEADS, seq // tm, HEAD_COLS, tm), BF16),
                   jax.ShapeDtypeStruct((batch, N_HEADS, seq // tk, V_DIM + ONES_ROWS, tk), BF16)),
        grid=(n // tm, IN_COLS // COL_BLOCK),
        in_specs=[
            pl.BlockSpec((tm, D_MODEL), lambda i, j: (i, 0)),
            pl.BlockSpec((D_MODEL, COL_BLOCK), lambda i, j: (0, jnp.where(j <= V_STEP, K_STEP, j))),
            wt_spec, wt_spec,
            pl.BlockSpec((1, COL_BLOCK), lambda i, j: (0, jnp.maximum(j - GATE_STEP, 0))),
            tab_spec, tab_spec, tab_spec, rtab_spec, rtab_spec,
        ],
        out_specs=(
            pl.BlockSpec((tm, COL_BLOCK), lambda i, j: (i, jnp.maximum(j - V_STEP, 0))),
            pl.BlockSpec((None, N_HEADS, 1, HEAD_COLS, tm),
                         lambda i, j: (i // pos_blocks, 0, i % pos_blocks, 0, 0)),
            pl.BlockSpec((None, N_HEADS, tm // tk, V_DIM + ONES_ROWS, tk),
                         lambda i, j: (i // pos_blocks, 0, i % pos_blocks, 0, 0)),
        ),
        scratch_shapes=[pltpu.VMEM((tm, D_MODEL), BF16)],
        compiler_params=pltpu.CompilerParams(
            dimension_semantics=("parallel", "arbitrary"), vmem_limit_bytes=VMEM_LIMIT),
        name="inproj",
    )(x2d, w_in_b, wqt_b, wvt_b, b_gate, cos_t, sa_t, sb_t, cos_r, sin_r)


def _attn_kernel(lam_ref, q_ref, k_ref, vt_ref, g_ref, o_ref, qt_s, acc_s, sa0, sa1, sb0, sb1, *, out_scale):
    n_q, _, tq = q_ref.shape
    n_chunks, _, tk = vt_ref.shape
    sa_s, sb_s = (sa0, sa1), (sb0, sb1)

    def load_q(qi):
        zero_half = jnp.zeros((QK_DIM, tq), BF16)
        qt_s[0, :QK_DIM, :] = q_ref[qi, :QK_DIM, :]
        qt_s[0, QK_DIM:, :] = zero_half
        qt_s[1, :QK_DIM, :] = zero_half
        qt_s[1, QK_DIM:, :] = q_ref[qi, QK_DIM:, :]

    def scores(c, s_refs, mp):
        k = k_ref[pl.ds(pl.multiple_of(c * tk, tk), tk), :]
        s = jnp.dot(k, qt_s[mp], preferred_element_type=F32)
        s_refs[mp][...] = s
        return jnp.max(s, axis=0, keepdims=True)

    def update(c, s_refs, mp, m_old, c_max):
        m_new = jnp.maximum(m_old, c_max)
        alpha = jnp.exp2(m_old - m_new)
        p = jnp.exp2(s_refs[mp][...] - m_new).astype(BF16)
        acc_s[mp] = alpha * acc_s[mp] + jnp.dot(vt_ref[c], p, preferred_element_type=F32)
        return m_new

    def step(c, cur, nxt, ms, cur_max, prefetch):
        new_ms, nxt_max = [], []
        for mp in range(2):
            if prefetch:
                nxt_max.append(scores(c + 1, nxt, mp))
            new_ms.append(update(c, cur, mp, ms[mp], cur_max[mp]))
        return tuple(new_ms), tuple(nxt_max)

    def pair(i, carry):
        ms, a_max = carry
        ms, b_max = step(2 * i, sa_s, sb_s, ms, a_max, True)
        return step(2 * i + 1, sb_s, sa_s, ms, b_max, True)

    def finalize(qi):
        lam = lam_ref[0]
        o1 = acc_s[0, :V_DIM, :] / acc_s[0, V_DIM:V_DIM + 1, :]
        o2 = acc_s[1, :V_DIM, :] / acc_s[1, V_DIM:V_DIM + 1, :]
        ot = o1 - lam * o2
        ot = ot * lax.rsqrt(jnp.mean(ot * ot, axis=0, keepdims=True) + SUBLN_EPS)
        rows = pl.ds(pl.multiple_of(qi * tq, tq), tq)
        o_ref[rows, :] = (ot.T * (g_ref[...] * out_scale)).astype(o_ref.dtype)

    def q_block(qi, a_max, last):
        acc_s[...] = jnp.zeros(acc_s.shape, F32)
        m0 = jnp.full((1, tq), -jnp.inf, F32)
        ms, a_max = lax.fori_loop(0, n_chunks // 2 - 1, pair, ((m0, m0), a_max))
        ms, b_max = step(n_chunks - 2, sa_s, sb_s, ms, a_max, True)
        step(n_chunks - 1, sb_s, sa_s, ms, b_max, False)
        if not last:
            load_q(qi + 1)
            a_max = tuple(scores(0, sa_s, mp) for mp in range(2))
        finalize(qi)
        return a_max

    load_q(0)
    a_max = tuple(scores(0, sa_s, mp) for mp in range(2))
    a_max = lax.fori_loop(0, n_q - 1, lambda qi, am: q_block(qi, am, False), a_max)
    q_block(n_q - 1, a_max, True)


def _attention(h3, qt, vt, lam, subln_g, lam_init):
    b, s, _ = h3.shape
    tq, tk = ATTN_TQ, ATTN_TK
    n_chunks = s // tk
    assert n_chunks % 2 == 0 and n_chunks >= 2
    vt_rows = V_DIM + ONES_ROWS
    kern = functools.partial(_attn_kernel, out_scale=1.0 - lam_init)
    return pl.pallas_call(
        kern,
        out_shape=jax.ShapeDtypeStruct((b, s, N_HEADS * V_DIM), BF16),
        grid=(b, N_HEADS),
        in_specs=[
            pl.BlockSpec(memory_space=pltpu.SMEM),
            pl.BlockSpec((None, None, s // tq, HEAD_COLS, tq), lambda bi, hi: (bi, hi, 0, 0, 0)),
            pl.BlockSpec((None, s, HEAD_COLS), lambda bi, hi: (bi, 0, K_HEAD0 + hi)),
            pl.BlockSpec((None, None, n_chunks, vt_rows, tk), lambda bi, hi: (bi, hi, 0, 0, 0)),
            pl.BlockSpec((1, V_DIM), lambda bi, hi: (0, 0)),
        ],
        out_specs=pl.BlockSpec((None, s, V_DIM), lambda bi, hi: (bi, 0, hi)),
        scratch_shapes=[
            pltpu.VMEM((2, HEAD_COLS, tq), BF16),
            pltpu.VMEM((2, vt_rows, tq), F32),
            pltpu.VMEM((tk, tq), F32), pltpu.VMEM((tk, tq), F32),
            pltpu.VMEM((tk, tq), F32), pltpu.VMEM((tk, tq), F32),
        ],
        compiler_params=pltpu.CompilerParams(
            dimension_semantics=("parallel", "arbitrary"),
            vmem_limit_bytes=VMEM_LIMIT),
        name="diff_attention",
    )(lam, qt, h3, vt, subln_g)


HALF = D_MODEL // 2


def _pack_row(x):
    return pltpu.pack_elementwise([x[:, :HALF], x[:, HALF:]], packed_dtype=BF16)


def _unpack_row(w):
    lo = pltpu.unpack_elementwise(w, index=0, packed_dtype=BF16, unpacked_dtype=F32)
    hi = pltpu.unpack_elementwise(w, index=1, packed_dtype=BF16, unpacked_dtype=F32)
    return jnp.concatenate([lo, hi], axis=1)


def _layer_norm(y, g, b):
    mu = jnp.mean(y, axis=-1, keepdims=True)
    d = y - mu
    var = jnp.mean(d * d, axis=-1, keepdims=True)
    return d * lax.rsqrt(var + LN_EPS) * g + b


def _mix_kernel(a_ref, cb_ref, cc_ref, cx_ref, ga_ref, gc_ref, ccp_ref, cxp_ref, ccn_ref, cxn_ref,
                x_ref, cw_ref, wa_ref, wc_ref, wo_ref, g1_ref, b1_ref, wrt_ref, brb_ref, triu_ref,
                x1_ref, x1p_ref, rt_ref, rtt_ref, cnt_ref, *, tiles_per_seq):
    i = pl.program_id(0)

    @pl.when(i == 0)
    def _():
        cnt_ref[...] = jnp.zeros(cnt_ref.shape, F32)

    tm = x_ref.shape[0]
    u = cc_ref[...].astype(F32) * cx_ref[...].astype(F32)
    first = (i % tiles_per_seq) == 0
    last = (i % tiles_per_seq) == tiles_per_seq - 1
    up = ccp_ref[SUBLANES - 1:SUBLANES, :].astype(F32) * cxp_ref[SUBLANES - 1:SUBLANES, :].astype(F32)
    un = ccn_ref[0:1, :].astype(F32) * cxn_ref[0:1, :].astype(F32)
    up = jnp.where(first, 0.0, up)
    un = jnp.where(last, 0.0, un)
    row = lax.broadcasted_iota(jnp.int32, u.shape, 0)
    u_prev = jnp.where(row == 0, up, pltpu.roll(u, 1, 0))
    u_next = jnp.where(row == tm - 1, un, pltpu.roll(u, tm - 1, 0))
    conv = cw_ref[0:1, :] * u_prev + cw_ref[1:2, :] * u + cw_ref[2:3, :] * u_next
    c = (cb_ref[...].astype(F32) * conv).astype(BF16)

    sub = triu_ref.shape[0]
    counts = cnt_ref[:, 0:1]
    tiles = [slice(s * sub, (s + 1) * sub) for s in range(tm // sub)]
    branches = [(jnp.dot(a_ref[rows, :], wa_ref[...], preferred_element_type=F32),
                 jnp.dot(c[rows, :], wc_ref[...], preferred_element_type=F32)) for rows in tiles]
    mixed = []
    for rows, (attn_b, conv_b) in zip(tiles, branches):
        merged = ga_ref[rows, :].astype(F32) * attn_b + gc_ref[rows, :].astype(F32) * conv_b
        mixed.append(jnp.dot(merged.astype(BF16), wo_ref[...], preferred_element_type=F32))
    normed = []
    for rows, m in zip(tiles, mixed):
        x1 = _layer_norm(DEEPNORM_ALPHA * x_ref[rows, :] + m, g1_ref[...], b1_ref[...])
        x1_ref[rows, :] = x1
        x1p_ref[rows, :] = _pack_row(x1)
        normed.append(x1)
    for rows, x1 in zip(tiles, normed):
        nt = (((1,), (1,)), ((), ()))
        logits_t = lax.dot_general(wrt_ref[...], x1.astype(BF16), nt, preferred_element_type=F32)
        work = logits_t[:N_EXPERTS, :] + brb_ref[...]

        row_f = lax.broadcasted_iota(jnp.int32, work.shape, 0).astype(F32)
        hots, vals, idxs = [], [], []
        for _ in range(TOP_K):
            mx = jnp.max(work, axis=0, keepdims=True)
            ix = jnp.min(jnp.where(work == mx, row_f, float(N_EXPERTS)), axis=0, keepdims=True)
            hot = row_f == ix
            work = jnp.where(hot, -jnp.inf, work)
            hots.append(hot)
            vals.append(mx)
            idxs.append(ix)
        exps = [jnp.exp(v - vals[0]) for v in vals]
        denom = exps[0] + exps[1] + exps[2] + exps[3]
        picked = jnp.zeros(work.shape, F32)
        for hot in hots:
            picked = picked + hot.astype(F32)
        before = counts + jnp.dot(picked.astype(BF16), triu_ref[...], preferred_element_type=F32)
        ranks = [jnp.sum(jnp.where(hot, before, 0.0), axis=0, keepdims=True) for hot in hots]
        fields = idxs + ranks + [e / denom for e in exps]
        rec_row = lax.broadcasted_iota(jnp.int32, (RT_ROWS, sub), 0)
        rec = jnp.zeros((RT_ROWS, sub), F32)
        for r, val in enumerate(fields):
            rec = jnp.where(rec_row == r, val, rec)
        rtt_ref[:, rows] = rec
        rt_ref[rows, :] = jnp.concatenate([rec, jnp.zeros((ROUTER_COLS - RT_ROWS, sub), F32)], axis=0).T
        counts = counts + jnp.sum(picked, axis=1, keepdims=True)
    cnt_ref[...] = jnp.broadcast_to(counts, cnt_ref.shape)


def _mix(a2d, h2d, x2d, seq, cw, wa, wc, wo, g1, b1, wr_t, br_b):
    n = x2d.shape[0]
    tm = MIX_TM
    tiles_per_seq = seq // tm
    halo_blocks = tm // SUBLANES
    n_halo = n // SUBLANES

    def col(blk):
        return pl.BlockSpec((tm, COL_BLOCK), lambda i: (i, blk))

    def prev(blk):
        return pl.BlockSpec((SUBLANES, COL_BLOCK), lambda i: (jnp.maximum(i * halo_blocks - 1, 0), blk))

    def nxt(blk):
        return pl.BlockSpec((SUBLANES, COL_BLOCK),
                            lambda i: (jnp.minimum((i + 1) * halo_blocks, n_halo - 1), blk))

    def full(shape):
        return pl.BlockSpec(shape, lambda i: (0,) * len(shape))

    sub = MIX_SUB
    tri_u = (lax.broadcasted_iota(jnp.int32, (sub, sub), 0)
             < lax.broadcasted_iota(jnp.int32, (sub, sub), 1)).astype(BF16)
    kern = functools.partial(_mix_kernel, tiles_per_seq=tiles_per_seq)
    return pl.pallas_call(
        kern,
        out_shape=(jax.ShapeDtypeStruct((n, D_MODEL), F32),
                   jax.ShapeDtypeStruct((n, HALF), jnp.int32),
                   jax.ShapeDtypeStruct((n, ROUTER_COLS), F32),
                   jax.ShapeDtypeStruct((RT_ROWS, n), F32),
                   jax.ShapeDtypeStruct((N_EXPERTS, LANES), F32)),
        grid=(n // tm,),
        in_specs=[
            pl.BlockSpec((tm, D_MODEL), lambda i: (i, 0)),
            col(CB_BLK), col(CC_BLK), col(CX_BLK), col(GA_BLK), col(GC_BLK),
            prev(CC_BLK), prev(CX_BLK), nxt(CC_BLK), nxt(CX_BLK),
            pl.BlockSpec((tm, D_MODEL), lambda i: (i, 0)),
            full((SUBLANES, D_MODEL)),
            full((D_MODEL, D_MODEL)), full((D_MODEL, D_MODEL)), full((D_MODEL, D_MODEL)),
            full((1, D_MODEL)), full((1, D_MODEL)),
            full((ROUTER_COLS, D_MODEL)), full((N_EXPERTS, sub)), full((sub, sub)),
        ],
        out_specs=(pl.BlockSpec((tm, D_MODEL), lambda i: (i, 0)),
                   pl.BlockSpec((tm, HALF), lambda i: (i, 0)),
                   pl.BlockSpec((tm, ROUTER_COLS), lambda i: (i, 0)),
                   pl.BlockSpec((RT_ROWS, tm), lambda i: (0, i)),
                   pl.BlockSpec((N_EXPERTS, LANES), lambda i: (0, 0))),
        compiler_params=pltpu.CompilerParams(
            dimension_semantics=("arbitrary",), vmem_limit_bytes=VMEM_LIMIT),
        name="mix_ln1_router",
    )(a2d, h2d, h2d, h2d, h2d, h2d, h2d, h2d, h2d, h2d, x2d, cw, wa, wc, wo, g1, b1, wr_t, br_b, tri_u)


def _moe_kernel(be_ref, nv_ref, nu_ref, xs_ref, wg_ref, bg_ref, wu_ref, bu_ref, wd_ref, bd_ref, o_ref,
                wg_s, wu_s, wd_s):
    i = pl.program_id(0)
    used = i < nu_ref[0]

    @pl.when(used & ((i == 0) | (be_ref[i] != be_ref[jnp.maximum(i - 1, 0)])))
    def _():
        wg_s[...] = wg_ref[...].astype(BF16)
        wu_s[...] = wu_ref[...].astype(BF16)
        wd_s[...] = wd_ref[...].astype(BF16)

    @pl.when(used)
    def _():
        row = lax.broadcasted_iota(jnp.int32, xs_ref.shape, 0)
        x = _unpack_row(jnp.where(row < nv_ref[i], xs_ref[...], 0)).astype(BF16)
        hg = jnp.minimum(jnp.dot(x, wg_s[...], preferred_element_type=F32) + bg_ref[...], SWIGLU_LIMIT)
        hu = jnp.clip(jnp.dot(x, wu_s[...], preferred_element_type=F32) + bu_ref[...],
                      -SWIGLU_LIMIT, SWIGLU_LIMIT)
        act = hg * jax.nn.sigmoid(SWIGLU_ALPHA * hg) * (hu + 1.0)
        y = jnp.dot(act.astype(BF16), wd_s[...], preferred_element_type=F32) + bd_ref[...]
        o_ref[...] = _pack_row(y)

    @pl.when(jnp.logical_not(used))
    def _():
        o_ref[...] = jnp.zeros(o_ref.shape, o_ref.dtype)


def _moe_experts(block_e, n_valid, n_used, xs, wg, bg, wu, bu, wd, bd):
    p = xs.shape[0]
    tm = MOE_TM
    nb = p // tm

    def row_map(i, be, nv, nu):
        return (jnp.minimum(i, nu[0] - 1), 0)

    def w_map(i, be, nv, nu):
        return (be[jnp.minimum(i, nu[0] - 1)], 0, 0)

    w_spec = pl.BlockSpec((None, D_MODEL, D_MODEL), w_map)
    b_spec = pl.BlockSpec((None, 1, D_MODEL), w_map)
    return pl.pallas_call(
        _moe_kernel,
        out_shape=jax.ShapeDtypeStruct((p, HALF), jnp.int32),
        grid_spec=pltpu.PrefetchScalarGridSpec(
            num_scalar_prefetch=3,
            grid=(nb,),
            in_specs=[
                pl.BlockSpec((tm, HALF), row_map),
                w_spec, b_spec, w_spec, b_spec, w_spec, b_spec,
            ],
            out_specs=pl.BlockSpec((tm, HALF), lambda i, be, nv, nu: (i, 0)),
            scratch_shapes=[pltpu.VMEM((D_MODEL, D_MODEL), BF16)] * 3,
        ),
        compiler_params=pltpu.CompilerParams(
            dimension_semantics=("arbitrary",), vmem_limit_bytes=VMEM_LIMIT),
        name="moe_experts",
    )(block_e, n_valid, n_used, xs, wg, bg, wu, bu, wd, bd)


def _sc_mesh():
    return plsc.VectorSubcoreMesh(core_axis_name="core", subcore_axis_name="subcore")


def _sc_index_rows(dest_t):
    k, n = dest_t.shape
    w = SC_WINDOW
    return jnp.pad(dest_t.reshape(k * n // w, w), ((0, 0), (0, LANES - w)))


def _sc_dispatch(x1, slot_idx, p):
    n, d = x1.shape
    w = SC_WINDOW
    windows = n // w

    @pl.kernel(out_type=jax.ShapeDtypeStruct((p, d), x1.dtype), mesh=_sc_mesh(), scratch_types=[],
               name="sc_dispatch")
    def run(x_hbm, d_hbm, o_hbm):
        def body(x_vmem, *d_vmems):
            for d_vmem in d_vmems:
                pltpu.sync_copy(x_vmem, o_hbm.at[d_vmem.at[0, pl.ds(0, w)]])

        pltpu.emit_pipeline(
            body,
            grid=(windows,),
            in_specs=[pl.BlockSpec((w, d), lambda i: (i, 0))]
            + [pl.BlockSpec((1, LANES), functools.partial(lambda k, i: (k * windows + i, 0), k))
               for k in range(TOP_K)],
            out_specs=[],
            core_axis_name=("core", "subcore"),
            dimension_semantics=(pltpu.PARALLEL,),
        )(x_hbm, *([d_hbm] * TOP_K))

    return run(x1, slot_idx)


def _sc_gather(ys, idx):
    w = SC_WINDOW
    r = idx.shape[0] * w
    d = ys.shape[1]

    @pl.kernel(out_type=jax.ShapeDtypeStruct((r, d), ys.dtype), mesh=_sc_mesh(), scratch_types=[],
               name="sc_gather")
    def run(y_hbm, i_hbm, o_hbm):
        def body(i_vmem, o_vmem):
            pltpu.sync_copy(y_hbm.at[i_vmem.at[0, pl.ds(0, w)]], o_vmem)

        pltpu.emit_pipeline(
            body,
            grid=(r // w,),
            in_specs=[pl.BlockSpec((1, LANES), lambda i: (i, 0))],
            out_specs=[pl.BlockSpec((w, d), lambda i: (i, 0))],
            core_axis_name=("core", "subcore"),
            dimension_semantics=(pltpu.PARALLEL,),
        )(i_hbm, o_hbm)

    return run(ys, idx)


def _ln2_kernel(x1_ref, yg_ref, rt_ref, g_ref, b_ref, o_ref):
    f = jnp.zeros(x1_ref.shape, F32)
    for k in range(TOP_K):
        f = f + _unpack_row(yg_ref[k]) * rt_ref[:, RT_GATE + k:RT_GATE + k + 1]
    o_ref[...] = _layer_norm(DEEPNORM_ALPHA * x1_ref[...] + f, g_ref[...], b_ref[...])


def _combine_ln2(x1, yg, rt, g2, b2):
    n = x1.shape[0]
    tm = LN2_TM
    return pl.pallas_call(
        _ln2_kernel,
        out_shape=jax.ShapeDtypeStruct((n, D_MODEL), F32),
        grid=(n // tm,),
        in_specs=[
            pl.BlockSpec((tm, D_MODEL), lambda i: (i, 0)),
            pl.BlockSpec((TOP_K, tm, HALF), lambda i: (0, i, 0)),
            pl.BlockSpec((tm, ROUTER_COLS), lambda i: (i, 0)),
            pl.BlockSpec((1, D_MODEL), lambda i: (0, 0)),
            pl.BlockSpec((1, D_MODEL), lambda i: (0, 0)),
        ],
        out_specs=pl.BlockSpec((tm, D_MODEL), lambda i: (i, 0)),
        compiler_params=pltpu.CompilerParams(
            dimension_semantics=("parallel",), vmem_limit_bytes=VMEM_LIMIT),
        name="combine_ln2",
    )(x1, yg, rt, g2, b2)


def _rotary_tables(seq):
    inv = ROPE_THETA ** (-jnp.arange(0, ROT_DIM, 2, dtype=F32) / ROT_DIM)
    ang = jnp.arange(seq, dtype=F32)[:, None] * inv[None, :]
    cos, sin = jnp.cos(ang), jnp.sin(ang)
    ones = jnp.ones((seq, QK_DIM - ROT_DIM), F32)
    zeros = jnp.zeros((seq, QK_DIM - ROT_DIM), F32)
    zh = jnp.zeros((seq, ROT_HALF), F32)
    cos64 = jnp.concatenate([cos, cos, ones], axis=1)
    sa64 = jnp.concatenate([-sin, zh, zeros], axis=1)
    sb64 = jnp.concatenate([zh, sin, zeros], axis=1)
    rep = LANES // QK_DIM
    return (jnp.tile(cos64, (1, rep)), jnp.tile(sa64, (1, rep)), jnp.tile(sb64, (1, rep)), cos.T, sin.T)


def _slot_layout(rtt, counts, tm):
    n = rtt.shape[1]
    p = n * TOP_K + N_EXPERTS * tm
    nb = p // tm
    expert = rtt[RT_EXPERT:RT_EXPERT + TOP_K].astype(jnp.int32)
    rank = rtt[RT_RANK:RT_RANK + TOP_K].astype(jnp.int32)
    padded = ((counts + tm - 1) // tm) * tm
    pad_ends = jnp.cumsum(padded)
    pad_starts = pad_ends - padded
    ids = jnp.arange(N_EXPERTS, dtype=jnp.int32)[:, None, None]
    dest_t = jnp.sum(jnp.where(expert[None] == ids, pad_starts[:, None, None], 0), axis=0) + rank
    block_start = jnp.arange(nb, dtype=jnp.int32) * tm
    block_e = jnp.minimum(jnp.sum(pad_ends[None, :] <= block_start[:, None], axis=1),
                          N_EXPERTS - 1).astype(jnp.int32)
    own = block_e[:, None] == jnp.arange(N_EXPERTS, dtype=jnp.int32)[None, :]
    valid_end = jnp.sum(jnp.where(own, (pad_starts + counts)[None, :], 0), axis=1)
    n_valid = jnp.clip(valid_end - block_start, 0, tm).astype(jnp.int32)
    n_used = (pad_ends[-1] // tm).astype(jnp.int32).reshape(1)
    return dest_t.astype(jnp.int32), block_e, n_valid, n_used, p


def _trunk(x, prm):
    b, s, d = x.shape
    n = b * s
    x2d = x.reshape(n, d)
    h, qt, vt = _inproj(x2d, prm["w_in"], prm["w_q_t"], prm["w_v_t"], prm["b_gate"], _rotary_tables(s), b, s)
    a = _attention(h.reshape(b, s, H_COLS), qt, vt, prm["lam"], prm["subln_g"], prm["lam_init"])
    x1, x1p, rt, rtt, cnt = _mix(a.reshape(n, d), h, x2d, s, prm["conv_w"], prm["w_attn_br"],
                                 prm["w_conv_br"], prm["w_o"], prm["ln1_g"], prm["ln1_b"],
                                 prm["w_router_t"], prm["b_router_b"])
    counts = cnt[:, 0].astype(jnp.int32)
    dest_t, block_e, n_valid, n_used, p = _slot_layout(rtt, counts, MOE_TM)
    slot_idx = _sc_index_rows(dest_t)
    xs = _sc_dispatch(x1p, slot_idx, p)
    ys = _moe_experts(block_e, n_valid, n_used, xs, prm["w_exp_gate"], prm["b_exp_gate"],
                      prm["w_exp_up"], prm["b_exp_up"], prm["w_exp_down"], prm["b_exp_down"])
    yg = _sc_gather(ys, slot_idx).reshape(TOP_K, n, HALF)
    y = _combine_ln2(x1, yg, rt, prm["ln2_g"], prm["ln2_b"])
    return y.reshape(b, s, d)


def kernel(x_prompt, x_sample, w_in, b_branch_gate, lambda_q1, lambda_k1, lambda_q2, lambda_k2, subln_g,
           conv_w, w_attn_br, w_conv_br, w_o, ln1_g, ln1_b, w_router, b_router, w_exp_gate, b_exp_gate,
           w_exp_up, b_exp_up, w_exp_down, b_exp_down, ln2_g, ln2_b):
    l = 0
    lam_init = 0.8 - 0.6 * math.exp(-0.3 * l)
    lam = (jnp.exp(jnp.sum(lambda_q1[l].astype(F32) * lambda_k1[l].astype(F32)))
           - jnp.exp(jnp.sum(lambda_q2[l].astype(F32) * lambda_k2[l].astype(F32))) + lam_init)
    prm = {
        "lam_init": lam_init,
        "lam": lam.reshape(1).astype(F32),
        "w_in": w_in[l].astype(BF16),
        "w_q_t": w_in[l][:, Q_STEP * COL_BLOCK:(Q_STEP + 1) * COL_BLOCK].T.astype(BF16),
        "w_v_t": w_in[l][:, V_STEP * COL_BLOCK:(V_STEP + 1) * COL_BLOCK].T.astype(BF16),
        "b_gate": b_branch_gate[l].reshape(1, 2 * D_MODEL),
        "subln_g": subln_g[l].reshape(1, V_DIM),
        "conv_w": jnp.pad(conv_w[l], ((0, SUBLANES - conv_w.shape[1]), (0, 0))),
        "w_attn_br": w_attn_br[l].astype(BF16),
        "w_conv_br": w_conv_br[l].astype(BF16),
        "w_o": w_o[l].astype(BF16),
        "ln1_g": ln1_g[l].reshape(1, D_MODEL),
        "ln1_b": ln1_b[l].reshape(1, D_MODEL),
        "w_router_t": jnp.pad(w_router[l].T, ((0, ROUTER_COLS - N_EXPERTS), (0, 0))).astype(BF16),
        "b_router_b": jnp.broadcast_to(b_router[l].astype(F32)[:, None], (N_EXPERTS, MIX_SUB)),
        "w_exp_gate": w_exp_gate[l],
        "b_exp_gate": b_exp_gate[l].reshape(N_EXPERTS, 1, D_MODEL),
        "w_exp_up": w_exp_up[l],
        "b_exp_up": b_exp_up[l].reshape(N_EXPERTS, 1, D_MODEL),
        "w_exp_down": w_exp_down[l],
        "b_exp_down": b_exp_down[l].reshape(N_EXPERTS, 1, D_MODEL),
        "ln2_g": ln2_g[l].reshape(1, D_MODEL),
        "ln2_b": ln2_b[l].reshape(1, D_MODEL),
    }
    y_sample = _trunk(x_sample, prm)
    y_prompt = _trunk(x_prompt, prm)
    return y_prompt, y_sample
```

```python
import functools
import math

import jax
import jax.numpy as jnp
from jax import lax
from jax.experimental import pallas as pl
from jax.experimental.pallas import tpu as pltpu
from jax.experimental.pallas import tpu_sc as plsc

F32 = jnp.float32
BF16 = jnp.bfloat16

D_MODEL = 1024
N_HEADS = 8
QK_DIM = 64
V_DIM = 128
HEAD_COLS = 2 * QK_DIM
ROT_DIM = QK_DIM // 4
ROT_HALF = ROT_DIM // 2
ROPE_THETA = 500000.0
SUBLN_EPS = 1e-5
IN_COLS = 8192
COL_BLOCK = 1024
Q_STEP, K_STEP, V_STEP = 0, 1, 2
GATE_STEP = 6
H_COLS = IN_COLS - 2 * COL_BLOCK
CB_BLK, CC_BLK, CX_BLK, GA_BLK, GC_BLK = 1, 2, 3, 4, 5
K_HEAD0 = 0
N_EXPERTS = 32
TOP_K = 4
SWIGLU_LIMIT = 7.0
SWIGLU_ALPHA = 1.702
LN_EPS = 1e-5
DEPTH = 1
DEEPNORM_ALPHA = (2 * DEPTH) ** 0.25
LANES = 128
SUBLANES = 8
ROUTER_COLS = LANES
RT_EXPERT, RT_RANK, RT_GATE = 0, 4, 8
RT_ROWS = 16
SC_WINDOW = 64
ONES_ROWS = 16
Q_SCALE = math.log2(math.e) * QK_DIM ** -0.5

VMEM_LIMIT = 56 * 1024 * 1024

PROJ_TM = 1024
ATTN_TQ = 1024
ATTN_TK = 512
MIX_TM = 512
MIX_SUB = 256
MOE_TM = 512
LN2_TM = 1024


def _rotate_rows(t1, t2, cos, sin):
    return t1 * cos - t2 * sin, t2 * cos + t1 * sin


def _inproj_kernel(x_ref, w_ref, wqt_ref, wvt_ref, bg_ref, cos_ref, sa_ref, sb_ref, cosr_ref, sinr_ref,
                   o_ref, qt_ref, vt_ref, xb_s):
    j = pl.program_id(1)

    @pl.when(j == 0)
    def _():
        xb_s[...] = x_ref[...].astype(BF16)

    xb = xb_s[...]
    nt = (((1,), (1,)), ((), ()))
    tm = x_ref.shape[0]
    halves = (slice(0, tm // 2), slice(tm // 2, tm))

    @pl.when(j == Q_STEP)
    def _():
        acc_t = lax.dot_general(wqt_ref[...], xb, nt, preferred_element_type=F32) * Q_SCALE
        cos, sin = cosr_ref[...], sinr_ref[...]
        for hh in range(N_HEADS):
            parts = []
            for mp in range(2):
                o = hh * HEAD_COLS + mp * QK_DIM
                parts += _rotate_rows(acc_t[o:o + ROT_HALF], acc_t[o + ROT_HALF:o + ROT_DIM], cos, sin)
                parts.append(acc_t[o + ROT_DIM:o + QK_DIM])
            qt_ref[hh, 0] = jnp.concatenate(parts, axis=0).astype(BF16)

    @pl.when(j == V_STEP)
    def _():
        acc_t = lax.dot_general(wvt_ref[...], xb, nt, preferred_element_type=F32)
        n_heads, n_chunks, _, tk = vt_ref.shape
        for hh in range(n_heads):
            for cc in range(n_chunks):
                vt_ref[hh, cc, :V_DIM, :] = acc_t[hh * V_DIM:(hh + 1) * V_DIM,
                                                  cc * tk:(cc + 1) * tk].astype(BF16)
                vt_ref[hh, cc, V_DIM:, :] = jnp.ones((ONES_ROWS, tk), BF16)

    @pl.when((j > V_STEP) & (j < GATE_STEP))
    def _():
        o_ref[...] = jnp.dot(xb, w_ref[...], preferred_element_type=F32).astype(o_ref.dtype)

    @pl.when(j >= GATE_STEP)
    def _():
        accs = [jnp.dot(xb_s[r, :], w_ref[...], preferred_element_type=F32) for r in halves]
        for r, acc in zip(halves, accs):
            o_ref[r, :] = jax.nn.sigmoid(acc + bg_ref[...]).astype(o_ref.dtype)

    @pl.when(j == K_STEP)
    def _():
        accs = [jnp.dot(xb_s[r, :], w_ref[...], preferred_element_type=F32) for r in halves]
        for r, acc in zip(halves, accs):
            cos, sa, sb = cos_ref[r, :], sa_ref[r, :], sb_ref[r, :]
            for c in range(COL_BLOCK // LANES):
                a = acc[:, c * LANES:(c + 1) * LANES]
                rot = (a * cos + pltpu.roll(a, LANES - ROT_HALF, 1) * sa + pltpu.roll(a, ROT_HALF, 1) * sb)
                o_ref[r, c * LANES:(c + 1) * LANES] = rot.astype(o_ref.dtype)


def _inproj(x2d, w_in_b, wqt_b, wvt_b, b_gate, tables, batch, seq):
    n = x2d.shape[0]
    tm, tk = PROJ_TM, ATTN_TK
    assert tm == ATTN_TQ
    pos_blocks = seq // tm
    cos_t, sa_t, sb_t, cos_r, sin_r = tables
    tab_spec = pl.BlockSpec((tm, LANES), lambda i, j: (i % pos_blocks, 0))
    rtab_spec = pl.BlockSpec((ROT_HALF, tm), lambda i, j: (0, i % pos_blocks))
    wt_spec = pl.BlockSpec((COL_BLOCK, D_MODEL), lambda i, j: (0, 0))
    return pl.pallas_call(
        _inproj_kernel,
        out_shape=(jax.ShapeDtypeStruct((n, H_COLS), BF16),
                   jax.ShapeDtypeStruct((batch, N_HEADS, seq // tm, HEAD_COLS, tm), BF16),
                   jax.ShapeDtypeStruct((batch, N_HEADS, seq // tk, V_DIM + ONES_ROWS, tk), BF16)),
        grid=(n // tm, IN_COLS // COL_BLOCK),
        in_specs=[
            pl.BlockSpec((tm, D_MODEL), lambda i, j: (i, 0)),
            pl.BlockSpec((D_MODEL, COL_BLOCK), lambda i, j: (0, jnp.where(j <= V_STEP, K_STEP, j))),
            wt_spec, wt_spec,
            pl.BlockSpec((1, COL_BLOCK), lambda i, j: (0, jnp.maximum(j - GATE_STEP, 0))),
            tab_spec, tab_spec, tab_spec, rtab_spec, rtab_spec,
        ],
        out_specs=(
            pl.BlockSpec((tm, COL_BLOCK), lambda i, j: (i, jnp.maximum(j - V_STEP, 0))),
            pl.BlockSpec((None, N_HEADS, 1, HEAD_COLS, tm),
                         lambda i, j: (i // pos_blocks, 0, i % pos_blocks, 0, 0)),
            pl.BlockSpec((None, N_HEADS, tm // tk, V_DIM + ONES_ROWS, tk),
                         lambda i, j: (i // pos_blocks, 0, i % pos_blocks, 0, 0)),
        ),
        scratch_shapes=[pltpu.VMEM((tm, D_MODEL), BF16)],
        compiler_params=pltpu.CompilerParams(
            dimension_semantics=("parallel", "arbitrary"), vmem_limit_bytes=VMEM_LIMIT),
        name="inproj",
    )(x2d, w_in_b, wqt_b, wvt_b, b_gate, cos_t, sa_t, sb_t, cos_r, sin_r)


def _attn_kernel(lam_ref, q_ref, k_ref, vt_ref, g_ref, o_ref, qt_s, acc_s, sa0, sa1, sb0, sb1, *, out_scale):
    n_q, _, tq = q_ref.shape
    n_chunks, _, tk = vt_ref.shape
    sa_s, sb_s = (sa0, sa1), (sb0, sb1)

    def load_q(qi):
        zero_half = jnp.zeros((QK_DIM, tq), BF16)
        qt_s[0, :QK_DIM, :] = q_ref[qi, :QK_DIM, :]
        qt_s[0, QK_DIM:, :] = zero_half
        qt_s[1, :QK_DIM, :] = zero_half
        qt_s[1, QK_DIM:, :] = q_ref[qi, QK_DIM:, :]

    def scores(c, s_refs, mp):
        k = k_ref[pl.ds(pl.multiple_of(c * tk, tk), tk), :]
        s = jnp.dot(k, qt_s[mp], preferred_element_type=F32)
        s_refs[mp][...] = s
        return jnp.max(s, axis=0, keepdims=True)

    def update(c, s_refs, mp, m_old, c_max):
        m_new = jnp.maximum(m_old, c_max)
        alpha = jnp.exp2(m_old - m_new)
        p = jnp.exp2(s_refs[mp][...] - m_new).astype(BF16)
        acc_s[mp] = alpha * acc_s[mp] + jnp.dot(vt_ref[c], p, preferred_element_type=F32)
        return m_new

    def step(c, cur, nxt, ms, cur_max, prefetch):
        new_ms, nxt_max = [], []
        for mp in range(2):
            if prefetch:
                nxt_max.append(scores(c + 1, nxt, mp))
            new_ms.append(update(c, cur, mp, ms[mp], cur_max[mp]))
        return tuple(new_ms), tuple(nxt_max)

    def pair(i, carry):
        ms, a_max = carry
        ms, b_max = step(2 * i, sa_s, sb_s, ms, a_max, True)
        return step(2 * i + 1, sb_s, sa_s, ms, b_max, True)

    def finalize(qi):
        lam = lam_ref[0]
        o1 = acc_s[0, :V_DIM, :] / acc_s[0, V_DIM:V_DIM + 1, :]
        o2 = acc_s[1, :V_DIM, :] / acc_s[1, V_DIM:V_DIM + 1, :]
        ot = o1 - lam * o2
        ot = ot * lax.rsqrt(jnp.mean(ot * ot, axis=0, keepdims=True) + SUBLN_EPS)
        rows = pl.ds(pl.multiple_of(qi * tq, tq), tq)
        o_ref[rows, :] = (ot.T * (g_ref[...] * out_scale)).astype(o_ref.dtype)

    def q_block(qi, a_max, last):
        acc_s[...] = jnp.zeros(acc_s.shape, F32)
        m0 = jnp.full((1, tq), -jnp.inf, F32)
        ms, a_max = lax.fori_loop(0, n_chunks // 2 - 1, pair, ((m0, m0), a_max))
        ms, b_max = step(n_chunks - 2, sa_s, sb_s, ms, a_max, True)
        step(n_chunks - 1, sb_s, sa_s, ms, b_max, False)
        if not last:
            load_q(qi + 1)
            a_max = tuple(scores(0, sa_s, mp) for mp in range(2))
        finalize(qi)
        return a_max

    load_q(0)
    a_max = tuple(scores(0, sa_s, mp) for mp in range(2))
    a_max = lax.fori_loop(0, n_q - 1, lambda qi, am: q_block(qi, am, False), a_max)
    q_block(n_q - 1, a_max, True)


def _attention(h3, qt, vt, lam, subln_g, lam_init):
    b, s, _ = h3.shape
    tq, tk = ATTN_TQ, ATTN_TK
    n_chunks = s // tk
    assert n_chunks % 2 == 0 and n_chunks >= 2
    vt_rows = V_DIM + ONES_ROWS
    kern = functools.partial(_attn_kernel, out_scale=1.0 - lam_init)
    return pl.pallas_call(
        kern,
        out_shape=jax.ShapeDtypeStruct((b, s, N_HEADS * V_DIM), BF16),
        grid=(b, N_HEADS),
        in_specs=[
            pl.BlockSpec(memory_space=pltpu.SMEM),
            pl.BlockSpec((None, None, s // tq, HEAD_COLS, tq), lambda bi, hi: (bi, hi, 0, 0, 0)),
            pl.BlockSpec((None, s, HEAD_COLS), lambda bi, hi: (bi, 0, K_HEAD0 + hi)),
            pl.BlockSpec((None, None, n_chunks, vt_rows, tk), lambda bi, hi: (bi, hi, 0, 0, 0)),
            pl.BlockSpec((1, V_DIM), lambda bi, hi: (0, 0)),
        ],
        out_specs=pl.BlockSpec((None, s, V_DIM), lambda bi, hi: (bi, 0, hi)),
        scratch_shapes=[
            pltpu.VMEM((2, HEAD_COLS, tq), BF16),
            pltpu.VMEM((2, vt_rows, tq), F32),
            pltpu.VMEM((tk, tq), F32), pltpu.VMEM((tk, tq), F32),
            pltpu.VMEM((tk, tq), F32), pltpu.VMEM((tk, tq), F32),
        ],
        compiler_params=pltpu.CompilerParams(
            dimension_semantics=("parallel", "arbitrary"),
            vmem_limit_bytes=VMEM_LIMIT),
        name="diff_attention",
    )(lam, qt, h3, vt, subln_g)


HALF = D_MODEL // 2


def _pack_row(x):
    return pltpu.pack_elementwise([x[:, :HALF], x[:, HALF:]], packed_dtype=BF16)


def _unpack_row(w):
    lo = pltpu.unpack_elementwise(w, index=0, packed_dtype=BF16, unpacked_dtype=F32)
    hi = pltpu.unpack_elementwise(w, index=1, packed_dtype=BF16, unpacked_dtype=F32)
    return jnp.concatenate([lo, hi], axis=1)


def _layer_norm(y, g, b):
    mu = jnp.mean(y, axis=-1, keepdims=True)
    d = y - mu
    var = jnp.mean(d * d, axis=-1, keepdims=True)
    return d * lax.rsqrt(var + LN_EPS) * g + b


def _mix_kernel(a_ref, cb_ref, cc_ref, cx_ref, ga_ref, gc_ref, ccp_ref, cxp_ref, ccn_ref, cxn_ref,
                x_ref, cw_ref, wa_ref, wc_ref, wo_ref, g1_ref, b1_ref, wrt_ref, brb_ref, triu_ref,
                x1_ref, x1p_ref, rt_ref, rtt_ref, cnt_ref, *, tiles_per_seq):
    i = pl.program_id(0)

    @pl.when(i == 0)
    def _():
        cnt_ref[...] = jnp.zeros(cnt_ref.shape, F32)

    tm = x_ref.shape[0]
    u = cc_ref[...].astype(F32) * cx_ref[...].astype(F32)
    first = (i % tiles_per_seq) == 0
    last = (i % tiles_per_seq) == tiles_per_seq - 1
    up = ccp_ref[SUBLANES - 1:SUBLANES, :].astype(F32) * cxp_ref[SUBLANES - 1:SUBLANES, :].astype(F32)
    un = ccn_ref[0:1, :].astype(F32) * cxn_ref[0:1, :].astype(F32)
    up = jnp.where(first, 0.0, up)
    un = jnp.where(last, 0.0, un)
    row = lax.broadcasted_iota(jnp.int32, u.shape, 0)
    u_prev = jnp.where(row == 0, up, pltpu.roll(u, 1, 0))
    u_next = jnp.where(row == tm - 1, un, pltpu.roll(u, tm - 1, 0))
    conv = cw_ref[0:1, :] * u_prev + cw_ref[1:2, :] * u + cw_ref[2:3, :] * u_next
    c = (cb_ref[...].astype(F32) * conv).astype(BF16)

    sub = triu_ref.shape[0]
    counts = cnt_ref[:, 0:1]
    tiles = [slice(s * sub, (s + 1) * sub) for s in range(tm // sub)]
    branches = [(jnp.dot(a_ref[rows, :], wa_ref[...], preferred_element_type=F32),
                 jnp.dot(c[rows, :], wc_ref[...], preferred_element_type=F32)) for rows in tiles]
    mixed = []
    for rows, (attn_b, conv_b) in zip(tiles, branches):
        merged = ga_ref[rows, :].astype(F32) * attn_b + gc_ref[rows, :].astype(F32) * conv_b
        mixed.append(jnp.dot(merged.astype(BF16), wo_ref[...], preferred_element_type=F32))
    normed = []
    for rows, m in zip(tiles, mixed):
        x1 = _layer_norm(DEEPNORM_ALPHA * x_ref[rows, :] + m, g1_ref[...], b1_ref[...])
        x1_ref[rows, :] = x1
        x1p_ref[rows, :] = _pack_row(x1)
        normed.append(x1)
    for rows, x1 in zip(tiles, normed):
        nt = (((1,), (1,)), ((), ()))
        logits_t = lax.dot_general(wrt_ref[...], x1.astype(BF16), nt, preferred_element_type=F32)
        work = logits_t[:N_EXPERTS, :] + brb_ref[...]

        row_f = lax.broadcasted_iota(jnp.int32, work.shape, 0).astype(F32)
        hots, vals, idxs = [], [], []
        for _ in range(TOP_K):
            mx = jnp.max(work, axis=0, keepdims=True)
            ix = jnp.min(jnp.where(work == mx, row_f, float(N_EXPERTS)), axis=0, keepdims=True)
            hot = row_f == ix
            work = jnp.where(hot, -jnp.inf, work)
            hots.append(hot)
            vals.append(mx)
            idxs.append(ix)
        exps = [jnp.exp(v - vals[0]) for v in vals]
        denom = exps[0] + exps[1] + exps[2] + exps[3]
        picked = jnp.zeros(work.shape, F32)
        for hot in hots:
            picked = picked + hot.astype(F32)
        before = counts + jnp.dot(picked.astype(BF16), triu_ref[...], preferred_element_type=F32)
        ranks = [jnp.sum(jnp.where(hot, before, 0.0), axis=0, keepdims=True) for hot in hots]
        fields = idxs + ranks + [e / denom for e in exps]
        rec_row = lax.broadcasted_iota(jnp.int32, (RT_ROWS, sub), 0)
        rec = jnp.zeros((RT_ROWS, sub), F32)
        for r, val in enumerate(fields):
            rec = jnp.where(rec_row == r, val, rec)
        rtt_ref[:, rows] = rec
        rt_ref[rows, :] = jnp.concatenate([rec, jnp.zeros((ROUTER_COLS - RT_ROWS, sub), F32)], axis=0).T
        counts = counts + jnp.sum(picked, axis=1, keepdims=True)
    cnt_ref[...] = jnp.broadcast_to(counts, cnt_ref.shape)


def _mix(a2d, h2d, x2d, seq, cw, wa, wc, wo, g1, b1, wr_t, br_b):
    n = x2d.shape[0]
    tm = MIX_TM
    tiles_per_seq = seq // tm
    halo_blocks = tm // SUBLANES
    n_halo = n // SUBLANES

    def col(blk):
        return pl.BlockSpec((tm, COL_BLOCK), lambda i: (i, blk))

    def prev(blk):
        return pl.BlockSpec((SUBLANES, COL_BLOCK), lambda i: (jnp.maximum(i * halo_blocks - 1, 0), blk))

    def nxt(blk):
        return pl.BlockSpec((SUBLANES, COL_BLOCK),
                            lambda i: (jnp.minimum((i + 1) * halo_blocks, n_halo - 1), blk))

    def full(shape):
        return pl.BlockSpec(shape, lambda i: (0,) * len(shape))

    sub = MIX_SUB
    tri_u = (lax.broadcasted_iota(jnp.int32, (sub, sub), 0)
             < lax.broadcasted_iota(jnp.int32, (sub, sub), 1)).astype(BF16)
    kern = functools.partial(_mix_kernel, tiles_per_seq=tiles_per_seq)
    return pl.pallas_call(
        kern,
        out_shape=(jax.ShapeDtypeStruct((n, D_MODEL), F32),
                   jax.ShapeDtypeStruct((n, HALF), jnp.int32),
                   jax.ShapeDtypeStruct((n, ROUTER_COLS), F32),
                   jax.ShapeDtypeStruct((RT_ROWS, n), F32),
                   jax.ShapeDtypeStruct((N_EXPERTS, LANES), F32)),
        grid=(n // tm,),
        in_specs=[
            pl.BlockSpec((tm, D_MODEL), lambda i: (i, 0)),
            col(CB_BLK), col(CC_BLK), col(CX_BLK), col(GA_BLK), col(GC_BLK),
            prev(CC_BLK), prev(CX_BLK), nxt(CC_BLK), nxt(CX_BLK),
            pl.BlockSpec((tm, D_MODEL), lambda i: (i, 0)),
            full((SUBLANES, D_MODEL)),
            full((D_MODEL, D_MODEL)), full((D_MODEL, D_MODEL)), full((D_MODEL, D_MODEL)),
            full((1, D_MODEL)), full((1, D_MODEL)),
            full((ROUTER_COLS, D_MODEL)), full((N_EXPERTS, sub)), full((sub, sub)),
        ],
        out_specs=(pl.BlockSpec((tm, D_MODEL), lambda i: (i, 0)),
                   pl.BlockSpec((tm, HALF), lambda i: (i, 0)),
                   pl.BlockSpec((tm, ROUTER_COLS), lambda i: (i, 0)),
                   pl.BlockSpec((RT_ROWS, tm), lambda i: (0, i)),
                   pl.BlockSpec((N_EXPERTS, LANES), lambda i: (0, 0))),
        compiler_params=pltpu.CompilerParams(
            dimension_semantics=("arbitrary",), vmem_limit_bytes=VMEM_LIMIT),
        name="mix_ln1_router",
    )(a2d, h2d, h2d, h2d, h2d, h2d, h2d, h2d, h2d, h2d, x2d, cw, wa, wc, wo, g1, b1, wr_t, br_b, tri_u)


def _moe_kernel(be_ref, nv_ref, nu_ref, xs_ref, wg_ref, bg_ref, wu_ref, bu_ref, wd_ref, bd_ref, o_ref,
                wg_s, wu_s, wd_s):
    i = pl.program_id(0)
    used = i < nu_ref[0]

    @pl.when(used & ((i == 0) | (be_ref[i] != be_ref[jnp.maximum(i - 1, 0)])))
    def _():
        wg_s[...] = wg_ref[...].astype(BF16)
        wu_s[...] = wu_ref[...].astype(BF16)
        wd_s[...] = wd_ref[...].astype(BF16)

    @pl.when(used)
    def _():
        row = lax.broadcasted_iota(jnp.int32, xs_ref.shape, 0)
        x = _unpack_row(jnp.where(row < nv_ref[i], xs_ref[...], 0)).astype(BF16)
        hg = jnp.minimum(jnp.dot(x, wg_s[...], preferred_element_type=F32) + bg_ref[...], SWIGLU_LIMIT)
        hu = jnp.clip(jnp.dot(x, wu_s[...], preferred_element_type=F32) + bu_ref[...],
                      -SWIGLU_LIMIT, SWIGLU_LIMIT)
        act = hg * jax.nn.sigmoid(SWIGLU_ALPHA * hg) * (hu + 1.0)
        y = jnp.dot(act.astype(BF16), wd_s[...], preferred_element_type=F32) + bd_ref[...]
        o_ref[...] = _pack_row(y)

    @pl.when(jnp.logical_not(used))
    def _():
        o_ref[...] = jnp.zeros(o_ref.shape, o_ref.dtype)


def _moe_experts(block_e, n_valid, n_used, xs, wg, bg, wu, bu, wd, bd):
    p = xs.shape[0]
    tm = MOE_TM
    nb = p // tm

    def row_map(i, be, nv, nu):
        return (jnp.minimum(i, nu[0] - 1), 0)

    def w_map(i, be, nv, nu):
        return (be[jnp.minimum(i, nu[0] - 1)], 0, 0)

    w_spec = pl.BlockSpec((None, D_MODEL, D_MODEL), w_map)
    b_spec = pl.BlockSpec((None, 1, D_MODEL), w_map)
    return pl.pallas_call(
        _moe_kernel,
        out_shape=jax.ShapeDtypeStruct((p, HALF), jnp.int32),
        grid_spec=pltpu.PrefetchScalarGridSpec(
            num_scalar_prefetch=3,
            grid=(nb,),
            in_specs=[
                pl.BlockSpec((tm, HALF), row_map),
                w_spec, b_spec, w_spec, b_spec, w_spec, b_spec,
            ],
            out_specs=pl.BlockSpec((tm, HALF), lambda i, be, nv, nu: (i, 0)),
            scratch_shapes=[pltpu.VMEM((D_MODEL, D_MODEL), BF16)] * 3,
        ),
        compiler_params=pltpu.CompilerParams(
            dimension_semantics=("arbitrary",), vmem_limit_bytes=VMEM_LIMIT),
        name="moe_experts",
    )(block_e, n_valid, n_used, xs, wg, bg, wu, bu, wd, bd)


def _sc_mesh():
    return plsc.VectorSubcoreMesh(core_axis_name="core", subcore_axis_name="subcore")


def _sc_index_rows(dest_t):
    k, n = dest_t.shape
    w = SC_WINDOW
    return jnp.pad(dest_t.reshape(k * n // w, w), ((0, 0), (0, LANES - w)))


def _sc_dispatch(x1, slot_idx, p):
    n, d = x1.shape
    w = SC_WINDOW
    windows = n // w

    @pl.kernel(out_type=jax.ShapeDtypeStruct((p, d), x1.dtype), mesh=_sc_mesh(), scratch_types=[],
               name="sc_dispatch")
    def run(x_hbm, d_hbm, o_hbm):
        def body(x_vmem, *d_vmems):
            for d_vmem in d_vmems:
                pltpu.sync_copy(x_vmem, o_hbm.at[d_vmem.at[0, pl.ds(0, w)]])

        pltpu.emit_pipeline(
            body,
            grid=(windows,),
            in_specs=[pl.BlockSpec((w, d), lambda i: (i, 0))]
            + [pl.BlockSpec((1, LANES), functools.partial(lambda k, i: (k * windows + i, 0), k))
               for k in range(TOP_K)],
            out_specs=[],
            core_axis_name=("core", "subcore"),
            dimension_semantics=(pltpu.PARALLEL,),
        )(x_hbm, *([d_hbm] * TOP_K))

    return run(x1, slot_idx)


def _sc_gather(ys, idx):
    w = SC_WINDOW
    r = idx.shape[0] * w
    d = ys.shape[1]

    @pl.kernel(out_type=jax.ShapeDtypeStruct((r, d), ys.dtype), mesh=_sc_mesh(), scratch_types=[],
               name="sc_gather")
    def run(y_hbm, i_hbm, o_hbm):
        def body(i_vmem, o_vmem):
            pltpu.sync_copy(y_hbm.at[i_vmem.at[0, pl.ds(0, w)]], o_vmem)

        pltpu.emit_pipeline(
            body,
            grid=(r // w,),
            in_specs=[pl.BlockSpec((1, LANES), lambda i: (i, 0))],
            out_specs=[pl.BlockSpec((w, d), lambda i: (i, 0))],
            core_axis_name=("core", "subcore"),
            dimension_semantics=(pltpu.PARALLEL,),
        )(i_hbm, o_hbm)

    return run(ys, idx)


def _ln2_kernel(x1_ref, yg_ref, rt_ref, g_ref, b_ref, o_ref):
    f = jnp.zeros(x1_ref.shape, F32)
    for k in range(TOP_K):
        f = f + _unpack_row(yg_ref[k]) * rt_ref[:, RT_GATE + k:RT_GATE + k + 1]
    o_ref[...] = _layer_norm(DEEPNORM_ALPHA * x1_ref[...] + f, g_ref[...], b_ref[...])


def _combine_ln2(x1, yg, rt, g2, b2):
    n = x1.shape[0]
    tm = LN2_TM
    return pl.pallas_call(
        _ln2_kernel,
        out_shape=jax.ShapeDtypeStruct((n, D_MODEL), F32),
        grid=(n // tm,),
        in_specs=[
            pl.BlockSpec((tm, D_MODEL), lambda i: (i, 0)),
            pl.BlockSpec((TOP_K, tm, HALF), lambda i: (0, i, 0)),
            pl.BlockSpec((tm, ROUTER_COLS), lambda i: (i, 0)),
            pl.BlockSpec((1, D_MODEL), lambda i: (0, 0)),
            pl.BlockSpec((1, D_MODEL), lambda i: (0, 0)),
        ],
        out_specs=pl.BlockSpec((tm, D_MODEL), lambda i: (i, 0)),
        compiler_params=pltpu.CompilerParams(
            dimension_semantics=("parallel",), vmem_limit_bytes=VMEM_LIMIT),
        name="combine_ln2",
    )(x1, yg, rt, g2, b2)


def _rotary_tables(seq):
    inv = ROPE_THETA ** (-jnp.arange(0, ROT_DIM, 2, dtype=F32) / ROT_DIM)
    ang = jnp.arange(seq, dtype=F32)[:, None] * inv[None, :]
    cos, sin = jnp.cos(ang), jnp.sin(ang)
    ones = jnp.ones((seq, QK_DIM - ROT_DIM), F32)
    zeros = jnp.zeros((seq, QK_DIM - ROT_DIM), F32)
    zh = jnp.zeros((seq, ROT_HALF), F32)
    cos64 = jnp.concatenate([cos, cos, ones], axis=1)
    sa64 = jnp.concatenate([-sin, zh, zeros], axis=1)
    sb64 = jnp.concatenate([zh, sin, zeros], axis=1)
    rep = LANES // QK_DIM
    return (jnp.tile(cos64, (1, rep)), jnp.tile(sa64, (1, rep)), jnp.tile(sb64, (1, rep)), cos.T, sin.T)


def _slot_layout(rtt, counts, tm):
    n = rtt.shape[1]
    p = n * TOP_K + N_EXPERTS * tm
    nb = p // tm
    expert = rtt[RT_EXPERT:RT_EXPERT + TOP_K].astype(jnp.int32)
    rank = rtt[RT_RANK:RT_RANK + TOP_K].astype(jnp.int32)
    padded = ((counts + tm - 1) // tm) * tm
    pad_ends = jnp.cumsum(padded)
    pad_starts = pad_ends - padded
    ids = jnp.arange(N_EXPERTS, dtype=jnp.int32)[:, None, None]
    dest_t = jnp.sum(jnp.where(expert[None] == ids, pad_starts[:, None, None], 0), axis=0) + rank
    block_start = jnp.arange(nb, dtype=jnp.int32) * tm
    block_e = jnp.minimum(jnp.sum(pad_ends[None, :] <= block_start[:, None], axis=1),
                          N_EXPERTS - 1).astype(jnp.int32)
    own = block_e[:, None] == jnp.arange(N_EXPERTS, dtype=jnp.int32)[None, :]
    valid_end = jnp.sum(jnp.where(own, (pad_starts + counts)[None, :], 0), axis=1)
    n_valid = jnp.clip(valid_end - block_start, 0, tm).astype(jnp.int32)
    n_used = (pad_ends[-1] // tm).astype(jnp.int32).reshape(1)
    return dest_t.astype(jnp.int32), block_e, n_valid, n_used, p


def _trunk(x, prm):
    b, s, d = x.shape
    n = b * s
    x2d = x.reshape(n, d)
    h, qt, vt = _inproj(x2d, prm["w_in"], prm["w_q_t"], prm["w_v_t"], prm["b_gate"], _rotary_tables(s), b, s)
    a = _attention(h.reshape(b, s, H_COLS), qt, vt, prm["lam"], prm["subln_g"], prm["lam_init"])
    x1, x1p, rt, rtt, cnt = _mix(a.reshape(n, d), h, x2d, s, prm["conv_w"], prm["w_attn_br"],
                                 prm["w_conv_br"], prm["w_o"], prm["ln1_g"], prm["ln1_b"],
                                 prm["w_router_t"], prm["b_router_b"])
    counts = cnt[:, 0].astype(jnp.int32)
    dest_t, block_e, n_valid, n_used, p = _slot_layout(rtt, counts, MOE_TM)
    slot_idx = _sc_index_rows(dest_t)
    xs = _sc_dispatch(x1p, slot_idx, p)
    ys = _moe_experts(block_e, n_valid, n_used, xs, prm["w_exp_gate"], prm["b_exp_gate"],
                      prm["w_exp_up"], prm["b_exp_up"], prm["w_exp_down"], prm["b_exp_down"])
    yg = _sc_gather(ys, slot_idx).reshape(TOP_K, n, HALF)
    y = _combine_ln2(x1, yg, rt, prm["ln2_g"], prm["ln2_b"])
    return y.reshape(b, s, d)


def kernel(x_prompt, x_sample, w_in, b_branch_gate, lambda_q1, lambda_k1, lambda_q2, lambda_k2, subln_g,
           conv_w, w_attn_br, w_conv_br, w_o, ln1_g, ln1_b, w_router, b_router, w_exp_gate, b_exp_gate,
           w_exp_up, b_exp_up, w_exp_down, b_exp_down, ln2_g, ln2_b):
    l = 0
    lam_init = 0.8 - 0.6 * math.exp(-0.3 * l)
    lam = (jnp.exp(jnp.sum(lambda_q1[l].astype(F32) * lambda_k1[l].astype(F32)))
           - jnp.exp(jnp.sum(lambda_q2[l].astype(F32) * lambda_k2[l].astype(F32))) + lam_init)
    prm = {
        "lam_init": lam_init,
        "lam": lam.reshape(1).astype(F32),
        "w_in": w_in[l].astype(BF16),
        "w_q_t": w_in[l][:, Q_STEP * COL_BLOCK:(Q_STEP + 1) * COL_BLOCK].T.astype(BF16),
        "w_v_t": w_in[l][:, V_STEP * COL_BLOCK:(V_STEP + 1) * COL_BLOCK].T.astype(BF16),
        "b_gate": b_branch_gate[l].reshape(1, 2 * D_MODEL),
        "subln_g": subln_g[l].reshape(1, V_DIM),
        "conv_w": jnp.pad(conv_w[l], ((0, SUBLANES - conv_w.shape[1]), (0, 0))),
        "w_attn_br": w_attn_br[l].astype(BF16),
        "w_conv_br": w_conv_br[l].astype(BF16),
        "w_o": w_o[l].astype(BF16),
        "ln1_g": ln1_g[l].reshape(1, D_MODEL),
        "ln1_b": ln1_b[l].reshape(1, D_MODEL),
        "w_router_t": jnp.pad(w_router[l].T, ((0, ROUTER_COLS - N_EXPERTS), (0, 0))).astype(BF16),
        "b_router_b": jnp.broadcast_to(b_router[l].astype(F32)[:, None], (N_EXPERTS, MIX_SUB)),
        "w_exp_gate": w_exp_gate[l],
        "b_exp_gate": b_exp_gate[l].reshape(N_EXPERTS, 1, D_MODEL),
        "w_exp_up": w_exp_up[l],
        "b_exp_up": b_exp_up[l].reshape(N_EXPERTS, 1, D_MODEL),
        "w_exp_down": w_exp_down[l],
        "b_exp_down": b_exp_down[l].reshape(N_EXPERTS, 1, D_MODEL),
        "ln2_g": ln2_g[l].reshape(1, D_MODEL),
        "ln2_b": ln2_b[l].reshape(1, D_MODEL),
    }
    return _trunk(x_prompt, prm), _trunk(x_sample, prm)
```

```python
import functools
import math

import jax
import jax.numpy as jnp
from jax import lax
from jax.experimental import pallas as pl
from jax.experimental.pallas import tpu as pltpu
from jax.experimental.pallas import tpu_sc as plsc

F32 = jnp.float32
BF16 = jnp.bfloat16

D_MODEL = 1024
N_HEADS = 8
QK_DIM = 64
V_DIM = 128
HEAD_COLS = 2 * QK_DIM
ROT_DIM = QK_DIM // 4
ROT_HALF = ROT_DIM // 2
ROPE_THETA = 500000.0
SUBLN_EPS = 1e-5
IN_COLS = 8192
COL_BLOCK = 1024
Q_STEP, K_STEP, V_STEP = 0, 1, 2
GATE_STEP = 6
H_COLS = IN_COLS - 2 * COL_BLOCK
CB_BLK, CC_BLK, CX_BLK, GA_BLK, GC_BLK = 1, 2, 3, 4, 5
K_HEAD0 = 0
N_EXPERTS = 32
TOP_K = 4
SWIGLU_LIMIT = 7.0
SWIGLU_ALPHA = 1.702
LN_EPS = 1e-5
DEPTH = 1
DEEPNORM_ALPHA = (2 * DEPTH) ** 0.25
LANES = 128
SUBLANES = 8
ROUTER_COLS = LANES
RT_EXPERT, RT_RANK, RT_GATE = 0, 4, 8
RT_ROWS = 16
SC_WINDOW = 32
ONES_ROWS = 16
Q_SCALE = math.log2(math.e) * QK_DIM ** -0.5

VMEM_LIMIT = 56 * 1024 * 1024

PROJ_TM = 1024
ATTN_TQ = 1024
ATTN_TK = 512
ATTN_PAIR_MAX_SEQ = 4096
MIX_TM = 512
MIX_SUB = 256
MOE_TM = 512
LN2_TM = 1024


def _rotate_rows(t1, t2, cos, sin):
    return t1 * cos - t2 * sin, t2 * cos + t1 * sin


def _inproj_kernel(x_ref, w_ref, wqt_ref, wvt_ref, bg_ref, cos_ref, sa_ref, sb_ref, cosr_ref, sinr_ref,
                   o_ref, qt_ref, vt_ref, xb_s):
    j = pl.program_id(1)

    @pl.when(j == 0)
    def _():
        xb_s[...] = x_ref[...].astype(BF16)

    xb = xb_s[...]
    nt = (((1,), (1,)), ((), ()))
    tm = x_ref.shape[0]
    halves = (slice(0, tm // 2), slice(tm // 2, tm))

    @pl.when(j == Q_STEP)
    def _():
        acc_t = lax.dot_general(wqt_ref[...], xb, nt, preferred_element_type=F32) * Q_SCALE
        cos, sin = cosr_ref[...], sinr_ref[...]
        for hh in range(N_HEADS):
            parts = []
            for mp in range(2):
                o = hh * HEAD_COLS + mp * QK_DIM
                parts += _rotate_rows(acc_t[o:o + ROT_HALF], acc_t[o + ROT_HALF:o + ROT_DIM], cos, sin)
                parts.append(acc_t[o + ROT_DIM:o + QK_DIM])
            qt_ref[hh, 0] = jnp.concatenate(parts, axis=0).astype(BF16)

    @pl.when(j == V_STEP)
    def _():
        acc_t = lax.dot_general(wvt_ref[...], xb, nt, preferred_element_type=F32)
        n_heads, n_chunks, _, tk = vt_ref.shape
        for hh in range(n_heads):
            for cc in range(n_chunks):
                vt_ref[hh, cc, :V_DIM, :] = acc_t[hh * V_DIM:(hh + 1) * V_DIM,
                                                  cc * tk:(cc + 1) * tk].astype(BF16)
                vt_ref[hh, cc, V_DIM:, :] = jnp.ones((ONES_ROWS, tk), BF16)

    @pl.when((j > V_STEP) & (j < GATE_STEP))
    def _():
        o_ref[...] = jnp.dot(xb, w_ref[...], preferred_element_type=F32).astype(o_ref.dtype)

    @pl.when(j >= GATE_STEP)
    def _():
        accs = [jnp.dot(xb_s[r, :], w_ref[...], preferred_element_type=F32) for r in halves]
        for r, acc in zip(halves, accs):
            o_ref[r, :] = jax.nn.sigmoid(acc + bg_ref[...]).astype(o_ref.dtype)

    @pl.when(j == K_STEP)
    def _():
        accs = [jnp.dot(xb_s[r, :], w_ref[...], preferred_element_type=F32) for r in halves]
        for r, acc in zip(halves, accs):
            cos, sa, sb = cos_ref[r, :], sa_ref[r, :], sb_ref[r, :]
            for c in range(COL_BLOCK // LANES):
                a = acc[:, c * LANES:(c + 1) * LANES]
                rot = (a * cos + pltpu.roll(a, LANES - ROT_HALF, 1) * sa + pltpu.roll(a, ROT_HALF, 1) * sb)
                o_ref[r, c * LANES:(c + 1) * LANES] = rot.astype(o_ref.dtype)


def _inproj(x2d, w_in_b, wqt_b, wvt_b, b_gate, tables, batch, seq):
    n = x2d.shape[0]
    tm, tk = PROJ_TM, ATTN_TK
    assert tm == ATTN_TQ
    pos_blocks = seq // tm
    cos_t, sa_t, sb_t, cos_r, sin_r = tables
    tab_spec = pl.BlockSpec((tm, LANES), lambda i, j: (i % pos_blocks, 0))
    rtab_spec = pl.BlockSpec((ROT_HALF, tm), lambda i, j: (0, i % pos_blocks))
    wt_spec = pl.BlockSpec((COL_BLOCK, D_MODEL), lambda i, j: (0, 0))
    return pl.pallas_call(
        _inproj_kernel,
        out_shape=(jax.ShapeDtypeStruct((n, H_COLS), BF16),
                   jax.ShapeDtypeStruct((batch, N_HEADS, seq // tm, HEAD_COLS, tm), BF16),
                   jax.ShapeDtypeStruct((batch, N_HEADS, seq // tk, V_DIM + ONES_ROWS, tk), BF16)),
        grid=(n // tm, IN_COLS // COL_BLOCK),
        in_specs=[
            pl.BlockSpec((tm, D_MODEL), lambda i, j: (i, 0)),
            pl.BlockSpec((D_MODEL, COL_BLOCK), lambda i, j: (0, jnp.where(j <= V_STEP, K_STEP, j))),
            wt_spec, wt_spec,
            pl.BlockSpec((1, COL_BLOCK), lambda i, j: (0, jnp.maximum(j - GATE_STEP, 0))),
            tab_spec, tab_spec, tab_spec, rtab_spec, rtab_spec,
        ],
        out_specs=(
            pl.BlockSpec((tm, COL_BLOCK), lambda i, j: (i, jnp.maximum(j - V_STEP, 0))),
            pl.BlockSpec((None, N_HEADS, 1, HEAD_COLS, tm),
                         lambda i, j: (i // pos_blocks, 0, i % pos_blocks, 0, 0)),
            pl.BlockSpec((None, N_HEADS, tm // tk, V_DIM + ONES_ROWS, tk),
                         lambda i, j: (i // pos_blocks, 0, i % pos_blocks, 0, 0)),
        ),
        scratch_shapes=[pltpu.VMEM((tm, D_MODEL), BF16)],
        compiler_params=pltpu.CompilerParams(
            dimension_semantics=("parallel", "arbitrary"), vmem_limit_bytes=VMEM_LIMIT),
        name="inproj",
    )(x2d, w_in_b, wqt_b, wvt_b, b_gate, cos_t, sa_t, sb_t, cos_r, sin_r)


def _attn_kernel(lam_ref, q_ref, k_ref, vt_ref, g_ref, o_ref, qt_s, acc_s, *s_bufs, out_scale):
    n_heads, n_q, _, tq = q_ref.shape
    n_chunks, _, tk = vt_ref.shape[1:]
    n_st = 2 * n_heads
    sa_s = tuple(s_bufs[2 * st] for st in range(n_st))
    sb_s = tuple(s_bufs[2 * st + 1] for st in range(n_st))

    def load_q(qi):
        zero_half = jnp.zeros((QK_DIM, tq), BF16)
        for hd in range(n_heads):
            qt_s[2 * hd, :QK_DIM, :] = q_ref[hd, qi, :QK_DIM, :]
            qt_s[2 * hd, QK_DIM:, :] = zero_half
            qt_s[2 * hd + 1, :QK_DIM, :] = zero_half
            qt_s[2 * hd + 1, QK_DIM:, :] = q_ref[hd, qi, QK_DIM:, :]

    def scores(c, s_refs, st):
        hd = st // 2
        k = k_ref[pl.ds(pl.multiple_of(c * tk, tk), tk), hd * HEAD_COLS:(hd + 1) * HEAD_COLS]
        s = jnp.dot(k, qt_s[st], preferred_element_type=F32)
        s_refs[st][...] = s
        return jnp.max(s, axis=0, keepdims=True)

    def update(c, s_refs, st, m_old, c_max):
        m_new = jnp.maximum(m_old, c_max)
        alpha = jnp.exp2(m_old - m_new)
        p = jnp.exp2(s_refs[st][...] - m_new).astype(BF16)
        acc_s[st] = alpha * acc_s[st] + jnp.dot(vt_ref[st // 2, c], p, preferred_element_type=F32)
        return m_new

    def step(c, cur, nxt, ms, cur_max, prefetch):
        new_ms, nxt_max = [], []
        for st in range(n_st):
            if prefetch:
                nxt_max.append(scores(c + 1, nxt, st))
            new_ms.append(update(c, cur, st, ms[st], cur_max[st]))
        return tuple(new_ms), tuple(nxt_max)

    def pair(i, carry):
        ms, a_max = carry
        ms, b_max = step(2 * i, sa_s, sb_s, ms, a_max, True)
        return step(2 * i + 1, sb_s, sa_s, ms, b_max, True)

    def finalize(qi):
        lam = lam_ref[0]
        rows = pl.ds(pl.multiple_of(qi * tq, tq), tq)
        for hd in range(n_heads):
            o1 = acc_s[2 * hd, :V_DIM, :] / acc_s[2 * hd, V_DIM:V_DIM + 1, :]
            o2 = acc_s[2 * hd + 1, :V_DIM, :] / acc_s[2 * hd + 1, V_DIM:V_DIM + 1, :]
            ot = o1 - lam * o2
            ot = ot * lax.rsqrt(jnp.mean(ot * ot, axis=0, keepdims=True) + SUBLN_EPS)
            o_ref[rows, hd * V_DIM:(hd + 1) * V_DIM] = (ot.T * (g_ref[...] * out_scale)).astype(o_ref.dtype)

    def q_block(qi, a_max, last):
        acc_s[...] = jnp.zeros(acc_s.shape, F32)
        m0 = tuple(jnp.full((1, tq), -jnp.inf, F32) for _ in range(n_st))
        ms, a_max = lax.fori_loop(0, n_chunks // 2 - 1, pair, (m0, a_max))
        ms, b_max = step(n_chunks - 2, sa_s, sb_s, ms, a_max, True)
        step(n_chunks - 1, sb_s, sa_s, ms, b_max, False)
        if not last:
            load_q(qi + 1)
            a_max = tuple(scores(0, sa_s, st) for st in range(n_st))
        finalize(qi)
        return a_max

    load_q(0)
    a_max = tuple(scores(0, sa_s, st) for st in range(n_st))
    a_max = lax.fori_loop(0, n_q - 1, lambda qi, am: q_block(qi, am, False), a_max)
    q_block(n_q - 1, a_max, True)


def _attention(h3, qt, vt, lam, subln_g, lam_init):
    b, s, _ = h3.shape
    tq, tk = ATTN_TQ, ATTN_TK
    n_chunks = s // tk
    assert n_chunks % 2 == 0 and n_chunks >= 2
    vt_rows = V_DIM + ONES_ROWS
    g = 2 if s <= ATTN_PAIR_MAX_SEQ else 1
    kern = functools.partial(_attn_kernel, out_scale=1.0 - lam_init)
    return pl.pallas_call(
        kern,
        out_shape=jax.ShapeDtypeStruct((b, s, N_HEADS * V_DIM), BF16),
        grid=(b, N_HEADS // g),
        in_specs=[
            pl.BlockSpec(memory_space=pltpu.SMEM),
            pl.BlockSpec((None, g, s // tq, HEAD_COLS, tq), lambda bi, hi: (bi, hi, 0, 0, 0)),
            pl.BlockSpec((None, s, g * HEAD_COLS), lambda bi, hi: (bi, 0, K_HEAD0 // g + hi)),
            pl.BlockSpec((None, g, n_chunks, vt_rows, tk), lambda bi, hi: (bi, hi, 0, 0, 0)),
            pl.BlockSpec((1, V_DIM), lambda bi, hi: (0, 0)),
        ],
        out_specs=pl.BlockSpec((None, s, g * V_DIM), lambda bi, hi: (bi, 0, hi)),
        scratch_shapes=[
            pltpu.VMEM((2 * g, HEAD_COLS, tq), BF16),
            pltpu.VMEM((2 * g, vt_rows, tq), F32),
        ] + [pltpu.VMEM((tk, tq), F32)] * (4 * g),
        compiler_params=pltpu.CompilerParams(
            dimension_semantics=("parallel", "arbitrary"),
            vmem_limit_bytes=VMEM_LIMIT),
        name="diff_attention",
    )(lam, qt, h3, vt, subln_g)


HALF = D_MODEL // 2


def _pack_row(x):
    return pltpu.pack_elementwise([x[:, :HALF], x[:, HALF:]], packed_dtype=BF16)


def _unpack_row(w):
    lo = pltpu.unpack_elementwise(w, index=0, packed_dtype=BF16, unpacked_dtype=F32)
    hi = pltpu.unpack_elementwise(w, index=1, packed_dtype=BF16, unpacked_dtype=F32)
    return jnp.concatenate([lo, hi], axis=1)


def _layer_norm(y, g, b):
    mu = jnp.mean(y, axis=-1, keepdims=True)
    d = y - mu
    var = jnp.mean(d * d, axis=-1, keepdims=True)
    return d * lax.rsqrt(var + LN_EPS) * g + b


def _mix_kernel(a_ref, cb_ref, cc_ref, cx_ref, ga_ref, gc_ref, ccp_ref, cxp_ref, ccn_ref, cxn_ref,
                x_ref, cw_ref, wa_ref, wc_ref, wo_ref, g1_ref, b1_ref, wrt_ref, brb_ref, triu_ref,
                x1_ref, x1p_ref, rt_ref, rtt_ref, cnt_ref, *, tiles_per_seq):
    i = pl.program_id(0)

    @pl.when(i == 0)
    def _():
        cnt_ref[...] = jnp.zeros(cnt_ref.shape, F32)

    tm = x_ref.shape[0]
    u = cc_ref[...].astype(F32) * cx_ref[...].astype(F32)
    first = (i % tiles_per_seq) == 0
    last = (i % tiles_per_seq) == tiles_per_seq - 1
    up = ccp_ref[SUBLANES - 1:SUBLANES, :].astype(F32) * cxp_ref[SUBLANES - 1:SUBLANES, :].astype(F32)
    un = ccn_ref[0:1, :].astype(F32) * cxn_ref[0:1, :].astype(F32)
    up = jnp.where(first, 0.0, up)
    un = jnp.where(last, 0.0, un)
    row = lax.broadcasted_iota(jnp.int32, u.shape, 0)
    u_prev = jnp.where(row == 0, up, pltpu.roll(u, 1, 0))
    u_next = jnp.where(row == tm - 1, un, pltpu.roll(u, tm - 1, 0))
    conv = cw_ref[0:1, :] * u_prev + cw_ref[1:2, :] * u + cw_ref[2:3, :] * u_next
    c = (cb_ref[...].astype(F32) * conv).astype(BF16)

    sub = triu_ref.shape[0]
    counts = cnt_ref[:, 0:1]
    tiles = [slice(s * sub, (s + 1) * sub) for s in range(tm // sub)]
    branches = [(jnp.dot(a_ref[rows, :], wa_ref[...], preferred_element_type=F32),
                 jnp.dot(c[rows, :], wc_ref[...], preferred_element_type=F32)) for rows in tiles]
    mixed = []
    for rows, (attn_b, conv_b) in zip(tiles, branches):
        merged = ga_ref[rows, :].astype(F32) * attn_b + gc_ref[rows, :].astype(F32) * conv_b
        mixed.append(jnp.dot(merged.astype(BF16), wo_ref[...], preferred_element_type=F32))
    normed = []
    for rows, m in zip(tiles, mixed):
        x1 = _layer_norm(DEEPNORM_ALPHA * x_ref[rows, :] + m, g1_ref[...], b1_ref[...])
        x1_ref[rows, :] = x1
        x1p_ref[rows, :] = _pack_row(x1)
        normed.append(x1)
    for rows, x1 in zip(tiles, normed):
        nt = (((1,), (1,)), ((), ()))
        logits_t = lax.dot_general(wrt_ref[...], x1.astype(BF16), nt, preferred_element_type=F32)
        work = logits_t[:N_EXPERTS, :] + brb_ref[...]

        row_f = lax.broadcasted_iota(jnp.int32, work.shape, 0).astype(F32)
        hots, vals, idxs = [], [], []
        for _ in range(TOP_K):
            mx = jnp.max(work, axis=0, keepdims=True)
            ix = jnp.min(jnp.where(work == mx, row_f, float(N_EXPERTS)), axis=0, keepdims=True)
            hot = row_f == ix
            work = jnp.where(hot, -jnp.inf, work)
            hots.append(hot)
            vals.append(mx)
            idxs.append(ix)
        exps = [jnp.exp(v - vals[0]) for v in vals]
        denom = exps[0] + exps[1] + exps[2] + exps[3]
        picked = jnp.zeros(work.shape, F32)
        for hot in hots:
            picked = picked + hot.astype(F32)
        before = counts + jnp.dot(picked.astype(BF16), triu_ref[...], preferred_element_type=F32)
        ranks = [jnp.sum(jnp.where(hot, before, 0.0), axis=0, keepdims=True) for hot in hots]
        fields = idxs + ranks + [e / denom for e in exps]
        rec_row = lax.broadcasted_iota(jnp.int32, (RT_ROWS, sub), 0)
        rec = jnp.zeros((RT_ROWS, sub), F32)
        for r, val in enumerate(fields):
            rec = jnp.where(rec_row == r, val, rec)
        rtt_ref[:, rows] = rec
        rt_ref[rows, :] = jnp.concatenate([rec, jnp.zeros((ROUTER_COLS - RT_ROWS, sub), F32)], axis=0).T
        counts = counts + jnp.sum(picked, axis=1, keepdims=True)
    cnt_ref[...] = jnp.broadcast_to(counts, cnt_ref.shape)


def _mix(a2d, h2d, x2d, seq, cw, wa, wc, wo, g1, b1, wr_t, br_b):
    n = x2d.shape[0]
    tm = MIX_TM
    tiles_per_seq = seq // tm
    halo_blocks = tm // SUBLANES
    n_halo = n // SUBLANES

    def col(blk):
        return pl.BlockSpec((tm, COL_BLOCK), lambda i: (i, blk))

    def prev(blk):
        return pl.BlockSpec((SUBLANES, COL_BLOCK), lambda i: (jnp.maximum(i * halo_blocks - 1, 0), blk))

    def nxt(blk):
        return pl.BlockSpec((SUBLANES, COL_BLOCK),
                            lambda i: (jnp.minimum((i + 1) * halo_blocks, n_halo - 1), blk))

    def full(shape):
        return pl.BlockSpec(shape, lambda i: (0,) * len(shape))

    sub = MIX_SUB
    tri_u = (lax.broadcasted_iota(jnp.int32, (sub, sub), 0)
             < lax.broadcasted_iota(jnp.int32, (sub, sub), 1)).astype(BF16)
    kern = functools.partial(_mix_kernel, tiles_per_seq=tiles_per_seq)
    return pl.pallas_call(
        kern,
        out_shape=(jax.ShapeDtypeStruct((n, D_MODEL), F32),
                   jax.ShapeDtypeStruct((n, HALF), jnp.int32),
                   jax.ShapeDtypeStruct((n, ROUTER_COLS), F32),
                   jax.ShapeDtypeStruct((RT_ROWS, n), F32),
                   jax.ShapeDtypeStruct((N_EXPERTS, LANES), F32)),
        grid=(n // tm,),
        in_specs=[
            pl.BlockSpec((tm, D_MODEL), lambda i: (i, 0)),
            col(CB_BLK), col(CC_BLK), col(CX_BLK), col(GA_BLK), col(GC_BLK),
            prev(CC_BLK), prev(CX_BLK), nxt(CC_BLK), nxt(CX_BLK),
            pl.BlockSpec((tm, D_MODEL), lambda i: (i, 0)),
            full((SUBLANES, D_MODEL)),
            full((D_MODEL, D_MODEL)), full((D_MODEL, D_MODEL)), full((D_MODEL, D_MODEL)),
            full((1, D_MODEL)), full((1, D_MODEL)),
            full((ROUTER_COLS, D_MODEL)), full((N_EXPERTS, sub)), full((sub, sub)),
        ],
        out_specs=(pl.BlockSpec((tm, D_MODEL), lambda i: (i, 0)),
                   pl.BlockSpec((tm, HALF), lambda i: (i, 0)),
                   pl.BlockSpec((tm, ROUTER_COLS), lambda i: (i, 0)),
                   pl.BlockSpec((RT_ROWS, tm), lambda i: (0, i)),
                   pl.BlockSpec((N_EXPERTS, LANES), lambda i: (0, 0))),
        compiler_params=pltpu.CompilerParams(
            dimension_semantics=("arbitrary",), vmem_limit_bytes=VMEM_LIMIT),
        name="mix_ln1_router",
    )(a2d, h2d, h2d, h2d, h2d, h2d, h2d, h2d, h2d, h2d, x2d, cw, wa, wc, wo, g1, b1, wr_t, br_b, tri_u)


def _moe_kernel(be_ref, nv_ref, nu_ref, xs_ref, wg_ref, bg_ref, wu_ref, bu_ref, wd_ref, bd_ref, o_ref,
                wg_s, wu_s, wd_s):
    i = pl.program_id(0)
    used = i < nu_ref[0]

    @pl.when(used & ((i == 0) | (be_ref[i] != be_ref[jnp.maximum(i - 1, 0)])))
    def _():
        wg_s[...] = wg_ref[...].astype(BF16)
        wu_s[...] = wu_ref[...].astype(BF16)
        wd_s[...] = wd_ref[...].astype(BF16)

    @pl.when(used)
    def _():
        row = lax.broadcasted_iota(jnp.int32, xs_ref.shape, 0)
        x = _unpack_row(jnp.where(row < nv_ref[i], xs_ref[...], 0)).astype(BF16)
        hg = jnp.minimum(jnp.dot(x, wg_s[...], preferred_element_type=F32) + bg_ref[...], SWIGLU_LIMIT)
        hu = jnp.clip(jnp.dot(x, wu_s[...], preferred_element_type=F32) + bu_ref[...],
                      -SWIGLU_LIMIT, SWIGLU_LIMIT)
        act = hg * jax.nn.sigmoid(SWIGLU_ALPHA * hg) * (hu + 1.0)
        y = jnp.dot(act.astype(BF16), wd_s[...], preferred_element_type=F32) + bd_ref[...]
        o_ref[...] = _pack_row(y)

    @pl.when(jnp.logical_not(used))
    def _():
        o_ref[...] = jnp.zeros(o_ref.shape, o_ref.dtype)


def _moe_experts(block_e, n_valid, n_used, xs, wg, bg, wu, bu, wd, bd):
    p = xs.shape[0]
    tm = MOE_TM
    nb = p // tm

    def row_map(i, be, nv, nu):
        return (jnp.minimum(i, nu[0] - 1), 0)

    def w_map(i, be, nv, nu):
        return (be[jnp.minimum(i, nu[0] - 1)], 0, 0)

    w_spec = pl.BlockSpec((None, D_MODEL, D_MODEL), w_map)
    b_spec = pl.BlockSpec((None, 1, D_MODEL), w_map)
    return pl.pallas_call(
        _moe_kernel,
        out_shape=jax.ShapeDtypeStruct((p, HALF), jnp.int32),
        grid_spec=pltpu.PrefetchScalarGridSpec(
            num_scalar_prefetch=3,
            grid=(nb,),
            in_specs=[
                pl.BlockSpec((tm, HALF), row_map),
                w_spec, b_spec, w_spec, b_spec, w_spec, b_spec,
            ],
            out_specs=pl.BlockSpec((tm, HALF), lambda i, be, nv, nu: (i, 0)),
            scratch_shapes=[pltpu.VMEM((D_MODEL, D_MODEL), BF16)] * 3,
        ),
        compiler_params=pltpu.CompilerParams(
            dimension_semantics=("arbitrary",), vmem_limit_bytes=VMEM_LIMIT),
        name="moe_experts",
    )(block_e, n_valid, n_used, xs, wg, bg, wu, bu, wd, bd)


def _sc_mesh():
    return plsc.VectorSubcoreMesh(core_axis_name="core", subcore_axis_name="subcore")


def _sc_index_rows(dest_t):
    k, n = dest_t.shape
    w = SC_WINDOW
    return jnp.pad(dest_t.reshape(k * n // w, w), ((0, 0), (0, LANES - w)))


def _sc_dispatch(x1, slot_idx, p):
    n, d = x1.shape
    w = SC_WINDOW
    windows = n // w

    @pl.kernel(out_type=jax.ShapeDtypeStruct((p, d), x1.dtype), mesh=_sc_mesh(), scratch_types=[],
               name="sc_dispatch")
    def run(x_hbm, d_hbm, o_hbm):
        def body(x_vmem, *d_vmems):
            for d_vmem in d_vmems:
                pltpu.sync_copy(x_vmem, o_hbm.at[d_vmem.at[0, pl.ds(0, w)]])

        pltpu.emit_pipeline(
            body,
            grid=(windows,),
            in_specs=[pl.BlockSpec((w, d), lambda i: (i, 0))]
            + [pl.BlockSpec((1, LANES), functools.partial(lambda k, i: (k * windows + i, 0), k))
               for k in range(TOP_K)],
            out_specs=[],
            core_axis_name=("core", "subcore"),
            dimension_semantics=(pltpu.PARALLEL,),
        )(x_hbm, *([d_hbm] * TOP_K))

    return run(x1, slot_idx)


def _sc_gather(ys, idx):
    w = SC_WINDOW
    r = idx.shape[0] * w
    d = ys.shape[1]

    @pl.kernel(out_type=jax.ShapeDtypeStruct((r, d), ys.dtype), mesh=_sc_mesh(), scratch_types=[],
               name="sc_gather")
    def run(y_hbm, i_hbm, o_hbm):
        def body(i_vmem, o_vmem):
            pltpu.sync_copy(y_hbm.at[i_vmem.at[0, pl.ds(0, w)]], o_vmem)

        pltpu.emit_pipeline(
            body,
            grid=(r // w,),
            in_specs=[pl.BlockSpec((1, LANES), lambda i: (i, 0))],
            out_specs=[pl.BlockSpec((w, d), lambda i: (i, 0))],
            core_axis_name=("core", "subcore"),
            dimension_semantics=(pltpu.PARALLEL,),
        )(i_hbm, o_hbm)

    return run(ys, idx)


def _ln2_kernel(x1_ref, yg_ref, rt_ref, g_ref, b_ref, o_ref):
    f = jnp.zeros(x1_ref.shape, F32)
    for k in range(TOP_K):
        f = f + _unpack_row(yg_ref[k]) * rt_ref[:, RT_GATE + k:RT_GATE + k + 1]
    o_ref[...] = _layer_norm(DEEPNORM_ALPHA * x1_ref[...] + f, g_ref[...], b_ref[...])


def _combine_ln2(x1, yg, rt, g2, b2):
    n = x1.shape[0]
    tm = LN2_TM
    return pl.pallas_call(
        _ln2_kernel,
        out_shape=jax.ShapeDtypeStruct((n, D_MODEL), F32),
        grid=(n // tm,),
        in_specs=[
            pl.BlockSpec((tm, D_MODEL), lambda i: (i, 0)),
            pl.BlockSpec((TOP_K, tm, HALF), lambda i: (0, i, 0)),
            pl.BlockSpec((tm, ROUTER_COLS), lambda i: (i, 0)),
            pl.BlockSpec((1, D_MODEL), lambda i: (0, 0)),
            pl.BlockSpec((1, D_MODEL), lambda i: (0, 0)),
        ],
        out_specs=pl.BlockSpec((tm, D_MODEL), lambda i: (i, 0)),
        compiler_params=pltpu.CompilerParams(
            dimension_semantics=("parallel",), vmem_limit_bytes=VMEM_LIMIT),
        name="combine_ln2",
    )(x1, yg, rt, g2, b2)


def _rotary_tables(seq):
    inv = ROPE_THETA ** (-jnp.arange(0, ROT_DIM, 2, dtype=F32) / ROT_DIM)
    ang = jnp.arange(seq, dtype=F32)[:, None] * inv[None, :]
    cos, sin = jnp.cos(ang), jnp.sin(ang)
    ones = jnp.ones((seq, QK_DIM - ROT_DIM), F32)
    zeros = jnp.zeros((seq, QK_DIM - ROT_DIM), F32)
    zh = jnp.zeros((seq, ROT_HALF), F32)
    cos64 = jnp.concatenate([cos, cos, ones], axis=1)
    sa64 = jnp.concatenate([-sin, zh, zeros], axis=1)
    sb64 = jnp.concatenate([zh, sin, zeros], axis=1)
    rep = LANES // QK_DIM
    return (jnp.tile(cos64, (1, rep)), jnp.tile(sa64, (1, rep)), jnp.tile(sb64, (1, rep)), cos.T, sin.T)


def _slot_layout(rtt, counts, tm):
    n = rtt.shape[1]
    p = n * TOP_K + N_EXPERTS * tm
    nb = p // tm
    expert = rtt[RT_EXPERT:RT_EXPERT + TOP_K].astype(jnp.int32)
    rank = rtt[RT_RANK:RT_RANK + TOP_K].astype(jnp.int32)
    padded = ((counts + tm - 1) // tm) * tm
    pad_ends = jnp.cumsum(padded)
    pad_starts = pad_ends - padded
    ids = jnp.arange(N_EXPERTS, dtype=jnp.int32)[:, None, None]
    dest_t = jnp.sum(jnp.where(expert[None] == ids, pad_starts[:, None, None], 0), axis=0) + rank
    block_start = jnp.arange(nb, dtype=jnp.int32) * tm
    block_e = jnp.minimum(jnp.sum(pad_ends[None, :] <= block_start[:, None], axis=1),
                          N_EXPERTS - 1).astype(jnp.int32)
    own = block_e[:, None] == jnp.arange(N_EXPERTS, dtype=jnp.int32)[None, :]
    valid_end = jnp.sum(jnp.where(own, (pad_starts + counts)[None, :], 0), axis=1)
    n_valid = jnp.clip(valid_end - block_start, 0, tm).astype(jnp.int32)
    n_used = (pad_ends[-1] // tm).astype(jnp.int32).reshape(1)
    return dest_t.astype(jnp.int32), block_e, n_valid, n_used, p


def _trunk(x, prm):
    b, s, d = x.shape
    n = b * s
    x2d = x.reshape(n, d)
    h, qt, vt = _inproj(x2d, prm["w_in"], prm["w_q_t"], prm["w_v_t"], prm["b_gate"], _rotary_tables(s), b, s)
    a = _attention(h.reshape(b, s, H_COLS), qt, vt, prm["lam"], prm["subln_g"], prm["lam_init"])
    x1, x1p, rt, rtt, cnt = _mix(a.reshape(n, d), h, x2d, s, prm["conv_w"], prm["w_attn_br"],
                                 prm["w_conv_br"], prm["w_o"], prm["ln1_g"], prm["ln1_b"],
                                 prm["w_router_t"], prm["b_router_b"])
    counts = cnt[:, 0].astype(jnp.int32)
    dest_t, block_e, n_valid, n_used, p = _slot_layout(rtt, counts, MOE_TM)
    slot_idx = _sc_index_rows(dest_t)
    xs = _sc_dispatch(x1p, slot_idx, p)
    ys = _moe_experts(block_e, n_valid, n_used, xs, prm["w_exp_gate"], prm["b_exp_gate"],
                      prm["w_exp_up"], prm["b_exp_up"], prm["w_exp_down"], prm["b_exp_down"])
    yg = _sc_gather(ys, slot_idx).reshape(TOP_K, n, HALF)
    y = _combine_ln2(x1, yg, rt, prm["ln2_g"], prm["ln2_b"])
    return y.reshape(b, s, d)


def kernel(x_prompt, x_sample, w_in, b_branch_gate, lambda_q1, lambda_k1, lambda_q2, lambda_k2, subln_g,
           conv_w, w_attn_br, w_conv_br, w_o, ln1_g, ln1_b, w_router, b_router, w_exp_gate, b_exp_gate,
           w_exp_up, b_exp_up, w_exp_down, b_exp_down, ln2_g, ln2_b):
    l = 0
    lam_init = 0.8 - 0.6 * math.exp(-0.3 * l)
    lam = (jnp.exp(jnp.sum(lambda_q1[l].astype(F32) * lambda_k1[l].astype(F32)))
           - jnp.exp(jnp.sum(lambda_q2[l].astype(F32) * lambda_k2[l].astype(F32))) + lam_init)
    prm = {
        "lam_init": lam_init,
        "lam": lam.reshape(1).astype(F32),
        "w_in": w_in[l].astype(BF16),
        "w_q_t": w_in[l][:, Q_STEP * COL_BLOCK:(Q_STEP + 1) * COL_BLOCK].T.astype(BF16),
        "w_v_t": w_in[l][:, V_STEP * COL_BLOCK:(V_STEP + 1) * COL_BLOCK].T.astype(BF16),
        "b_gate": b_branch_gate[l].reshape(1, 2 * D_MODEL),
        "subln_g": subln_g[l].reshape(1, V_DIM),
        "conv_w": jnp.pad(conv_w[l], ((0, SUBLANES - conv_w.shape[1]), (0, 0))),
        "w_attn_br": w_attn_br[l].astype(BF16),
        "w_conv_br": w_conv_br[l].astype(BF16),
        "w_o": w_o[l].astype(BF16),
        "ln1_g": ln1_g[l].reshape(1, D_MODEL),
        "ln1_b": ln1_b[l].reshape(1, D_MODEL),
        "w_router_t": jnp.pad(w_router[l].T, ((0, ROUTER_COLS - N_EXPERTS), (0, 0))).astype(BF16),
        "b_router_b": jnp.broadcast_to(b_router[l].astype(F32)[:, None], (N_EXPERTS, MIX_SUB)),
        "w_exp_gate": w_exp_gate[l],
        "b_exp_gate": b_exp_gate[l].reshape(N_EXPERTS, 1, D_MODEL),
        "w_exp_up": w_exp_up[l],
        "b_exp_up": b_exp_up[l].reshape(N_EXPERTS, 1, D_MODEL),
        "w_exp_down": w_exp_down[l],
        "b_exp_down": b_exp_down[l].reshape(N_EXPERTS, 1, D_MODEL),
        "ln2_g": ln2_g[l].reshape(1, D_MODEL),
        "ln2_b": ln2_b[l].reshape(1, D_MODEL),
    }
    return _trunk(x_prompt, prm), _trunk(x_sample, prm)
```
